```python
import math
import jax, jax.numpy as jnp
from jax import lax
import numpy as np

D_MODEL = 1024
BATCH = 8
SEQ = 8192
DEPTH = 2

N_MIXERS = 2
EPS = 1e-6
POOL_WINDOWS = (2, 4, 8, 16)
N_POOL_GROUPS = len(POOL_WINDOWS)
POOL_GROUP_DIM = D_MODEL // N_POOL_GROUPS
HEAD_DIM = 64
N_HEADS = D_MODEL // HEAD_DIM
D_ATTN = N_HEADS * HEAD_DIM
ATTN_PATTERNS = ((128, 1), (512, 4), (2048, 16))
N_ATTN_GROUPS = len(ATTN_PATTERNS)
HEAD_GROUPS = tuple(N_HEADS // N_ATTN_GROUPS + (1 if g < N_HEADS % N_ATTN_GROUPS else 0) for g in range(N_ATTN_GROUPS))
ROPE_THETA = 10000.0
D_FF = -(-8 * D_MODEL // (3 * 256)) * 256
N_POOL_LAYERS = (DEPTH + 1) // 2
N_ATTN_LAYERS = DEPTH // 2
NEG_INF = -1e30

kernel_name = "hybrid_pool_dilated_swa_swiglu"


def rmsnorm(x, g):
    xf = x.astype(jnp.float32)
    y = xf * lax.rsqrt(jnp.mean(xf * xf, axis=-1, keepdims=True) + EPS)
    return (y * g.astype(jnp.float32)).astype(x.dtype)


def rope(t):
    S, hd = t.shape[1], t.shape[-1]
    inv_freq = 1.0 / (ROPE_THETA ** (jnp.arange(0, hd, 2, dtype=jnp.float32) / hd))
    ang = jnp.arange(S, dtype=jnp.float32)[:, None] * inv_freq[None, :]
    ang = jnp.concatenate([ang, ang], axis=-1)[None, :, None, :]
    tf = t.astype(jnp.float32)
    t1, t2 = tf[..., : hd // 2], tf[..., hd // 2 :]
    rot = jnp.concatenate([-t2, t1], axis=-1)
    return (tf * jnp.cos(ang) + rot * jnp.sin(ang)).astype(t.dtype)


def pool_mixer(h, w_in, w_group, scale, w_out):
    B, S, _ = h.shape
    u = jnp.einsum('bsd,de->bse', h, w_in).reshape(B, S, N_POOL_GROUPS, POOL_GROUP_DIM)
    cs = jnp.cumsum(u.astype(jnp.float32), axis=1)
    pos = jnp.arange(S)
    outs = []
    for g, w in enumerate(POOL_WINDOWS):
        c = cs[:, :, g]
        lag = jnp.pad(c, ((0, 0), (w, 0), (0, 0)))[:, :S]
        cnt = jnp.minimum(pos + 1, w).astype(jnp.float32)[None, :, None]
        outs.append((c - lag) / cnt - u[:, :, g].astype(jnp.float32))
    p = jnp.stack(outs, axis=2).astype(h.dtype)
    z = jnp.einsum('bsgc,gce->bsge', p, w_group).reshape(B, S, D_MODEL) * scale
    return jnp.einsum('bsd,de->bse', z, w_out)


def dilated_window_attention(q, k, v, window, dilation):
    B, S, H, hd = q.shape
    w = window // dilation
    L = S // dilation
    nb = -(-L // w)
    Lp = nb * w

    def to_blocks(t):
        t = t.reshape(B, L, dilation, H, hd)
        t = jnp.pad(t, ((0, 0), (0, Lp - L), (0, 0), (0, 0), (0, 0)))
        return t.reshape(B, nb, w, dilation, H, hd)

    qb, kb, vb = to_blocks(q), to_blocks(k), to_blocks(v)

    def with_prev(t):
        prev = jnp.concatenate([jnp.zeros_like(t[:, :1]), t[:, :-1]], axis=1)
        return jnp.concatenate([prev, t], axis=2)

    kc, vc = with_prev(kb), with_prev(vb)
    s = jnp.einsum('bnqrhd,bnkrhd->bnrhqk', qb, kc).astype(jnp.float32)
    qi = jnp.arange(w)[:, None]
    kj = jnp.arange(2 * w)[None, :]
    dist = w + qi - kj
    key_idx = jnp.arange(nb)[:, None, None] * w - w + kj[None]
    mask = (dist >= 0)[None] & (dist <= w)[None] & (key_idx >= 0)
    s = jnp.where(mask[None, :, None, None], s, NEG_INF)
    m = jnp.max(s, axis=-1, keepdims=True)
    e = jnp.exp(s - m)
    den = jnp.sum(e, axis=-1, keepdims=True)
    p = (e / den).astype(v.dtype)
    o = jnp.einsum('bnrhqk,bnkrhd->bnqrhd', p, vc)
    lse = (m + jnp.log(den))[..., 0]
    lse = jnp.transpose(lse, (0, 1, 4, 2, 3))
    o = o.reshape(B, Lp, dilation, H, hd)[:, :L].reshape(B, S, H, hd)
    lse = lse.reshape(B, Lp, dilation, H)[:, :L].reshape(B, S, H)
    return o, lse


def attn_mixer(h, w_qkv, w_out):
    B, S, _ = h.shape
    qkv = jnp.einsum('bsd,de->bse', h, w_qkv).reshape(B, S, 3, N_HEADS, HEAD_DIM)
    q = rope(qkv[:, :, 0]) * jnp.asarray(HEAD_DIM ** -0.5, dtype=h.dtype)
    k = rope(qkv[:, :, 1])
    v = qkv[:, :, 2]
    outs, lses = [], []
    start = 0
    for (window, dilation), n_g in zip(ATTN_PATTERNS, HEAD_GROUPS):
        sl = slice(start, start + n_g)
        o_g, lse_g = dilated_window_attention(q[:, :, sl], k[:, :, sl], v[:, :, sl], window, dilation)
        outs.append(o_g)
        lses.append(jax.nn.logsumexp(lse_g, axis=-1) - math.log(n_g))
        start += n_g
    alpha = jax.nn.softmax(jnp.stack(lses, axis=-1), axis=-1)
    merged = jnp.concatenate(
        [o_g * (N_ATTN_GROUPS * alpha[:, :, g]).astype(o_g.dtype)[:, :, None, None] for g, o_g in enumerate(outs)],
        axis=2,
    ).reshape(B, S, D_ATTN)
    return jnp.einsum('bse,ed->bsd', merged, w_out)


def swiglu(h, w_gate, w_up, w_down):
    g = jnp.einsum('bsd,df->bsf', h, w_gate)
    u = jnp.einsum('bsd,df->bsf', h, w_up)
    return jnp.einsum('bsf,fd->bsd', jax.nn.silu(g) * u, w_down)


def _fwd_setup_inputs(seed: int = 0) -> dict:
    key = jax.random.key(seed)
    ks = jax.random.split(key, 14)
    f32 = jnp.float32
    nrm = lambda k, shape, fan_in: jax.random.normal(k, shape, f32) * (fan_in ** -0.5)
    return {
        "x": jax.random.normal(ks[0], (BATCH, SEQ, D_MODEL), f32),
        "norm_mix": 1.0 + 0.05 * jax.random.normal(ks[1], (DEPTH, D_MODEL), f32),
        "norm_ffn": 1.0 + 0.05 * jax.random.normal(ks[2], (DEPTH, D_MODEL), f32),
        "norm_final": 1.0 + 0.05 * jax.random.normal(ks[3], (D_MODEL,), f32),
        "pool_w_in": nrm(ks[4], (N_POOL_LAYERS, D_MODEL, D_MODEL), D_MODEL),
        "pool_w_group": nrm(ks[5], (N_POOL_LAYERS, N_POOL_GROUPS, POOL_GROUP_DIM, POOL_GROUP_DIM), POOL_GROUP_DIM),
        "pool_scale": 1.0 + 0.1 * jax.random.normal(ks[6], (N_POOL_LAYERS, D_MODEL), f32),
        "pool_w_out": nrm(ks[7], (N_POOL_LAYERS, D_MODEL, D_MODEL), D_MODEL),
        "attn_w_qkv": nrm(ks[8], (N_ATTN_LAYERS, D_MODEL, 3 * D_ATTN), D_MODEL),
        "attn_w_out": nrm(ks[9], (N_ATTN_LAYERS, D_ATTN, D_MODEL), D_ATTN),
        "ffn_w_gate": nrm(ks[10], (DEPTH, D_MODEL, D_FF), D_MODEL),
        "ffn_w_up": nrm(ks[11], (DEPTH, D_MODEL, D_FF), D_MODEL),
        "ffn_w_down": nrm(ks[12], (DEPTH, D_FF, D_MODEL), D_FF),
    }


def _fwd_reference(x, norm_mix, norm_ffn, norm_final, pool_w_in, pool_w_group, pool_scale, pool_w_out,
              attn_w_qkv, attn_w_out, ffn_w_gate, ffn_w_up, ffn_w_down):
    for i in range(DEPTH):
        h = rmsnorm(x, norm_mix[i])
        j = i // N_MIXERS
        if i % N_MIXERS == 0:
            y = pool_mixer(h, pool_w_in[j], pool_w_group[j], pool_scale[j], pool_w_out[j])
        else:
            y = attn_mixer(h, attn_w_qkv[j], attn_w_out[j])
        x = x + y
        h = rmsnorm(x, norm_ffn[i])
        x = x + swiglu(h, ffn_w_gate[i], ffn_w_up[i], ffn_w_down[i])
    return rmsnorm(x, norm_final)


import jax as _jax
import jax.numpy as _jnp

TWIN_FORMAT = 'train_step'
FWD_PARAMS = ['x', 'norm_mix', 'norm_ffn', 'norm_final', 'pool_w_in', 'pool_w_group', 'pool_scale', 'pool_w_out', 'attn_w_qkv', 'attn_w_out', 'ffn_w_gate', 'ffn_w_up', 'ffn_w_down']
TWIN_WEIGHTS = ['norm_mix', 'norm_ffn', 'norm_final', 'pool_w_in', 'pool_w_group', 'pool_scale', 'pool_w_out', 'attn_w_qkv', 'attn_w_out', 'ffn_w_gate', 'ffn_w_up', 'ffn_w_down']
TWIN_DIFF_INPUT = 'x'
TWIN_INPUTS = ['x', 'norm_mix', 'norm_ffn', 'norm_final', 'pool_w_in', 'pool_w_group', 'pool_scale', 'pool_w_out', 'attn_w_qkv', 'attn_w_out', 'ffn_w_gate', 'ffn_w_up', 'ffn_w_down', 'loss_target', 'm_norm_mix', 'm_norm_ffn', 'm_norm_final', 'm_pool_w_in', 'm_pool_w_group', 'm_pool_scale', 'm_pool_w_out', 'm_attn_w_qkv', 'm_attn_w_out', 'm_ffn_w_gate', 'm_ffn_w_up', 'm_ffn_w_down', 'v_norm_mix', 'v_norm_ffn', 'v_norm_final', 'v_pool_w_in', 'v_pool_w_group', 'v_pool_scale', 'v_pool_w_out', 'v_attn_w_qkv', 'v_attn_w_out', 'v_ffn_w_gate', 'v_ffn_w_up', 'v_ffn_w_down']
TWIN_OUTPUTS = ['loss', 'grad_x', 'grad_norm_mix', 'grad_norm_ffn', 'grad_norm_final', 'grad_pool_w_in', 'grad_pool_w_group', 'grad_pool_scale', 'grad_pool_w_out', 'grad_attn_w_qkv', 'grad_attn_w_out', 'grad_ffn_w_gate', 'grad_ffn_w_up', 'grad_ffn_w_down', 'delta_norm_mix', 'delta_norm_ffn', 'delta_norm_final', 'delta_pool_w_in', 'delta_pool_w_group', 'delta_pool_scale', 'delta_pool_w_out', 'delta_attn_w_qkv', 'delta_attn_w_out', 'delta_ffn_w_gate', 'delta_ffn_w_up', 'delta_ffn_w_down', 'new_m_norm_mix', 'new_m_norm_ffn', 'new_m_norm_final', 'new_m_pool_w_in', 'new_m_pool_w_group', 'new_m_pool_scale', 'new_m_pool_w_out', 'new_m_attn_w_qkv', 'new_m_attn_w_out', 'new_m_ffn_w_gate', 'new_m_ffn_w_up', 'new_m_ffn_w_down', 'new_v_norm_mix', 'new_v_norm_ffn', 'new_v_norm_final', 'new_v_pool_w_in', 'new_v_pool_w_group', 'new_v_pool_scale', 'new_v_pool_w_out', 'new_v_attn_w_qkv', 'new_v_attn_w_out', 'new_v_ffn_w_gate', 'new_v_ffn_w_up', 'new_v_ffn_w_down']
TWIN_LEAF_KINDS = {'loss': 'loss', 'grad_x': 'grad_x', 'grad_norm_mix': 'grad_w', 'grad_norm_ffn': 'grad_w', 'grad_norm_final': 'grad_w', 'grad_pool_w_in': 'grad_w', 'grad_pool_w_group': 'grad_w', 'grad_pool_scale': 'grad_w', 'grad_pool_w_out': 'grad_w', 'grad_attn_w_qkv': 'grad_w', 'grad_attn_w_out': 'grad_w', 'grad_ffn_w_gate': 'grad_w', 'grad_ffn_w_up': 'grad_w', 'grad_ffn_w_down': 'grad_w', 'delta_norm_mix': 'delta_w', 'delta_norm_ffn': 'delta_w', 'delta_norm_final': 'delta_w', 'delta_pool_w_in': 'delta_w', 'delta_pool_w_group': 'delta_w', 'delta_pool_scale': 'delta_w', 'delta_pool_w_out': 'delta_w', 'delta_attn_w_qkv': 'delta_w', 'delta_attn_w_out': 'delta_w', 'delta_ffn_w_gate': 'delta_w', 'delta_ffn_w_up': 'delta_w', 'delta_ffn_w_down': 'delta_w', 'new_m_norm_mix': 'new_m', 'new_m_norm_ffn': 'new_m', 'new_m_norm_final': 'new_m', 'new_m_pool_w_in': 'new_m', 'new_m_pool_w_group': 'new_m', 'new_m_pool_scale': 'new_m', 'new_m_pool_w_out': 'new_m', 'new_m_attn_w_qkv': 'new_m', 'new_m_attn_w_out': 'new_m', 'new_m_ffn_w_gate': 'new_m', 'new_m_ffn_w_up': 'new_m', 'new_m_ffn_w_down': 'new_m', 'new_v_norm_mix': 'new_v', 'new_v_norm_ffn': 'new_v', 'new_v_norm_final': 'new_v', 'new_v_pool_w_in': 'new_v', 'new_v_pool_w_group': 'new_v', 'new_v_pool_scale': 'new_v', 'new_v_pool_w_out': 'new_v', 'new_v_attn_w_qkv': 'new_v', 'new_v_attn_w_out': 'new_v', 'new_v_ffn_w_gate': 'new_v', 'new_v_ffn_w_up': 'new_v', 'new_v_ffn_w_down': 'new_v'}


def _forward(args):
    return _fwd_reference(*[args[k] for k in FWD_PARAMS])


def _output_shape():
    def fwd():
        inp = _fwd_setup_inputs(0)
        return _fwd_reference(*[inp[k] for k in FWD_PARAMS])
    out = _jax.eval_shape(fwd)
    return out.shape, out.dtype

N_MICROBATCH = 1
ADAM_LR = 0.001
ADAM_B1 = 0.9
ADAM_B2 = 0.999
ADAM_EPS = 1e-08
ADAM_WD = 0.01
ADAM_STEP = 10
PER_EXAMPLE_BATCH_AXIS = {'x': 0, 'loss_target': 0}
SHARED_INPUTS = []
_WEIGHT_DTYPES = {'norm_mix': _jnp.float32, 'norm_ffn': _jnp.float32, 'norm_final': _jnp.float32, 'pool_w_in': _jnp.float32, 'pool_w_group': _jnp.float32, 'pool_scale': _jnp.float32, 'pool_w_out': _jnp.float32, 'attn_w_qkv': _jnp.float32, 'attn_w_out': _jnp.float32, 'ffn_w_gate': _jnp.float32, 'ffn_w_up': _jnp.float32, 'ffn_w_down': _jnp.float32}
MOMENT_SCALE = {'norm_mix': 1.536189e-01, 'norm_ffn': 1.515991e-01, 'norm_final': 6.408391e+01, 'pool_w_in': 1.987070e-01, 'pool_w_group': 2.018403e-01, 'pool_scale': 2.409214e-01, 'pool_w_out': 2.049159e-01, 'attn_w_qkv': 5.117715e-02, 'attn_w_out': 5.491150e-02, 'ffn_w_gate': 6.510169e-02, 'ffn_w_up': 6.317290e-02, 'ffn_w_down': 1.048327e-01}


def _to_microbatches(a, axis):
    t = _jnp.moveaxis(a, axis, 0)
    t = t.reshape((N_MICROBATCH, t.shape[0] // N_MICROBATCH) + t.shape[1:])
    return _jnp.moveaxis(t, 1, axis + 1)


def setup_inputs(seed: int = 0) -> dict:
    inp = _fwd_setup_inputs(seed)
    key = _jax.random.fold_in(_jax.random.key(seed), 7919)
    shape, _ = _output_shape()
    out = dict(inp)
    out["loss_target"] = _jax.random.normal(_jax.random.fold_in(key, 0), shape, _jnp.float32)
    for i, name in enumerate(TWIN_WEIGHTS):
        w = inp[name].astype(_jnp.float32)
        if MOMENT_SCALE is None:
            s = _jnp.sqrt(_jnp.mean(_jnp.square(w)) + 1e-30)
        else:
            s = MOMENT_SCALE[name]
        km, kv = _jax.random.split(_jax.random.fold_in(key, i + 1))
        out[name] = w
        out["m_" + name] = s * _jax.random.normal(km, w.shape, _jnp.float32)
        out["v_" + name] = (s * s) * _jax.random.uniform(kv, w.shape, _jnp.float32, 0.5, 1.5)
    if N_MICROBATCH > 1:
        for name, axis in PER_EXAMPLE_BATCH_AXIS.items():
            out[name] = _to_microbatches(out[name], axis)
    return {'x': out['x'], 'norm_mix': out['norm_mix'], 'norm_ffn': out['norm_ffn'], 'norm_final': out['norm_final'], 'pool_w_in': out['pool_w_in'], 'pool_w_group': out['pool_w_group'], 'pool_scale': out['pool_scale'], 'pool_w_out': out['pool_w_out'], 'attn_w_qkv': out['attn_w_qkv'], 'attn_w_out': out['attn_w_out'], 'ffn_w_gate': out['ffn_w_gate'], 'ffn_w_up': out['ffn_w_up'], 'ffn_w_down': out['ffn_w_down'], 'loss_target': out['loss_target'], 'm_norm_mix': out['m_norm_mix'], 'm_norm_ffn': out['m_norm_ffn'], 'm_norm_final': out['m_norm_final'], 'm_pool_w_in': out['m_pool_w_in'], 'm_pool_w_group': out['m_pool_w_group'], 'm_pool_scale': out['m_pool_scale'], 'm_pool_w_out': out['m_pool_w_out'], 'm_attn_w_qkv': out['m_attn_w_qkv'], 'm_attn_w_out': out['m_attn_w_out'], 'm_ffn_w_gate': out['m_ffn_w_gate'], 'm_ffn_w_up': out['m_ffn_w_up'], 'm_ffn_w_down': out['m_ffn_w_down'], 'v_norm_mix': out['v_norm_mix'], 'v_norm_ffn': out['v_norm_ffn'], 'v_norm_final': out['v_norm_final'], 'v_pool_w_in': out['v_pool_w_in'], 'v_pool_w_group': out['v_pool_w_group'], 'v_pool_scale': out['v_pool_scale'], 'v_pool_w_out': out['v_pool_w_out'], 'v_attn_w_qkv': out['v_attn_w_qkv'], 'v_attn_w_out': out['v_attn_w_out'], 'v_ffn_w_gate': out['v_ffn_w_gate'], 'v_ffn_w_up': out['v_ffn_w_up'], 'v_ffn_w_down': out['v_ffn_w_down']}


def _loss(weights, diff, rest, loss_target):
    with _jax.named_scope("forward"):
        args = {**rest, TWIN_DIFF_INPUT: diff, **{k: w.astype(_WEIGHT_DTYPES[k]) for k, w in weights.items()}}
        y = _forward(args)
    with _jax.named_scope("loss_head"):
        err = _jnp.square(y.astype(_jnp.float32) - loss_target)
        return 0.5 * _jnp.sum(_jnp.mean(err, axis=-1)) if err.ndim else 0.5 * err


def _adamw(w, g, m, v):
    m = ADAM_B1 * m + (1.0 - ADAM_B1) * g
    v = ADAM_B2 * v + (1.0 - ADAM_B2) * _jnp.square(g)
    m_hat = m / (1.0 - ADAM_B1 ** ADAM_STEP)
    v_hat = v / (1.0 - ADAM_B2 ** ADAM_STEP)
    delta = -ADAM_LR * (m_hat / (_jnp.sqrt(v_hat) + ADAM_EPS) + ADAM_WD * w)
    return delta, m, v


def reference(x, norm_mix, norm_ffn, norm_final, pool_w_in, pool_w_group, pool_scale, pool_w_out, attn_w_qkv, attn_w_out, ffn_w_gate, ffn_w_up, ffn_w_down, loss_target, m_norm_mix, m_norm_ffn, m_norm_final, m_pool_w_in, m_pool_w_group, m_pool_scale, m_pool_w_out, m_attn_w_qkv, m_attn_w_out, m_ffn_w_gate, m_ffn_w_up, m_ffn_w_down, v_norm_mix, v_norm_ffn, v_norm_final, v_pool_w_in, v_pool_w_group, v_pool_scale, v_pool_w_out, v_attn_w_qkv, v_attn_w_out, v_ffn_w_gate, v_ffn_w_up, v_ffn_w_down):
    given = dict(x=x, norm_mix=norm_mix, norm_ffn=norm_ffn, norm_final=norm_final, pool_w_in=pool_w_in, pool_w_group=pool_w_group, pool_scale=pool_scale, pool_w_out=pool_w_out, attn_w_qkv=attn_w_qkv, attn_w_out=attn_w_out, ffn_w_gate=ffn_w_gate, ffn_w_up=ffn_w_up, ffn_w_down=ffn_w_down, loss_target=loss_target, m_norm_mix=m_norm_mix, m_norm_ffn=m_norm_ffn, m_norm_final=m_norm_final, m_pool_w_in=m_pool_w_in, m_pool_w_group=m_pool_w_group, m_pool_scale=m_pool_scale, m_pool_w_out=m_pool_w_out, m_attn_w_qkv=m_attn_w_qkv, m_attn_w_out=m_attn_w_out, m_ffn_w_gate=m_ffn_w_gate, m_ffn_w_up=m_ffn_w_up, m_ffn_w_down=m_ffn_w_down, v_norm_mix=v_norm_mix, v_norm_ffn=v_norm_ffn, v_norm_final=v_norm_final, v_pool_w_in=v_pool_w_in, v_pool_w_group=v_pool_w_group, v_pool_scale=v_pool_scale, v_pool_w_out=v_pool_w_out, v_attn_w_qkv=v_attn_w_qkv, v_attn_w_out=v_attn_w_out, v_ffn_w_gate=v_ffn_w_gate, v_ffn_w_up=v_ffn_w_up, v_ffn_w_down=v_ffn_w_down)
    weights = {n: given[n] for n in TWIN_WEIGHTS}
    shared = {n: given[n] for n in SHARED_INPUTS}
    per_example = {n: given[n] for n in ['x']}
    grad_fn = _jax.value_and_grad(_loss, argnums=(0, 1))

    def one_microbatch(ex, loss_target):
        ex = dict(ex)
        diff = ex.pop(TWIN_DIFF_INPUT)
        return grad_fn(weights, diff, {**shared, **ex}, loss_target)

    if N_MICROBATCH == 1:
        loss, (grad_w, grad_x) = one_microbatch(per_example, given["loss_target"])
    else:
        def body(carry, xs):
            loss_sum, grad_sum = carry
            l_k, (gw_k, gx_k) = one_microbatch(xs[0], xs[1])
            with _jax.named_scope("update"):
                return (loss_sum + l_k, _jax.tree.map(_jnp.add, grad_sum, gw_k)), gx_k

        init = (_jnp.zeros((), _jnp.float32), _jax.tree.map(_jnp.zeros_like, weights))
        (loss, grad_w), grad_x = _jax.lax.scan(body, init, (per_example, given["loss_target"]))
    with _jax.named_scope("update"):
        delta_w, new_m, new_v = {}, {}, {}
        for n in TWIN_WEIGHTS:
            delta_w[n], new_m[n], new_v[n] = _adamw(weights[n], grad_w[n], given["m_" + n], given["v_" + n])
    return (loss, grad_x, *[grad_w[n] for n in TWIN_WEIGHTS], *[delta_w[n] for n in TWIN_WEIGHTS],
            *[new_m[n] for n in TWIN_WEIGHTS], *[new_v[n] for n in TWIN_WEIGHTS])
```

```python
import math

import jax
import jax.numpy as jnp
from jax import lax
from jax.experimental import pallas as pl
from jax.experimental.pallas import tpu as pltpu

D = 1024
F = 2816
N_DEV = 8
EPS = 1e-6
NEG_INF = -1e30
POOL_WINDOWS = (2, 4, 8, 16)
POOL_HALO = 16
POOL_GC = 256
HEAD_DIM = 64
HEAD_GROUPS = (6, 5, 5)
HEAD_OFFS = (0, 6, 11)
DILATIONS = (1, 4, 16)
ATT_W = 128
GROUP_LANES = 384
LSE_GROUP_LANE = 8
ROPE_THETA = 10000.0
ADAM_LR, ADAM_B1, ADAM_B2, ADAM_EPS, ADAM_WD, ADAM_STEP = 0.001, 0.9, 0.999, 1e-08, 0.01, 10

BF = jnp.bfloat16
F32 = jnp.float32
VMEM_LIMIT = 56 * 1024 * 1024
MESH = pl.DeviceIdType.MESH


def _params(*sem):
    return pltpu.CompilerParams(dimension_semantics=sem, vmem_limit_bytes=VMEM_LIMIT)


def _dot(a, b):
    return jnp.dot(a, b, preferred_element_type=F32)


def _dot_nt(a, b):
    return lax.dot_general(a, b, (((1,), (1,)), ((), ())), preferred_element_type=F32)


def _dot_tn(a, b):
    return lax.dot_general(a, b, (((0,), (0,)), ((), ())), preferred_element_type=F32)


def _rms_fwd(xv, g):
    r = lax.rsqrt(jnp.mean(xv * xv, axis=-1, keepdims=True) + EPS)
    return (xv * r) * g


def _rms_bwd(xv, g, dh):
    r = lax.rsqrt(jnp.mean(xv * xv, axis=-1, keepdims=True) + EPS)
    xhat = xv * r
    dg = jnp.sum(dh * xhat, axis=0, keepdims=True)
    dxh = dh * g
    dx = r * (dxh - xhat * jnp.mean(dxh * xhat, axis=-1, keepdims=True))
    return dx, dg


def _lane_col(tile, j):
    lane = lax.broadcasted_iota(jnp.int32, tile.shape, 1)
    return jnp.sum(jnp.where(lane == j, tile, 0.0), axis=-1, keepdims=True)


def _row_spec(tm, n):
    return pl.BlockSpec((tm, n), lambda i: (i, 0))


def _full_spec(shape):
    nd = len(shape)
    return pl.BlockSpec(shape, lambda *_: (0,) * nd)


def _norm_fwd(x, g, name, tm=512):
    T = x.shape[0]

    def body(x_ref, g_ref, h_ref):
        h_ref[...] = _rms_fwd(x_ref[...], g_ref[...]).astype(BF)

    return pl.pallas_call(
        body, name=name, grid=(T // tm,),
        in_specs=[_row_spec(tm, D), _full_spec((1, D))],
        out_specs=_row_spec(tm, D),
        out_shape=jax.ShapeDtypeStruct((T, D), BF),
        compiler_params=_params("parallel"),
    )(x, g)


def _pool_fwd(x, h, w_in, w_grp, scale, w_out, tm=512):
    T = x.shape[0]
    n = tm + POOL_HALO

    def body(x_ref, h_ref, win_ref, wg_ref, sc_ref, wout_ref, x1_ref, p_ref, tail_ref, z_ref):
        i = pl.program_id(0)

        @pl.when(i == 0)
        def _():
            tail_ref[...] = jnp.zeros_like(tail_ref)

        u = _dot(h_ref[...], win_ref[...])
        pos = i * tm + lax.broadcasted_iota(jnp.int32, (tm, 1), 0)
        for g, w in enumerate(POOL_WINDOWS):
            sl = slice(g * POOL_GC, (g + 1) * POOL_GC)
            ug = u[:, sl]
            s = jnp.concatenate([tail_ref[:, sl], ug], axis=0)
            step = 1
            while step < w:
                s = s + pltpu.roll(s, step, 0)
                step *= 2
            cnt = jnp.minimum(pos + 1, w).astype(F32)
            pg = (s[POOL_HALO:, :] / cnt - ug).astype(BF)
            p_ref[:, sl] = pg
            z_ref[:, sl] = (_dot(pg, wg_ref[g]) * sc_ref[:, sl]).astype(BF)
        tail_ref[...] = u[tm - POOL_HALO:, :]
        x1_ref[...] = x_ref[...] + _dot(z_ref[...], wout_ref[...])

    return pl.pallas_call(
        body, name="pool_fwd", grid=(T // tm,),
        in_specs=[_row_spec(tm, D), _row_spec(tm, D), _full_spec((D, D)), _full_spec((4, POOL_GC, POOL_GC)),
                  _full_spec((1, D)), _full_spec((D, D))],
        out_specs=[_row_spec(tm, D), _row_spec(tm, D)],
        out_shape=[jax.ShapeDtypeStruct((T, D), F32), jax.ShapeDtypeStruct((T, D), BF)],
        scratch_shapes=[pltpu.VMEM((POOL_HALO, D), F32), pltpu.VMEM((tm, D), BF)],
        compiler_params=_params("arbitrary"),
    )(x, h, w_in, w_grp, scale, w_out)


def _pool_bwd(dx1, x0, g0, p, w_in, w_grp, scale, w_out, tm=512):
    T = x0.shape[0]
    nt = T // tm
    n = tm + POOL_HALO
    rev = lambda i: (nt - 1 - i, 0)

    def body(dx1_ref, x0_ref, g_ref, p_ref, win_ref, wg_ref, sc_ref, wout_ref,
             dx0_ref, z_ref, dzp_ref, du_ref, h0_ref, dsc_ref, dg_ref, head_ref):
        i = pl.program_id(0)

        @pl.when(i == 0)
        def _():
            head_ref[...] = jnp.zeros_like(head_ref)
            dsc_ref[...] = jnp.zeros_like(dsc_ref)
            dg_ref[...] = jnp.zeros_like(dg_ref)

        dx1v = dx1_ref[...]
        dz = _dot_nt(dx1v.astype(BF), wout_ref[...])
        pos = (nt - 1 - i) * tm + lax.broadcasted_iota(jnp.int32, (tm, 1), 0)
        for g, w in enumerate(POOL_WINDOWS):
            sl = slice(g * POOL_GC, (g + 1) * POOL_GC)
            zpre = _dot(p_ref[:, sl], wg_ref[g])
            dzg = dz[:, sl]
            dsc_ref[:, sl] += jnp.sum(dzg * zpre, axis=0, keepdims=True)
            z_ref[:, sl] = (zpre * sc_ref[:, sl]).astype(BF)
            dzp = (dzg * sc_ref[:, sl]).astype(BF)
            dzp_ref[:, sl] = dzp
            dp = _dot_nt(dzp, wg_ref[g])
            cnt = jnp.minimum(pos + 1, w).astype(F32)
            dpc = dp / cnt
            s = jnp.concatenate([dpc, head_ref[:, sl]], axis=0)
            step = 1
            while step < w:
                s = s + pltpu.roll(s, n - step, 0)
                step *= 2
            head_ref[:, sl] = dpc[:POOL_HALO, :]
            du_ref[:, sl] = (s[:tm, :] - dp).astype(BF)
        dh0 = _dot_nt(du_ref[...], win_ref[...])
        x0v = x0_ref[...]
        h0_ref[...] = _rms_fwd(x0v, g_ref[...]).astype(BF)
        dx, dg = _rms_bwd(x0v, g_ref[...], dh0)
        dx0_ref[...] = dx1v + dx
        dg_ref[...] += dg

    bf_rows = jax.ShapeDtypeStruct((T, D), BF)
    vec = jax.ShapeDtypeStruct((1, D), F32)
    return pl.pallas_call(
        body, name="pool_bwd", grid=(nt,),
        in_specs=[pl.BlockSpec((tm, D), rev), pl.BlockSpec((tm, D), rev), _full_spec((1, D)), pl.BlockSpec((tm, D), rev),
                  _full_spec((D, D)), _full_spec((4, POOL_GC, POOL_GC)), _full_spec((1, D)), _full_spec((D, D))],
        out_specs=[pl.BlockSpec((tm, D), rev)] * 5 + [_full_spec((1, D))] * 2,
        out_shape=[jax.ShapeDtypeStruct((T, D), F32), bf_rows, bf_rows, bf_rows, bf_rows, vec, vec],
        scratch_shapes=[pltpu.VMEM((POOL_HALO, D), F32)],
        compiler_params=_params("arbitrary"),
    )(dx1, x0, g0, p, w_in, w_grp, scale, w_out)


def _ffn_fwd(x, h, wg_t, wu_t, wd, name, tm=512, fk=1408):
    T = x.shape[0]

    def body(x_ref, h_ref, wg_ref, wu_ref, wd_ref, xo_ref, a_ref, b_ref, s_ref, acc_ref):
        k = pl.program_id(1)

        @pl.when(k == 0)
        def _():
            acc_ref[...] = jnp.zeros_like(acc_ref)

        hv = h_ref[...]
        a = _dot_nt(hv, wg_ref[...])
        b = _dot_nt(hv, wu_ref[...])
        s = ((a * jax.nn.sigmoid(a)) * b).astype(BF)
        a_ref[...] = a.astype(BF)
        b_ref[...] = b.astype(BF)
        s_ref[...] = s
        acc_ref[...] += _dot(s, wd_ref[...])

        @pl.when(k == pl.num_programs(1) - 1)
        def _():
            xo_ref[...] = x_ref[...] + acc_ref[...]

    row = pl.BlockSpec((tm, D), lambda i, k: (i, 0))
    wsp = pl.BlockSpec((fk, D), lambda i, k: (k, 0))
    act = pl.BlockSpec((tm, fk), lambda i, k: (i, k))
    act_shape = jax.ShapeDtypeStruct((T, F), BF)
    return pl.pallas_call(
        body, name=name, grid=(T // tm, F // fk),
        in_specs=[row, row, wsp, wsp, wsp],
        out_specs=[row, act, act, act],
        out_shape=[jax.ShapeDtypeStruct((T, D), F32), act_shape, act_shape, act_shape],
        scratch_shapes=[pltpu.VMEM((tm, D), F32)],
        compiler_params=_params("parallel", "arbitrary"),
    )(x, h, wg_t, wu_t, wd)


def _ffn_bwd(dxo, x_in, g, a, b, wg_t, wu_t, wd, name, tm=512, fk=1408):
    T = x_in.shape[0]

    def body(dxo_ref, x_ref, g_ref, a_ref, b_ref, wg_ref, wu_ref, wd_ref,
             dx_ref, da_ref, db_ref, h_ref, dg_ref, dy_ref, dh_ref):
        i = pl.program_id(0)
        k = pl.program_id(1)

        @pl.when(jnp.logical_and(i == 0, k == 0))
        def _():
            dg_ref[...] = jnp.zeros_like(dg_ref)

        @pl.when(k == 0)
        def _():
            h_ref[...] = _rms_fwd(x_ref[...], g_ref[...]).astype(BF)
            dy_ref[...] = dxo_ref[...].astype(BF)
            dh_ref[...] = jnp.zeros_like(dh_ref)

        ds = _dot_nt(dy_ref[...], wd_ref[...])
        av = a_ref[...].astype(F32)
        bv = b_ref[...].astype(F32)
        sig = jax.nn.sigmoid(av)
        db = (ds * (av * sig)).astype(BF)
        da = (ds * bv * (sig * (1.0 + av * (1.0 - sig)))).astype(BF)
        da_ref[...] = da
        db_ref[...] = db
        dh_ref[...] += _dot(da, wg_ref[...]) + _dot(db, wu_ref[...])

        @pl.when(k == pl.num_programs(1) - 1)
        def _():
            dx, dg = _rms_bwd(x_ref[...], g_ref[...], dh_ref[...])
            dx_ref[...] = dxo_ref[...] + dx
            dg_ref[...] += dg

    row = pl.BlockSpec((tm, D), lambda i, k: (i, 0))
    wsp = pl.BlockSpec((fk, D), lambda i, k: (k, 0))
    act = pl.BlockSpec((tm, fk), lambda i, k: (i, k))
    vec = pl.BlockSpec((1, D), lambda i, k: (0, 0))
    act_shape = jax.ShapeDtypeStruct((T, F), BF)
    return pl.pallas_call(
        body, name=name, grid=(T // tm, F // fk),
        in_specs=[row, row, vec, act, act, wsp, wsp, wsp],
        out_specs=[row, act, act, row, vec],
        out_shape=[jax.ShapeDtypeStruct((T, D), F32), act_shape, act_shape, jax.ShapeDtypeStruct((T, D), BF),
                   jax.ShapeDtypeStruct((1, D), F32)],
        scratch_shapes=[pltpu.VMEM((tm, D), BF), pltpu.VMEM((tm, D), F32)],
        compiler_params=_params("arbitrary", "arbitrary"),
    )(dxo, x_in, g, a, b, wg_t, wu_t, wd)


def _ffn_wgrad(s, da, db, dxo, h, name, tk=256, fm=1408):
    T = h.shape[0]

    def body(s_ref, da_ref, db_ref, dxo_ref, h_ref, gd_ref, gg_ref, gu_ref):
        @pl.when(pl.program_id(1) == 0)
        def _():
            gd_ref[...] = jnp.zeros_like(gd_ref)
            gg_ref[...] = jnp.zeros_like(gg_ref)
            gu_ref[...] = jnp.zeros_like(gu_ref)

        hv = h_ref[...]
        gd_ref[...] += _dot_tn(s_ref[...], dxo_ref[...].astype(BF))
        gg_ref[...] += _dot_tn(da_ref[...], hv)
        gu_ref[...] += _dot_tn(db_ref[...], hv)

    act = pl.BlockSpec((tk, fm), lambda m, t: (t, m))
    row = pl.BlockSpec((tk, D), lambda m, t: (t, 0))
    out = pl.BlockSpec((fm, D), lambda m, t: (m, 0))
    shp = jax.ShapeDtypeStruct((F, D), F32)
    return pl.pallas_call(
        body, name=name, grid=(F // fm, T // tk),
        in_specs=[act, act, act, row, row],
        out_specs=[out, out, out],
        out_shape=[shp, shp, shp],
        compiler_params=_params("parallel", "arbitrary"),
    )(s, da, db, dxo, h)


def _wgrad(a, b, name, tk=512):
    T, M = a.shape
    N = b.shape[1]

    def body(a_ref, b_ref, o_ref):
        @pl.when(pl.program_id(0) == 0)
        def _():
            o_ref[...] = jnp.zeros_like(o_ref)

        o_ref[...] += _dot_tn(a_ref[...].astype(BF), b_ref[...].astype(BF))

    return pl.pallas_call(
        body, name=name, grid=(T // tk,),
        in_specs=[_row_spec(tk, M), _row_spec(tk, N)],
        out_specs=_full_spec((M, N)),
        out_shape=jax.ShapeDtypeStruct((M, N), F32),
        compiler_params=_params("arbitrary"),
    )(a, b)


def _wgrad_pool_groups(p, dzp, tk=512):
    T = p.shape[0]

    def body(p_ref, d_ref, o_ref):
        @pl.when(pl.program_id(1) == 0)
        def _():
            o_ref[...] = jnp.zeros_like(o_ref)

        o_ref[...] += _dot_tn(p_ref[...], d_ref[...])

    blk = pl.BlockSpec((tk, POOL_GC), lambda g, t: (t, g))
    return pl.pallas_call(
        body, name="pool_wgrad_groups", grid=(4, T // tk),
        in_specs=[blk, blk],
        out_specs=pl.BlockSpec((None, POOL_GC, POOL_GC), lambda g, t: (g, 0, 0)),
        out_shape=jax.ShapeDtypeStruct((4, POOL_GC, POOL_GC), F32),
        compiler_params=_params("parallel", "arbitrary"),
    )(p, dzp)


def _wgrad_stack(a, b3, name, tk=512):
    T, M = a.shape
    N = b3.shape[2]

    def body(a_ref, b_ref, o_ref):
        @pl.when(pl.program_id(0) == 0)
        def _():
            o_ref[...] = jnp.zeros_like(o_ref)

        av = a_ref[...]
        for w in range(3):
            o_ref[w] += _dot_tn(av, b_ref[w])

    return pl.pallas_call(
        body, name=name, grid=(T // tk,),
        in_specs=[_row_spec(tk, M), pl.BlockSpec((3, tk, N), lambda t: (0, t, 0))],
        out_specs=_full_spec((3, M, N)),
        out_shape=jax.ShapeDtypeStruct((3, M, N), F32),
        compiler_params=_params("arbitrary"),
    )(a, b3)


def _rot_half(t):
    lane = lax.broadcasted_iota(jnp.int32, t.shape, 1)
    first = (lane % HEAD_DIM) < (HEAD_DIM // 2)
    return jnp.where(first, -pltpu.roll(t, 128 - HEAD_DIM // 2, 1), pltpu.roll(t, HEAD_DIM // 2, 1))


def _scatter_rows(dst_ref, scr_ref, d, cast):
    nc, rows, _ = scr_ref.shape
    n = rows // d
    for c in range(nc):
        sl = slice(c * 128, (c + 1) * 128)
        for r in range(d):
            src = scr_ref[c] if d == 1 else scr_ref.at[c][pl.ds(r, n, stride=d), :]
            dst_ref[r, :, sl] = src.astype(cast)


def _gather_rows(scr_ref, src_ref, d):
    nc, rows, _ = scr_ref.shape
    n = rows // d
    for c in range(nc):
        sl = slice(c * 128, (c + 1) * 128)
        for r in range(d):
            val = src_ref[r, :, sl].astype(F32)
            if d == 1:
                scr_ref[c] = val
            else:
                scr_ref.at[c][pl.ds(r, n, stride=d), :] = val


def _chunks_to_rows(scr_ref):
    nc = scr_ref.shape[0]
    return scr_ref[0] if nc == 1 else jnp.concatenate([scr_ref[c] for c in range(nc)], axis=1)


def _rows_to_chunks(scr_ref, val):
    for c in range(scr_ref.shape[0]):
        scr_ref[c] = val[:, c * 128:(c + 1) * 128]


def _qkv_fwd(h, w3, cos, sin, d, name, tm=512):
    T = h.shape[0]
    L = T // d
    C = GROUP_LANES

    def body(h_ref, w_ref, cos_ref, sin_ref, o_ref, scr_ref):
        w = pl.program_id(1)
        _rows_to_chunks(scr_ref, _dot(h_ref[...], w_ref[...]))

        @pl.when(w < 2)
        def _():
            scale = jnp.where(w == 0, HEAD_DIM ** -0.5, 1.0).astype(F32)
            cv = cos_ref[...]
            sv = sin_ref[...]
            for c in range(C // 128):
                t = scr_ref[c]
                scr_ref[c] = (t * cv + _rot_half(t) * sv) * scale

        _scatter_rows(o_ref, scr_ref, d, BF)

    return pl.pallas_call(
        body, name=name, grid=(T // tm, 3),
        in_specs=[pl.BlockSpec((tm, D), lambda i, w: (i, 0)), pl.BlockSpec((None, D, C), lambda i, w: (w, 0, 0)),
                  pl.BlockSpec((tm, 128), lambda i, w: (i, 0)), pl.BlockSpec((tm, 128), lambda i, w: (i, 0))],
        out_specs=pl.BlockSpec((None, d, tm // d, C), lambda i, w: (w, 0, i, 0)),
        out_shape=jax.ShapeDtypeStruct((3, d, L, C), BF),
        scratch_shapes=[pltpu.VMEM((C // 128, tm, 128), F32)],
        compiler_params=_params("parallel", "arbitrary"),
    )(h, w3, cos, sin)


def _att_chunk(L):
    return min(L, 1024)


def _head_mask(shape, h):
    lane = lax.broadcasted_iota(jnp.int32, shape, 1)
    return (lane // HEAD_DIM) == (h % 2)


def _attn_fwd(qkv, nh, name):
    _, d, L, C = qkv.shape
    lc = _att_chunk(L)
    nblk = lc // ATT_W

    def body(q_ref, k_ref, kh_ref, v_ref, vh_ref, o_ref, st_ref):
        i = pl.program_id(1)
        qi = lax.broadcasted_iota(jnp.int32, (ATT_W, 2 * ATT_W), 0)
        kj = lax.broadcasted_iota(jnp.int32, (ATT_W, 2 * ATT_W), 1)
        band = jnp.logical_and(kj >= qi, kj <= qi + ATT_W)
        lane = lax.broadcasted_iota(jnp.int32, (ATT_W, 128), 1)

        def block(row0, kc, vc, mask):
            rows = pl.ds(row0, ATT_W)
            lses = []
            for hp in range(C // 128):
                sl = slice(hp * 128, (hp + 1) * 128)
                qp = q_ref[rows, sl]
                kp = kc[:, sl]
                vp = vc[:, sl]
                outs = []
                for h in range(2 * hp, min(2 * hp + 2, nh)):
                    hm = _head_mask(qp.shape, h)
                    s = _dot_nt(jnp.where(hm, qp, jnp.zeros_like(qp)), kp)
                    s = jnp.where(mask, s, NEG_INF)
                    m = jnp.max(s, axis=-1, keepdims=True)
                    e = jnp.exp(s - m)
                    den = jnp.sum(e, axis=-1, keepdims=True)
                    p = (e * pl.reciprocal(den)).astype(BF)
                    outs.append(_dot(p, vp))
                    lses.append(m + jnp.log(den))
                if len(outs) == 2:
                    o = jnp.where(_head_mask(outs[0].shape, 0), outs[0], outs[1])
                else:
                    o = jnp.where(_head_mask(outs[0].shape, 0), outs[0], 0.0)
                o_ref[rows, sl] = o.astype(BF)
            mm = lses[0]
            for l in lses[1:]:
                mm = jnp.maximum(mm, l)
            tot = jnp.exp(lses[0] - mm)
            for l in lses[1:]:
                tot = tot + jnp.exp(l - mm)
            tile = jnp.where(lane == LSE_GROUP_LANE, mm + jnp.log(tot) - math.log(nh), 0.0)
            for h, l in enumerate(lses):
                tile = jnp.where(lane == h, l, tile)
            st_ref[rows, :] = tile

        first_mask = jnp.logical_and(band, jnp.logical_or(kj >= ATT_W, i > 0))
        block(0, jnp.concatenate([kh_ref[...], k_ref[pl.ds(0, ATT_W), :]], axis=0),
              jnp.concatenate([vh_ref[...], v_ref[pl.ds(0, ATT_W), :]], axis=0), first_mask)

        if nblk > 1:
            def step(blk, carry):
                prev = pl.ds(pl.multiple_of((blk - 1) * ATT_W, ATT_W), 2 * ATT_W)
                block(pl.multiple_of(blk * ATT_W, ATT_W), k_ref[prev, :], v_ref[prev, :], band)
                return carry
            lax.fori_loop(1, nblk, step, 0)

    main = lambda w: pl.BlockSpec((None, None, lc, C), lambda r, i: (w, r, i, 0))
    halo = lambda w: pl.BlockSpec((None, None, ATT_W, C), lambda r, i: (w, r, jnp.maximum(i * nblk - 1, 0), 0))
    return pl.pallas_call(
        body, name=name, grid=(d, L // lc),
        in_specs=[main(0), main(1), halo(1), main(2), halo(2)],
        out_specs=[pl.BlockSpec((None, lc, C), lambda r, i: (r, i, 0)), pl.BlockSpec((None, lc, 128), lambda r, i: (r, i, 0))],
        out_shape=[jax.ShapeDtypeStruct((d, L, C), BF), jax.ShapeDtypeStruct((d, L, 128), F32)],
        compiler_params=_params("parallel", "arbitrary"),
    )(qkv, qkv, qkv, qkv, qkv)


def _attn_bwd(qkv, do, st, dst, nh, name):
    _, d, L, C = qkv.shape
    lc = _att_chunk(L)
    nblk = lc // ATT_W
    nchunk = L // lc

    def body(q_ref, qn_ref, k_ref, kh_ref, v_ref, vh_ref, do_ref, don_ref, st_ref, stn_ref, ds_ref, dsn_ref, o_ref):
        i = pl.program_id(1)
        qi = lax.broadcasted_iota(jnp.int32, (ATT_W, 2 * ATT_W), 0)
        kj = lax.broadcasted_iota(jnp.int32, (ATT_W, 2 * ATT_W), 1)
        band_q = jnp.logical_and(kj >= qi, kj <= qi + ATT_W)
        qa = lax.broadcasted_iota(jnp.int32, (2 * ATT_W, ATT_W), 0)
        kb = lax.broadcasted_iota(jnp.int32, (2 * ATT_W, ATT_W), 1)
        band_k = jnp.logical_and(qa >= kb, qa <= kb + ATT_W)

        def probs(qm, kp, lse, mask):
            s = _dot_nt(qm, kp)
            return jnp.where(mask, jnp.exp(s - lse), 0.0)

        def q_block(row0, kc, vc, mask):
            rows = pl.ds(row0, ATT_W)
            stv = st_ref[rows, :]
            dsv = ds_ref[rows, :]
            for hp in range(C // 128):
                sl = slice(hp * 128, (hp + 1) * 128)
                qp = q_ref[rows, sl]
                dop = do_ref[rows, sl]
                kp = kc[:, sl]
                vp = vc[:, sl]
                outs = []
                for h in range(2 * hp, min(2 * hp + 2, nh)):
                    hm = _head_mask(qp.shape, h)
                    p = probs(jnp.where(hm, qp, jnp.zeros_like(qp)), kp, _lane_col(stv, h), mask)
                    dp = _dot_nt(jnp.where(hm, dop, jnp.zeros_like(dop)), vp)
                    dsc = (p * (dp - _lane_col(dsv, h))).astype(BF)
                    outs.append(_dot(dsc, kp))
                if len(outs) == 2:
                    dq = jnp.where(_head_mask(outs[0].shape, 0), outs[0], outs[1])
                else:
                    dq = jnp.where(_head_mask(outs[0].shape, 0), outs[0], 0.0)
                o_ref[0, rows, sl] = dq

        def k_block(row0, qq, doo, stv, dsv, mask):
            rows = pl.ds(row0, ATT_W)
            for hp in range(C // 128):
                sl = slice(hp * 128, (hp + 1) * 128)
                qp = qq[:, sl]
                dop = doo[:, sl]
                kp = k_ref[rows, sl]
                vp = v_ref[rows, sl]
                dks, dvs = [], []
                for h in range(2 * hp, min(2 * hp + 2, nh)):
                    hm = _head_mask(qp.shape, h)
                    qm = jnp.where(hm, qp, jnp.zeros_like(qp))
                    dom = jnp.where(hm, dop, jnp.zeros_like(dop))
                    p = probs(qm, kp, _lane_col(stv, h), mask)
                    dp = _dot_nt(dom, vp)
                    dsc = (p * (dp - _lane_col(dsv, h))).astype(BF)
                    dks.append(_dot_tn(dsc, qm))
                    dvs.append(_dot_tn(p.astype(BF), dom))
                if len(dks) == 2:
                    o_ref[1, rows, sl] = dks[0] + dks[1]
                    o_ref[2, rows, sl] = dvs[0] + dvs[1]
                else:
                    o_ref[1, rows, sl] = dks[0]
                    o_ref[2, rows, sl] = dvs[0]

        first_mask = jnp.logical_and(band_q, jnp.logical_or(kj >= ATT_W, i > 0))
        q_block(0, jnp.concatenate([kh_ref[...], k_ref[pl.ds(0, ATT_W), :]], axis=0),
                jnp.concatenate([vh_ref[...], v_ref[pl.ds(0, ATT_W), :]], axis=0), first_mask)
        if nblk > 1:
            def q_step(blk, carry):
                prev = pl.ds(pl.multiple_of((blk - 1) * ATT_W, ATT_W), 2 * ATT_W)
                q_block(pl.multiple_of(blk * ATT_W, ATT_W), k_ref[prev, :], v_ref[prev, :], band_q)
                return carry
            lax.fori_loop(1, nblk, q_step, 0)

            def k_step(blk, carry):
                two = pl.ds(pl.multiple_of(blk * ATT_W, ATT_W), 2 * ATT_W)
                k_block(pl.multiple_of(blk * ATT_W, ATT_W), q_ref[two, :], do_ref[two, :], st_ref[two, :], ds_ref[two, :], band_k)
                return carry
            lax.fori_loop(0, nblk - 1, k_step, 0)

        last = pl.ds((nblk - 1) * ATT_W, ATT_W)
        last_mask = jnp.logical_and(band_k, jnp.logical_or(qa < ATT_W, i < nchunk - 1))
        k_block((nblk - 1) * ATT_W,
                jnp.concatenate([q_ref[last, :], qn_ref[...]], axis=0),
                jnp.concatenate([do_ref[last, :], don_ref[...]], axis=0),
                jnp.concatenate([st_ref[last, :], stn_ref[...]], axis=0),
                jnp.concatenate([ds_ref[last, :], dsn_ref[...]], axis=0), last_mask)

    nb_all = L // ATT_W
    main4 = lambda w: pl.BlockSpec((None, None, lc, C), lambda r, i: (w, r, i, 0))
    prev4 = lambda w: pl.BlockSpec((None, None, ATT_W, C), lambda r, i: (w, r, jnp.maximum(i * nblk - 1, 0), 0))
    next4 = lambda w: pl.BlockSpec((None, None, ATT_W, C), lambda r, i: (w, r, jnp.minimum((i + 1) * nblk, nb_all - 1), 0))
    main3 = lambda n: pl.BlockSpec((None, lc, n), lambda r, i: (r, i, 0))
    next3 = lambda n: pl.BlockSpec((None, ATT_W, n), lambda r, i: (r, jnp.minimum((i + 1) * nblk, nb_all - 1), 0))
    return pl.pallas_call(
        body, name=name, grid=(d, nchunk),
        in_specs=[main4(0), next4(0), main4(1), prev4(1), main4(2), prev4(2),
                  main3(C), next3(C), main3(128), next3(128), main3(128), next3(128)],
        out_specs=pl.BlockSpec((3, None, lc, C), lambda r, i: (0, r, i, 0)),
        out_shape=jax.ShapeDtypeStruct((3, d, L, C), F32),
        compiler_params=_params("parallel", "arbitrary"),
    )(qkv, qkv, qkv, qkv, qkv, qkv, do, do, st, st, dst, dst)


def _alpha_from(lse_nat):
    m = jnp.maximum(jnp.maximum(lse_nat[0], lse_nat[1]), lse_nat[2])
    e = [jnp.exp(l - m) for l in lse_nat]
    inv = 1.0 / (e[0] + e[1] + e[2])
    return [ei * inv for ei in e]


def _attn_out_fwd(x, os_, sts, wos, tm=512):
    T = x.shape[0]
    C = GROUP_LANES

    def body(x_ref, o0, o1, o2, s0, s1, s2, w0, w1, w2, xo_ref, m0, m1, m2, al_ref, oscr, sscr):
        o_refs, st_refs, w_refs, m_refs = (o0, o1, o2), (s0, s1, s2), (w0, w1, w2), (m0, m1, m2)
        lses = []
        for g, d in enumerate(DILATIONS):
            _gather_rows(sscr.at[g], st_refs[g], d)
            lses.append(_lane_col(sscr[g, 0], LSE_GROUP_LANE))
        alpha = _alpha_from(lses)
        y = x_ref[...]
        for g, d in enumerate(DILATIONS):
            _gather_rows(oscr, o_refs[g], d)
            mg = (_chunks_to_rows(oscr) * (3.0 * alpha[g])).astype(BF)
            m_refs[g][...] = mg
            y = y + _dot(mg, w_refs[g][...])
        xo_ref[...] = y
        lane = lax.broadcasted_iota(jnp.int32, (tm, 128), 1)
        al_ref[...] = jnp.where(lane == 0, alpha[0], jnp.where(lane == 1, alpha[1], jnp.where(lane == 2, alpha[2], 0.0)))

    o_specs = [pl.BlockSpec((d, tm // d, C), lambda i: (0, i, 0)) for d in DILATIONS]
    st_specs = [pl.BlockSpec((d, tm // d, 128), lambda i: (0, i, 0)) for d in DILATIONS]
    mshape = jax.ShapeDtypeStruct((T, C), BF)
    return pl.pallas_call(
        body, name="attn_out_fwd", grid=(T // tm,),
        in_specs=[_row_spec(tm, D)] + o_specs + st_specs + [_full_spec((C, D))] * 3,
        out_specs=[_row_spec(tm, D), _row_spec(tm, C), _row_spec(tm, C), _row_spec(tm, C), _row_spec(tm, 128)],
        out_shape=[jax.ShapeDtypeStruct((T, D), F32), mshape, mshape, mshape, jax.ShapeDtypeStruct((T, 128), F32)],
        scratch_shapes=[pltpu.VMEM((C // 128, tm, 128), F32), pltpu.VMEM((3, 1, tm, 128), F32)],
        compiler_params=_params("parallel"),
    )(x, *os_, *sts, *wos)


def _attn_out_bwd(dx, os_, sts, alpha, wos, tm=512):
    T = dx.shape[0]
    C = GROUP_LANES

    def body(dx_ref, o0, o1, o2, s0, s1, s2, al_ref, w0, w1, w2, do0, do1, do2, ds0, ds1, ds2, dmscr, oscr, sscr, tscr):
        o_refs, st_refs, w_refs = (o0, o1, o2), (s0, s1, s2), (w0, w1, w2)
        do_refs, ds_refs = (do0, do1, do2), (ds0, ds1, ds2)
        dyb = dx_ref[...].astype(BF)
        alv = al_ref[...]
        alpha_g = [_lane_col(alv, g) for g in range(3)]
        dalpha = []
        for g, d in enumerate(DILATIONS):
            dm = _dot_nt(dyb, w_refs[g][...])
            _rows_to_chunks(dmscr.at[g], dm)
            _gather_rows(oscr.at[g], o_refs[g], d)
            _gather_rows(sscr.at[g], st_refs[g], d)
            dalpha.append(3.0 * jnp.sum(dm * _chunks_to_rows(oscr.at[g]), axis=-1, keepdims=True))
        mean_da = alpha_g[0] * dalpha[0] + alpha_g[1] * dalpha[1] + alpha_g[2] * dalpha[2]
        lane = lax.broadcasted_iota(jnp.int32, (tm, 128), 1)
        head_lane = lax.broadcasted_iota(jnp.int32, (tm, C), 1) // HEAD_DIM
        for g, d in enumerate(DILATIONS):
            nh = HEAD_GROUPS[g]
            dlse_g = alpha_g[g] * (dalpha[g] - mean_da)
            do_nat = _chunks_to_rows(dmscr.at[g]) * (3.0 * alpha_g[g])
            prod = do_nat * _chunks_to_rows(oscr.at[g])
            stv = sscr[g, 0]
            lse_g = _lane_col(stv, LSE_GROUP_LANE)
            tile = jnp.zeros((tm, 128), F32)
            for h in range(nh):
                delta = jnp.sum(jnp.where(head_lane == h, prod, 0.0), axis=-1, keepdims=True)
                dlse = dlse_g * jnp.exp(_lane_col(stv, h) - lse_g) * (1.0 / nh)
                tile = jnp.where(lane == h, delta - dlse, tile)
            _rows_to_chunks(dmscr.at[g], do_nat)
            _scatter_rows(do_refs[g], dmscr.at[g], d, BF)
            tscr[0] = tile
            _scatter_rows(ds_refs[g], tscr, d, F32)

    o_specs = [pl.BlockSpec((d, tm // d, C), lambda i: (0, i, 0)) for d in DILATIONS]
    st_specs = [pl.BlockSpec((d, tm // d, 128), lambda i: (0, i, 0)) for d in DILATIONS]
    return pl.pallas_call(
        body, name="attn_out_bwd", grid=(T // tm,),
        in_specs=[_row_spec(tm, D)] + o_specs + st_specs + [_row_spec(tm, 128)] + [_full_spec((C, D))] * 3,
        out_specs=o_specs + st_specs,
        out_shape=[jax.ShapeDtypeStruct((d, T // d, C), BF) for d in DILATIONS]
        + [jax.ShapeDtypeStruct((d, T // d, 128), F32) for d in DILATIONS],
        scratch_shapes=[pltpu.VMEM((3, C // 128, tm, 128), F32), pltpu.VMEM((3, C // 128, tm, 128), F32),
                        pltpu.VMEM((3, 1, tm, 128), F32), pltpu.VMEM((1, tm, 128), F32)],
        compiler_params=_params("parallel"),
    )(dx, *os_, *sts, alpha, *wos)


def _qkv_bwd(dqkvs, cos, sin, x2, g, w3s, dx3, tm=256):
    T = x2.shape[0]
    C = GROUP_LANES

    def body(dq0, dq1, dq2, cos_ref, sin_ref, x_ref, g_ref, w0, w1, w2, dx3_ref,
             dx_ref, n0, n1, n2, h_ref, dg_ref, scr, dh_ref):
        dq_refs, w_refs, n_refs = (dq0, dq1, dq2), (w0, w1, w2), (n0, n1, n2)

        @pl.when(pl.program_id(0) == 0)
        def _():
            dg_ref[...] = jnp.zeros_like(dg_ref)

        cv = cos_ref[...]
        sv = sin_ref[...]
        dh_ref[...] = jnp.zeros_like(dh_ref)
        for gi, d in enumerate(DILATIONS):
            for w in range(3):
                _gather_rows(scr, dq_refs[gi].at[w], d)
                if w < 2:
                    scale = HEAD_DIM ** -0.5 if w == 0 else 1.0
                    for c in range(C // 128):
                        t = scr[c]
                        scr[c] = (t * cv - _rot_half(t) * sv) * scale
                tb = _chunks_to_rows(scr).astype(BF)
                n_refs[gi][w] = tb
                dh_ref[...] += _dot_nt(tb, w_refs[gi][w])
        xv = x_ref[...]
        h_ref[...] = _rms_fwd(xv, g_ref[...]).astype(BF)
        dx, dg = _rms_bwd(xv, g_ref[...], dh_ref[...])
        dx_ref[...] = dx3_ref[...] + dx
        dg_ref[...] += dg

    dq_specs = [pl.BlockSpec((3, d, tm // d, C), lambda i: (0, 0, i, 0)) for d in DILATIONS]
    nat = pl.BlockSpec((3, tm, C), lambda i: (0, i, 0))
    nshape = jax.ShapeDtypeStruct((3, T, C), BF)
    return pl.pallas_call(
        body, name="qkv_bwd", grid=(T // tm,),
        in_specs=dq_specs + [_row_spec(tm, 128), _row_spec(tm, 128), _row_spec(tm, D), _full_spec((1, D))]
        + [_full_spec((3, D, C))] * 3 + [_row_spec(tm, D)],
        out_specs=[_row_spec(tm, D), nat, nat, nat, _row_spec(tm, D), _full_spec((1, D))],
        out_shape=[jax.ShapeDtypeStruct((T, D), F32), nshape, nshape, nshape, jax.ShapeDtypeStruct((T, D), BF),
                   jax.ShapeDtypeStruct((1, D), F32)],
        scratch_shapes=[pltpu.VMEM((C // 128, tm, 128), F32), pltpu.VMEM((tm, D), F32)],
        compiler_params=_params("arbitrary"),
    )(*dqkvs, cos, sin, x2, g, *w3s, dx3)


def _final_bwd(x, target, g, tm=512):
    T = x.shape[0]

    def body(x_ref, t_ref, g_ref, dx_ref, loss_ref, dg_ref):
        @pl.when(pl.program_id(0) == 0)
        def _():
            loss_ref[...] = jnp.zeros_like(loss_ref)
            dg_ref[...] = jnp.zeros_like(dg_ref)

        xv = x_ref[...]
        gv = g_ref[...]
        diff = _rms_fwd(xv, gv) - t_ref[...]
        loss_ref[...] += 0.5 * jnp.sum(jnp.mean(diff * diff, axis=-1, keepdims=True), axis=0, keepdims=True)
        dx, dg = _rms_bwd(xv, gv, diff * (1.0 / D))
        dx_ref[...] = dx
        dg_ref[...] += dg

    return pl.pallas_call(
        body, name="final_bwd", grid=(T // tm,),
        in_specs=[_row_spec(tm, D), _row_spec(tm, D), _full_spec((1, D))],
        out_specs=[_row_spec(tm, D), _full_spec((1, 1)), _full_spec((1, D))],
        out_shape=[jax.ShapeDtypeStruct((T, D), F32), jax.ShapeDtypeStruct((1, 1), F32), jax.ShapeDtypeStruct((1, D), F32)],
        compiler_params=_params("arbitrary"),
    )(x, target, g)


def _place():
    x, y, c = lax.axis_index("x"), lax.axis_index("y"), lax.axis_index("c")
    return x, y, c


def _allgather(arrs, name):
    n = len(arrs)

    def body(*refs):
        ins, outs = refs[:n], refs[n:2 * n]
        send_sems, recv_sems, local_sems = refs[2 * n:]
        x, y, c = _place()
        me, sibling = (x, y, c), (x, y, 1 - c)
        chips = [(1 - x, y), (x, 1 - y), (1 - x, 1 - y)]

        def slot(a, px, py, pc):
            return outs[a].at[4 * px + 2 * py + pc]

        def copy(a, k, block, to, src=None):
            return pltpu.make_async_remote_copy(
                src_ref=slot(a, *block) if src is None else src, dst_ref=slot(a, *block),
                send_sem=send_sems.at[a, k], recv_sem=recv_sems.at[a, k], device_id=to, device_id_type=MESH)

        mine = [pltpu.make_async_copy(ins[a], slot(a, *me), local_sems.at[a]) for a in range(n)]
        for cp in mine:
            cp.start()
        first = []
        for a in range(n):
            first.append(copy(a, 0, me, sibling, src=ins[a]))
            first += [copy(a, 1 + j, me, (*chip, c), src=ins[a]) for j, chip in enumerate(chips)]
        for cp in first:
            cp.start()
        passed = []
        for a in range(n):
            for j, chip in enumerate(chips):
                copy(a, 1 + j, (*chip, c), me).wait_recv()
                fwd = copy(a, 4 + j, (*chip, c), sibling)
                fwd.start()
                passed.append(fwd)
        for a in range(n):
            copy(a, 0, sibling, me).wait_recv()
            for j, chip in enumerate(chips):
                copy(a, 4 + j, (*chip, 1 - c), me).wait_recv()
        for cp in first + passed:
            cp.wait_send()
        for cp in mine:
            cp.wait()

    hbm = pl.BlockSpec(memory_space=pl.ANY)
    return pl.pallas_call(
        body, name=name,
        in_specs=[hbm] * n, out_specs=[hbm] * n,
        out_shape=[jax.ShapeDtypeStruct((N_DEV,) + a.shape, a.dtype) for a in arrs],
        scratch_shapes=[pltpu.SemaphoreType.DMA((n, 7)), pltpu.SemaphoreType.DMA((n, 7)), pltpu.SemaphoreType.DMA((n,))],
    )(*arrs)


def _exchange(arrs, name):
    n = len(arrs)

    def body(*refs):
        ins, outs = refs[:n], refs[n:2 * n]
        send_sems, recv_sems, local_sems = refs[2 * n:]
        x, y, c = _place()
        me = 4 * x + 2 * y + c

        def peer(k):
            px = 1 - x if k & 4 else x
            py = 1 - y if k & 2 else y
            pc = 1 - c if k & 1 else c
            return (px, py, pc), 4 * px + 2 * py + pc

        mine = [pltpu.make_async_copy(ins[a].at[me], outs[a].at[me], local_sems.at[a]) for a in range(n)]
        for cp in mine:
            cp.start()
        sends = []
        for a in range(n):
            for k in range(1, N_DEV):
                to, to_idx = peer(k)
                cp = pltpu.make_async_remote_copy(
                    src_ref=ins[a].at[to_idx], dst_ref=outs[a].at[me],
                    send_sem=send_sems.at[a, k - 1], recv_sem=recv_sems.at[a, k - 1], device_id=to, device_id_type=MESH)
                cp.start()
                sends.append(cp)
        for a in range(n):
            for k in range(1, N_DEV):
                frm, frm_idx = peer(k)
                pltpu.make_async_remote_copy(
                    src_ref=ins[a].at[frm_idx], dst_ref=outs[a].at[frm_idx],
                    send_sem=send_sems.at[a, k - 1], recv_sem=recv_sems.at[a, k - 1], device_id=frm, device_id_type=MESH).wait_recv()
        for cp in sends:
            cp.wait_send()
        for cp in mine:
            cp.wait()

    hbm = pl.BlockSpec(memory_space=pl.ANY)
    return pl.pallas_call(
        body, name=name,
        in_specs=[hbm] * n, out_specs=[hbm] * n,
        out_shape=[jax.ShapeDtypeStruct(a.shape, a.dtype) for a in arrs],
        scratch_shapes=[pltpu.SemaphoreType.DMA((n, 7)), pltpu.SemaphoreType.DMA((n, 7)), pltpu.SemaphoreType.DMA((n,))],
    )(*arrs)


def _row_tile(rows):
    for t in (256, 176, 128, 8):
        if rows % t == 0:
            return t
    return rows


def _sum_parts(parts, name):
    K, R, C = parts.shape
    tr = _row_tile(R)

    def body(p_ref, o_ref):
        g = p_ref[0]
        for k in range(1, K):
            g = g + p_ref[k]
        o_ref[...] = g

    return pl.pallas_call(
        body, name=name, grid=(R // tr,),
        in_specs=[pl.BlockSpec((K, tr, C), lambda i: (0, i, 0))],
        out_specs=_row_spec(tr, C),
        out_shape=jax.ShapeDtypeStruct((R, C), F32),
        compiler_params=_params("parallel"),
    )(parts)


def _adamw(parts, w, m, v, name):
    K, R, C = parts.shape
    tr = _row_tile(R)

    def body(p_ref, w_ref, m_ref, v_ref, g_ref, d_ref, nm_ref, nv_ref):
        g = p_ref[0]
        for k in range(1, K):
            g = g + p_ref[k]
        nm = ADAM_B1 * m_ref[...] + (1.0 - ADAM_B1) * g
        nv = ADAM_B2 * v_ref[...] + (1.0 - ADAM_B2) * jnp.square(g)
        m_hat = nm / (1.0 - ADAM_B1 ** ADAM_STEP)
        v_hat = nv / (1.0 - ADAM_B2 ** ADAM_STEP)
        g_ref[...] = g
        d_ref[...] = -ADAM_LR * (m_hat / (jnp.sqrt(v_hat) + ADAM_EPS) + ADAM_WD * w_ref[...])
        nm_ref[...] = nm
        nv_ref[...] = nv

    blk = _row_spec(tr, C)
    shp = jax.ShapeDtypeStruct((R, C), F32)
    return pl.pallas_call(
        body, name=name, grid=(R // tr,),
        in_specs=[pl.BlockSpec((K, tr, C), lambda i: (0, i, 0)), blk, blk, blk],
        out_specs=[blk] * 4,
        out_shape=[shp] * 4,
        compiler_params=_params("parallel"),
    )(parts, w, m, v)


def _rope_tables(T):
    inv_freq = 1.0 / (ROPE_THETA ** (jnp.arange(0, HEAD_DIM, 2, dtype=F32) / HEAD_DIM))
    ang = jnp.arange(T, dtype=F32)[:, None] * inv_freq[None, :]
    ang = jnp.concatenate([ang, ang, ang, ang], axis=-1)
    return jnp.cos(ang), jnp.sin(ang)


def _pad_lanes(a, n):
    return jnp.pad(a, ((0, 0),) * (a.ndim - 1) + ((0, n - a.shape[-1]),))


def _layout_weights(gw):
    w = {}
    w["pool_in"] = gw["pool_in"].reshape(D, D)
    w["pool_grp"] = jnp.transpose(gw["pool_grp"], (1, 0, 2, 3)).reshape(4, POOL_GC, POOL_GC)
    w["pool_out"] = gw["pool_out"].reshape(D, D)
    wqkv = jnp.transpose(gw["qkv"], (1, 0, 2)).reshape(D, 3 * D)
    wo = gw["attn_out"].reshape(D, D)
    w["qkv"], w["attn_out"] = [], []
    for nh, off in zip(HEAD_GROUPS, HEAD_OFFS):
        lo, n = off * HEAD_DIM, nh * HEAD_DIM
        w["qkv"].append(jnp.stack([_pad_lanes(wqkv[:, k * D + lo:k * D + lo + n], GROUP_LANES) for k in range(3)]))
        w["attn_out"].append(jnp.pad(wo[lo:lo + n], ((0, GROUP_LANES - n), (0, 0))))
    for l in range(2):
        for nm in ("gate", "up", "down"):
            w[f"{nm}{l}"] = gw[f"{nm}{l}"].reshape(F, D)
    return w


def _local_step(x, target, w, norm_mix, norm_ffn, norm_final, pool_scale):
    T = x.shape[0]
    cos, sin = _rope_tables(T)
    nm = [norm_mix[i:i + 1] for i in range(2)]
    nf = [norm_ffn[i:i + 1] for i in range(2)]
    nfin = norm_final.reshape(1, D)

    h0 = _norm_fwd(x, nm[0], "norm_mix0")
    x1, p = _pool_fwd(x, h0, w["pool_in"], w["pool_grp"], pool_scale, w["pool_out"])
    h1 = _norm_fwd(x1, nf[0], "norm_ffn0")
    x2, a0, b0, s0 = _ffn_fwd(x1, h1, w["gate0"], w["up0"], w["down0"], "ffn_fwd0")
    h2 = _norm_fwd(x2, nm[1], "norm_mix1")
    qkvs = [_qkv_fwd(h2, w["qkv"][g], cos, sin, DILATIONS[g], f"qkv_fwd{g}") for g in range(3)]
    att = [_attn_fwd(qkvs[g], HEAD_GROUPS[g], f"attn_fwd{g}") for g in range(3)]
    os_, sts = [a[0] for a in att], [a[1] for a in att]
    x3, m0, m1, m2, alpha = _attn_out_fwd(x2, os_, sts, w["attn_out"])
    h3 = _norm_fwd(x3, nf[1], "norm_ffn1")
    x4, a1, b1, s1 = _ffn_fwd(x3, h3, w["gate1"], w["up1"], w["down1"], "ffn_fwd1")

    g = {}
    dx4, loss, dg_final = _final_bwd(x4, target, nfin)
    dx3, da1, db1, _, dg_ffn1 = _ffn_bwd(dx4, x3, nf[1], a1, b1, w["gate1"], w["up1"], w["down1"], "ffn_bwd1")
    g["down1"], g["gate1"], g["up1"] = _ffn_wgrad(s1, da1, db1, dx4, h3, "ffn_wgrad1")
    dos_and_stats = _attn_out_bwd(dx3, os_, sts, alpha, w["attn_out"])
    dos, dsts = dos_and_stats[:3], dos_and_stats[3:]
    g["attn_out"] = [_wgrad(m, dx3, f"attn_out_wgrad{i}") for i, m in enumerate((m0, m1, m2))]
    dqkvs = [_attn_bwd(qkvs[gi], dos[gi], sts[gi], dsts[gi], HEAD_GROUPS[gi], f"attn_bwd{gi}") for gi in range(3)]
    dx2, n0, n1, n2, h2b, dg_mix1 = _qkv_bwd(dqkvs, cos, sin, x2, nm[1], w["qkv"], dx3)
    g["qkv"] = [_wgrad_stack(h2b, n, f"qkv_wgrad{i}") for i, n in enumerate((n0, n1, n2))]
    dx1, da0, db0, _, dg_ffn0 = _ffn_bwd(dx2, x1, nf[0], a0, b0, w["gate0"], w["up0"], w["down0"], "ffn_bwd0")
    g["down0"], g["gate0"], g["up0"] = _ffn_wgrad(s0, da0, db0, dx2, h1, "ffn_wgrad0")
    dx0, z, dzp, du, h0b, dscale, dg_mix0 = _pool_bwd(dx1, x, nm[0], p, w["pool_in"], w["pool_grp"], pool_scale, w["pool_out"])
    g["pool_out"] = _wgrad(z, dx1, "pool_out_wgrad")
    g["pool_in"] = _wgrad(h0b, du, "pool_in_wgrad")
    g["pool_grp"] = _wgrad_pool_groups(p, dzp)

    small = jnp.concatenate([dg_mix0, dg_mix1, dg_ffn0, dg_ffn1, dg_final, dscale,
                             jnp.broadcast_to(loss, (1, D)), jnp.zeros((1, D), F32)], axis=0)
    return dx0, g, small


def _grad_blocks(g):
    blocks = {}
    blocks["pool_in"] = g["pool_in"].reshape(N_DEV, D // N_DEV, D)
    blocks["pool_out"] = g["pool_out"].reshape(N_DEV, D // N_DEV, D)
    blocks["pool_grp"] = jnp.transpose(g["pool_grp"].reshape(4, N_DEV, POOL_GC // N_DEV, POOL_GC), (1, 0, 2, 3)).reshape(N_DEV, 4 * POOL_GC // N_DEV, POOL_GC)
    wo = jnp.concatenate([gw[:nh * HEAD_DIM] for gw, nh in zip(g["attn_out"], HEAD_GROUPS)], axis=0)
    blocks["attn_out"] = wo.reshape(N_DEV, D // N_DEV, D)
    wqkv = jnp.concatenate([g["qkv"][gi][k][:, :HEAD_GROUPS[gi] * HEAD_DIM] for k in range(3) for gi in range(3)], axis=1)
    blocks["qkv"] = jnp.transpose(wqkv.reshape(D, N_DEV, 3 * D // N_DEV), (1, 0, 2))
    for l in range(2):
        for nm in ("gate", "up", "down"):
            blocks[f"{nm}{l}"] = g[f"{nm}{l}"].reshape(N_DEV, F // N_DEV, D)
    return blocks


def kernel(x, norm_mix, norm_ffn, norm_final, pool_w_in, pool_w_group, pool_scale, pool_w_out, attn_w_qkv, attn_w_out, ffn_w_gate, ffn_w_up, ffn_w_down, loss_target, m_norm_mix, m_norm_ffn, m_norm_final, m_pool_w_in, m_pool_w_group, m_pool_scale, m_pool_w_out, m_attn_w_qkv, m_attn_w_out, m_ffn_w_gate, m_ffn_w_up, m_ffn_w_down, v_norm_mix, v_norm_ffn, v_norm_final, v_pool_w_in, v_pool_w_group, v_pool_scale, v_pool_w_out, v_attn_w_qkv, v_attn_w_out, v_ffn_w_gate, v_ffn_w_up, v_ffn_w_down):
    shard = {
        "pool_in": pool_w_in[0], "pool_grp": pool_w_group[0], "pool_out": pool_w_out[0],
        "qkv": attn_w_qkv[0], "attn_out": attn_w_out[0],
    }
    for l in range(2):
        shard[f"gate{l}"] = ffn_w_gate[l].T
        shard[f"up{l}"] = ffn_w_up[l].T
        shard[f"down{l}"] = ffn_w_down[l]
    names = list(shard)
    gathered = _allgather([shard[k].astype(BF) for k in names], "weights_allgather")
    w = _layout_weights(dict(zip(names, gathered)))

    grad_x, g, small = _local_step(x[0], loss_target[0], w, norm_mix, norm_ffn, norm_final, pool_scale)

    blocks = _grad_blocks(g)
    received = dict(zip(names, _exchange([blocks[k] for k in names], "grads_exchange")))
    small_all = _allgather([small], "small_allgather")[0]

    def upd(parts, wt, mt, vt, name):
        shape = wt.shape
        r2 = lambda t: t.reshape(parts.shape[1:])
        outs = _adamw(parts, r2(wt), r2(mt), r2(vt), name)
        return [o.reshape(shape) for o in outs]

    res = {}
    res["pool_w_in"] = upd(received["pool_in"], pool_w_in, m_pool_w_in, v_pool_w_in, "adamw_pool_in")
    res["pool_w_group"] = upd(received["pool_grp"], pool_w_group, m_pool_w_group, v_pool_w_group, "adamw_pool_grp")
    res["pool_w_out"] = upd(received["pool_out"], pool_w_out, m_pool_w_out, v_pool_w_out, "adamw_pool_out")
    res["attn_w_qkv"] = upd(received["qkv"], attn_w_qkv, m_attn_w_qkv, v_attn_w_qkv, "adamw_qkv")
    res["attn_w_out"] = upd(received["attn_out"], attn_w_out, m_attn_w_out, v_attn_w_out, "adamw_attn_out")
    for nm, wt, mt, vt in (("gate", ffn_w_gate, m_ffn_w_gate, v_ffn_w_gate), ("up", ffn_w_up, m_ffn_w_up, v_ffn_w_up)):
        per_layer = []
        for l in range(2):
            gt = _sum_parts(received[f"{nm}{l}"], f"sum_{nm}{l}").T
            per_layer.append(_adamw(gt[None], wt[l], mt[l], vt[l], f"adamw_{nm}{l}"))
        res[f"ffn_w_{nm}"] = [jnp.stack([per_layer[0][i], per_layer[1][i]]) for i in range(4)]
    per_layer = [_adamw(received[f"down{l}"], ffn_w_down[l], m_ffn_w_down[l], v_ffn_w_down[l], f"adamw_down{l}") for l in range(2)]
    res["ffn_w_down"] = [jnp.stack([per_layer[0][i], per_layer[1][i]]) for i in range(4)]

    small_w = jnp.concatenate([norm_mix, norm_ffn, norm_final[None], pool_scale, jnp.zeros((2, D), F32)], axis=0)
    small_m = jnp.concatenate([m_norm_mix, m_norm_ffn, m_norm_final[None], m_pool_scale, jnp.zeros((2, D), F32)], axis=0)
    small_v = jnp.concatenate([v_norm_mix, v_norm_ffn, v_norm_final[None], v_pool_scale, jnp.ones((2, D), F32)], axis=0)
    sg, sd, sm, sv = _adamw(small_all, small_w, small_m, small_v, "adamw_small")
    loss = sg[6, 0]
    res["norm_mix"] = [t[0:2] for t in (sg, sd, sm, sv)]
    res["norm_ffn"] = [t[2:4] for t in (sg, sd, sm, sv)]
    res["norm_final"] = [t[4] for t in (sg, sd, sm, sv)]
    res["pool_scale"] = [t[5:6] for t in (sg, sd, sm, sv)]

    order = ["norm_mix", "norm_ffn", "norm_final", "pool_w_in", "pool_w_group", "pool_scale", "pool_w_out",
             "attn_w_qkv", "attn_w_out", "ffn_w_gate", "ffn_w_up", "ffn_w_down"]
    return (loss, grad_x[None], *[res[k][0] for k in order], *[res[k][1] for k in order],
            *[res[k][2] for k in order], *[res[k][3] for k in order])
```

```python
import math

import jax
import jax.numpy as jnp
from jax import lax
from jax.experimental import pallas as pl
from jax.experimental.pallas import tpu as pltpu

D = 1024
F = 2816
N_DEV = 8
EPS = 1e-6
NEG_INF = -1e30
POOL_WINDOWS = (2, 4, 8, 16)
POOL_HALO = 16
POOL_GC = 256
HEAD_DIM = 64
HEAD_GROUPS = (6, 5, 5)
HEAD_OFFS = (0, 6, 11)
DILATIONS = (1, 4, 16)
ATT_W = 128
GROUP_LANES = 384
LSE_GROUP_LANE = 8
ROPE_THETA = 10000.0
ADAM_LR, ADAM_B1, ADAM_B2, ADAM_EPS, ADAM_WD, ADAM_STEP = 0.001, 0.9, 0.999, 1e-08, 0.01, 10

BF = jnp.bfloat16
F32 = jnp.float32
VMEM_LIMIT = 56 * 1024 * 1024
MESH = pl.DeviceIdType.MESH


def _params(*sem):
    return pltpu.CompilerParams(dimension_semantics=sem, vmem_limit_bytes=VMEM_LIMIT)


def _dot(a, b):
    return jnp.dot(a, b, preferred_element_type=F32)


def _dot_nt(a, b):
    return lax.dot_general(a, b, (((1,), (1,)), ((), ())), preferred_element_type=F32)


def _dot_tn(a, b):
    return lax.dot_general(a, b, (((0,), (0,)), ((), ())), preferred_element_type=F32)


def _rms_fwd(xv, g):
    r = lax.rsqrt(jnp.mean(xv * xv, axis=-1, keepdims=True) + EPS)
    return (xv * r) * g


def _rms_bwd(xv, g, dh):
    r = lax.rsqrt(jnp.mean(xv * xv, axis=-1, keepdims=True) + EPS)
    xhat = xv * r
    dg = jnp.sum(dh * xhat, axis=0, keepdims=True)
    dxh = dh * g
    dx = r * (dxh - xhat * jnp.mean(dxh * xhat, axis=-1, keepdims=True))
    return dx, dg


def _lane_col(tile, j):
    lane = lax.broadcasted_iota(jnp.int32, tile.shape, 1)
    return jnp.sum(jnp.where(lane == j, tile, 0.0), axis=-1, keepdims=True)


def _row_spec(tm, n):
    return pl.BlockSpec((tm, n), lambda i: (i, 0))


def _full_spec(shape):
    nd = len(shape)
    return pl.BlockSpec(shape, lambda *_: (0,) * nd)


def _norm_fwd(x, g, name, tm=512):
    T = x.shape[0]

    def body(x_ref, g_ref, h_ref):
        h_ref[...] = _rms_fwd(x_ref[...], g_ref[...]).astype(BF)

    return pl.pallas_call(
        body, name=name, grid=(T // tm,),
        in_specs=[_row_spec(tm, D), _full_spec((1, D))],
        out_specs=_row_spec(tm, D),
        out_shape=jax.ShapeDtypeStruct((T, D), BF),
        compiler_params=_params("parallel"),
    )(x, g)


def _pool_fwd(x, h, w_in, w_grp, scale, w_out, tm=512):
    T = x.shape[0]
    n = tm + POOL_HALO

    def body(x_ref, h_ref, win_ref, wg_ref, sc_ref, wout_ref, x1_ref, p_ref, tail_ref, z_ref):
        i = pl.program_id(0)

        @pl.when(i == 0)
        def _():
            tail_ref[...] = jnp.zeros_like(tail_ref)

        u = _dot(h_ref[...], win_ref[...])
        pos = i * tm + lax.broadcasted_iota(jnp.int32, (tm, 1), 0)
        for g, w in enumerate(POOL_WINDOWS):
            sl = slice(g * POOL_GC, (g + 1) * POOL_GC)
            ug = u[:, sl]
            s = jnp.concatenate([tail_ref[:, sl], ug], axis=0)
            step = 1
            while step < w:
                s = s + pltpu.roll(s, step, 0)
                step *= 2
            cnt = jnp.minimum(pos + 1, w).astype(F32)
            pg = (s[POOL_HALO:, :] / cnt - ug).astype(BF)
            p_ref[:, sl] = pg
            z_ref[:, sl] = (_dot(pg, wg_ref[g]) * sc_ref[:, sl]).astype(BF)
        tail_ref[...] = u[tm - POOL_HALO:, :]
        x1_ref[...] = x_ref[...] + _dot(z_ref[...], wout_ref[...])

    return pl.pallas_call(
        body, name="pool_fwd", grid=(T // tm,),
        in_specs=[_row_spec(tm, D), _row_spec(tm, D), _full_spec((D, D)), _full_spec((4, POOL_GC, POOL_GC)),
                  _full_spec((1, D)), _full_spec((D, D))],
        out_specs=[_row_spec(tm, D), _row_spec(tm, D)],
        out_shape=[jax.ShapeDtypeStruct((T, D), F32), jax.ShapeDtypeStruct((T, D), BF)],
        scratch_shapes=[pltpu.VMEM((POOL_HALO, D), F32), pltpu.VMEM((tm, D), BF)],
        compiler_params=_params("arbitrary"),
    )(x, h, w_in, w_grp, scale, w_out)


def _pool_bwd(dx1, x0, g0, p, w_in, w_grp, scale, w_out, tm=512):
    T = x0.shape[0]
    nt = T // tm
    n = tm + POOL_HALO
    rev = lambda i: (nt - 1 - i, 0)

    def body(dx1_ref, x0_ref, g_ref, p_ref, win_ref, wg_ref, sc_ref, wout_ref,
             dx0_ref, z_ref, dzp_ref, du_ref, h0_ref, dsc_ref, dg_ref, head_ref):
        i = pl.program_id(0)

        @pl.when(i == 0)
        def _():
            head_ref[...] = jnp.zeros_like(head_ref)
            dsc_ref[...] = jnp.zeros_like(dsc_ref)
            dg_ref[...] = jnp.zeros_like(dg_ref)

        dx1v = dx1_ref[...]
        dz = _dot_nt(dx1v.astype(BF), wout_ref[...])
        pos = (nt - 1 - i) * tm + lax.broadcasted_iota(jnp.int32, (tm, 1), 0)
        for g, w in enumerate(POOL_WINDOWS):
            sl = slice(g * POOL_GC, (g + 1) * POOL_GC)
            zpre = _dot(p_ref[:, sl], wg_ref[g])
            dzg = dz[:, sl]
            dsc_ref[:, sl] += jnp.sum(dzg * zpre, axis=0, keepdims=True)
            z_ref[:, sl] = (zpre * sc_ref[:, sl]).astype(BF)
            dzp = (dzg * sc_ref[:, sl]).astype(BF)
            dzp_ref[:, sl] = dzp
            dp = _dot_nt(dzp, wg_ref[g])
            cnt = jnp.minimum(pos + 1, w).astype(F32)
            dpc = dp / cnt
            s = jnp.concatenate([dpc, head_ref[:, sl]], axis=0)
            step = 1
            while step < w:
                s = s + pltpu.roll(s, n - step, 0)
                step *= 2
            head_ref[:, sl] = dpc[:POOL_HALO, :]
            du_ref[:, sl] = (s[:tm, :] - dp).astype(BF)
        dh0 = _dot_nt(du_ref[...], win_ref[...])
        x0v = x0_ref[...]
        h0_ref[...] = _rms_fwd(x0v, g_ref[...]).astype(BF)
        dx, dg = _rms_bwd(x0v, g_ref[...], dh0)
        dx0_ref[...] = dx1v + dx
        dg_ref[...] += dg

    bf_rows = jax.ShapeDtypeStruct((T, D), BF)
    vec = jax.ShapeDtypeStruct((1, D), F32)
    return pl.pallas_call(
        body, name="pool_bwd", grid=(nt,),
        in_specs=[pl.BlockSpec((tm, D), rev), pl.BlockSpec((tm, D), rev), _full_spec((1, D)), pl.BlockSpec((tm, D), rev),
                  _full_spec((D, D)), _full_spec((4, POOL_GC, POOL_GC)), _full_spec((1, D)), _full_spec((D, D))],
        out_specs=[pl.BlockSpec((tm, D), rev)] * 5 + [_full_spec((1, D))] * 2,
        out_shape=[jax.ShapeDtypeStruct((T, D), F32), bf_rows, bf_rows, bf_rows, bf_rows, vec, vec],
        scratch_shapes=[pltpu.VMEM((POOL_HALO, D), F32)],
        compiler_params=_params("arbitrary"),
    )(dx1, x0, g0, p, w_in, w_grp, scale, w_out)


def _ffn_fwd(x, h, wg_t, wu_t, wd, name, tm=512, fk=1408):
    T = x.shape[0]

    def body(x_ref, h_ref, wg_ref, wu_ref, wd_ref, xo_ref, a_ref, b_ref, s_ref, acc_ref):
        k = pl.program_id(1)

        @pl.when(k == 0)
        def _():
            acc_ref[...] = jnp.zeros_like(acc_ref)

        hv = h_ref[...]
        a = _dot_nt(hv, wg_ref[...])
        b = _dot_nt(hv, wu_ref[...])
        s = ((a * jax.nn.sigmoid(a)) * b).astype(BF)
        a_ref[...] = a.astype(BF)
        b_ref[...] = b.astype(BF)
        s_ref[...] = s
        acc_ref[...] += _dot(s, wd_ref[...])

        @pl.when(k == pl.num_programs(1) - 1)
        def _():
            xo_ref[...] = x_ref[...] + acc_ref[...]

    row = pl.BlockSpec((tm, D), lambda i, k: (i, 0))
    wsp = pl.BlockSpec((fk, D), lambda i, k: (k, 0))
    act = pl.BlockSpec((tm, fk), lambda i, k: (i, k))
    act_shape = jax.ShapeDtypeStruct((T, F), BF)
    return pl.pallas_call(
        body, name=name, grid=(T // tm, F // fk),
        in_specs=[row, row, wsp, wsp, wsp],
        out_specs=[row, act, act, act],
        out_shape=[jax.ShapeDtypeStruct((T, D), F32), act_shape, act_shape, act_shape],
        scratch_shapes=[pltpu.VMEM((tm, D), F32)],
        compiler_params=_params("parallel", "arbitrary"),
    )(x, h, wg_t, wu_t, wd)


def _ffn_bwd(dxo, x_in, g, a, b, wg_t, wu_t, wd, name, tm=512, fk=1408):
    T = x_in.shape[0]

    def body(dxo_ref, x_ref, g_ref, a_ref, b_ref, wg_ref, wu_ref, wd_ref,
             dx_ref, da_ref, db_ref, h_ref, dg_ref, dy_ref, dh_ref):
        i = pl.program_id(0)
        k = pl.program_id(1)

        @pl.when(jnp.logical_and(i == 0, k == 0))
        def _():
            dg_ref[...] = jnp.zeros_like(dg_ref)

        @pl.when(k == 0)
        def _():
            h_ref[...] = _rms_fwd(x_ref[...], g_ref[...]).astype(BF)
            dy_ref[...] = dxo_ref[...].astype(BF)
            dh_ref[...] = jnp.zeros_like(dh_ref)

        ds = _dot_nt(dy_ref[...], wd_ref[...])
        av = a_ref[...].astype(F32)
        bv = b_ref[...].astype(F32)
        sig = jax.nn.sigmoid(av)
        db = (ds * (av * sig)).astype(BF)
        da = (ds * bv * (sig * (1.0 + av * (1.0 - sig)))).astype(BF)
        da_ref[...] = da
        db_ref[...] = db
        dh_ref[...] += _dot(da, wg_ref[...]) + _dot(db, wu_ref[...])

        @pl.when(k == pl.num_programs(1) - 1)
        def _():
            dx, dg = _rms_bwd(x_ref[...], g_ref[...], dh_ref[...])
            dx_ref[...] = dxo_ref[...] + dx
            dg_ref[...] += dg

    row = pl.BlockSpec((tm, D), lambda i, k: (i, 0))
    wsp = pl.BlockSpec((fk, D), lambda i, k: (k, 0))
    act = pl.BlockSpec((tm, fk), lambda i, k: (i, k))
    vec = pl.BlockSpec((1, D), lambda i, k: (0, 0))
    act_shape = jax.ShapeDtypeStruct((T, F), BF)
    return pl.pallas_call(
        body, name=name, grid=(T // tm, F // fk),
        in_specs=[row, row, vec, act, act, wsp, wsp, wsp],
        out_specs=[row, act, act, row, vec],
        out_shape=[jax.ShapeDtypeStruct((T, D), F32), act_shape, act_shape, jax.ShapeDtypeStruct((T, D), BF),
                   jax.ShapeDtypeStruct((1, D), F32)],
        scratch_shapes=[pltpu.VMEM((tm, D), BF), pltpu.VMEM((tm, D), F32)],
        compiler_params=_params("arbitrary", "arbitrary"),
    )(dxo, x_in, g, a, b, wg_t, wu_t, wd)


def _ffn_wgrad(s, da, db, dxo, h, name, tk=256, fm=1408):
    T = h.shape[0]

    def body(s_ref, da_ref, db_ref, dxo_ref, h_ref, gd_ref, gg_ref, gu_ref, ad_ref, ag_ref, au_ref):
        t = pl.program_id(1)

        @pl.when(t == 0)
        def _():
            ad_ref[...] = jnp.zeros_like(ad_ref)
            ag_ref[...] = jnp.zeros_like(ag_ref)
            au_ref[...] = jnp.zeros_like(au_ref)

        hv = h_ref[...]
        ad_ref[...] += _dot_tn(s_ref[...], dxo_ref[...].astype(BF))
        ag_ref[...] += _dot_tn(da_ref[...], hv)
        au_ref[...] += _dot_tn(db_ref[...], hv)

        @pl.when(t == pl.num_programs(1) - 1)
        def _():
            gd_ref[...] = ad_ref[...].astype(BF)
            gg_ref[...] = ag_ref[...].astype(BF)
            gu_ref[...] = au_ref[...].astype(BF)

    act = pl.BlockSpec((tk, fm), lambda m, t: (t, m))
    row = pl.BlockSpec((tk, D), lambda m, t: (t, 0))
    out = pl.BlockSpec((fm, D), lambda m, t: (m, 0))
    shp = jax.ShapeDtypeStruct((F, D), BF)
    return pl.pallas_call(
        body, name=name, grid=(F // fm, T // tk),
        in_specs=[act, act, act, row, row],
        out_specs=[out, out, out],
        out_shape=[shp, shp, shp],
        scratch_shapes=[pltpu.VMEM((fm, D), F32)] * 3,
        compiler_params=_params("parallel", "arbitrary"),
    )(s, da, db, dxo, h)


def _wgrad(a, b, name, tk=512):
    T, M = a.shape
    N = b.shape[1]

    def body(a_ref, b_ref, o_ref, acc_ref):
        t = pl.program_id(0)

        @pl.when(t == 0)
        def _():
            acc_ref[...] = jnp.zeros_like(acc_ref)

        acc_ref[...] += _dot_tn(a_ref[...].astype(BF), b_ref[...].astype(BF))

        @pl.when(t == pl.num_programs(0) - 1)
        def _():
            o_ref[...] = acc_ref[...].astype(BF)

    return pl.pallas_call(
        body, name=name, grid=(T // tk,),
        in_specs=[_row_spec(tk, M), _row_spec(tk, N)],
        out_specs=_full_spec((M, N)),
        out_shape=jax.ShapeDtypeStruct((M, N), BF),
        scratch_shapes=[pltpu.VMEM((M, N), F32)],
        compiler_params=_params("arbitrary"),
    )(a, b)


def _wgrad_pool_groups(p, dzp, tk=512):
    T = p.shape[0]

    def body(p_ref, d_ref, o_ref, acc_ref):
        t = pl.program_id(1)

        @pl.when(t == 0)
        def _():
            acc_ref[...] = jnp.zeros_like(acc_ref)

        acc_ref[...] += _dot_tn(p_ref[...], d_ref[...])

        @pl.when(t == pl.num_programs(1) - 1)
        def _():
            o_ref[...] = acc_ref[...].astype(BF)

    blk = pl.BlockSpec((tk, POOL_GC), lambda g, t: (t, g))
    return pl.pallas_call(
        body, name="pool_wgrad_groups", grid=(4, T // tk),
        in_specs=[blk, blk],
        out_specs=pl.BlockSpec((None, POOL_GC, POOL_GC), lambda g, t: (g, 0, 0)),
        out_shape=jax.ShapeDtypeStruct((4, POOL_GC, POOL_GC), BF),
        scratch_shapes=[pltpu.VMEM((POOL_GC, POOL_GC), F32)],
        compiler_params=_params("parallel", "arbitrary"),
    )(p, dzp)


def _wgrad_stack(a, b3, name, tk=512):
    T, M = a.shape
    N = b3.shape[2]

    def body(a_ref, b_ref, o_ref, acc_ref):
        t = pl.program_id(0)

        @pl.when(t == 0)
        def _():
            acc_ref[...] = jnp.zeros_like(acc_ref)

        av = a_ref[...]
        for w in range(3):
            acc_ref[w] += _dot_tn(av, b_ref[w])

        @pl.when(t == pl.num_programs(0) - 1)
        def _():
            o_ref[...] = acc_ref[...].astype(BF)

    return pl.pallas_call(
        body, name=name, grid=(T // tk,),
        in_specs=[_row_spec(tk, M), pl.BlockSpec((3, tk, N), lambda t: (0, t, 0))],
        out_specs=_full_spec((3, M, N)),
        out_shape=jax.ShapeDtypeStruct((3, M, N), BF),
        scratch_shapes=[pltpu.VMEM((3, M, N), F32)],
        compiler_params=_params("arbitrary"),
    )(a, b3)


def _rot_half(t):
    lane = lax.broadcasted_iota(jnp.int32, t.shape, 1)
    first = (lane % HEAD_DIM) < (HEAD_DIM // 2)
    return jnp.where(first, -pltpu.roll(t, 128 - HEAD_DIM // 2, 1), pltpu.roll(t, HEAD_DIM // 2, 1))


def _scatter_rows(dst_ref, scr_ref, d, cast):
    nc, rows, _ = scr_ref.shape
    n = rows // d
    for c in range(nc):
        sl = slice(c * 128, (c + 1) * 128)
        for r in range(d):
            src = scr_ref[c] if d == 1 else scr_ref.at[c][pl.ds(r, n, stride=d), :]
            dst_ref[r, :, sl] = src.astype(cast)


def _gather_rows(scr_ref, src_ref, d):
    nc, rows, _ = scr_ref.shape
    n = rows // d
    for c in range(nc):
        sl = slice(c * 128, (c + 1) * 128)
        for r in range(d):
            val = src_ref[r, :, sl].astype(F32)
            if d == 1:
                scr_ref[c] = val
            else:
                scr_ref.at[c][pl.ds(r, n, stride=d), :] = val


def _chunks_to_rows(scr_ref):
    nc = scr_ref.shape[0]
    return scr_ref[0] if nc == 1 else jnp.concatenate([scr_ref[c] for c in range(nc)], axis=1)


def _rows_to_chunks(scr_ref, val):
    for c in range(scr_ref.shape[0]):
        scr_ref[c] = val[:, c * 128:(c + 1) * 128]


def _qkv_fwd(h, w3, cos, sin, d, name, tm=512):
    T = h.shape[0]
    L = T // d
    C = GROUP_LANES

    def body(h_ref, w_ref, cos_ref, sin_ref, o_ref, scr_ref):
        w = pl.program_id(1)
        _rows_to_chunks(scr_ref, _dot(h_ref[...], w_ref[...]))

        @pl.when(w < 2)
        def _():
            scale = jnp.where(w == 0, HEAD_DIM ** -0.5, 1.0).astype(F32)
            cv = cos_ref[...]
            sv = sin_ref[...]
            for c in range(C // 128):
                t = scr_ref[c]
                scr_ref[c] = (t * cv + _rot_half(t) * sv) * scale

        _scatter_rows(o_ref, scr_ref, d, BF)

    return pl.pallas_call(
        body, name=name, grid=(T // tm, 3),
        in_specs=[pl.BlockSpec((tm, D), lambda i, w: (i, 0)), pl.BlockSpec((None, D, C), lambda i, w: (w, 0, 0)),
                  pl.BlockSpec((tm, 128), lambda i, w: (i, 0)), pl.BlockSpec((tm, 128), lambda i, w: (i, 0))],
        out_specs=pl.BlockSpec((None, d, tm // d, C), lambda i, w: (w, 0, i, 0)),
        out_shape=jax.ShapeDtypeStruct((3, d, L, C), BF),
        scratch_shapes=[pltpu.VMEM((C // 128, tm, 128), F32)],
        compiler_params=_params("parallel", "arbitrary"),
    )(h, w3, cos, sin)


def _att_chunk(L):
    return min(L, 1024)


def _head_mask(shape, h):
    lane = lax.broadcasted_iota(jnp.int32, shape, 1)
    return (lane // HEAD_DIM) == (h % 2)


def _attn_fwd(qkv, nh, name):
    _, d, L, C = qkv.shape
    lc = _att_chunk(L)
    nblk = lc // ATT_W

    def body(q_ref, k_ref, kh_ref, v_ref, vh_ref, o_ref, st_ref):
        i = pl.program_id(1)
        qi = lax.broadcasted_iota(jnp.int32, (ATT_W, 2 * ATT_W), 0)
        kj = lax.broadcasted_iota(jnp.int32, (ATT_W, 2 * ATT_W), 1)
        band = jnp.logical_and(kj >= qi, kj <= qi + ATT_W)
        lane = lax.broadcasted_iota(jnp.int32, (ATT_W, 128), 1)

        def block(row0, kc, vc, mask):
            rows = pl.ds(row0, ATT_W)
            lses = []
            for hp in range(C // 128):
                sl = slice(hp * 128, (hp + 1) * 128)
                qp = q_ref[rows, sl]
                kp = kc[:, sl]
                vp = vc[:, sl]
                outs = []
                for h in range(2 * hp, min(2 * hp + 2, nh)):
                    hm = _head_mask(qp.shape, h)
                    s = _dot_nt(jnp.where(hm, qp, jnp.zeros_like(qp)), kp)
                    s = jnp.where(mask, s, NEG_INF)
                    m = jnp.max(s, axis=-1, keepdims=True)
                    e = jnp.exp(s - m)
                    den = jnp.sum(e, axis=-1, keepdims=True)
                    p = (e * pl.reciprocal(den)).astype(BF)
                    outs.append(_dot(p, vp))
                    lses.append(m + jnp.log(den))
                if len(outs) == 2:
                    o = jnp.where(_head_mask(outs[0].shape, 0), outs[0], outs[1])
                else:
                    o = jnp.where(_head_mask(outs[0].shape, 0), outs[0], 0.0)
                o_ref[rows, sl] = o.astype(BF)
            mm = lses[0]
            for l in lses[1:]:
                mm = jnp.maximum(mm, l)
            tot = jnp.exp(lses[0] - mm)
            for l in lses[1:]:
                tot = tot + jnp.exp(l - mm)
            tile = jnp.where(lane == LSE_GROUP_LANE, mm + jnp.log(tot) - math.log(nh), 0.0)
            for h, l in enumerate(lses):
                tile = jnp.where(lane == h, l, tile)
            st_ref[rows, :] = tile

        first_mask = jnp.logical_and(band, jnp.logical_or(kj >= ATT_W, i > 0))
        block(0, jnp.concatenate([kh_ref[...], k_ref[pl.ds(0, ATT_W), :]], axis=0),
              jnp.concatenate([vh_ref[...], v_ref[pl.ds(0, ATT_W), :]], axis=0), first_mask)

        if nblk > 1:
            def step(blk, carry):
                prev = pl.ds(pl.multiple_of((blk - 1) * ATT_W, ATT_W), 2 * ATT_W)
                block(pl.multiple_of(blk * ATT_W, ATT_W), k_ref[prev, :], v_ref[prev, :], band)
                return carry
            lax.fori_loop(1, nblk, step, 0)

    main = lambda w: pl.BlockSpec((None, None, lc, C), lambda r, i: (w, r, i, 0))
    halo = lambda w: pl.BlockSpec((None, None, ATT_W, C), lambda r, i: (w, r, jnp.maximum(i * nblk - 1, 0), 0))
    return pl.pallas_call(
        body, name=name, grid=(d, L // lc),
        in_specs=[main(0), main(1), halo(1), main(2), halo(2)],
        out_specs=[pl.BlockSpec((None, lc, C), lambda r, i: (r, i, 0)), pl.BlockSpec((None, lc, 128), lambda r, i: (r, i, 0))],
        out_shape=[jax.ShapeDtypeStruct((d, L, C), BF), jax.ShapeDtypeStruct((d, L, 128), F32)],
        compiler_params=_params("parallel", "arbitrary"),
    )(qkv, qkv, qkv, qkv, qkv)


def _attn_bwd(qkv, do, st, dst, nh, name):
    _, d, L, C = qkv.shape
    lc = _att_chunk(L)
    nblk = lc // ATT_W
    nchunk = L // lc

    def body(q_ref, qn_ref, k_ref, kh_ref, v_ref, vh_ref, do_ref, don_ref, st_ref, stn_ref, ds_ref, dsn_ref, o_ref):
        i = pl.program_id(1)
        qi = lax.broadcasted_iota(jnp.int32, (ATT_W, 2 * ATT_W), 0)
        kj = lax.broadcasted_iota(jnp.int32, (ATT_W, 2 * ATT_W), 1)
        band_q = jnp.logical_and(kj >= qi, kj <= qi + ATT_W)
        qa = lax.broadcasted_iota(jnp.int32, (2 * ATT_W, ATT_W), 0)
        kb = lax.broadcasted_iota(jnp.int32, (2 * ATT_W, ATT_W), 1)
        band_k = jnp.logical_and(qa >= kb, qa <= kb + ATT_W)

        def probs(qm, kp, lse, mask):
            s = _dot_nt(qm, kp)
            return jnp.where(mask, jnp.exp(s - lse), 0.0)

        def q_block(row0, kc, vc, mask):
            rows = pl.ds(row0, ATT_W)
            stv = st_ref[rows, :]
            dsv = ds_ref[rows, :]
            for hp in range(C // 128):
                sl = slice(hp * 128, (hp + 1) * 128)
                qp = q_ref[rows, sl]
                dop = do_ref[rows, sl]
                kp = kc[:, sl]
                vp = vc[:, sl]
                outs = []
                for h in range(2 * hp, min(2 * hp + 2, nh)):
                    hm = _head_mask(qp.shape, h)
                    p = probs(jnp.where(hm, qp, jnp.zeros_like(qp)), kp, _lane_col(stv, h), mask)
                    dp = _dot_nt(jnp.where(hm, dop, jnp.zeros_like(dop)), vp)
                    dsc = (p * (dp - _lane_col(dsv, h))).astype(BF)
                    outs.append(_dot(dsc, kp))
                if len(outs) == 2:
                    dq = jnp.where(_head_mask(outs[0].shape, 0), outs[0], outs[1])
                else:
                    dq = jnp.where(_head_mask(outs[0].shape, 0), outs[0], 0.0)
                o_ref[0, rows, sl] = dq

        def k_block(row0, qq, doo, stv, dsv, mask):
            rows = pl.ds(row0, ATT_W)
            for hp in range(C // 128):
                sl = slice(hp * 128, (hp + 1) * 128)
                qp = qq[:, sl]
                dop = doo[:, sl]
                kp = k_ref[rows, sl]
                vp = v_ref[rows, sl]
                dks, dvs = [], []
                for h in range(2 * hp, min(2 * hp + 2, nh)):
                    hm = _head_mask(qp.shape, h)
                    qm = jnp.where(hm, qp, jnp.zeros_like(qp))
                    dom = jnp.where(hm, dop, jnp.zeros_like(dop))
                    p = probs(qm, kp, _lane_col(stv, h), mask)
                    dp = _dot_nt(dom, vp)
                    dsc = (p * (dp - _lane_col(dsv, h))).astype(BF)
                    dks.append(_dot_tn(dsc, qm))
                    dvs.append(_dot_tn(p.astype(BF), dom))
                if len(dks) == 2:
                    o_ref[1, rows, sl] = dks[0] + dks[1]
                    o_ref[2, rows, sl] = dvs[0] + dvs[1]
                else:
                    o_ref[1, rows, sl] = dks[0]
                    o_ref[2, rows, sl] = dvs[0]

        first_mask = jnp.logical_and(band_q, jnp.logical_or(kj >= ATT_W, i > 0))
        q_block(0, jnp.concatenate([kh_ref[...], k_ref[pl.ds(0, ATT_W), :]], axis=0),
                jnp.concatenate([vh_ref[...], v_ref[pl.ds(0, ATT_W), :]], axis=0), first_mask)
        if nblk > 1:
            def q_step(blk, carry):
                prev = pl.ds(pl.multiple_of((blk - 1) * ATT_W, ATT_W), 2 * ATT_W)
                q_block(pl.multiple_of(blk * ATT_W, ATT_W), k_ref[prev, :], v_ref[prev, :], band_q)
                return carry
            lax.fori_loop(1, nblk, q_step, 0)

            def k_step(blk, carry):
                two = pl.ds(pl.multiple_of(blk * ATT_W, ATT_W), 2 * ATT_W)
                k_block(pl.multiple_of(blk * ATT_W, ATT_W), q_ref[two, :], do_ref[two, :], st_ref[two, :], ds_ref[two, :], band_k)
                return carry
            lax.fori_loop(0, nblk - 1, k_step, 0)

        last = pl.ds((nblk - 1) * ATT_W, ATT_W)
        last_mask = jnp.logical_and(band_k, jnp.logical_or(qa < ATT_W, i < nchunk - 1))
        k_block((nblk - 1) * ATT_W,
                jnp.concatenate([q_ref[last, :], qn_ref[...]], axis=0),
                jnp.concatenate([do_ref[last, :], don_ref[...]], axis=0),
                jnp.concatenate([st_ref[last, :], stn_ref[...]], axis=0),
                jnp.concatenate([ds_ref[last, :], dsn_ref[...]], axis=0), last_mask)

    nb_all = L // ATT_W
    main4 = lambda w: pl.BlockSpec((None, None, lc, C), lambda r, i: (w, r, i, 0))
    prev4 = lambda w: pl.BlockSpec((None, None, ATT_W, C), lambda r, i: (w, r, jnp.maximum(i * nblk - 1, 0), 0))
    next4 = lambda w: pl.BlockSpec((None, None, ATT_W, C), lambda r, i: (w, r, jnp.minimum((i + 1) * nblk, nb_all - 1), 0))
    main3 = lambda n: pl.BlockSpec((None, lc, n), lambda r, i: (r, i, 0))
    next3 = lambda n: pl.BlockSpec((None, ATT_W, n), lambda r, i: (r, jnp.minimum((i + 1) * nblk, nb_all - 1), 0))
    return pl.pallas_call(
        body, name=name, grid=(d, nchunk),
        in_specs=[main4(0), next4(0), main4(1), prev4(1), main4(2), prev4(2),
                  main3(C), next3(C), main3(128), next3(128), main3(128), next3(128)],
        out_specs=pl.BlockSpec((3, None, lc, C), lambda r, i: (0, r, i, 0)),
        out_shape=jax.ShapeDtypeStruct((3, d, L, C), F32),
        compiler_params=_params("parallel", "arbitrary"),
    )(qkv, qkv, qkv, qkv, qkv, qkv, do, do, st, st, dst, dst)


def _alpha_from(lse_nat):
    m = jnp.maximum(jnp.maximum(lse_nat[0], lse_nat[1]), lse_nat[2])
    e = [jnp.exp(l - m) for l in lse_nat]
    inv = 1.0 / (e[0] + e[1] + e[2])
    return [ei * inv for ei in e]


def _attn_out_fwd(x, os_, sts, wos, tm=512):
    T = x.shape[0]
    C = GROUP_LANES

    def body(x_ref, o0, o1, o2, s0, s1, s2, w0, w1, w2, xo_ref, m0, m1, m2, al_ref, oscr, sscr):
        o_refs, st_refs, w_refs, m_refs = (o0, o1, o2), (s0, s1, s2), (w0, w1, w2), (m0, m1, m2)
        lses = []
        for g, d in enumerate(DILATIONS):
            _gather_rows(sscr.at[g], st_refs[g], d)
            lses.append(_lane_col(sscr[g, 0], LSE_GROUP_LANE))
        alpha = _alpha_from(lses)
        y = x_ref[...]
        for g, d in enumerate(DILATIONS):
            _gather_rows(oscr, o_refs[g], d)
            mg = (_chunks_to_rows(oscr) * (3.0 * alpha[g])).astype(BF)
            m_refs[g][...] = mg
            y = y + _dot(mg, w_refs[g][...])
        xo_ref[...] = y
        lane = lax.broadcasted_iota(jnp.int32, (tm, 128), 1)
        al_ref[...] = jnp.where(lane == 0, alpha[0], jnp.where(lane == 1, alpha[1], jnp.where(lane == 2, alpha[2], 0.0)))

    o_specs = [pl.BlockSpec((d, tm // d, C), lambda i: (0, i, 0)) for d in DILATIONS]
    st_specs = [pl.BlockSpec((d, tm // d, 128), lambda i: (0, i, 0)) for d in DILATIONS]
    mshape = jax.ShapeDtypeStruct((T, C), BF)
    return pl.pallas_call(
        body, name="attn_out_fwd", grid=(T // tm,),
        in_specs=[_row_spec(tm, D)] + o_specs + st_specs + [_full_spec((C, D))] * 3,
        out_specs=[_row_spec(tm, D), _row_spec(tm, C), _row_spec(tm, C), _row_spec(tm, C), _row_spec(tm, 128)],
        out_shape=[jax.ShapeDtypeStruct((T, D), F32), mshape, mshape, mshape, jax.ShapeDtypeStruct((T, 128), F32)],
        scratch_shapes=[pltpu.VMEM((C // 128, tm, 128), F32), pltpu.VMEM((3, 1, tm, 128), F32)],
        compiler_params=_params("parallel"),
    )(x, *os_, *sts, *wos)


def _attn_out_bwd(dx, os_, sts, alpha, wos, tm=512):
    T = dx.shape[0]
    C = GROUP_LANES

    def body(dx_ref, o0, o1, o2, s0, s1, s2, al_ref, w0, w1, w2, do0, do1, do2, ds0, ds1, ds2, dmscr, oscr, sscr, tscr):
        o_refs, st_refs, w_refs = (o0, o1, o2), (s0, s1, s2), (w0, w1, w2)
        do_refs, ds_refs = (do0, do1, do2), (ds0, ds1, ds2)
        dyb = dx_ref[...].astype(BF)
        alv = al_ref[...]
        alpha_g = [_lane_col(alv, g) for g in range(3)]
        dalpha = []
        for g, d in enumerate(DILATIONS):
            dm = _dot_nt(dyb, w_refs[g][...])
            _rows_to_chunks(dmscr.at[g], dm)
            _gather_rows(oscr.at[g], o_refs[g], d)
            _gather_rows(sscr.at[g], st_refs[g], d)
            dalpha.append(3.0 * jnp.sum(dm * _chunks_to_rows(oscr.at[g]), axis=-1, keepdims=True))
        mean_da = alpha_g[0] * dalpha[0] + alpha_g[1] * dalpha[1] + alpha_g[2] * dalpha[2]
        lane = lax.broadcasted_iota(jnp.int32, (tm, 128), 1)
        head_lane = lax.broadcasted_iota(jnp.int32, (tm, C), 1) // HEAD_DIM
        for g, d in enumerate(DILATIONS):
            nh = HEAD_GROUPS[g]
            dlse_g = alpha_g[g] * (dalpha[g] - mean_da)
            do_nat = _chunks_to_rows(dmscr.at[g]) * (3.0 * alpha_g[g])
            prod = do_nat * _chunks_to_rows(oscr.at[g])
            stv = sscr[g, 0]
            lse_g = _lane_col(stv, LSE_GROUP_LANE)
            tile = jnp.zeros((tm, 128), F32)
            for h in range(nh):
                delta = jnp.sum(jnp.where(head_lane == h, prod, 0.0), axis=-1, keepdims=True)
                dlse = dlse_g * jnp.exp(_lane_col(stv, h) - lse_g) * (1.0 / nh)
                tile = jnp.where(lane == h, delta - dlse, tile)
            _rows_to_chunks(dmscr.at[g], do_nat)
            _scatter_rows(do_refs[g], dmscr.at[g], d, BF)
            tscr[0] = tile
            _scatter_rows(ds_refs[g], tscr, d, F32)

    o_specs = [pl.BlockSpec((d, tm // d, C), lambda i: (0, i, 0)) for d in DILATIONS]
    st_specs = [pl.BlockSpec((d, tm // d, 128), lambda i: (0, i, 0)) for d in DILATIONS]
    return pl.pallas_call(
        body, name="attn_out_bwd", grid=(T // tm,),
        in_specs=[_row_spec(tm, D)] + o_specs + st_specs + [_row_spec(tm, 128)] + [_full_spec((C, D))] * 3,
        out_specs=o_specs + st_specs,
        out_shape=[jax.ShapeDtypeStruct((d, T // d, C), BF) for d in DILATIONS]
        + [jax.ShapeDtypeStruct((d, T // d, 128), F32) for d in DILATIONS],
        scratch_shapes=[pltpu.VMEM((3, C // 128, tm, 128), F32), pltpu.VMEM((3, C // 128, tm, 128), F32),
                        pltpu.VMEM((3, 1, tm, 128), F32), pltpu.VMEM((1, tm, 128), F32)],
        compiler_params=_params("parallel"),
    )(dx, *os_, *sts, alpha, *wos)


def _qkv_bwd(dqkvs, cos, sin, x2, g, w3s, dx3, tm=256):
    T = x2.shape[0]
    C = GROUP_LANES

    def body(dq0, dq1, dq2, cos_ref, sin_ref, x_ref, g_ref, w0, w1, w2, dx3_ref,
             dx_ref, n0, n1, n2, h_ref, dg_ref, scr, dh_ref):
        dq_refs, w_refs, n_refs = (dq0, dq1, dq2), (w0, w1, w2), (n0, n1, n2)

        @pl.when(pl.program_id(0) == 0)
        def _():
            dg_ref[...] = jnp.zeros_like(dg_ref)

        cv = cos_ref[...]
        sv = sin_ref[...]
        dh_ref[...] = jnp.zeros_like(dh_ref)
        for gi, d in enumerate(DILATIONS):
            for w in range(3):
                _gather_rows(scr, dq_refs[gi].at[w], d)
                if w < 2:
                    scale = HEAD_DIM ** -0.5 if w == 0 else 1.0
                    for c in range(C // 128):
                        t = scr[c]
                        scr[c] = (t * cv - _rot_half(t) * sv) * scale
                tb = _chunks_to_rows(scr).astype(BF)
                n_refs[gi][w] = tb
                dh_ref[...] += _dot_nt(tb, w_refs[gi][w])
        xv = x_ref[...]
        h_ref[...] = _rms_fwd(xv, g_ref[...]).astype(BF)
        dx, dg = _rms_bwd(xv, g_ref[...], dh_ref[...])
        dx_ref[...] = dx3_ref[...] + dx
        dg_ref[...] += dg

    dq_specs = [pl.BlockSpec((3, d, tm // d, C), lambda i: (0, 0, i, 0)) for d in DILATIONS]
    nat = pl.BlockSpec((3, tm, C), lambda i: (0, i, 0))
    nshape = jax.ShapeDtypeStruct((3, T, C), BF)
    return pl.pallas_call(
        body, name="qkv_bwd", grid=(T // tm,),
        in_specs=dq_specs + [_row_spec(tm, 128), _row_spec(tm, 128), _row_spec(tm, D), _full_spec((1, D))]
        + [_full_spec((3, D, C))] * 3 + [_row_spec(tm, D)],
        out_specs=[_row_spec(tm, D), nat, nat, nat, _row_spec(tm, D), _full_spec((1, D))],
        out_shape=[jax.ShapeDtypeStruct((T, D), F32), nshape, nshape, nshape, jax.ShapeDtypeStruct((T, D), BF),
                   jax.ShapeDtypeStruct((1, D), F32)],
        scratch_shapes=[pltpu.VMEM((C // 128, tm, 128), F32), pltpu.VMEM((tm, D), F32)],
        compiler_params=_params("arbitrary"),
    )(*dqkvs, cos, sin, x2, g, *w3s, dx3)


def _final_bwd(x, target, g, tm=512):
    T = x.shape[0]

    def body(x_ref, t_ref, g_ref, dx_ref, loss_ref, dg_ref):
        @pl.when(pl.program_id(0) == 0)
        def _():
            loss_ref[...] = jnp.zeros_like(loss_ref)
            dg_ref[...] = jnp.zeros_like(dg_ref)

        xv = x_ref[...]
        gv = g_ref[...]
        diff = _rms_fwd(xv, gv) - t_ref[...]
        loss_ref[...] += 0.5 * jnp.sum(jnp.mean(diff * diff, axis=-1, keepdims=True), axis=0, keepdims=True)
        dx, dg = _rms_bwd(xv, gv, diff * (1.0 / D))
        dx_ref[...] = dx
        dg_ref[...] += dg

    return pl.pallas_call(
        body, name="final_bwd", grid=(T // tm,),
        in_specs=[_row_spec(tm, D), _row_spec(tm, D), _full_spec((1, D))],
        out_specs=[_row_spec(tm, D), _full_spec((1, 1)), _full_spec((1, D))],
        out_shape=[jax.ShapeDtypeStruct((T, D), F32), jax.ShapeDtypeStruct((1, 1), F32), jax.ShapeDtypeStruct((1, D), F32)],
        compiler_params=_params("arbitrary"),
    )(x, target, g)


def _place():
    x, y, c = lax.axis_index("x"), lax.axis_index("y"), lax.axis_index("c")
    return x, y, c


def _allgather(arrs, name):
    n = len(arrs)

    def body(*refs):
        ins, outs = refs[:n], refs[n:2 * n]
        send_sems, recv_sems, local_sems = refs[2 * n:]
        x, y, c = _place()
        me, sibling = (x, y, c), (x, y, 1 - c)
        chips = [(1 - x, y), (x, 1 - y), (1 - x, 1 - y)]

        def slot(a, px, py, pc):
            return outs[a].at[4 * px + 2 * py + pc]

        def copy(a, k, block, to, src=None):
            return pltpu.make_async_remote_copy(
                src_ref=slot(a, *block) if src is None else src, dst_ref=slot(a, *block),
                send_sem=send_sems.at[a, k], recv_sem=recv_sems.at[a, k], device_id=to, device_id_type=MESH)

        mine = [pltpu.make_async_copy(ins[a], slot(a, *me), local_sems.at[a]) for a in range(n)]
        for cp in mine:
            cp.start()
        first = []
        for a in range(n):
            first.append(copy(a, 0, me, sibling, src=ins[a]))
            first += [copy(a, 1 + j, me, (*chip, c), src=ins[a]) for j, chip in enumerate(chips)]
        for cp in first:
            cp.start()
        passed = []
        for a in range(n):
            for j, chip in enumerate(chips):
                copy(a, 1 + j, (*chip, c), me).wait_recv()
                fwd = copy(a, 4 + j, (*chip, c), sibling)
                fwd.start()
                passed.append(fwd)
        for a in range(n):
            copy(a, 0, sibling, me).wait_recv()
            for j, chip in enumerate(chips):
                copy(a, 4 + j, (*chip, 1 - c), me).wait_recv()
        for cp in first + passed:
            cp.wait_send()
        for cp in mine:
            cp.wait()

    hbm = pl.BlockSpec(memory_space=pl.ANY)
    return pl.pallas_call(
        body, name=name,
        in_specs=[hbm] * n, out_specs=[hbm] * n,
        out_shape=[jax.ShapeDtypeStruct((N_DEV,) + a.shape, a.dtype) for a in arrs],
        scratch_shapes=[pltpu.SemaphoreType.DMA((n, 7)), pltpu.SemaphoreType.DMA((n, 7)), pltpu.SemaphoreType.DMA((n,))],
    )(*arrs)


def _peer(k):
    x, y, c = _place()
    px = 1 - x if k & 4 else x
    py = 1 - y if k & 2 else y
    pc = 1 - c if k & 1 else c
    return (px, py, pc), 4 * px + 2 * py + pc


HBM_SPEC = pl.BlockSpec(memory_space=pltpu.HBM)
SEM_SPEC = pl.BlockSpec(memory_space=pltpu.SEMAPHORE)
EFFECT = pltpu.SideEffectType.DATAFLOW_SIDE_EFFECTING


def _exchange_start(arrs, name):
    n = len(arrs)

    def body(*refs):
        srcs, lands = refs[:n], refs[n:2 * n]
        send_sems, recv_sems = refs[2 * n], refs[2 * n + 1]
        token = refs[4 * n + 2]
        local_sems = refs[4 * n + 3]
        x, y, c = _place()
        me = 4 * x + 2 * y + c
        mine = [pltpu.make_async_copy(srcs[a].at[me], lands[a].at[me], local_sems.at[a]) for a in range(n)]
        for cp in mine:
            cp.start()
        for a in range(n):
            for k in range(1, N_DEV):
                to, to_idx = _peer(k)
                pltpu.make_async_remote_copy(
                    src_ref=srcs[a].at[to_idx], dst_ref=lands[a].at[me],
                    send_sem=send_sems.at[a * (N_DEV - 1) + k - 1], recv_sem=recv_sems.at[a * (N_DEV - 1) + k - 1], device_id=to, device_id_type=MESH).start()
        for cp in mine:
            cp.wait()
        token[...] = jnp.zeros_like(token)

    hbm_shapes = [pltpu.HBM(a.shape, a.dtype) for a in arrs]
    outs = pl.pallas_call(
        body, name=name,
        out_shape=(pltpu.SemaphoreType.DMA((n * (N_DEV - 1),)), pltpu.SemaphoreType.DMA((n * (N_DEV - 1),)), *hbm_shapes, *hbm_shapes,
                   jax.ShapeDtypeStruct((8, 128), F32)),
        in_specs=[HBM_SPEC] * (2 * n),
        out_specs=(SEM_SPEC, SEM_SPEC, *([HBM_SPEC] * (2 * n)), pl.BlockSpec(memory_space=pltpu.VMEM)),
        input_output_aliases={i: 2 + i for i in range(2 * n)},
        scratch_shapes=[pltpu.SemaphoreType.DMA((n,))],
        compiler_params=pltpu.CompilerParams(has_side_effects=EFFECT),
    )(*[pltpu.with_memory_space_constraint(a, pltpu.HBM) for a in arrs],
      *[pltpu.with_memory_space_constraint(lax.empty(a.shape, a.dtype), pltpu.HBM) for a in arrs])
    return outs[0], outs[1], outs[2:2 + n], outs[2 + n:2 + 2 * n], outs[2 + 2 * n]


def _exchange_wait(send_sems, recv_sems, src_thru, land_thru, after, name):
    n = len(src_thru)

    def body(*refs):
        srcs, lands = refs[:n], refs[n:2 * n]
        send_sems, recv_sems = refs[2 * n], refs[2 * n + 1]
        for a in range(n):
            for k in range(1, N_DEV):
                frm, frm_idx = _peer(k)
                cp = pltpu.make_async_remote_copy(
                    src_ref=srcs[a].at[frm_idx], dst_ref=lands[a].at[frm_idx],
                    send_sem=send_sems.at[a * (N_DEV - 1) + k - 1], recv_sem=recv_sems.at[a * (N_DEV - 1) + k - 1], device_id=frm, device_id_type=MESH)
                cp.wait_send()
                cp.wait_recv()

    hbm_shapes = [pltpu.HBM(a.shape, a.dtype) for a in src_thru]
    outs = pl.pallas_call(
        body, name=name,
        out_shape=(*hbm_shapes, *hbm_shapes),
        in_specs=[HBM_SPEC] * (2 * n) + [SEM_SPEC, SEM_SPEC, pl.BlockSpec(memory_space=pl.ANY)],
        out_specs=[HBM_SPEC] * (2 * n),
        input_output_aliases={i: i for i in range(2 * n)},
        compiler_params=pltpu.CompilerParams(has_side_effects=EFFECT),
    )(*src_thru, *land_thru, send_sems, recv_sems, after)
    return outs[n:]


def _row_tile(rows):
    for t in (256, 176, 128, 8):
        if rows % t == 0:
            return t
    return rows


def _sum_parts(parts, name):
    K, R, C = parts.shape
    tr = _row_tile(R)

    def body(p_ref, o_ref):
        g = p_ref[0].astype(F32)
        for k in range(1, K):
            g = g + p_ref[k].astype(F32)
        o_ref[...] = g

    return pl.pallas_call(
        body, name=name, grid=(R // tr,),
        in_specs=[pl.BlockSpec((K, tr, C), lambda i: (0, i, 0))],
        out_specs=_row_spec(tr, C),
        out_shape=jax.ShapeDtypeStruct((R, C), F32),
        compiler_params=_params("parallel"),
    )(parts)


def _adamw(parts, w, m, v, name):
    K, R, C = parts.shape
    tr = _row_tile(R)

    def body(p_ref, w_ref, m_ref, v_ref, g_ref, d_ref, nm_ref, nv_ref):
        g = p_ref[0].astype(F32)
        for k in range(1, K):
            g = g + p_ref[k].astype(F32)
        nm = ADAM_B1 * m_ref[...] + (1.0 - ADAM_B1) * g
        nv = ADAM_B2 * v_ref[...] + (1.0 - ADAM_B2) * jnp.square(g)
        m_hat = nm / (1.0 - ADAM_B1 ** ADAM_STEP)
        v_hat = nv / (1.0 - ADAM_B2 ** ADAM_STEP)
        g_ref[...] = g
        d_ref[...] = -ADAM_LR * (m_hat / (jnp.sqrt(v_hat) + ADAM_EPS) + ADAM_WD * w_ref[...])
        nm_ref[...] = nm
        nv_ref[...] = nv

    blk = _row_spec(tr, C)
    shp = jax.ShapeDtypeStruct((R, C), F32)
    return pl.pallas_call(
        body, name=name, grid=(R // tr,),
        in_specs=[pl.BlockSpec((K, tr, C), lambda i: (0, i, 0)), blk, blk, blk],
        out_specs=[blk] * 4,
        out_shape=[shp] * 4,
        compiler_params=_params("parallel"),
    )(parts, w, m, v)


def _rope_tables(T):
    inv_freq = 1.0 / (ROPE_THETA ** (jnp.arange(0, HEAD_DIM, 2, dtype=F32) / HEAD_DIM))
    ang = jnp.arange(T, dtype=F32)[:, None] * inv_freq[None, :]
    ang = jnp.concatenate([ang, ang, ang, ang], axis=-1)
    return jnp.cos(ang), jnp.sin(ang)


def _pad_lanes(a, n):
    return jnp.pad(a, ((0, 0),) * (a.ndim - 1) + ((0, n - a.shape[-1]),))


def _layout_weights(gw):
    w = {}
    w["pool_in"] = gw["pool_in"].reshape(D, D)
    w["pool_grp"] = jnp.transpose(gw["pool_grp"], (1, 0, 2, 3)).reshape(4, POOL_GC, POOL_GC)
    w["pool_out"] = gw["pool_out"].reshape(D, D)
    wqkv = jnp.transpose(gw["qkv"], (1, 0, 2)).reshape(D, 3 * D)
    wo = gw["attn_out"].reshape(D, D)
    w["qkv"], w["attn_out"] = [], []
    for nh, off in zip(HEAD_GROUPS, HEAD_OFFS):
        lo, n = off * HEAD_DIM, nh * HEAD_DIM
        w["qkv"].append(jnp.stack([_pad_lanes(wqkv[:, k * D + lo:k * D + lo + n], GROUP_LANES) for k in range(3)]))
        w["attn_out"].append(jnp.pad(wo[lo:lo + n], ((0, GROUP_LANES - n), (0, 0))))
    for l in range(2):
        for nm in ("gate", "up", "down"):
            w[f"{nm}{l}"] = gw[f"{nm}{l}"].reshape(F, D)
    return w


def _local_step(x, target, w, norm_mix, norm_ffn, norm_final, pool_scale, emit):
    T = x.shape[0]
    cos, sin = _rope_tables(T)
    nm = [norm_mix[i:i + 1] for i in range(2)]
    nf = [norm_ffn[i:i + 1] for i in range(2)]
    nfin = norm_final.reshape(1, D)

    h0 = _norm_fwd(x, nm[0], "norm_mix0")
    x1, p = _pool_fwd(x, h0, w["pool_in"], w["pool_grp"], pool_scale, w["pool_out"])
    h1 = _norm_fwd(x1, nf[0], "norm_ffn0")
    x2, a0, b0, s0 = _ffn_fwd(x1, h1, w["gate0"], w["up0"], w["down0"], "ffn_fwd0")
    h2 = _norm_fwd(x2, nm[1], "norm_mix1")
    qkvs = [_qkv_fwd(h2, w["qkv"][g], cos, sin, DILATIONS[g], f"qkv_fwd{g}") for g in range(3)]
    att = [_attn_fwd(qkvs[g], HEAD_GROUPS[g], f"attn_fwd{g}") for g in range(3)]
    os_, sts = [a[0] for a in att], [a[1] for a in att]
    x3, m0, m1, m2, alpha = _attn_out_fwd(x2, os_, sts, w["attn_out"])
    h3 = _norm_fwd(x3, nf[1], "norm_ffn1")
    x4, a1, b1, s1 = _ffn_fwd(x3, h3, w["gate1"], w["up1"], w["down1"], "ffn_fwd1")

    g = {}
    dx4, loss, dg_final = _final_bwd(x4, target, nfin)
    dx3, da1, db1, _, dg_ffn1 = _ffn_bwd(dx4, x3, nf[1], a1, b1, w["gate1"], w["up1"], w["down1"], "ffn_bwd1")
    g["down1"], g["gate1"], g["up1"] = _ffn_wgrad(s1, da1, db1, dx4, h3, "ffn_wgrad1")
    emit("ffn1", g)
    dos_and_stats = _attn_out_bwd(dx3, os_, sts, alpha, w["attn_out"])
    dos, dsts = dos_and_stats[:3], dos_and_stats[3:]
    g["attn_out"] = [_wgrad(m, dx3, f"attn_out_wgrad{i}") for i, m in enumerate((m0, m1, m2))]
    dqkvs = [_attn_bwd(qkvs[gi], dos[gi], sts[gi], dsts[gi], HEAD_GROUPS[gi], f"attn_bwd{gi}") for gi in range(3)]
    dx2, n0, n1, n2, h2b, dg_mix1 = _qkv_bwd(dqkvs, cos, sin, x2, nm[1], w["qkv"], dx3)
    g["qkv"] = [_wgrad_stack(h2b, n, f"qkv_wgrad{i}") for i, n in enumerate((n0, n1, n2))]
    emit("attn", g)
    dx1, da0, db0, _, dg_ffn0 = _ffn_bwd(dx2, x1, nf[0], a0, b0, w["gate0"], w["up0"], w["down0"], "ffn_bwd0")
    g["down0"], g["gate0"], g["up0"] = _ffn_wgrad(s0, da0, db0, dx2, h1, "ffn_wgrad0")
    emit("ffn0", g)
    dx0, z, dzp, du, h0b, dscale, dg_mix0 = _pool_bwd(dx1, x, nm[0], p, w["pool_in"], w["pool_grp"], pool_scale, w["pool_out"])
    g["pool_out"] = _wgrad(z, dx1, "pool_out_wgrad")
    g["pool_in"] = _wgrad(h0b, du, "pool_in_wgrad")
    g["pool_grp"] = _wgrad_pool_groups(p, dzp)
    emit("pool", g)

    small = jnp.concatenate([dg_mix0, dg_mix1, dg_ffn0, dg_ffn1, dg_final, dscale,
                             jnp.broadcast_to(loss, (1, D)), jnp.zeros((1, D), F32)], axis=0)
    return dx0, small


GROUPS = {"ffn1": ("down1", "gate1", "up1"), "attn": ("qkv", "attn_out"), "ffn0": ("down0", "gate0", "up0"),
          "pool": ("pool_in", "pool_out", "pool_grp")}


def _grad_blocks(group, g):
    blocks = {}
    if group == "pool":
        blocks["pool_in"] = g["pool_in"].reshape(N_DEV, D // N_DEV, D)
        blocks["pool_out"] = g["pool_out"].reshape(N_DEV, D // N_DEV, D)
        blocks["pool_grp"] = jnp.transpose(g["pool_grp"].reshape(4, N_DEV, POOL_GC // N_DEV, POOL_GC), (1, 0, 2, 3)).reshape(N_DEV, 4 * POOL_GC // N_DEV, POOL_GC)
    elif group == "attn":
        wo = jnp.concatenate([gw[:nh * HEAD_DIM] for gw, nh in zip(g["attn_out"], HEAD_GROUPS)], axis=0)
        blocks["attn_out"] = wo.reshape(N_DEV, D // N_DEV, D)
        wqkv = jnp.concatenate([g["qkv"][gi][k][:, :HEAD_GROUPS[gi] * HEAD_DIM] for k in range(3) for gi in range(3)], axis=1)
        blocks["qkv"] = jnp.transpose(wqkv.reshape(D, N_DEV, 3 * D // N_DEV), (1, 0, 2))
    else:
        for nm in GROUPS[group]:
            blocks[nm] = g[nm].reshape(N_DEV, F // N_DEV, D)
    return [blocks[nm] for nm in GROUPS[group]]


def kernel(x, norm_mix, norm_ffn, norm_final, pool_w_in, pool_w_group, pool_scale, pool_w_out, attn_w_qkv, attn_w_out, ffn_w_gate, ffn_w_up, ffn_w_down, loss_target, m_norm_mix, m_norm_ffn, m_norm_final, m_pool_w_in, m_pool_w_group, m_pool_scale, m_pool_w_out, m_attn_w_qkv, m_attn_w_out, m_ffn_w_gate, m_ffn_w_up, m_ffn_w_down, v_norm_mix, v_norm_ffn, v_norm_final, v_pool_w_in, v_pool_w_group, v_pool_scale, v_pool_w_out, v_attn_w_qkv, v_attn_w_out, v_ffn_w_gate, v_ffn_w_up, v_ffn_w_down):
    shard = {
        "pool_in": pool_w_in[0], "pool_grp": pool_w_group[0], "pool_out": pool_w_out[0],
        "qkv": attn_w_qkv[0], "attn_out": attn_w_out[0],
    }
    for l in range(2):
        shard[f"gate{l}"] = ffn_w_gate[l].T
        shard[f"up{l}"] = ffn_w_up[l].T
        shard[f"down{l}"] = ffn_w_down[l]
    names = list(shard)
    gathered = _allgather([shard[k].astype(BF) for k in names], "weights_allgather")
    w = _layout_weights(dict(zip(names, gathered)))

    started = {}

    def emit(group, g):
        started[group] = _exchange_start(_grad_blocks(group, g), f"grads_start_{group}")

    grad_x, small = _local_step(x[0], loss_target[0], w, norm_mix, norm_ffn, norm_final, pool_scale, emit)

    received = {}
    after = started["pool"][4]
    for group, (send_sems, recv_sems, src_thru, land_thru, _) in started.items():
        lands = _exchange_wait(send_sems, recv_sems, src_thru, land_thru, after, f"grads_wait_{group}")
        received.update(zip(GROUPS[group], lands))
    small_all = _allgather([small], "small_allgather")[0]

    def upd(parts, wt, mt, vt, name):
        shape = wt.shape
        r2 = lambda t: t.reshape(parts.shape[1:])
        outs = _adamw(parts, r2(wt), r2(mt), r2(vt), name)
        return [o.reshape(shape) for o in outs]

    res = {}
    res["pool_w_in"] = upd(received["pool_in"], pool_w_in, m_pool_w_in, v_pool_w_in, "adamw_pool_in")
    res["pool_w_group"] = upd(received["pool_grp"], pool_w_group, m_pool_w_group, v_pool_w_group, "adamw_pool_grp")
    res["pool_w_out"] = upd(received["pool_out"], pool_w_out, m_pool_w_out, v_pool_w_out, "adamw_pool_out")
    res["attn_w_qkv"] = upd(received["qkv"], attn_w_qkv, m_attn_w_qkv, v_attn_w_qkv, "adamw_qkv")
    res["attn_w_out"] = upd(received["attn_out"], attn_w_out, m_attn_w_out, v_attn_w_out, "adamw_attn_out")
    for nm, wt, mt, vt in (("gate", ffn_w_gate, m_ffn_w_gate, v_ffn_w_gate), ("up", ffn_w_up, m_ffn_w_up, v_ffn_w_up)):
        per_layer = []
        for l in range(2):
            gt = _sum_parts(received[f"{nm}{l}"], f"sum_{nm}{l}").T
            per_layer.append(_adamw(gt[None], wt[l], mt[l], vt[l], f"adamw_{nm}{l}"))
        res[f"ffn_w_{nm}"] = [jnp.stack([per_layer[0][i], per_layer[1][i]]) for i in range(4)]
    per_layer = [_adamw(received[f"down{l}"], ffn_w_down[l], m_ffn_w_down[l], v_ffn_w_down[l], f"adamw_down{l}") for l in range(2)]
    res["ffn_w_down"] = [jnp.stack([per_layer[0][i], per_layer[1][i]]) for i in range(4)]

    small_w = jnp.concatenate([norm_mix, norm_ffn, norm_final[None], pool_scale, jnp.zeros((2, D), F32)], axis=0)
    small_m = jnp.concatenate([m_norm_mix, m_norm_ffn, m_norm_final[None], m_pool_scale, jnp.zeros((2, D), F32)], axis=0)
    small_v = jnp.concatenate([v_norm_mix, v_norm_ffn, v_norm_final[None], v_pool_scale, jnp.ones((2, D), F32)], axis=0)
    sg, sd, sm, sv = _adamw(small_all, small_w, small_m, small_v, "adamw_small")
    loss = sg[6, 0]
    res["norm_mix"] = [t[0:2] for t in (sg, sd, sm, sv)]
    res["norm_ffn"] = [t[2:4] for t in (sg, sd, sm, sv)]
    res["norm_final"] = [t[4] for t in (sg, sd, sm, sv)]
    res["pool_scale"] = [t[5:6] for t in (sg, sd, sm, sv)]

    order = ["norm_mix", "norm_ffn", "norm_final", "pool_w_in", "pool_w_group", "pool_scale", "pool_w_out",
             "attn_w_qkv", "attn_w_out", "ffn_w_gate", "ffn_w_up", "ffn_w_down"]
    return (loss, grad_x[None], *[res[k][0] for k in order], *[res[k][1] for k in order],
            *[res[k][2] for k in order], *[res[k][3] for k in order])
```

```python
import math

import jax
import jax.numpy as jnp
from jax import lax
from jax.experimental import pallas as pl
from jax.experimental.pallas import tpu as pltpu

D = 1024
F = 2816
N_DEV = 8
EPS = 1e-6
NEG_INF = -1e30
POOL_WINDOWS = (2, 4, 8, 16)
POOL_HALO = 16
POOL_GC = 256
HEAD_DIM = 64
HEAD_GROUPS = (6, 5, 5)
HEAD_OFFS = (0, 6, 11)
DILATIONS = (1, 4, 16)
ATT_W = 128
GROUP_LANES = 384
LSE_GROUP_LANE = 8
ROPE_THETA = 10000.0
ADAM_LR, ADAM_B1, ADAM_B2, ADAM_EPS, ADAM_WD, ADAM_STEP = 0.001, 0.9, 0.999, 1e-08, 0.01, 10

BF = jnp.bfloat16
F32 = jnp.float32
VMEM_LIMIT = 56 * 1024 * 1024
MESH = pl.DeviceIdType.MESH


def _params(*sem):
    return pltpu.CompilerParams(dimension_semantics=sem, vmem_limit_bytes=VMEM_LIMIT)


def _dot(a, b):
    return jnp.dot(a, b, preferred_element_type=F32)


def _dot_nt(a, b):
    return lax.dot_general(a, b, (((1,), (1,)), ((), ())), preferred_element_type=F32)


def _dot_tn(a, b):
    return lax.dot_general(a, b, (((0,), (0,)), ((), ())), preferred_element_type=F32)


def _rms_fwd(xv, g):
    r = lax.rsqrt(jnp.mean(xv * xv, axis=-1, keepdims=True) + EPS)
    return (xv * r) * g


def _rms_bwd(xv, g, dh):
    r = lax.rsqrt(jnp.mean(xv * xv, axis=-1, keepdims=True) + EPS)
    xhat = xv * r
    dg = jnp.sum(dh * xhat, axis=0, keepdims=True)
    dxh = dh * g
    dx = r * (dxh - xhat * jnp.mean(dxh * xhat, axis=-1, keepdims=True))
    return dx, dg


def _lane_col(tile, j):
    lane = lax.broadcasted_iota(jnp.int32, tile.shape, 1)
    return jnp.sum(jnp.where(lane == j, tile, 0.0), axis=-1, keepdims=True)


def _row_spec(tm, n):
    return pl.BlockSpec((tm, n), lambda i: (i, 0))


def _full_spec(shape):
    nd = len(shape)
    return pl.BlockSpec(shape, lambda *_: (0,) * nd)


def _after(dep, body, in_specs, args):
    if dep is None:
        return body, list(in_specs), list(args)

    def body_after(dep_ref, *refs):
        body(*refs)

    return body_after, [pl.BlockSpec(memory_space=pl.ANY)] + list(in_specs), [dep] + list(args)


def _norm_fwd(x, g, name, tm=512, dep=None):
    T = x.shape[0]

    def body(x_ref, g_ref, h_ref):
        h_ref[...] = _rms_fwd(x_ref[...], g_ref[...]).astype(BF)

    body, in_specs, args = _after(dep, body, [_row_spec(tm, D), _full_spec((1, D))], [x, g])
    return pl.pallas_call(
        body, name=name, grid=(T // tm,),
        in_specs=in_specs,
        out_specs=_row_spec(tm, D),
        out_shape=jax.ShapeDtypeStruct((T, D), BF),
        compiler_params=_params("parallel"),
    )(*args)


def _pool_fwd(x, h, w_in, w_grp, scale, w_out, tm=512):
    T = x.shape[0]
    n = tm + POOL_HALO

    def body(x_ref, h_ref, win_ref, wg_ref, sc_ref, wout_ref, x1_ref, p_ref, tail_ref, z_ref):
        i = pl.program_id(0)

        @pl.when(i == 0)
        def _():
            tail_ref[...] = jnp.zeros_like(tail_ref)

        u = _dot(h_ref[...], win_ref[...])
        pos = i * tm + lax.broadcasted_iota(jnp.int32, (tm, 1), 0)
        for g, w in enumerate(POOL_WINDOWS):
            sl = slice(g * POOL_GC, (g + 1) * POOL_GC)
            ug = u[:, sl]
            s = jnp.concatenate([tail_ref[:, sl], ug], axis=0)
            step = 1
            while step < w:
                s = s + pltpu.roll(s, step, 0)
                step *= 2
            cnt = jnp.minimum(pos + 1, w).astype(F32)
            pg = (s[POOL_HALO:, :] / cnt - ug).astype(BF)
            p_ref[:, sl] = pg
            z_ref[:, sl] = (_dot(pg, wg_ref[g]) * sc_ref[:, sl]).astype(BF)
        tail_ref[...] = u[tm - POOL_HALO:, :]
        x1_ref[...] = x_ref[...] + _dot(z_ref[...], wout_ref[...])

    return pl.pallas_call(
        body, name="pool_fwd", grid=(T // tm,),
        in_specs=[_row_spec(tm, D), _row_spec(tm, D), _full_spec((D, D)), _full_spec((4, POOL_GC, POOL_GC)),
                  _full_spec((1, D)), _full_spec((D, D))],
        out_specs=[_row_spec(tm, D), _row_spec(tm, D)],
        out_shape=[jax.ShapeDtypeStruct((T, D), F32), jax.ShapeDtypeStruct((T, D), BF)],
        scratch_shapes=[pltpu.VMEM((POOL_HALO, D), F32), pltpu.VMEM((tm, D), BF)],
        compiler_params=_params("arbitrary"),
    )(x, h, w_in, w_grp, scale, w_out)


def _pool_bwd(dx1, x0, g0, p, w_in, w_grp, scale, w_out, tm=512, dep=None):
    T = x0.shape[0]
    nt = T // tm
    n = tm + POOL_HALO
    rev = lambda i: (nt - 1 - i, 0)

    def body(dx1_ref, x0_ref, g_ref, p_ref, win_ref, wg_ref, sc_ref, wout_ref,
             dx0_ref, z_ref, dzp_ref, du_ref, h0_ref, dsc_ref, dg_ref, head_ref):
        i = pl.program_id(0)

        @pl.when(i == 0)
        def _():
            head_ref[...] = jnp.zeros_like(head_ref)
            dsc_ref[...] = jnp.zeros_like(dsc_ref)
            dg_ref[...] = jnp.zeros_like(dg_ref)

        dx1v = dx1_ref[...]
        dz = _dot_nt(dx1v.astype(BF), wout_ref[...])
        pos = (nt - 1 - i) * tm + lax.broadcasted_iota(jnp.int32, (tm, 1), 0)
        for g, w in enumerate(POOL_WINDOWS):
            sl = slice(g * POOL_GC, (g + 1) * POOL_GC)
            zpre = _dot(p_ref[:, sl], wg_ref[g])
            dzg = dz[:, sl]
            dsc_ref[:, sl] += jnp.sum(dzg * zpre, axis=0, keepdims=True)
            z_ref[:, sl] = (zpre * sc_ref[:, sl]).astype(BF)
            dzp = (dzg * sc_ref[:, sl]).astype(BF)
            dzp_ref[:, sl] = dzp
            dp = _dot_nt(dzp, wg_ref[g])
            cnt = jnp.minimum(pos + 1, w).astype(F32)
            dpc = dp / cnt
            s = jnp.concatenate([dpc, head_ref[:, sl]], axis=0)
            step = 1
            while step < w:
                s = s + pltpu.roll(s, n - step, 0)
                step *= 2
            head_ref[:, sl] = dpc[:POOL_HALO, :]
            du_ref[:, sl] = (s[:tm, :] - dp).astype(BF)
        dh0 = _dot_nt(du_ref[...], win_ref[...])
        x0v = x0_ref[...]
        h0_ref[...] = _rms_fwd(x0v, g_ref[...]).astype(BF)
        dx, dg = _rms_bwd(x0v, g_ref[...], dh0)
        dx0_ref[...] = dx1v + dx
        dg_ref[...] += dg

    bf_rows = jax.ShapeDtypeStruct((T, D), BF)
    vec = jax.ShapeDtypeStruct((1, D), F32)
    body, in_specs, args = _after(
        dep, body,
        [pl.BlockSpec((tm, D), rev), pl.BlockSpec((tm, D), rev), _full_spec((1, D)), pl.BlockSpec((tm, D), rev),
         _full_spec((D, D)), _full_spec((4, POOL_GC, POOL_GC)), _full_spec((1, D)), _full_spec((D, D))],
        [dx1, x0, g0, p, w_in, w_grp, scale, w_out])
    return pl.pallas_call(
        body, name="pool_bwd", grid=(nt,),
        in_specs=in_specs,
        out_specs=[pl.BlockSpec((tm, D), rev)] * 5 + [_full_spec((1, D))] * 2,
        out_shape=[jax.ShapeDtypeStruct((T, D), F32), bf_rows, bf_rows, bf_rows, bf_rows, vec, vec],
        scratch_shapes=[pltpu.VMEM((POOL_HALO, D), F32)],
        compiler_params=_params("arbitrary"),
    )(*args)


def _ffn_fwd(x, h, wg_t, wu_t, wd, name, tm=512, fk=1408):
    T = x.shape[0]

    def body(x_ref, h_ref, wg_ref, wu_ref, wd_ref, xo_ref, a_ref, b_ref, s_ref, acc_ref):
        k = pl.program_id(1)

        @pl.when(k == 0)
        def _():
            acc_ref[...] = jnp.zeros_like(acc_ref)

        hv = h_ref[...]
        a = _dot_nt(hv, wg_ref[...])
        b = _dot_nt(hv, wu_ref[...])
        s = ((a * jax.nn.sigmoid(a)) * b).astype(BF)
        a_ref[...] = a.astype(BF)
        b_ref[...] = b.astype(BF)
        s_ref[...] = s
        acc_ref[...] += _dot(s, wd_ref[...])

        @pl.when(k == pl.num_programs(1) - 1)
        def _():
            xo_ref[...] = x_ref[...] + acc_ref[...]

    row = pl.BlockSpec((tm, D), lambda i, k: (i, 0))
    wsp = pl.BlockSpec((fk, D), lambda i, k: (k, 0))
    act = pl.BlockSpec((tm, fk), lambda i, k: (i, k))
    act_shape = jax.ShapeDtypeStruct((T, F), BF)
    return pl.pallas_call(
        body, name=name, grid=(T // tm, F // fk),
        in_specs=[row, row, wsp, wsp, wsp],
        out_specs=[row, act, act, act],
        out_shape=[jax.ShapeDtypeStruct((T, D), F32), act_shape, act_shape, act_shape],
        scratch_shapes=[pltpu.VMEM((tm, D), F32)],
        compiler_params=_params("parallel", "arbitrary"),
    )(x, h, wg_t, wu_t, wd)


def _ffn_bwd(dxo, x_in, g, a, b, wg_t, wu_t, wd, name, tm=512, fk=1408, dep=None):
    T = x_in.shape[0]

    def body(dxo_ref, x_ref, g_ref, a_ref, b_ref, wg_ref, wu_ref, wd_ref,
             dx_ref, da_ref, db_ref, h_ref, dg_ref, dy_ref, dh_ref):
        i = pl.program_id(0)
        k = pl.program_id(1)

        @pl.when(jnp.logical_and(i == 0, k == 0))
        def _():
            dg_ref[...] = jnp.zeros_like(dg_ref)

        @pl.when(k == 0)
        def _():
            h_ref[...] = _rms_fwd(x_ref[...], g_ref[...]).astype(BF)
            dy_ref[...] = dxo_ref[...].astype(BF)
            dh_ref[...] = jnp.zeros_like(dh_ref)

        ds = _dot_nt(dy_ref[...], wd_ref[...])
        av = a_ref[...].astype(F32)
        bv = b_ref[...].astype(F32)
        sig = jax.nn.sigmoid(av)
        db = (ds * (av * sig)).astype(BF)
        da = (ds * bv * (sig * (1.0 + av * (1.0 - sig)))).astype(BF)
        da_ref[...] = da
        db_ref[...] = db
        dh_ref[...] += _dot(da, wg_ref[...]) + _dot(db, wu_ref[...])

        @pl.when(k == pl.num_programs(1) - 1)
        def _():
            dx, dg = _rms_bwd(x_ref[...], g_ref[...], dh_ref[...])
            dx_ref[...] = dxo_ref[...] + dx
            dg_ref[...] += dg

    row = pl.BlockSpec((tm, D), lambda i, k: (i, 0))
    wsp = pl.BlockSpec((fk, D), lambda i, k: (k, 0))
    act = pl.BlockSpec((tm, fk), lambda i, k: (i, k))
    vec = pl.BlockSpec((1, D), lambda i, k: (0, 0))
    act_shape = jax.ShapeDtypeStruct((T, F), BF)
    body, in_specs, args = _after(dep, body, [row, row, vec, act, act, wsp, wsp, wsp], [dxo, x_in, g, a, b, wg_t, wu_t, wd])
    return pl.pallas_call(
        body, name=name, grid=(T // tm, F // fk),
        in_specs=in_specs,
        out_specs=[row, act, act, row, vec],
        out_shape=[jax.ShapeDtypeStruct((T, D), F32), act_shape, act_shape, jax.ShapeDtypeStruct((T, D), BF),
                   jax.ShapeDtypeStruct((1, D), F32)],
        scratch_shapes=[pltpu.VMEM((tm, D), BF), pltpu.VMEM((tm, D), F32)],
        compiler_params=_params("arbitrary", "arbitrary"),
    )(*args)


def _ffn_wgrad(s, da, db, dxo, h, name, tk=256, fm=1408):
    T = h.shape[0]

    def body(s_ref, da_ref, db_ref, dxo_ref, h_ref, gd_ref, gg_ref, gu_ref, ad_ref, ag_ref, au_ref):
        t = pl.program_id(1)

        @pl.when(t == 0)
        def _():
            ad_ref[...] = jnp.zeros_like(ad_ref)
            ag_ref[...] = jnp.zeros_like(ag_ref)
            au_ref[...] = jnp.zeros_like(au_ref)

        hv = h_ref[...]
        ad_ref[...] += _dot_tn(s_ref[...], dxo_ref[...].astype(BF))
        ag_ref[...] += _dot_tn(da_ref[...], hv)
        au_ref[...] += _dot_tn(db_ref[...], hv)

        @pl.when(t == pl.num_programs(1) - 1)
        def _():
            gd_ref[...] = ad_ref[...].astype(BF)
            gg_ref[...] = ag_ref[...].astype(BF)
            gu_ref[...] = au_ref[...].astype(BF)

    act = pl.BlockSpec((tk, fm), lambda m, t: (t, m))
    row = pl.BlockSpec((tk, D), lambda m, t: (t, 0))
    out = pl.BlockSpec((fm, D), lambda m, t: (m, 0))
    shp = jax.ShapeDtypeStruct((F, D), BF)
    return pl.pallas_call(
        body, name=name, grid=(F // fm, T // tk),
        in_specs=[act, act, act, row, row],
        out_specs=[out, out, out],
        out_shape=[shp, shp, shp],
        scratch_shapes=[pltpu.VMEM((fm, D), F32)] * 3,
        compiler_params=_params("parallel", "arbitrary"),
    )(s, da, db, dxo, h)


def _wgrad(a, b, name, tk=512):
    T, M = a.shape
    N = b.shape[1]

    def body(a_ref, b_ref, o_ref, acc_ref):
        t = pl.program_id(0)

        @pl.when(t == 0)
        def _():
            acc_ref[...] = jnp.zeros_like(acc_ref)

        acc_ref[...] += _dot_tn(a_ref[...].astype(BF), b_ref[...].astype(BF))

        @pl.when(t == pl.num_programs(0) - 1)
        def _():
            o_ref[...] = acc_ref[...].astype(BF)

    return pl.pallas_call(
        body, name=name, grid=(T // tk,),
        in_specs=[_row_spec(tk, M), _row_spec(tk, N)],
        out_specs=_full_spec((M, N)),
        out_shape=jax.ShapeDtypeStruct((M, N), BF),
        scratch_shapes=[pltpu.VMEM((M, N), F32)],
        compiler_params=_params("arbitrary"),
    )(a, b)


def _wgrad_pool_groups(p, dzp, tk=512):
    T = p.shape[0]

    def body(p_ref, d_ref, o_ref, acc_ref):
        t = pl.program_id(1)

        @pl.when(t == 0)
        def _():
            acc_ref[...] = jnp.zeros_like(acc_ref)

        acc_ref[...] += _dot_tn(p_ref[...], d_ref[...])

        @pl.when(t == pl.num_programs(1) - 1)
        def _():
            o_ref[...] = acc_ref[...].astype(BF)

    blk = pl.BlockSpec((tk, POOL_GC), lambda g, t: (t, g))
    return pl.pallas_call(
        body, name="pool_wgrad_groups", grid=(4, T // tk),
        in_specs=[blk, blk],
        out_specs=pl.BlockSpec((None, POOL_GC, POOL_GC), lambda g, t: (g, 0, 0)),
        out_shape=jax.ShapeDtypeStruct((4, POOL_GC, POOL_GC), BF),
        scratch_shapes=[pltpu.VMEM((POOL_GC, POOL_GC), F32)],
        compiler_params=_params("parallel", "arbitrary"),
    )(p, dzp)


def _wgrad_stack(a, b3, name, tk=512):
    T, M = a.shape
    N = b3.shape[2]

    def body(a_ref, b_ref, o_ref, acc_ref):
        t = pl.program_id(0)

        @pl.when(t == 0)
        def _():
            acc_ref[...] = jnp.zeros_like(acc_ref)

        av = a_ref[...]
        for w in range(3):
            acc_ref[w] += _dot_tn(av, b_ref[w])

        @pl.when(t == pl.num_programs(0) - 1)
        def _():
            o_ref[...] = acc_ref[...].astype(BF)

    return pl.pallas_call(
        body, name=name, grid=(T // tk,),
        in_specs=[_row_spec(tk, M), pl.BlockSpec((3, tk, N), lambda t: (0, t, 0))],
        out_specs=_full_spec((3, M, N)),
        out_shape=jax.ShapeDtypeStruct((3, M, N), BF),
        scratch_shapes=[pltpu.VMEM((3, M, N), F32)],
        compiler_params=_params("arbitrary"),
    )(a, b3)


def _rot_half(t):
    lane = lax.broadcasted_iota(jnp.int32, t.shape, 1)
    first = (lane % HEAD_DIM) < (HEAD_DIM // 2)
    return jnp.where(first, -pltpu.roll(t, 128 - HEAD_DIM // 2, 1), pltpu.roll(t, HEAD_DIM // 2, 1))


def _scatter_rows(dst_ref, scr_ref, d, cast):
    nc, rows, _ = scr_ref.shape
    n = rows // d
    for c in range(nc):
        sl = slice(c * 128, (c + 1) * 128)
        for r in range(d):
            src = scr_ref[c] if d == 1 else scr_ref.at[c][pl.ds(r, n, stride=d), :]
            dst_ref[r, :, sl] = src.astype(cast)


def _gather_rows(scr_ref, src_ref, d):
    nc, rows, _ = scr_ref.shape
    n = rows // d
    for c in range(nc):
        sl = slice(c * 128, (c + 1) * 128)
        for r in range(d):
            val = src_ref[r, :, sl].astype(F32)
            if d == 1:
                scr_ref[c] = val
            else:
                scr_ref.at[c][pl.ds(r, n, stride=d), :] = val


def _chunks_to_rows(scr_ref):
    nc = scr_ref.shape[0]
    return scr_ref[0] if nc == 1 else jnp.concatenate([scr_ref[c] for c in range(nc)], axis=1)


def _rows_to_chunks(scr_ref, val):
    for c in range(scr_ref.shape[0]):
        scr_ref[c] = val[:, c * 128:(c + 1) * 128]


def _qkv_fwd(h, w3, cos, sin, d, name, tm=512):
    T = h.shape[0]
    L = T // d
    C = GROUP_LANES

    def body(h_ref, w_ref, cos_ref, sin_ref, o_ref, scr_ref):
        w = pl.program_id(1)
        _rows_to_chunks(scr_ref, _dot(h_ref[...], w_ref[...]))

        @pl.when(w < 2)
        def _():
            scale = jnp.where(w == 0, HEAD_DIM ** -0.5, 1.0).astype(F32)
            cv = cos_ref[...]
            sv = sin_ref[...]
            for c in range(C // 128):
                t = scr_ref[c]
                scr_ref[c] = (t * cv + _rot_half(t) * sv) * scale

        _scatter_rows(o_ref, scr_ref, d, BF)

    return pl.pallas_call(
        body, name=name, grid=(T // tm, 3),
        in_specs=[pl.BlockSpec((tm, D), lambda i, w: (i, 0)), pl.BlockSpec((None, D, C), lambda i, w: (w, 0, 0)),
                  pl.BlockSpec((tm, 128), lambda i, w: (i, 0)), pl.BlockSpec((tm, 128), lambda i, w: (i, 0))],
        out_specs=pl.BlockSpec((None, d, tm // d, C), lambda i, w: (w, 0, i, 0)),
        out_shape=jax.ShapeDtypeStruct((3, d, L, C), BF),
        scratch_shapes=[pltpu.VMEM((C // 128, tm, 128), F32)],
        compiler_params=_params("parallel", "arbitrary"),
    )(h, w3, cos, sin)


def _att_chunk(L):
    return min(L, 1024)


def _head_mask(shape, h):
    lane = lax.broadcasted_iota(jnp.int32, shape, 1)
    return (lane // HEAD_DIM) == (h % 2)


def _attn_fwd(qkv, nh, name):
    _, d, L, C = qkv.shape
    lc = _att_chunk(L)
    nblk = lc // ATT_W

    def body(q_ref, k_ref, kh_ref, v_ref, vh_ref, o_ref, st_ref):
        i = pl.program_id(1)
        qi = lax.broadcasted_iota(jnp.int32, (ATT_W, 2 * ATT_W), 0)
        kj = lax.broadcasted_iota(jnp.int32, (ATT_W, 2 * ATT_W), 1)
        band = jnp.logical_and(kj >= qi, kj <= qi + ATT_W)
        lane = lax.broadcasted_iota(jnp.int32, (ATT_W, 128), 1)

        def block(row0, kc, vc, mask):
            rows = pl.ds(row0, ATT_W)
            lses = []
            for hp in range(C // 128):
                sl = slice(hp * 128, (hp + 1) * 128)
                qp = q_ref[rows, sl]
                kp = kc[:, sl]
                vp = vc[:, sl]
                outs = []
                for h in range(2 * hp, min(2 * hp + 2, nh)):
                    hm = _head_mask(qp.shape, h)
                    s = _dot_nt(jnp.where(hm, qp, jnp.zeros_like(qp)), kp)
                    s = jnp.where(mask, s, NEG_INF)
                    m = jnp.max(s, axis=-1, keepdims=True)
                    e = jnp.exp(s - m)
                    den = jnp.sum(e, axis=-1, keepdims=True)
                    p = (e * pl.reciprocal(den)).astype(BF)
                    outs.append(_dot(p, vp))
                    lses.append(m + jnp.log(den))
                if len(outs) == 2:
                    o = jnp.where(_head_mask(outs[0].shape, 0), outs[0], outs[1])
                else:
                    o = jnp.where(_head_mask(outs[0].shape, 0), outs[0], 0.0)
                o_ref[rows, sl] = o.astype(BF)
            mm = lses[0]
            for l in lses[1:]:
                mm = jnp.maximum(mm, l)
            tot = jnp.exp(lses[0] - mm)
            for l in lses[1:]:
                tot = tot + jnp.exp(l - mm)
            tile = jnp.where(lane == LSE_GROUP_LANE, mm + jnp.log(tot) - math.log(nh), 0.0)
            for h, l in enumerate(lses):
                tile = jnp.where(lane == h, l, tile)
            st_ref[rows, :] = tile

        first_mask = jnp.logical_and(band, jnp.logical_or(kj >= ATT_W, i > 0))
        block(0, jnp.concatenate([kh_ref[...], k_ref[pl.ds(0, ATT_W), :]], axis=0),
              jnp.concatenate([vh_ref[...], v_ref[pl.ds(0, ATT_W), :]], axis=0), first_mask)

        if nblk > 1:
            def step(blk, carry):
                prev = pl.ds(pl.multiple_of((blk - 1) * ATT_W, ATT_W), 2 * ATT_W)
                block(pl.multiple_of(blk * ATT_W, ATT_W), k_ref[prev, :], v_ref[prev, :], band)
                return carry
            lax.fori_loop(1, nblk, step, 0)

    main = lambda w: pl.BlockSpec((None, None, lc, C), lambda r, i: (w, r, i, 0))
    halo = lambda w: pl.BlockSpec((None, None, ATT_W, C), lambda r, i: (w, r, jnp.maximum(i * nblk - 1, 0), 0))
    return pl.pallas_call(
        body, name=name, grid=(d, L // lc),
        in_specs=[main(0), main(1), halo(1), main(2), halo(2)],
        out_specs=[pl.BlockSpec((None, lc, C), lambda r, i: (r, i, 0)), pl.BlockSpec((None, lc, 128), lambda r, i: (r, i, 0))],
        out_shape=[jax.ShapeDtypeStruct((d, L, C), BF), jax.ShapeDtypeStruct((d, L, 128), F32)],
        compiler_params=_params("parallel", "arbitrary"),
    )(qkv, qkv, qkv, qkv, qkv)


def _attn_bwd(qkv, do, st, dst, nh, name):
    _, d, L, C = qkv.shape
    lc = _att_chunk(L)
    nblk = lc // ATT_W
    nchunk = L // lc

    def body(q_ref, qn_ref, k_ref, kh_ref, v_ref, vh_ref, do_ref, don_ref, st_ref, stn_ref, ds_ref, dsn_ref, o_ref):
        i = pl.program_id(1)
        qi = lax.broadcasted_iota(jnp.int32, (ATT_W, 2 * ATT_W), 0)
        kj = lax.broadcasted_iota(jnp.int32, (ATT_W, 2 * ATT_W), 1)
        band_q = jnp.logical_and(kj >= qi, kj <= qi + ATT_W)
        qa = lax.broadcasted_iota(jnp.int32, (2 * ATT_W, ATT_W), 0)
        kb = lax.broadcasted_iota(jnp.int32, (2 * ATT_W, ATT_W), 1)
        band_k = jnp.logical_and(qa >= kb, qa <= kb + ATT_W)

        def probs(qm, kp, lse, mask):
            s = _dot_nt(qm, kp)
            return jnp.where(mask, jnp.exp(s - lse), 0.0)

        def q_block(row0, kc, vc, mask):
            rows = pl.ds(row0, ATT_W)
            stv = st_ref[rows, :]
            dsv = ds_ref[rows, :]
            for hp in range(C // 128):
                sl = slice(hp * 128, (hp + 1) * 128)
                qp = q_ref[rows, sl]
                dop = do_ref[rows, sl]
                kp = kc[:, sl]
                vp = vc[:, sl]
                outs = []
                for h in range(2 * hp, min(2 * hp + 2, nh)):
                    hm = _head_mask(qp.shape, h)
                    p = probs(jnp.where(hm, qp, jnp.zeros_like(qp)), kp, _lane_col(stv, h), mask)
                    dp = _dot_nt(jnp.where(hm, dop, jnp.zeros_like(dop)), vp)
                    dsc = (p * (dp - _lane_col(dsv, h))).astype(BF)
                    outs.append(_dot(dsc, kp))
                if len(outs) == 2:
                    dq = jnp.where(_head_mask(outs[0].shape, 0), outs[0], outs[1])
                else:
                    dq = jnp.where(_head_mask(outs[0].shape, 0), outs[0], 0.0)
                o_ref[0, rows, sl] = dq

        def k_block(row0, qq, doo, stv, dsv, mask):
            rows = pl.ds(row0, ATT_W)
            for hp in range(C // 128):
                sl = slice(hp * 128, (hp + 1) * 128)
                qp = qq[:, sl]
                dop = doo[:, sl]
                kp = k_ref[rows, sl]
                vp = v_ref[rows, sl]
                dks, dvs = [], []
                for h in range(2 * hp, min(2 * hp + 2, nh)):
                    hm = _head_mask(qp.shape, h)
                    qm = jnp.where(hm, qp, jnp.zeros_like(qp))
                    dom = jnp.where(hm, dop, jnp.zeros_like(dop))
                    p = probs(qm, kp, _lane_col(stv, h), mask)
                    dp = _dot_nt(dom, vp)
                    dsc = (p * (dp - _lane_col(dsv, h))).astype(BF)
                    dks.append(_dot_tn(dsc, qm))
                    dvs.append(_dot_tn(p.astype(BF), dom))
                if len(dks) == 2:
                    o_ref[1, rows, sl] = dks[0] + dks[1]
                    o_ref[2, rows, sl] = dvs[0] + dvs[1]
                else:
                    o_ref[1, rows, sl] = dks[0]
                    o_ref[2, rows, sl] = dvs[0]

        first_mask = jnp.logical_and(band_q, jnp.logical_or(kj >= ATT_W, i > 0))
        q_block(0, jnp.concatenate([kh_ref[...], k_ref[pl.ds(0, ATT_W), :]], axis=0),
                jnp.concatenate([vh_ref[...], v_ref[pl.ds(0, ATT_W), :]], axis=0), first_mask)
        if nblk > 1:
            def q_step(blk, carry):
                prev = pl.ds(pl.multiple_of((blk - 1) * ATT_W, ATT_W), 2 * ATT_W)
                q_block(pl.multiple_of(blk * ATT_W, ATT_W), k_ref[prev, :], v_ref[prev, :], band_q)
                return carry
            lax.fori_loop(1, nblk, q_step, 0)

            def k_step(blk, carry):
                two = pl.ds(pl.multiple_of(blk * ATT_W, ATT_W), 2 * ATT_W)
                k_block(pl.multiple_of(blk * ATT_W, ATT_W), q_ref[two, :], do_ref[two, :], st_ref[two, :], ds_ref[two, :], band_k)
                return carry
            lax.fori_loop(0, nblk - 1, k_step, 0)

        last = pl.ds((nblk - 1) * ATT_W, ATT_W)
        last_mask = jnp.logical_and(band_k, jnp.logical_or(qa < ATT_W, i < nchunk - 1))
        k_block((nblk - 1) * ATT_W,
                jnp.concatenate([q_ref[last, :], qn_ref[...]], axis=0),
                jnp.concatenate([do_ref[last, :], don_ref[...]], axis=0),
                jnp.concatenate([st_ref[last, :], stn_ref[...]], axis=0),
                jnp.concatenate([ds_ref[last, :], dsn_ref[...]], axis=0), last_mask)

    nb_all = L // ATT_W
    main4 = lambda w: pl.BlockSpec((None, None, lc, C), lambda r, i: (w, r, i, 0))
    prev4 = lambda w: pl.BlockSpec((None, None, ATT_W, C), lambda r, i: (w, r, jnp.maximum(i * nblk - 1, 0), 0))
    next4 = lambda w: pl.BlockSpec((None, None, ATT_W, C), lambda r, i: (w, r, jnp.minimum((i + 1) * nblk, nb_all - 1), 0))
    main3 = lambda n: pl.BlockSpec((None, lc, n), lambda r, i: (r, i, 0))
    next3 = lambda n: pl.BlockSpec((None, ATT_W, n), lambda r, i: (r, jnp.minimum((i + 1) * nblk, nb_all - 1), 0))
    return pl.pallas_call(
        body, name=name, grid=(d, nchunk),
        in_specs=[main4(0), next4(0), main4(1), prev4(1), main4(2), prev4(2),
                  main3(C), next3(C), main3(128), next3(128), main3(128), next3(128)],
        out_specs=pl.BlockSpec((3, None, lc, C), lambda r, i: (0, r, i, 0)),
        out_shape=jax.ShapeDtypeStruct((3, d, L, C), F32),
        compiler_params=_params("parallel", "arbitrary"),
    )(qkv, qkv, qkv, qkv, qkv, qkv, do, do, st, st, dst, dst)


def _alpha_from(lse_nat):
    m = jnp.maximum(jnp.maximum(lse_nat[0], lse_nat[1]), lse_nat[2])
    e = [jnp.exp(l - m) for l in lse_nat]
    inv = 1.0 / (e[0] + e[1] + e[2])
    return [ei * inv for ei in e]


def _attn_out_fwd(x, os_, sts, wos, tm=512):
    T = x.shape[0]
    C = GROUP_LANES

    def body(x_ref, o0, o1, o2, s0, s1, s2, w0, w1, w2, xo_ref, m0, m1, m2, al_ref, oscr, sscr):
        o_refs, st_refs, w_refs, m_refs = (o0, o1, o2), (s0, s1, s2), (w0, w1, w2), (m0, m1, m2)
        lses = []
        for g, d in enumerate(DILATIONS):
            _gather_rows(sscr.at[g], st_refs[g], d)
            lses.append(_lane_col(sscr[g, 0], LSE_GROUP_LANE))
        alpha = _alpha_from(lses)
        y = x_ref[...]
        for g, d in enumerate(DILATIONS):
            _gather_rows(oscr, o_refs[g], d)
            mg = (_chunks_to_rows(oscr) * (3.0 * alpha[g])).astype(BF)
            m_refs[g][...] = mg
            y = y + _dot(mg, w_refs[g][...])
        xo_ref[...] = y
        lane = lax.broadcasted_iota(jnp.int32, (tm, 128), 1)
        al_ref[...] = jnp.where(lane == 0, alpha[0], jnp.where(lane == 1, alpha[1], jnp.where(lane == 2, alpha[2], 0.0)))

    o_specs = [pl.BlockSpec((d, tm // d, C), lambda i: (0, i, 0)) for d in DILATIONS]
    st_specs = [pl.BlockSpec((d, tm // d, 128), lambda i: (0, i, 0)) for d in DILATIONS]
    mshape = jax.ShapeDtypeStruct((T, C), BF)
    return pl.pallas_call(
        body, name="attn_out_fwd", grid=(T // tm,),
        in_specs=[_row_spec(tm, D)] + o_specs + st_specs + [_full_spec((C, D))] * 3,
        out_specs=[_row_spec(tm, D), _row_spec(tm, C), _row_spec(tm, C), _row_spec(tm, C), _row_spec(tm, 128)],
        out_shape=[jax.ShapeDtypeStruct((T, D), F32), mshape, mshape, mshape, jax.ShapeDtypeStruct((T, 128), F32)],
        scratch_shapes=[pltpu.VMEM((C // 128, tm, 128), F32), pltpu.VMEM((3, 1, tm, 128), F32)],
        compiler_params=_params("parallel"),
    )(x, *os_, *sts, *wos)


def _attn_out_bwd(dx, os_, sts, alpha, wos, tm=512, dep=None):
    T = dx.shape[0]
    C = GROUP_LANES

    def body(dx_ref, o0, o1, o2, s0, s1, s2, al_ref, w0, w1, w2, do0, do1, do2, ds0, ds1, ds2, dmscr, oscr, sscr, tscr):
        o_refs, st_refs, w_refs = (o0, o1, o2), (s0, s1, s2), (w0, w1, w2)
        do_refs, ds_refs = (do0, do1, do2), (ds0, ds1, ds2)
        dyb = dx_ref[...].astype(BF)
        alv = al_ref[...]
        alpha_g = [_lane_col(alv, g) for g in range(3)]
        dalpha = []
        for g, d in enumerate(DILATIONS):
            dm = _dot_nt(dyb, w_refs[g][...])
            _rows_to_chunks(dmscr.at[g], dm)
            _gather_rows(oscr.at[g], o_refs[g], d)
            _gather_rows(sscr.at[g], st_refs[g], d)
            dalpha.append(3.0 * jnp.sum(dm * _chunks_to_rows(oscr.at[g]), axis=-1, keepdims=True))
        mean_da = alpha_g[0] * dalpha[0] + alpha_g[1] * dalpha[1] + alpha_g[2] * dalpha[2]
        lane = lax.broadcasted_iota(jnp.int32, (tm, 128), 1)
        head_lane = lax.broadcasted_iota(jnp.int32, (tm, C), 1) // HEAD_DIM
        for g, d in enumerate(DILATIONS):
            nh = HEAD_GROUPS[g]
            dlse_g = alpha_g[g] * (dalpha[g] - mean_da)
            do_nat = _chunks_to_rows(dmscr.at[g]) * (3.0 * alpha_g[g])
            prod = do_nat * _chunks_to_rows(oscr.at[g])
            stv = sscr[g, 0]
            lse_g = _lane_col(stv, LSE_GROUP_LANE)
            tile = jnp.zeros((tm, 128), F32)
            for h in range(nh):
                delta = jnp.sum(jnp.where(head_lane == h, prod, 0.0), axis=-1, keepdims=True)
                dlse = dlse_g * jnp.exp(_lane_col(stv, h) - lse_g) * (1.0 / nh)
                tile = jnp.where(lane == h, delta - dlse, tile)
            _rows_to_chunks(dmscr.at[g], do_nat)
            _scatter_rows(do_refs[g], dmscr.at[g], d, BF)
            tscr[0] = tile
            _scatter_rows(ds_refs[g], tscr, d, F32)

    o_specs = [pl.BlockSpec((d, tm // d, C), lambda i: (0, i, 0)) for d in DILATIONS]
    st_specs = [pl.BlockSpec((d, tm // d, 128), lambda i: (0, i, 0)) for d in DILATIONS]
    body, in_specs, args = _after(
        dep, body, [_row_spec(tm, D)] + o_specs + st_specs + [_row_spec(tm, 128)] + [_full_spec((C, D))] * 3,
        [dx, *os_, *sts, alpha, *wos])
    return pl.pallas_call(
        body, name="attn_out_bwd", grid=(T // tm,),
        in_specs=in_specs,
        out_specs=o_specs + st_specs,
        out_shape=[jax.ShapeDtypeStruct((d, T // d, C), BF) for d in DILATIONS]
        + [jax.ShapeDtypeStruct((d, T // d, 128), F32) for d in DILATIONS],
        scratch_shapes=[pltpu.VMEM((3, C // 128, tm, 128), F32), pltpu.VMEM((3, C // 128, tm, 128), F32),
                        pltpu.VMEM((3, 1, tm, 128), F32), pltpu.VMEM((1, tm, 128), F32)],
        compiler_params=_params("parallel"),
    )(*args)


def _qkv_bwd(dqkvs, cos, sin, x2, g, w3s, dx3, tm=256):
    T = x2.shape[0]
    C = GROUP_LANES

    def body(dq0, dq1, dq2, cos_ref, sin_ref, x_ref, g_ref, w0, w1, w2, dx3_ref,
             dx_ref, n0, n1, n2, h_ref, dg_ref, scr, dh_ref):
        dq_refs, w_refs, n_refs = (dq0, dq1, dq2), (w0, w1, w2), (n0, n1, n2)

        @pl.when(pl.program_id(0) == 0)
        def _():
            dg_ref[...] = jnp.zeros_like(dg_ref)

        cv = cos_ref[...]
        sv = sin_ref[...]
        dh_ref[...] = jnp.zeros_like(dh_ref)
        for gi, d in enumerate(DILATIONS):
            for w in range(3):
                _gather_rows(scr, dq_refs[gi].at[w], d)
                if w < 2:
                    scale = HEAD_DIM ** -0.5 if w == 0 else 1.0
                    for c in range(C // 128):
                        t = scr[c]
                        scr[c] = (t * cv - _rot_half(t) * sv) * scale
                tb = _chunks_to_rows(scr).astype(BF)
                n_refs[gi][w] = tb
                dh_ref[...] += _dot_nt(tb, w_refs[gi][w])
        xv = x_ref[...]
        h_ref[...] = _rms_fwd(xv, g_ref[...]).astype(BF)
        dx, dg = _rms_bwd(xv, g_ref[...], dh_ref[...])
        dx_ref[...] = dx3_ref[...] + dx
        dg_ref[...] += dg

    dq_specs = [pl.BlockSpec((3, d, tm // d, C), lambda i: (0, 0, i, 0)) for d in DILATIONS]
    nat = pl.BlockSpec((3, tm, C), lambda i: (0, i, 0))
    nshape = jax.ShapeDtypeStruct((3, T, C), BF)
    return pl.pallas_call(
        body, name="qkv_bwd", grid=(T // tm,),
        in_specs=dq_specs + [_row_spec(tm, 128), _row_spec(tm, 128), _row_spec(tm, D), _full_spec((1, D))]
        + [_full_spec((3, D, C))] * 3 + [_row_spec(tm, D)],
        out_specs=[_row_spec(tm, D), nat, nat, nat, _row_spec(tm, D), _full_spec((1, D))],
        out_shape=[jax.ShapeDtypeStruct((T, D), F32), nshape, nshape, nshape, jax.ShapeDtypeStruct((T, D), BF),
                   jax.ShapeDtypeStruct((1, D), F32)],
        scratch_shapes=[pltpu.VMEM((C // 128, tm, 128), F32), pltpu.VMEM((tm, D), F32)],
        compiler_params=_params("arbitrary"),
    )(*dqkvs, cos, sin, x2, g, *w3s, dx3)


def _final_bwd(x, target, g, tm=512):
    T = x.shape[0]

    def body(x_ref, t_ref, g_ref, dx_ref, loss_ref, dg_ref):
        @pl.when(pl.program_id(0) == 0)
        def _():
            loss_ref[...] = jnp.zeros_like(loss_ref)
            dg_ref[...] = jnp.zeros_like(dg_ref)

        xv = x_ref[...]
        gv = g_ref[...]
        diff = _rms_fwd(xv, gv) - t_ref[...]
        loss_ref[...] += 0.5 * jnp.sum(jnp.mean(diff * diff, axis=-1, keepdims=True), axis=0, keepdims=True)
        dx, dg = _rms_bwd(xv, gv, diff * (1.0 / D))
        dx_ref[...] = dx
        dg_ref[...] += dg

    return pl.pallas_call(
        body, name="final_bwd", grid=(T // tm,),
        in_specs=[_row_spec(tm, D), _row_spec(tm, D), _full_spec((1, D))],
        out_specs=[_row_spec(tm, D), _full_spec((1, 1)), _full_spec((1, D))],
        out_shape=[jax.ShapeDtypeStruct((T, D), F32), jax.ShapeDtypeStruct((1, 1), F32), jax.ShapeDtypeStruct((1, D), F32)],
        compiler_params=_params("arbitrary"),
    )(x, target, g)


def _place():
    x, y, c = lax.axis_index("x"), lax.axis_index("y"), lax.axis_index("c")
    return x, y, c


def _allgather(arrs, name):
    n = len(arrs)

    def body(*refs):
        ins, outs = refs[:n], refs[n:2 * n]
        send_sems, recv_sems, local_sems = refs[2 * n:]
        x, y, c = _place()
        me, sibling = (x, y, c), (x, y, 1 - c)
        chips = [(1 - x, y), (x, 1 - y), (1 - x, 1 - y)]

        def slot(a, px, py, pc):
            return outs[a].at[4 * px + 2 * py + pc]

        def copy(a, k, block, to, src=None):
            return pltpu.make_async_remote_copy(
                src_ref=slot(a, *block) if src is None else src, dst_ref=slot(a, *block),
                send_sem=send_sems.at[a, k], recv_sem=recv_sems.at[a, k], device_id=to, device_id_type=MESH)

        mine = [pltpu.make_async_copy(ins[a], slot(a, *me), local_sems.at[a]) for a in range(n)]
        for cp in mine:
            cp.start()
        first = []
        for a in range(n):
            first.append(copy(a, 0, me, sibling, src=ins[a]))
            first += [copy(a, 1 + j, me, (*chip, c), src=ins[a]) for j, chip in enumerate(chips)]
        for cp in first:
            cp.start()
        passed = []
        for a in range(n):
            for j, chip in enumerate(chips):
                copy(a, 1 + j, (*chip, c), me).wait_recv()
                fwd = copy(a, 4 + j, (*chip, c), sibling)
                fwd.start()
                passed.append(fwd)
        for a in range(n):
            copy(a, 0, sibling, me).wait_recv()
            for j, chip in enumerate(chips):
                copy(a, 4 + j, (*chip, 1 - c), me).wait_recv()
        for cp in first + passed:
            cp.wait_send()
        for cp in mine:
            cp.wait()

    hbm = pl.BlockSpec(memory_space=pl.ANY)
    return pl.pallas_call(
        body, name=name,
        in_specs=[hbm] * n, out_specs=[hbm] * n,
        out_shape=[jax.ShapeDtypeStruct((N_DEV,) + a.shape, a.dtype) for a in arrs],
        scratch_shapes=[pltpu.SemaphoreType.DMA((n, 7)), pltpu.SemaphoreType.DMA((n, 7)), pltpu.SemaphoreType.DMA((n,))],
    )(*arrs)


def _peer(k):
    x, y, c = _place()
    px = 1 - x if k & 4 else x
    py = 1 - y if k & 2 else y
    pc = 1 - c if k & 1 else c
    return (px, py, pc), 4 * px + 2 * py + pc


HBM_SPEC = pl.BlockSpec(memory_space=pltpu.HBM)
SEM_SPEC = pl.BlockSpec(memory_space=pltpu.SEMAPHORE)
EFFECT = pltpu.SideEffectType.DATAFLOW_SIDE_EFFECTING


def _exchange_start(arrs, name, same_block=False, dep=None):
    n = len(arrs)
    n_dep = 0 if dep is None else 1

    def body(*refs):
        srcs, lands = refs[:n], refs[n:2 * n]
        send_sems, recv_sems = refs[2 * n + n_dep], refs[2 * n + n_dep + 1]
        token, local_sems = refs[-2], refs[-1]
        x, y, c = _place()
        me = 4 * x + 2 * y + c
        block = (lambda a, j: srcs[a]) if same_block else (lambda a, j: srcs[a].at[j])
        mine = [pltpu.make_async_copy(block(a, me), lands[a].at[me], local_sems.at[a]) for a in range(n)]
        for cp in mine:
            cp.start()
        for cp in mine:
            cp.wait()
        for a in range(n):
            for k in range(1, N_DEV):
                to, to_idx = _peer(k)
                pltpu.make_async_remote_copy(
                    src_ref=block(a, to_idx), dst_ref=lands[a].at[me],
                    send_sem=send_sems.at[a * (N_DEV - 1) + k - 1], recv_sem=recv_sems.at[a * (N_DEV - 1) + k - 1], device_id=to, device_id_type=MESH).start()
        token[...] = jnp.zeros_like(token)

    land_shape = (lambda a: (N_DEV,) + a.shape) if same_block else (lambda a: a.shape)
    src_shapes = [pltpu.HBM(a.shape, a.dtype) for a in arrs]
    land_shapes = [pltpu.HBM(land_shape(a), a.dtype) for a in arrs]
    outs = pl.pallas_call(
        body, name=name,
        out_shape=(pltpu.SemaphoreType.DMA((n * (N_DEV - 1),)), pltpu.SemaphoreType.DMA((n * (N_DEV - 1),)), *src_shapes, *land_shapes,
                   jax.ShapeDtypeStruct((8, 128), F32)),
        in_specs=[HBM_SPEC] * (2 * n) + [pl.BlockSpec(memory_space=pl.ANY)] * n_dep,
        out_specs=(SEM_SPEC, SEM_SPEC, *([HBM_SPEC] * (2 * n)), pl.BlockSpec(memory_space=pltpu.VMEM)),
        input_output_aliases={i: 2 + i for i in range(2 * n)},
        scratch_shapes=[pltpu.SemaphoreType.DMA((n,))],
        compiler_params=pltpu.CompilerParams(has_side_effects=EFFECT),
    )(*[pltpu.with_memory_space_constraint(a, pltpu.HBM) for a in arrs],
      *[pltpu.with_memory_space_constraint(lax.empty(land_shape(a), a.dtype), pltpu.HBM) for a in arrs],
      *([] if dep is None else [dep]))
    return outs[0], outs[1], outs[2:2 + n], outs[2 + n:2 + 2 * n], outs[2 + 2 * n]


def _exchange_wait(send_sems, recv_sems, src_thru, land_thru, after, name):
    n = len(src_thru)
    same_block = src_thru[0].shape != land_thru[0].shape

    def body(*refs):
        srcs, lands = refs[:n], refs[n:2 * n]
        send_sems, recv_sems = refs[2 * n], refs[2 * n + 1]
        for a in range(n):
            for k in range(1, N_DEV):
                frm, frm_idx = _peer(k)
                cp = pltpu.make_async_remote_copy(
                    src_ref=srcs[a] if same_block else srcs[a].at[frm_idx], dst_ref=lands[a].at[frm_idx],
                    send_sem=send_sems.at[a * (N_DEV - 1) + k - 1], recv_sem=recv_sems.at[a * (N_DEV - 1) + k - 1], device_id=frm, device_id_type=MESH)
                cp.wait_send()
                cp.wait_recv()

    outs = pl.pallas_call(
        body, name=name,
        out_shape=tuple(pltpu.HBM(a.shape, a.dtype) for a in (*src_thru, *land_thru)),
        in_specs=[HBM_SPEC] * (2 * n) + [SEM_SPEC, SEM_SPEC, pl.BlockSpec(memory_space=pl.ANY)],
        out_specs=[HBM_SPEC] * (2 * n),
        input_output_aliases={i: i for i in range(2 * n)},
        compiler_params=pltpu.CompilerParams(has_side_effects=EFFECT),
    )(*src_thru, *land_thru, send_sems, recv_sems, after)
    return outs[n:]


def _row_tile(rows):
    for t in (256, 176, 128, 8):
        if rows % t == 0:
            return t
    return rows


def _sum_parts(parts, name):
    K, R, C = parts.shape
    tr = _row_tile(R)

    def body(p_ref, o_ref):
        g = p_ref[0].astype(F32)
        for k in range(1, K):
            g = g + p_ref[k].astype(F32)
        o_ref[...] = g

    return pl.pallas_call(
        body, name=name, grid=(R // tr,),
        in_specs=[pl.BlockSpec((K, tr, C), lambda i: (0, i, 0))],
        out_specs=_row_spec(tr, C),
        out_shape=jax.ShapeDtypeStruct((R, C), F32),
        compiler_params=_params("parallel"),
    )(parts)


def _adamw(parts, w, m, v, name):
    K, R, C = parts.shape
    tr = _row_tile(R)

    def body(p_ref, w_ref, m_ref, v_ref, g_ref, d_ref, nm_ref, nv_ref):
        g = p_ref[0].astype(F32)
        for k in range(1, K):
            g = g + p_ref[k].astype(F32)
        nm = ADAM_B1 * m_ref[...] + (1.0 - ADAM_B1) * g
        nv = ADAM_B2 * v_ref[...] + (1.0 - ADAM_B2) * jnp.square(g)
        m_hat = nm / (1.0 - ADAM_B1 ** ADAM_STEP)
        v_hat = nv / (1.0 - ADAM_B2 ** ADAM_STEP)
        g_ref[...] = g
        d_ref[...] = -ADAM_LR * (m_hat / (jnp.sqrt(v_hat) + ADAM_EPS) + ADAM_WD * w_ref[...])
        nm_ref[...] = nm
        nv_ref[...] = nv

    blk = _row_spec(tr, C)
    shp = jax.ShapeDtypeStruct((R, C), F32)
    return pl.pallas_call(
        body, name=name, grid=(R // tr,),
        in_specs=[pl.BlockSpec((K, tr, C), lambda i: (0, i, 0)), blk, blk, blk],
        out_specs=[blk] * 4,
        out_shape=[shp] * 4,
        compiler_params=_params("parallel"),
    )(parts, w, m, v)


def _rope_tables(T):
    inv_freq = 1.0 / (ROPE_THETA ** (jnp.arange(0, HEAD_DIM, 2, dtype=F32) / HEAD_DIM))
    ang = jnp.arange(T, dtype=F32)[:, None] * inv_freq[None, :]
    ang = jnp.concatenate([ang, ang, ang, ang], axis=-1)
    return jnp.cos(ang), jnp.sin(ang)


def _pad_lanes(a, n):
    return jnp.pad(a, ((0, 0),) * (a.ndim - 1) + ((0, n - a.shape[-1]),))


LAYER0 = ("pool_in", "pool_grp", "pool_out", "gate0", "up0", "down0")
LAYER1 = ("qkv", "attn_out", "gate1", "up1", "down1")


def _layout_weights(gw):
    w = {}
    if "pool_in" in gw:
        w["pool_in"] = gw["pool_in"].reshape(D, D)
        w["pool_grp"] = jnp.transpose(gw["pool_grp"], (1, 0, 2, 3)).reshape(4, POOL_GC, POOL_GC)
        w["pool_out"] = gw["pool_out"].reshape(D, D)
    if "qkv" in gw:
        wqkv = jnp.transpose(gw["qkv"], (1, 0, 2)).reshape(D, 3 * D)
        wo = gw["attn_out"].reshape(D, D)
        w["qkv"], w["attn_out"] = [], []
        for nh, off in zip(HEAD_GROUPS, HEAD_OFFS):
            lo, n = off * HEAD_DIM, nh * HEAD_DIM
            w["qkv"].append(jnp.stack([_pad_lanes(wqkv[:, k * D + lo:k * D + lo + n], GROUP_LANES) for k in range(3)]))
            w["attn_out"].append(jnp.pad(wo[lo:lo + n], ((0, GROUP_LANES - n), (0, 0))))
    for nm in gw:
        if nm[:-1] in ("gate", "up", "down"):
            w[nm] = gw[nm].reshape(F, D)
    return w


def _local_step(x, target, w, layer1, dep0, norm_mix, norm_ffn, norm_final, pool_scale, emit):
    T = x.shape[0]
    cos, sin = _rope_tables(T)
    nm = [norm_mix[i:i + 1] for i in range(2)]
    nf = [norm_ffn[i:i + 1] for i in range(2)]
    nfin = norm_final.reshape(1, D)

    h0 = _norm_fwd(x, nm[0], "norm_mix0", dep=dep0)
    x1, p = _pool_fwd(x, h0, w["pool_in"], w["pool_grp"], pool_scale, w["pool_out"])
    h1 = _norm_fwd(x1, nf[0], "norm_ffn0")
    x2, a0, b0, s0 = _ffn_fwd(x1, h1, w["gate0"], w["up0"], w["down0"], "ffn_fwd0")
    h2 = _norm_fwd(x2, nm[1], "norm_mix1")
    w = {**w, **layer1(h2)}
    qkvs = [_qkv_fwd(h2, w["qkv"][g], cos, sin, DILATIONS[g], f"qkv_fwd{g}") for g in range(3)]
    att = [_attn_fwd(qkvs[g], HEAD_GROUPS[g], f"attn_fwd{g}") for g in range(3)]
    os_, sts = [a[0] for a in att], [a[1] for a in att]
    x3, m0, m1, m2, alpha = _attn_out_fwd(x2, os_, sts, w["attn_out"])
    h3 = _norm_fwd(x3, nf[1], "norm_ffn1")
    x4, a1, b1, s1 = _ffn_fwd(x3, h3, w["gate1"], w["up1"], w["down1"], "ffn_fwd1")

    g = {}
    dx4, loss, dg_final = _final_bwd(x4, target, nfin)
    dx3, da1, db1, _, dg_ffn1 = _ffn_bwd(dx4, x3, nf[1], a1, b1, w["gate1"], w["up1"], w["down1"], "ffn_bwd1")
    g["down1"], g["gate1"], g["up1"] = _ffn_wgrad(s1, da1, db1, dx4, h3, "ffn_wgrad1")
    dep = emit("ffn1", g)
    dos_and_stats = _attn_out_bwd(dx3, os_, sts, alpha, w["attn_out"], dep=dep)
    dos, dsts = dos_and_stats[:3], dos_and_stats[3:]
    g["attn_out"] = [_wgrad(m, dx3, f"attn_out_wgrad{i}") for i, m in enumerate((m0, m1, m2))]
    dqkvs = [_attn_bwd(qkvs[gi], dos[gi], sts[gi], dsts[gi], HEAD_GROUPS[gi], f"attn_bwd{gi}") for gi in range(3)]
    dx2, n0, n1, n2, h2b, dg_mix1 = _qkv_bwd(dqkvs, cos, sin, x2, nm[1], w["qkv"], dx3)
    g["qkv"] = [_wgrad_stack(h2b, n, f"qkv_wgrad{i}") for i, n in enumerate((n0, n1, n2))]
    dep = emit("attn", g)
    dx1, da0, db0, _, dg_ffn0 = _ffn_bwd(dx2, x1, nf[0], a0, b0, w["gate0"], w["up0"], w["down0"], "ffn_bwd0", dep=dep)
    g["down0"], g["gate0"], g["up0"] = _ffn_wgrad(s0, da0, db0, dx2, h1, "ffn_wgrad0")
    dep = emit("ffn0", g)
    dx0, z, dzp, du, h0b, dscale, dg_mix0 = _pool_bwd(dx1, x, nm[0], p, w["pool_in"], w["pool_grp"], pool_scale, w["pool_out"], dep=dep)
    g["pool_out"] = _wgrad(z, dx1, "pool_out_wgrad")
    g["pool_in"] = _wgrad(h0b, du, "pool_in_wgrad")
    g["pool_grp"] = _wgrad_pool_groups(p, dzp)
    emit("pool", g)

    small = jnp.concatenate([dg_mix0, dg_mix1, dg_ffn0, dg_ffn1, dg_final, dscale,
                             jnp.broadcast_to(loss, (1, D)), jnp.zeros((1, D), F32)], axis=0)
    return dx0, small


GROUPS = {"ffn1": ("down1", "gate1", "up1"), "attn": ("qkv", "attn_out"), "ffn0": ("down0", "gate0", "up0"),
          "pool": ("pool_in", "pool_out", "pool_grp")}


def _grad_blocks(group, g):
    blocks = {}
    if group == "pool":
        blocks["pool_in"] = g["pool_in"].reshape(N_DEV, D // N_DEV, D)
        blocks["pool_out"] = g["pool_out"].reshape(N_DEV, D // N_DEV, D)
        blocks["pool_grp"] = jnp.transpose(g["pool_grp"].reshape(4, N_DEV, POOL_GC // N_DEV, POOL_GC), (1, 0, 2, 3)).reshape(N_DEV, 4 * POOL_GC // N_DEV, POOL_GC)
    elif group == "attn":
        wo = jnp.concatenate([gw[:nh * HEAD_DIM] for gw, nh in zip(g["attn_out"], HEAD_GROUPS)], axis=0)
        blocks["attn_out"] = wo.reshape(N_DEV, D // N_DEV, D)
        wqkv = jnp.concatenate([g["qkv"][gi][k][:, :HEAD_GROUPS[gi] * HEAD_DIM] for k in range(3) for gi in range(3)], axis=1)
        blocks["qkv"] = jnp.transpose(wqkv.reshape(D, N_DEV, 3 * D // N_DEV), (1, 0, 2))
    else:
        for nm in GROUPS[group]:
            blocks[nm] = g[nm].reshape(N_DEV, F // N_DEV, D)
    return [blocks[nm] for nm in GROUPS[group]]


def kernel(x, norm_mix, norm_ffn, norm_final, pool_w_in, pool_w_group, pool_scale, pool_w_out, attn_w_qkv, attn_w_out, ffn_w_gate, ffn_w_up, ffn_w_down, loss_target, m_norm_mix, m_norm_ffn, m_norm_final, m_pool_w_in, m_pool_w_group, m_pool_scale, m_pool_w_out, m_attn_w_qkv, m_attn_w_out, m_ffn_w_gate, m_ffn_w_up, m_ffn_w_down, v_norm_mix, v_norm_ffn, v_norm_final, v_pool_w_in, v_pool_w_group, v_pool_scale, v_pool_w_out, v_attn_w_qkv, v_attn_w_out, v_ffn_w_gate, v_ffn_w_up, v_ffn_w_down):
    shard = {
        "pool_in": pool_w_in[0], "pool_grp": pool_w_group[0], "pool_out": pool_w_out[0],
        "qkv": attn_w_qkv[0], "attn_out": attn_w_out[0],
    }
    for l in range(2):
        shard[f"gate{l}"] = ffn_w_gate[l].T
        shard[f"up{l}"] = ffn_w_up[l].T
        shard[f"down{l}"] = ffn_w_down[l]
    shard = {k: v.astype(BF) for k, v in shard.items()}
    w0 = _layout_weights(dict(zip(LAYER0, _allgather([shard[k] for k in LAYER0], "weights_allgather"))))
    l1 = _exchange_start([shard[k] for k in LAYER1], "weights_start_layer1", same_block=True, dep=w0["pool_in"])

    def layer1(after):
        return _layout_weights(dict(zip(LAYER1, _exchange_wait(*l1[:4], after, "weights_wait_layer1"))))

    started = {}

    def emit(group, g):
        started[group] = _exchange_start(_grad_blocks(group, g), f"grads_start_{group}")
        return started[group][4]

    grad_x, small = _local_step(x[0], loss_target[0], w0, layer1, l1[4], norm_mix, norm_ffn, norm_final, pool_scale, emit)

    received = {}
    after = started["pool"][4]
    for group, (send_sems, recv_sems, src_thru, land_thru, _) in started.items():
        lands = _exchange_wait(send_sems, recv_sems, src_thru, land_thru, after, f"grads_wait_{group}")
        received.update(zip(GROUPS[group], lands))
    small_all = _allgather([small], "small_allgather")[0]

    def upd(parts, wt, mt, vt, name):
        shape = wt.shape
        r2 = lambda t: t.reshape(parts.shape[1:])
        outs = _adamw(parts, r2(wt), r2(mt), r2(vt), name)
        return [o.reshape(shape) for o in outs]

    res = {}
    res["pool_w_in"] = upd(received["pool_in"], pool_w_in, m_pool_w_in, v_pool_w_in, "adamw_pool_in")
    res["pool_w_group"] = upd(received["pool_grp"], pool_w_group, m_pool_w_group, v_pool_w_group, "adamw_pool_grp")
    res["pool_w_out"] = upd(received["pool_out"], pool_w_out, m_pool_w_out, v_pool_w_out, "adamw_pool_out")
    res["attn_w_qkv"] = upd(received["qkv"], attn_w_qkv, m_attn_w_qkv, v_attn_w_qkv, "adamw_qkv")
    res["attn_w_out"] = upd(received["attn_out"], attn_w_out, m_attn_w_out, v_attn_w_out, "adamw_attn_out")
    for nm, wt, mt, vt in (("gate", ffn_w_gate, m_ffn_w_gate, v_ffn_w_gate), ("up", ffn_w_up, m_ffn_w_up, v_ffn_w_up)):
        per_layer = []
        for l in range(2):
            gt = _sum_parts(received[f"{nm}{l}"], f"sum_{nm}{l}").T
            per_layer.append(_adamw(gt[None], wt[l], mt[l], vt[l], f"adamw_{nm}{l}"))
        res[f"ffn_w_{nm}"] = [jnp.stack([per_layer[0][i], per_layer[1][i]]) for i in range(4)]
    per_layer = [_adamw(received[f"down{l}"], ffn_w_down[l], m_ffn_w_down[l], v_ffn_w_down[l], f"adamw_down{l}") for l in range(2)]
    res["ffn_w_down"] = [jnp.stack([per_layer[0][i], per_layer[1][i]]) for i in range(4)]

    small_w = jnp.concatenate([norm_mix, norm_ffn, norm_final[None], pool_scale, jnp.zeros((2, D), F32)], axis=0)
    small_m = jnp.concatenate([m_norm_mix, m_norm_ffn, m_norm_final[None], m_pool_scale, jnp.zeros((2, D), F32)], axis=0)
    small_v = jnp.concatenate([v_norm_mix, v_norm_ffn, v_norm_final[None], v_pool_scale, jnp.ones((2, D), F32)], axis=0)
    sg, sd, sm, sv = _adamw(small_all, small_w, small_m, small_v, "adamw_small")
    loss = sg[6, 0]
    res["norm_mix"] = [t[0:2] for t in (sg, sd, sm, sv)]
    res["norm_ffn"] = [t[2:4] for t in (sg, sd, sm, sv)]
    res["norm_final"] = [t[4] for t in (sg, sd, sm, sv)]
    res["pool_scale"] = [t[5:6] for t in (sg, sd, sm, sv)]

    order = ["norm_mix", "norm_ffn", "norm_final", "pool_w_in", "pool_w_group", "pool_scale", "pool_w_out",
             "attn_w_qkv", "attn_w_out", "ffn_w_gate", "ffn_w_up", "ffn_w_down"]
    return (loss, grad_x[None], *[res[k][0] for k in order], *[res[k][1] for k in order],
            *[res[k][2] for k in order], *[res[k][3] for k in order])
```

```python
import math

import jax
import jax.numpy as jnp
from jax import lax
from jax.experimental import pallas as pl
from jax.experimental.pallas import tpu as pltpu

D = 1024
F = 2816
N_DEV = 8
EPS = 1e-6
NEG_INF = -1e30
POOL_WINDOWS = (2, 4, 8, 16)
POOL_HALO = 16
POOL_GC = 256
HEAD_DIM = 64
HEAD_GROUPS = (6, 5, 5)
HEAD_OFFS = (0, 6, 11)
DILATIONS = (1, 4, 16)
ATT_W = 128
GROUP_LANES = 384
LSE_GROUP_LANE = 8
ROPE_THETA = 10000.0
ADAM_LR, ADAM_B1, ADAM_B2, ADAM_EPS, ADAM_WD, ADAM_STEP = 0.001, 0.9, 0.999, 1e-08, 0.01, 10

BF = jnp.bfloat16
F32 = jnp.float32
VMEM_LIMIT = 56 * 1024 * 1024
MESH = pl.DeviceIdType.MESH


def _params(*sem):
    return pltpu.CompilerParams(dimension_semantics=sem, vmem_limit_bytes=VMEM_LIMIT)


def _dot(a, b):
    return jnp.dot(a, b, preferred_element_type=F32)


def _dot_nt(a, b):
    return lax.dot_general(a, b, (((1,), (1,)), ((), ())), preferred_element_type=F32)


def _dot_tn(a, b):
    return lax.dot_general(a, b, (((0,), (0,)), ((), ())), preferred_element_type=F32)


def _rms_fwd(xv, g):
    r = lax.rsqrt(jnp.mean(xv * xv, axis=-1, keepdims=True) + EPS)
    return (xv * r) * g


def _rms_bwd(xv, g, dh):
    r = lax.rsqrt(jnp.mean(xv * xv, axis=-1, keepdims=True) + EPS)
    xhat = xv * r
    dg = jnp.sum(dh * xhat, axis=0, keepdims=True)
    dxh = dh * g
    dx = r * (dxh - xhat * jnp.mean(dxh * xhat, axis=-1, keepdims=True))
    return dx, dg


def _lane_col(tile, j):
    lane = lax.broadcasted_iota(jnp.int32, tile.shape, 1)
    return jnp.sum(jnp.where(lane == j, tile, 0.0), axis=-1, keepdims=True)


def _row_spec(tm, n):
    return pl.BlockSpec((tm, n), lambda i: (i, 0))


def _full_spec(shape):
    nd = len(shape)
    return pl.BlockSpec(shape, lambda *_: (0,) * nd)


def _after(dep, body, in_specs, args):
    if dep is None:
        return body, list(in_specs), list(args)

    def body_after(dep_ref, *refs):
        body(*refs)

    return body_after, [pl.BlockSpec(memory_space=pl.ANY)] + list(in_specs), [dep] + list(args)


def _norm_fwd(x, g, name, tm=512, dep=None):
    T = x.shape[0]

    def body(x_ref, g_ref, h_ref):
        h_ref[...] = _rms_fwd(x_ref[...], g_ref[...]).astype(BF)

    body, in_specs, args = _after(dep, body, [_row_spec(tm, D), _full_spec((1, D))], [x, g])
    return pl.pallas_call(
        body, name=name, grid=(T // tm,),
        in_specs=in_specs,
        out_specs=_row_spec(tm, D),
        out_shape=jax.ShapeDtypeStruct((T, D), BF),
        compiler_params=_params("parallel"),
    )(*args)


def _pool_fwd(x, h, w_in, w_grp, scale, w_out, tm=512):
    T = x.shape[0]
    n = tm + POOL_HALO

    def body(x_ref, h_ref, win_ref, wg_ref, sc_ref, wout_ref, x1_ref, p_ref, tail_ref, z_ref):
        i = pl.program_id(0)

        @pl.when(i == 0)
        def _():
            tail_ref[...] = jnp.zeros_like(tail_ref)

        u = _dot(h_ref[...], win_ref[...])
        pos = i * tm + lax.broadcasted_iota(jnp.int32, (tm, 1), 0)
        for g, w in enumerate(POOL_WINDOWS):
            sl = slice(g * POOL_GC, (g + 1) * POOL_GC)
            ug = u[:, sl]
            s = jnp.concatenate([tail_ref[:, sl], ug], axis=0)
            step = 1
            while step < w:
                s = s + pltpu.roll(s, step, 0)
                step *= 2
            cnt = jnp.minimum(pos + 1, w).astype(F32)
            pg = (s[POOL_HALO:, :] / cnt - ug).astype(BF)
            p_ref[:, sl] = pg
            z_ref[:, sl] = (_dot(pg, wg_ref[g]) * sc_ref[:, sl]).astype(BF)
        tail_ref[...] = u[tm - POOL_HALO:, :]
        x1_ref[...] = x_ref[...] + _dot(z_ref[...], wout_ref[...])

    return pl.pallas_call(
        body, name="pool_fwd", grid=(T // tm,),
        in_specs=[_row_spec(tm, D), _row_spec(tm, D), _full_spec((D, D)), _full_spec((4, POOL_GC, POOL_GC)),
                  _full_spec((1, D)), _full_spec((D, D))],
        out_specs=[_row_spec(tm, D), _row_spec(tm, D)],
        out_shape=[jax.ShapeDtypeStruct((T, D), F32), jax.ShapeDtypeStruct((T, D), BF)],
        scratch_shapes=[pltpu.VMEM((POOL_HALO, D), F32), pltpu.VMEM((tm, D), BF)],
        compiler_params=_params("arbitrary"),
    )(x, h, w_in, w_grp, scale, w_out)


def _pool_bwd(dx1, x0, g0, p, w_in, w_grp, scale, w_out, tm=512, dep=None):
    T = x0.shape[0]
    nt = T // tm
    n = tm + POOL_HALO
    rev = lambda i: (nt - 1 - i, 0)

    def body(dx1_ref, x0_ref, g_ref, p_ref, win_ref, wg_ref, sc_ref, wout_ref,
             dx0_ref, z_ref, dzp_ref, du_ref, h0_ref, dsc_ref, dg_ref, head_ref):
        i = pl.program_id(0)

        @pl.when(i == 0)
        def _():
            head_ref[...] = jnp.zeros_like(head_ref)
            dsc_ref[...] = jnp.zeros_like(dsc_ref)
            dg_ref[...] = jnp.zeros_like(dg_ref)

        dx1v = dx1_ref[...]
        dz = _dot_nt(dx1v.astype(BF), wout_ref[...])
        pos = (nt - 1 - i) * tm + lax.broadcasted_iota(jnp.int32, (tm, 1), 0)
        for g, w in enumerate(POOL_WINDOWS):
            sl = slice(g * POOL_GC, (g + 1) * POOL_GC)
            zpre = _dot(p_ref[:, sl], wg_ref[g])
            dzg = dz[:, sl]
            dsc_ref[:, sl] += jnp.sum(dzg * zpre, axis=0, keepdims=True)
            z_ref[:, sl] = (zpre * sc_ref[:, sl]).astype(BF)
            dzp = (dzg * sc_ref[:, sl]).astype(BF)
            dzp_ref[:, sl] = dzp
            dp = _dot_nt(dzp, wg_ref[g])
            cnt = jnp.minimum(pos + 1, w).astype(F32)
            dpc = dp / cnt
            s = jnp.concatenate([dpc, head_ref[:, sl]], axis=0)
            step = 1
            while step < w:
                s = s + pltpu.roll(s, n - step, 0)
                step *= 2
            head_ref[:, sl] = dpc[:POOL_HALO, :]
            du_ref[:, sl] = (s[:tm, :] - dp).astype(BF)
        dh0 = _dot_nt(du_ref[...], win_ref[...])
        x0v = x0_ref[...]
        h0_ref[...] = _rms_fwd(x0v, g_ref[...]).astype(BF)
        dx, dg = _rms_bwd(x0v, g_ref[...], dh0)
        dx0_ref[...] = dx1v + dx
        dg_ref[...] += dg

    bf_rows = jax.ShapeDtypeStruct((T, D), BF)
    vec = jax.ShapeDtypeStruct((1, D), F32)
    body, in_specs, args = _after(
        dep, body,
        [pl.BlockSpec((tm, D), rev), pl.BlockSpec((tm, D), rev), _full_spec((1, D)), pl.BlockSpec((tm, D), rev),
         _full_spec((D, D)), _full_spec((4, POOL_GC, POOL_GC)), _full_spec((1, D)), _full_spec((D, D))],
        [dx1, x0, g0, p, w_in, w_grp, scale, w_out])
    return pl.pallas_call(
        body, name="pool_bwd", grid=(nt,),
        in_specs=in_specs,
        out_specs=[pl.BlockSpec((tm, D), rev)] * 5 + [_full_spec((1, D))] * 2,
        out_shape=[jax.ShapeDtypeStruct((T, D), F32), bf_rows, bf_rows, bf_rows, bf_rows, vec, vec],
        scratch_shapes=[pltpu.VMEM((POOL_HALO, D), F32)],
        compiler_params=_params("arbitrary"),
    )(*args)


def _ffn_fwd(x, h, wg_t, wu_t, wd, name, tm=512, fk=1408):
    T = x.shape[0]

    def body(x_ref, h_ref, wg_ref, wu_ref, wd_ref, xo_ref, a_ref, b_ref, s_ref, acc_ref):
        k = pl.program_id(1)

        @pl.when(k == 0)
        def _():
            acc_ref[...] = jnp.zeros_like(acc_ref)

        hv = h_ref[...]
        a = _dot_nt(hv, wg_ref[...])
        b = _dot_nt(hv, wu_ref[...])
        s = ((a * jax.nn.sigmoid(a)) * b).astype(BF)
        a_ref[...] = a.astype(BF)
        b_ref[...] = b.astype(BF)
        s_ref[...] = s
        acc_ref[...] += _dot(s, wd_ref[...])

        @pl.when(k == pl.num_programs(1) - 1)
        def _():
            xo_ref[...] = x_ref[...] + acc_ref[...]

    row = pl.BlockSpec((tm, D), lambda i, k: (i, 0))
    wsp = pl.BlockSpec((fk, D), lambda i, k: (k, 0))
    act = pl.BlockSpec((tm, fk), lambda i, k: (i, k))
    act_shape = jax.ShapeDtypeStruct((T, F), BF)
    return pl.pallas_call(
        body, name=name, grid=(T // tm, F // fk),
        in_specs=[row, row, wsp, wsp, wsp],
        out_specs=[row, act, act, act],
        out_shape=[jax.ShapeDtypeStruct((T, D), F32), act_shape, act_shape, act_shape],
        scratch_shapes=[pltpu.VMEM((tm, D), F32)],
        compiler_params=_params("parallel", "arbitrary"),
    )(x, h, wg_t, wu_t, wd)


def _ffn_bwd(dxo, x_in, g, a, b, wg_t, wu_t, wd, name, tm=512, fk=1408, dep=None):
    T = x_in.shape[0]

    def body(dxo_ref, x_ref, g_ref, a_ref, b_ref, wg_ref, wu_ref, wd_ref,
             dx_ref, da_ref, db_ref, h_ref, dg_ref, dy_ref, dh_ref):
        i = pl.program_id(0)
        k = pl.program_id(1)

        @pl.when(jnp.logical_and(i == 0, k == 0))
        def _():
            dg_ref[...] = jnp.zeros_like(dg_ref)

        @pl.when(k == 0)
        def _():
            h_ref[...] = _rms_fwd(x_ref[...], g_ref[...]).astype(BF)
            dy_ref[...] = dxo_ref[...].astype(BF)
            dh_ref[...] = jnp.zeros_like(dh_ref)

        ds = _dot_nt(dy_ref[...], wd_ref[...])
        av = a_ref[...].astype(F32)
        bv = b_ref[...].astype(F32)
        sig = jax.nn.sigmoid(av)
        db = (ds * (av * sig)).astype(BF)
        da = (ds * bv * (sig * (1.0 + av * (1.0 - sig)))).astype(BF)
        da_ref[...] = da
        db_ref[...] = db
        dh_ref[...] += _dot(da, wg_ref[...]) + _dot(db, wu_ref[...])

        @pl.when(k == pl.num_programs(1) - 1)
        def _():
            dx, dg = _rms_bwd(x_ref[...], g_ref[...], dh_ref[...])
            dx_ref[...] = dxo_ref[...] + dx
            dg_ref[...] += dg

    row = pl.BlockSpec((tm, D), lambda i, k: (i, 0))
    wsp = pl.BlockSpec((fk, D), lambda i, k: (k, 0))
    act = pl.BlockSpec((tm, fk), lambda i, k: (i, k))
    vec = pl.BlockSpec((1, D), lambda i, k: (0, 0))
    act_shape = jax.ShapeDtypeStruct((T, F), BF)
    body, in_specs, args = _after(dep, body, [row, row, vec, act, act, wsp, wsp, wsp], [dxo, x_in, g, a, b, wg_t, wu_t, wd])
    return pl.pallas_call(
        body, name=name, grid=(T // tm, F // fk),
        in_specs=in_specs,
        out_specs=[row, act, act, row, vec],
        out_shape=[jax.ShapeDtypeStruct((T, D), F32), act_shape, act_shape, jax.ShapeDtypeStruct((T, D), BF),
                   jax.ShapeDtypeStruct((1, D), F32)],
        scratch_shapes=[pltpu.VMEM((tm, D), BF), pltpu.VMEM((tm, D), F32)],
        compiler_params=_params("arbitrary", "arbitrary"),
    )(*args)


def _ffn_wgrad(s, da, db, dxo, h, name, tk=256, fm=1408):
    T = h.shape[0]

    def body(s_ref, da_ref, db_ref, dxo_ref, h_ref, gd_ref, gg_ref, gu_ref, ad_ref, ag_ref, au_ref):
        t = pl.program_id(1)

        @pl.when(t == 0)
        def _():
            ad_ref[...] = jnp.zeros_like(ad_ref)
            ag_ref[...] = jnp.zeros_like(ag_ref)
            au_ref[...] = jnp.zeros_like(au_ref)

        hv = h_ref[...]
        ad_ref[...] += _dot_tn(s_ref[...], dxo_ref[...].astype(BF))
        ag_ref[...] += _dot_tn(da_ref[...], hv)
        au_ref[...] += _dot_tn(db_ref[...], hv)

        @pl.when(t == pl.num_programs(1) - 1)
        def _():
            gd_ref[...] = ad_ref[...].astype(BF)
            gg_ref[...] = ag_ref[...].astype(BF)
            gu_ref[...] = au_ref[...].astype(BF)

    act = pl.BlockSpec((tk, fm), lambda m, t: (t, m))
    row = pl.BlockSpec((tk, D), lambda m, t: (t, 0))
    out = pl.BlockSpec((fm, D), lambda m, t: (m, 0))
    shp = jax.ShapeDtypeStruct((F, D), BF)
    return pl.pallas_call(
        body, name=name, grid=(F // fm, T // tk),
        in_specs=[act, act, act, row, row],
        out_specs=[out, out, out],
        out_shape=[shp, shp, shp],
        scratch_shapes=[pltpu.VMEM((fm, D), F32)] * 3,
        compiler_params=_params("parallel", "arbitrary"),
    )(s, da, db, dxo, h)


def _wgrad(a, b, name, tk=512):
    T, M = a.shape
    N = b.shape[1]

    def body(a_ref, b_ref, o_ref, acc_ref):
        t = pl.program_id(0)

        @pl.when(t == 0)
        def _():
            acc_ref[...] = jnp.zeros_like(acc_ref)

        acc_ref[...] += _dot_tn(a_ref[...].astype(BF), b_ref[...].astype(BF))

        @pl.when(t == pl.num_programs(0) - 1)
        def _():
            o_ref[...] = acc_ref[...].astype(BF)

    return pl.pallas_call(
        body, name=name, grid=(T // tk,),
        in_specs=[_row_spec(tk, M), _row_spec(tk, N)],
        out_specs=_full_spec((M, N)),
        out_shape=jax.ShapeDtypeStruct((M, N), BF),
        scratch_shapes=[pltpu.VMEM((M, N), F32)],
        compiler_params=_params("arbitrary"),
    )(a, b)


def _wgrad_pool_groups(p, dzp, tk=512):
    T = p.shape[0]

    def body(p_ref, d_ref, o_ref, acc_ref):
        t = pl.program_id(1)

        @pl.when(t == 0)
        def _():
            acc_ref[...] = jnp.zeros_like(acc_ref)

        acc_ref[...] += _dot_tn(p_ref[...], d_ref[...])

        @pl.when(t == pl.num_programs(1) - 1)
        def _():
            o_ref[...] = acc_ref[...].astype(BF)

    blk = pl.BlockSpec((tk, POOL_GC), lambda g, t: (t, g))
    return pl.pallas_call(
        body, name="pool_wgrad_groups", grid=(4, T // tk),
        in_specs=[blk, blk],
        out_specs=pl.BlockSpec((None, POOL_GC, POOL_GC), lambda g, t: (g, 0, 0)),
        out_shape=jax.ShapeDtypeStruct((4, POOL_GC, POOL_GC), BF),
        scratch_shapes=[pltpu.VMEM((POOL_GC, POOL_GC), F32)],
        compiler_params=_params("parallel", "arbitrary"),
    )(p, dzp)


def _wgrad_stack(a, b3, name, tk=512):
    T, M = a.shape
    N = b3.shape[2]

    def body(a_ref, b_ref, o_ref, acc_ref):
        t = pl.program_id(0)

        @pl.when(t == 0)
        def _():
            acc_ref[...] = jnp.zeros_like(acc_ref)

        av = a_ref[...]
        for w in range(3):
            acc_ref[w] += _dot_tn(av, b_ref[w])

        @pl.when(t == pl.num_programs(0) - 1)
        def _():
            o_ref[...] = acc_ref[...].astype(BF)

    return pl.pallas_call(
        body, name=name, grid=(T // tk,),
        in_specs=[_row_spec(tk, M), pl.BlockSpec((3, tk, N), lambda t: (0, t, 0))],
        out_specs=_full_spec((3, M, N)),
        out_shape=jax.ShapeDtypeStruct((3, M, N), BF),
        scratch_shapes=[pltpu.VMEM((3, M, N), F32)],
        compiler_params=_params("arbitrary"),
    )(a, b3)


def _rot_half(t):
    lane = lax.broadcasted_iota(jnp.int32, t.shape, 1)
    first = (lane % HEAD_DIM) < (HEAD_DIM // 2)
    return jnp.where(first, -pltpu.roll(t, 128 - HEAD_DIM // 2, 1), pltpu.roll(t, HEAD_DIM // 2, 1))


def _scatter_rows(dst_ref, scr_ref, d, cast):
    nc, rows, _ = scr_ref.shape
    n = rows // d
    for c in range(nc):
        sl = slice(c * 128, (c + 1) * 128)
        for r in range(d):
            src = scr_ref[c] if d == 1 else scr_ref.at[c][pl.ds(r, n, stride=d), :]
            dst_ref[r, :, sl] = src.astype(cast)


def _gather_rows(scr_ref, src_ref, d):
    nc, rows, _ = scr_ref.shape
    n = rows // d
    for c in range(nc):
        sl = slice(c * 128, (c + 1) * 128)
        for r in range(d):
            val = src_ref[r, :, sl].astype(F32)
            if d == 1:
                scr_ref[c] = val
            else:
                scr_ref.at[c][pl.ds(r, n, stride=d), :] = val


def _chunks_to_rows(scr_ref):
    nc = scr_ref.shape[0]
    return scr_ref[0] if nc == 1 else jnp.concatenate([scr_ref[c] for c in range(nc)], axis=1)


def _rows_to_chunks(scr_ref, val):
    for c in range(scr_ref.shape[0]):
        scr_ref[c] = val[:, c * 128:(c + 1) * 128]


def _qkv_fwd(h, w3, cos, sin, d, name, tm=512):
    T = h.shape[0]
    L = T // d
    C = GROUP_LANES

    def body(h_ref, w_ref, cos_ref, sin_ref, o_ref, scr_ref):
        w = pl.program_id(1)
        _rows_to_chunks(scr_ref, _dot(h_ref[...], w_ref[...]))

        @pl.when(w < 2)
        def _():
            scale = jnp.where(w == 0, HEAD_DIM ** -0.5, 1.0).astype(F32)
            cv = cos_ref[...]
            sv = sin_ref[...]
            for c in range(C // 128):
                t = scr_ref[c]
                scr_ref[c] = (t * cv + _rot_half(t) * sv) * scale

        _scatter_rows(o_ref, scr_ref, d, BF)

    return pl.pallas_call(
        body, name=name, grid=(T // tm, 3),
        in_specs=[pl.BlockSpec((tm, D), lambda i, w: (i, 0)), pl.BlockSpec((None, D, C), lambda i, w: (w, 0, 0)),
                  pl.BlockSpec((tm, 128), lambda i, w: (i, 0)), pl.BlockSpec((tm, 128), lambda i, w: (i, 0))],
        out_specs=pl.BlockSpec((None, d, tm // d, C), lambda i, w: (w, 0, i, 0)),
        out_shape=jax.ShapeDtypeStruct((3, d, L, C), BF),
        scratch_shapes=[pltpu.VMEM((C // 128, tm, 128), F32)],
        compiler_params=_params("parallel", "arbitrary"),
    )(h, w3, cos, sin)


def _att_chunk(L):
    return min(L, 1024)


def _head_mask(shape, h):
    lane = lax.broadcasted_iota(jnp.int32, shape, 1)
    return (lane // HEAD_DIM) == (h % 2)


def _attn_fwd(qkv, nh, name):
    _, d, L, C = qkv.shape
    lc = _att_chunk(L)
    nblk = lc // ATT_W

    def body(q_ref, k_ref, kh_ref, v_ref, vh_ref, o_ref, st_ref):
        i = pl.program_id(1)
        qi = lax.broadcasted_iota(jnp.int32, (ATT_W, 2 * ATT_W), 0)
        kj = lax.broadcasted_iota(jnp.int32, (ATT_W, 2 * ATT_W), 1)
        band = jnp.logical_and(kj >= qi, kj <= qi + ATT_W)
        lane = lax.broadcasted_iota(jnp.int32, (ATT_W, 128), 1)

        def block(row0, kc, vc, mask):
            rows = pl.ds(row0, ATT_W)
            lses = []
            for hp in range(C // 128):
                sl = slice(hp * 128, (hp + 1) * 128)
                qp = q_ref[rows, sl]
                kp = kc[:, sl]
                vp = vc[:, sl]
                outs = []
                for h in range(2 * hp, min(2 * hp + 2, nh)):
                    hm = _head_mask(qp.shape, h)
                    s = _dot_nt(jnp.where(hm, qp, jnp.zeros_like(qp)), kp)
                    s = jnp.where(mask, s, NEG_INF)
                    m = jnp.max(s, axis=-1, keepdims=True)
                    e = jnp.exp(s - m)
                    den = jnp.sum(e, axis=-1, keepdims=True)
                    p = (e * pl.reciprocal(den)).astype(BF)
                    outs.append(_dot(p, vp))
                    lses.append(m + jnp.log(den))
                if len(outs) == 2:
                    o = jnp.where(_head_mask(outs[0].shape, 0), outs[0], outs[1])
                else:
                    o = jnp.where(_head_mask(outs[0].shape, 0), outs[0], 0.0)
                o_ref[rows, sl] = o.astype(BF)
            mm = lses[0]
            for l in lses[1:]:
                mm = jnp.maximum(mm, l)
            tot = jnp.exp(lses[0] - mm)
            for l in lses[1:]:
                tot = tot + jnp.exp(l - mm)
            tile = jnp.where(lane == LSE_GROUP_LANE, mm + jnp.log(tot) - math.log(nh), 0.0)
            for h, l in enumerate(lses):
                tile = jnp.where(lane == h, l, tile)
            st_ref[rows, :] = tile

        first_mask = jnp.logical_and(band, jnp.logical_or(kj >= ATT_W, i > 0))
        block(0, jnp.concatenate([kh_ref[...], k_ref[pl.ds(0, ATT_W), :]], axis=0),
              jnp.concatenate([vh_ref[...], v_ref[pl.ds(0, ATT_W), :]], axis=0), first_mask)

        if nblk > 1:
            def step(blk, carry):
                prev = pl.ds(pl.multiple_of((blk - 1) * ATT_W, ATT_W), 2 * ATT_W)
                block(pl.multiple_of(blk * ATT_W, ATT_W), k_ref[prev, :], v_ref[prev, :], band)
                return carry
            lax.fori_loop(1, nblk, step, 0)

    main = lambda w: pl.BlockSpec((None, None, lc, C), lambda r, i: (w, r, i, 0))
    halo = lambda w: pl.BlockSpec((None, None, ATT_W, C), lambda r, i: (w, r, jnp.maximum(i * nblk - 1, 0), 0))
    return pl.pallas_call(
        body, name=name, grid=(d, L // lc),
        in_specs=[main(0), main(1), halo(1), main(2), halo(2)],
        out_specs=[pl.BlockSpec((None, lc, C), lambda r, i: (r, i, 0)), pl.BlockSpec((None, lc, 128), lambda r, i: (r, i, 0))],
        out_shape=[jax.ShapeDtypeStruct((d, L, C), BF), jax.ShapeDtypeStruct((d, L, 128), F32)],
        compiler_params=_params("parallel", "arbitrary"),
    )(qkv, qkv, qkv, qkv, qkv)


def _attn_bwd(qkv, do, st, dst, nh, name):
    _, d, L, C = qkv.shape
    lc = _att_chunk(L)
    nblk = lc // ATT_W
    nchunk = L // lc

    def body(q_ref, qn_ref, k_ref, kh_ref, v_ref, vh_ref, do_ref, don_ref, st_ref, stn_ref, ds_ref, dsn_ref, o_ref):
        i = pl.program_id(1)
        qi = lax.broadcasted_iota(jnp.int32, (ATT_W, 2 * ATT_W), 0)
        kj = lax.broadcasted_iota(jnp.int32, (ATT_W, 2 * ATT_W), 1)
        band_q = jnp.logical_and(kj >= qi, kj <= qi + ATT_W)
        qa = lax.broadcasted_iota(jnp.int32, (2 * ATT_W, ATT_W), 0)
        kb = lax.broadcasted_iota(jnp.int32, (2 * ATT_W, ATT_W), 1)
        band_k = jnp.logical_and(qa >= kb, qa <= kb + ATT_W)

        def probs(qm, kp, lse, mask):
            s = _dot_nt(qm, kp)
            return jnp.where(mask, jnp.exp(s - lse), 0.0)

        def q_block(row0, kc, vc, mask):
            rows = pl.ds(row0, ATT_W)
            stv = st_ref[rows, :]
            dsv = ds_ref[rows, :]
            for hp in range(C // 128):
                sl = slice(hp * 128, (hp + 1) * 128)
                qp = q_ref[rows, sl]
                dop = do_ref[rows, sl]
                kp = kc[:, sl]
                vp = vc[:, sl]
                outs = []
                for h in range(2 * hp, min(2 * hp + 2, nh)):
                    hm = _head_mask(qp.shape, h)
                    p = probs(jnp.where(hm, qp, jnp.zeros_like(qp)), kp, _lane_col(stv, h), mask)
                    dp = _dot_nt(jnp.where(hm, dop, jnp.zeros_like(dop)), vp)
                    dsc = (p * (dp - _lane_col(dsv, h))).astype(BF)
                    outs.append(_dot(dsc, kp))
                if len(outs) == 2:
                    dq = jnp.where(_head_mask(outs[0].shape, 0), outs[0], outs[1])
                else:
                    dq = jnp.where(_head_mask(outs[0].shape, 0), outs[0], 0.0)
                o_ref[0, rows, sl] = dq

        def k_block(row0, qq, doo, stv, dsv, mask):
            rows = pl.ds(row0, ATT_W)
            for hp in range(C // 128):
                sl = slice(hp * 128, (hp + 1) * 128)
                qp = qq[:, sl]
                dop = doo[:, sl]
                kp = k_ref[rows, sl]
                vp = v_ref[rows, sl]
                dks, dvs = [], []
                for h in range(2 * hp, min(2 * hp + 2, nh)):
                    hm = _head_mask(qp.shape, h)
                    qm = jnp.where(hm, qp, jnp.zeros_like(qp))
                    dom = jnp.where(hm, dop, jnp.zeros_like(dop))
                    p = probs(qm, kp, _lane_col(stv, h), mask)
                    dp = _dot_nt(dom, vp)
                    dsc = (p * (dp - _lane_col(dsv, h))).astype(BF)
                    dks.append(_dot_tn(dsc, qm))
                    dvs.append(_dot_tn(p.astype(BF), dom))
                if len(dks) == 2:
                    o_ref[1, rows, sl] = dks[0] + dks[1]
                    o_ref[2, rows, sl] = dvs[0] + dvs[1]
                else:
                    o_ref[1, rows, sl] = dks[0]
                    o_ref[2, rows, sl] = dvs[0]

        first_mask = jnp.logical_and(band_q, jnp.logical_or(kj >= ATT_W, i > 0))
        q_block(0, jnp.concatenate([kh_ref[...], k_ref[pl.ds(0, ATT_W), :]], axis=0),
                jnp.concatenate([vh_ref[...], v_ref[pl.ds(0, ATT_W), :]], axis=0), first_mask)
        if nblk > 1:
            def q_step(blk, carry):
                prev = pl.ds(pl.multiple_of((blk - 1) * ATT_W, ATT_W), 2 * ATT_W)
                q_block(pl.multiple_of(blk * ATT_W, ATT_W), k_ref[prev, :], v_ref[prev, :], band_q)
                return carry
            lax.fori_loop(1, nblk, q_step, 0)

            def k_step(blk, carry):
                two = pl.ds(pl.multiple_of(blk * ATT_W, ATT_W), 2 * ATT_W)
                k_block(pl.multiple_of(blk * ATT_W, ATT_W), q_ref[two, :], do_ref[two, :], st_ref[two, :], ds_ref[two, :], band_k)
                return carry
            lax.fori_loop(0, nblk - 1, k_step, 0)

        last = pl.ds((nblk - 1) * ATT_W, ATT_W)
        last_mask = jnp.logical_and(band_k, jnp.logical_or(qa < ATT_W, i < nchunk - 1))
        k_block((nblk - 1) * ATT_W,
                jnp.concatenate([q_ref[last, :], qn_ref[...]], axis=0),
                jnp.concatenate([do_ref[last, :], don_ref[...]], axis=0),
                jnp.concatenate([st_ref[last, :], stn_ref[...]], axis=0),
                jnp.concatenate([ds_ref[last, :], dsn_ref[...]], axis=0), last_mask)

    nb_all = L // ATT_W
    main4 = lambda w: pl.BlockSpec((None, None, lc, C), lambda r, i: (w, r, i, 0))
    prev4 = lambda w: pl.BlockSpec((None, None, ATT_W, C), lambda r, i: (w, r, jnp.maximum(i * nblk - 1, 0), 0))
    next4 = lambda w: pl.BlockSpec((None, None, ATT_W, C), lambda r, i: (w, r, jnp.minimum((i + 1) * nblk, nb_all - 1), 0))
    main3 = lambda n: pl.BlockSpec((None, lc, n), lambda r, i: (r, i, 0))
    next3 = lambda n: pl.BlockSpec((None, ATT_W, n), lambda r, i: (r, jnp.minimum((i + 1) * nblk, nb_all - 1), 0))
    return pl.pallas_call(
        body, name=name, grid=(d, nchunk),
        in_specs=[main4(0), next4(0), main4(1), prev4(1), main4(2), prev4(2),
                  main3(C), next3(C), main3(128), next3(128), main3(128), next3(128)],
        out_specs=pl.BlockSpec((3, None, lc, C), lambda r, i: (0, r, i, 0)),
        out_shape=jax.ShapeDtypeStruct((3, d, L, C), F32),
        compiler_params=_params("parallel", "arbitrary"),
    )(qkv, qkv, qkv, qkv, qkv, qkv, do, do, st, st, dst, dst)


def _alpha_from(lse_nat):
    m = jnp.maximum(jnp.maximum(lse_nat[0], lse_nat[1]), lse_nat[2])
    e = [jnp.exp(l - m) for l in lse_nat]
    inv = 1.0 / (e[0] + e[1] + e[2])
    return [ei * inv for ei in e]


def _attn_out_fwd(x, os_, sts, wos, tm=512):
    T = x.shape[0]
    C = GROUP_LANES

    def body(x_ref, o0, o1, o2, s0, s1, s2, w0, w1, w2, xo_ref, m0, m1, m2, al_ref, oscr, sscr):
        o_refs, st_refs, w_refs, m_refs = (o0, o1, o2), (s0, s1, s2), (w0, w1, w2), (m0, m1, m2)
        lses = []
        for g, d in enumerate(DILATIONS):
            _gather_rows(sscr.at[g], st_refs[g], d)
            lses.append(_lane_col(sscr[g, 0], LSE_GROUP_LANE))
        alpha = _alpha_from(lses)
        y = x_ref[...]
        for g, d in enumerate(DILATIONS):
            _gather_rows(oscr, o_refs[g], d)
            mg = (_chunks_to_rows(oscr) * (3.0 * alpha[g])).astype(BF)
            m_refs[g][...] = mg
            y = y + _dot(mg, w_refs[g][...])
        xo_ref[...] = y
        lane = lax.broadcasted_iota(jnp.int32, (tm, 128), 1)
        al_ref[...] = jnp.where(lane == 0, alpha[0], jnp.where(lane == 1, alpha[1], jnp.where(lane == 2, alpha[2], 0.0)))

    o_specs = [pl.BlockSpec((d, tm // d, C), lambda i: (0, i, 0)) for d in DILATIONS]
    st_specs = [pl.BlockSpec((d, tm // d, 128), lambda i: (0, i, 0)) for d in DILATIONS]
    mshape = jax.ShapeDtypeStruct((T, C), BF)
    return pl.pallas_call(
        body, name="attn_out_fwd", grid=(T // tm,),
        in_specs=[_row_spec(tm, D)] + o_specs + st_specs + [_full_spec((C, D))] * 3,
        out_specs=[_row_spec(tm, D), _row_spec(tm, C), _row_spec(tm, C), _row_spec(tm, C), _row_spec(tm, 128)],
        out_shape=[jax.ShapeDtypeStruct((T, D), F32), mshape, mshape, mshape, jax.ShapeDtypeStruct((T, 128), F32)],
        scratch_shapes=[pltpu.VMEM((C // 128, tm, 128), F32), pltpu.VMEM((3, 1, tm, 128), F32)],
        compiler_params=_params("parallel"),
    )(x, *os_, *sts, *wos)


def _attn_out_bwd(dx, os_, sts, alpha, wos, tm=512, dep=None):
    T = dx.shape[0]
    C = GROUP_LANES

    def body(dx_ref, o0, o1, o2, s0, s1, s2, al_ref, w0, w1, w2, do0, do1, do2, ds0, ds1, ds2, dmscr, oscr, sscr, tscr):
        o_refs, st_refs, w_refs = (o0, o1, o2), (s0, s1, s2), (w0, w1, w2)
        do_refs, ds_refs = (do0, do1, do2), (ds0, ds1, ds2)
        dyb = dx_ref[...].astype(BF)
        alv = al_ref[...]
        alpha_g = [_lane_col(alv, g) for g in range(3)]
        dalpha = []
        for g, d in enumerate(DILATIONS):
            dm = _dot_nt(dyb, w_refs[g][...])
            _rows_to_chunks(dmscr.at[g], dm)
            _gather_rows(oscr.at[g], o_refs[g], d)
            _gather_rows(sscr.at[g], st_refs[g], d)
            dalpha.append(3.0 * jnp.sum(dm * _chunks_to_rows(oscr.at[g]), axis=-1, keepdims=True))
        mean_da = alpha_g[0] * dalpha[0] + alpha_g[1] * dalpha[1] + alpha_g[2] * dalpha[2]
        lane = lax.broadcasted_iota(jnp.int32, (tm, 128), 1)
        head_lane = lax.broadcasted_iota(jnp.int32, (tm, C), 1) // HEAD_DIM
        for g, d in enumerate(DILATIONS):
            nh = HEAD_GROUPS[g]
            dlse_g = alpha_g[g] * (dalpha[g] - mean_da)
            do_nat = _chunks_to_rows(dmscr.at[g]) * (3.0 * alpha_g[g])
            prod = do_nat * _chunks_to_rows(oscr.at[g])
            stv = sscr[g, 0]
            lse_g = _lane_col(stv, LSE_GROUP_LANE)
            tile = jnp.zeros((tm, 128), F32)
            for h in range(nh):
                delta = jnp.sum(jnp.where(head_lane == h, prod, 0.0), axis=-1, keepdims=True)
                dlse = dlse_g * jnp.exp(_lane_col(stv, h) - lse_g) * (1.0 / nh)
                tile = jnp.where(lane == h, delta - dlse, tile)
            _rows_to_chunks(dmscr.at[g], do_nat)
            _scatter_rows(do_refs[g], dmscr.at[g], d, BF)
            tscr[0] = tile
            _scatter_rows(ds_refs[g], tscr, d, F32)

    o_specs = [pl.BlockSpec((d, tm // d, C), lambda i: (0, i, 0)) for d in DILATIONS]
    st_specs = [pl.BlockSpec((d, tm // d, 128), lambda i: (0, i, 0)) for d in DILATIONS]
    body, in_specs, args = _after(
        dep, body, [_row_spec(tm, D)] + o_specs + st_specs + [_row_spec(tm, 128)] + [_full_spec((C, D))] * 3,
        [dx, *os_, *sts, alpha, *wos])
    return pl.pallas_call(
        body, name="attn_out_bwd", grid=(T // tm,),
        in_specs=in_specs,
        out_specs=o_specs + st_specs,
        out_shape=[jax.ShapeDtypeStruct((d, T // d, C), BF) for d in DILATIONS]
        + [jax.ShapeDtypeStruct((d, T // d, 128), F32) for d in DILATIONS],
        scratch_shapes=[pltpu.VMEM((3, C // 128, tm, 128), F32), pltpu.VMEM((3, C // 128, tm, 128), F32),
                        pltpu.VMEM((3, 1, tm, 128), F32), pltpu.VMEM((1, tm, 128), F32)],
        compiler_params=_params("parallel"),
    )(*args)


def _qkv_bwd(dqkvs, cos, sin, x2, g, w3s, dx3, tm=256):
    T = x2.shape[0]
    C = GROUP_LANES

    def body(dq0, dq1, dq2, cos_ref, sin_ref, x_ref, g_ref, w0, w1, w2, dx3_ref,
             dx_ref, n0, n1, n2, h_ref, dg_ref, scr, dh_ref):
        dq_refs, w_refs, n_refs = (dq0, dq1, dq2), (w0, w1, w2), (n0, n1, n2)

        @pl.when(pl.program_id(0) == 0)
        def _():
            dg_ref[...] = jnp.zeros_like(dg_ref)

        cv = cos_ref[...]
        sv = sin_ref[...]
        dh_ref[...] = jnp.zeros_like(dh_ref)
        for gi, d in enumerate(DILATIONS):
            for w in range(3):
                _gather_rows(scr, dq_refs[gi].at[w], d)
                if w < 2:
                    scale = HEAD_DIM ** -0.5 if w == 0 else 1.0
                    for c in range(C // 128):
                        t = scr[c]
                        scr[c] = (t * cv - _rot_half(t) * sv) * scale
                tb = _chunks_to_rows(scr).astype(BF)
                n_refs[gi][w] = tb
                dh_ref[...] += _dot_nt(tb, w_refs[gi][w])
        xv = x_ref[...]
        h_ref[...] = _rms_fwd(xv, g_ref[...]).astype(BF)
        dx, dg = _rms_bwd(xv, g_ref[...], dh_ref[...])
        dx_ref[...] = dx3_ref[...] + dx
        dg_ref[...] += dg

    dq_specs = [pl.BlockSpec((3, d, tm // d, C), lambda i: (0, 0, i, 0)) for d in DILATIONS]
    nat = pl.BlockSpec((3, tm, C), lambda i: (0, i, 0))
    nshape = jax.ShapeDtypeStruct((3, T, C), BF)
    return pl.pallas_call(
        body, name="qkv_bwd", grid=(T // tm,),
        in_specs=dq_specs + [_row_spec(tm, 128), _row_spec(tm, 128), _row_spec(tm, D), _full_spec((1, D))]
        + [_full_spec((3, D, C))] * 3 + [_row_spec(tm, D)],
        out_specs=[_row_spec(tm, D), nat, nat, nat, _row_spec(tm, D), _full_spec((1, D))],
        out_shape=[jax.ShapeDtypeStruct((T, D), F32), nshape, nshape, nshape, jax.ShapeDtypeStruct((T, D), BF),
                   jax.ShapeDtypeStruct((1, D), F32)],
        scratch_shapes=[pltpu.VMEM((C // 128, tm, 128), F32), pltpu.VMEM((tm, D), F32)],
        compiler_params=_params("arbitrary"),
    )(*dqkvs, cos, sin, x2, g, *w3s, dx3)


def _final_bwd(x, target, g, tm=512):
    T = x.shape[0]

    def body(x_ref, t_ref, g_ref, dx_ref, loss_ref, dg_ref):
        @pl.when(pl.program_id(0) == 0)
        def _():
            loss_ref[...] = jnp.zeros_like(loss_ref)
            dg_ref[...] = jnp.zeros_like(dg_ref)

        xv = x_ref[...]
        gv = g_ref[...]
        diff = _rms_fwd(xv, gv) - t_ref[...]
        loss_ref[...] += 0.5 * jnp.sum(jnp.mean(diff * diff, axis=-1, keepdims=True), axis=0, keepdims=True)
        dx, dg = _rms_bwd(xv, gv, diff * (1.0 / D))
        dx_ref[...] = dx
        dg_ref[...] += dg

    return pl.pallas_call(
        body, name="final_bwd", grid=(T // tm,),
        in_specs=[_row_spec(tm, D), _row_spec(tm, D), _full_spec((1, D))],
        out_specs=[_row_spec(tm, D), _full_spec((1, 1)), _full_spec((1, D))],
        out_shape=[jax.ShapeDtypeStruct((T, D), F32), jax.ShapeDtypeStruct((1, 1), F32), jax.ShapeDtypeStruct((1, D), F32)],
        compiler_params=_params("arbitrary"),
    )(x, target, g)


def _place():
    x, y, c = lax.axis_index("x"), lax.axis_index("y"), lax.axis_index("c")
    return x, y, c


def _allgather(arrs, name):
    n = len(arrs)

    def body(*refs):
        ins, outs = refs[:n], refs[n:2 * n]
        send_sems, recv_sems, local_sems = refs[2 * n:]
        x, y, c = _place()
        me, sibling = (x, y, c), (x, y, 1 - c)
        chips = [(1 - x, y), (x, 1 - y), (1 - x, 1 - y)]

        def slot(a, px, py, pc):
            return outs[a].at[4 * px + 2 * py + pc]

        def copy(a, k, block, to, src=None):
            return pltpu.make_async_remote_copy(
                src_ref=slot(a, *block) if src is None else src, dst_ref=slot(a, *block),
                send_sem=send_sems.at[a, k], recv_sem=recv_sems.at[a, k], device_id=to, device_id_type=MESH)

        mine = [pltpu.make_async_copy(ins[a], slot(a, *me), local_sems.at[a]) for a in range(n)]
        for cp in mine:
            cp.start()
        first = []
        for a in range(n):
            first.append(copy(a, 0, me, sibling, src=ins[a]))
            first += [copy(a, 1 + j, me, (*chip, c), src=ins[a]) for j, chip in enumerate(chips)]
        for cp in first:
            cp.start()
        passed = []
        for a in range(n):
            for j, chip in enumerate(chips):
                copy(a, 1 + j, (*chip, c), me).wait_recv()
                fwd = copy(a, 4 + j, (*chip, c), sibling)
                fwd.start()
                passed.append(fwd)
        for a in range(n):
            copy(a, 0, sibling, me).wait_recv()
            for j, chip in enumerate(chips):
                copy(a, 4 + j, (*chip, 1 - c), me).wait_recv()
        for cp in first + passed:
            cp.wait_send()
        for cp in mine:
            cp.wait()

    hbm = pl.BlockSpec(memory_space=pl.ANY)
    return pl.pallas_call(
        body, name=name,
        in_specs=[hbm] * n, out_specs=[hbm] * n,
        out_shape=[jax.ShapeDtypeStruct((N_DEV,) + a.shape, a.dtype) for a in arrs],
        scratch_shapes=[pltpu.SemaphoreType.DMA((n, 7)), pltpu.SemaphoreType.DMA((n, 7)), pltpu.SemaphoreType.DMA((n,))],
    )(*arrs)


def _peer(k):
    x, y, c = _place()
    px = 1 - x if k & 4 else x
    py = 1 - y if k & 2 else y
    pc = 1 - c if k & 1 else c
    return (px, py, pc), 4 * px + 2 * py + pc


HBM_SPEC = pl.BlockSpec(memory_space=pltpu.HBM)
SEM_SPEC = pl.BlockSpec(memory_space=pltpu.SEMAPHORE)
EFFECT = pltpu.SideEffectType.DATAFLOW_SIDE_EFFECTING


def _exchange_start(arrs, name, same_block=False, dep=None):
    n = len(arrs)
    n_dep = 0 if dep is None else 1

    def body(*refs):
        srcs, lands = refs[:n], refs[n:2 * n]
        send_sems, recv_sems, local_sems = refs[2 * n + n_dep:2 * n + n_dep + 3]
        token = refs[-1]
        x, y, c = _place()
        me = 4 * x + 2 * y + c
        block = (lambda a, j: srcs[a]) if same_block else (lambda a, j: srcs[a].at[j])
        for a in range(n):
            pltpu.make_async_copy(block(a, me), lands[a].at[me], local_sems.at[a]).start()
        for a in range(n):
            for k in range(1, N_DEV):
                to, to_idx = _peer(k)
                pltpu.make_async_remote_copy(
                    src_ref=block(a, to_idx), dst_ref=lands[a].at[me],
                    send_sem=send_sems.at[a * (N_DEV - 1) + k - 1], recv_sem=recv_sems.at[a * (N_DEV - 1) + k - 1], device_id=to, device_id_type=MESH).start()
        token[...] = jnp.zeros_like(token)

    land_shape = (lambda a: (N_DEV,) + a.shape) if same_block else (lambda a: a.shape)
    src_shapes = [pltpu.HBM(a.shape, a.dtype) for a in arrs]
    land_shapes = [pltpu.HBM(land_shape(a), a.dtype) for a in arrs]
    outs = pl.pallas_call(
        body, name=name,
        out_shape=(pltpu.SemaphoreType.DMA((n * (N_DEV - 1),)), pltpu.SemaphoreType.DMA((n * (N_DEV - 1),)),
                   pltpu.SemaphoreType.DMA((n,)), *src_shapes, *land_shapes, jax.ShapeDtypeStruct((8, 128), F32)),
        in_specs=[HBM_SPEC] * (2 * n) + [pl.BlockSpec(memory_space=pl.ANY)] * n_dep,
        out_specs=(SEM_SPEC, SEM_SPEC, SEM_SPEC, *([HBM_SPEC] * (2 * n)), pl.BlockSpec(memory_space=pltpu.VMEM)),
        input_output_aliases={i: 3 + i for i in range(2 * n)},
        compiler_params=pltpu.CompilerParams(has_side_effects=EFFECT),
    )(*[pltpu.with_memory_space_constraint(a, pltpu.HBM) for a in arrs],
      *[pltpu.with_memory_space_constraint(lax.empty(land_shape(a), a.dtype), pltpu.HBM) for a in arrs],
      *([] if dep is None else [dep]))
    return outs[0], outs[1], outs[2], outs[3:3 + n], outs[3 + n:3 + 2 * n], outs[3 + 2 * n]


def _exchange_wait(send_sems, recv_sems, local_sems, src_thru, land_thru, after, name):
    n = len(src_thru)
    same_block = src_thru[0].shape != land_thru[0].shape

    def body(*refs):
        srcs, lands = refs[:n], refs[n:2 * n]
        send_sems, recv_sems, local_sems = refs[2 * n:2 * n + 3]
        x, y, c = _place()
        me = 4 * x + 2 * y + c
        for a in range(n):
            pltpu.make_async_copy(srcs[a] if same_block else srcs[a].at[me], lands[a].at[me], local_sems.at[a]).wait()
            for k in range(1, N_DEV):
                frm, frm_idx = _peer(k)
                cp = pltpu.make_async_remote_copy(
                    src_ref=srcs[a] if same_block else srcs[a].at[frm_idx], dst_ref=lands[a].at[frm_idx],
                    send_sem=send_sems.at[a * (N_DEV - 1) + k - 1], recv_sem=recv_sems.at[a * (N_DEV - 1) + k - 1], device_id=frm, device_id_type=MESH)
                cp.wait_send()
                cp.wait_recv()

    outs = pl.pallas_call(
        body, name=name,
        out_shape=tuple(pltpu.HBM(a.shape, a.dtype) for a in (*src_thru, *land_thru)),
        in_specs=[HBM_SPEC] * (2 * n) + [SEM_SPEC, SEM_SPEC, SEM_SPEC, pl.BlockSpec(memory_space=pl.ANY)],
        out_specs=[HBM_SPEC] * (2 * n),
        input_output_aliases={i: i for i in range(2 * n)},
        compiler_params=pltpu.CompilerParams(has_side_effects=EFFECT),
    )(*src_thru, *land_thru, send_sems, recv_sems, local_sems, after)
    return outs[n:]


def _row_tile(rows):
    for t in (256, 176, 128, 8):
        if rows % t == 0:
            return t
    return rows


def _sum_parts(parts, name):
    K, R, C = parts.shape
    tr = _row_tile(R)

    def body(p_ref, o_ref):
        g = p_ref[0].astype(F32)
        for k in range(1, K):
            g = g + p_ref[k].astype(F32)
        o_ref[...] = g

    return pl.pallas_call(
        body, name=name, grid=(R // tr,),
        in_specs=[pl.BlockSpec((K, tr, C), lambda i: (0, i, 0))],
        out_specs=_row_spec(tr, C),
        out_shape=jax.ShapeDtypeStruct((R, C), F32),
        compiler_params=_params("parallel"),
    )(parts)


def _adamw(parts, w, m, v, name):
    K, R, C = parts.shape
    tr = _row_tile(R)

    def body(p_ref, w_ref, m_ref, v_ref, g_ref, d_ref, nm_ref, nv_ref):
        g = p_ref[0].astype(F32)
        for k in range(1, K):
            g = g + p_ref[k].astype(F32)
        nm = ADAM_B1 * m_ref[...] + (1.0 - ADAM_B1) * g
        nv = ADAM_B2 * v_ref[...] + (1.0 - ADAM_B2) * jnp.square(g)
        m_hat = nm / (1.0 - ADAM_B1 ** ADAM_STEP)
        v_hat = nv / (1.0 - ADAM_B2 ** ADAM_STEP)
        g_ref[...] = g
        d_ref[...] = -ADAM_LR * (m_hat / (jnp.sqrt(v_hat) + ADAM_EPS) + ADAM_WD * w_ref[...])
        nm_ref[...] = nm
        nv_ref[...] = nv

    blk = _row_spec(tr, C)
    shp = jax.ShapeDtypeStruct((R, C), F32)
    return pl.pallas_call(
        body, name=name, grid=(R // tr,),
        in_specs=[pl.BlockSpec((K, tr, C), lambda i: (0, i, 0)), blk, blk, blk],
        out_specs=[blk] * 4,
        out_shape=[shp] * 4,
        compiler_params=_params("parallel"),
    )(parts, w, m, v)


def _rope_tables(T):
    inv_freq = 1.0 / (ROPE_THETA ** (jnp.arange(0, HEAD_DIM, 2, dtype=F32) / HEAD_DIM))
    ang = jnp.arange(T, dtype=F32)[:, None] * inv_freq[None, :]
    ang = jnp.concatenate([ang, ang, ang, ang], axis=-1)
    return jnp.cos(ang), jnp.sin(ang)


def _pad_lanes(a, n):
    return jnp.pad(a, ((0, 0),) * (a.ndim - 1) + ((0, n - a.shape[-1]),))


LAYER0 = ("pool_in", "pool_grp", "pool_out", "gate0", "up0", "down0")
LAYER1 = ("qkv", "attn_out", "gate1", "up1", "down1")


def _layout_weights(gw):
    w = {}
    if "pool_in" in gw:
        w["pool_in"] = gw["pool_in"].reshape(D, D)
        w["pool_grp"] = jnp.transpose(gw["pool_grp"], (1, 0, 2, 3)).reshape(4, POOL_GC, POOL_GC)
        w["pool_out"] = gw["pool_out"].reshape(D, D)
    if "qkv" in gw:
        wqkv = jnp.transpose(gw["qkv"], (1, 0, 2)).reshape(D, 3 * D)
        wo = gw["attn_out"].reshape(D, D)
        w["qkv"], w["attn_out"] = [], []
        for nh, off in zip(HEAD_GROUPS, HEAD_OFFS):
            lo, n = off * HEAD_DIM, nh * HEAD_DIM
            w["qkv"].append(jnp.stack([_pad_lanes(wqkv[:, k * D + lo:k * D + lo + n], GROUP_LANES) for k in range(3)]))
            w["attn_out"].append(jnp.pad(wo[lo:lo + n], ((0, GROUP_LANES - n), (0, 0))))
    for nm in gw:
        if nm[:-1] in ("gate", "up", "down"):
            w[nm] = gw[nm].reshape(F, D)
    return w


def _local_step(x, target, w, layer1, dep0, norm_mix, norm_ffn, norm_final, pool_scale, emit):
    T = x.shape[0]
    cos, sin = _rope_tables(T)
    nm = [norm_mix[i:i + 1] for i in range(2)]
    nf = [norm_ffn[i:i + 1] for i in range(2)]
    nfin = norm_final.reshape(1, D)

    h0 = _norm_fwd(x, nm[0], "norm_mix0", dep=dep0)
    x1, p = _pool_fwd(x, h0, w["pool_in"], w["pool_grp"], pool_scale, w["pool_out"])
    h1 = _norm_fwd(x1, nf[0], "norm_ffn0")
    x2, a0, b0, s0 = _ffn_fwd(x1, h1, w["gate0"], w["up0"], w["down0"], "ffn_fwd0")
    h2 = _norm_fwd(x2, nm[1], "norm_mix1")
    w = {**w, **layer1(h2)}
    qkvs = [_qkv_fwd(h2, w["qkv"][g], cos, sin, DILATIONS[g], f"qkv_fwd{g}") for g in range(3)]
    att = [_attn_fwd(qkvs[g], HEAD_GROUPS[g], f"attn_fwd{g}") for g in range(3)]
    os_, sts = [a[0] for a in att], [a[1] for a in att]
    x3, m0, m1, m2, alpha = _attn_out_fwd(x2, os_, sts, w["attn_out"])
    h3 = _norm_fwd(x3, nf[1], "norm_ffn1")
    x4, a1, b1, s1 = _ffn_fwd(x3, h3, w["gate1"], w["up1"], w["down1"], "ffn_fwd1")

    g = {}
    dx4, loss, dg_final = _final_bwd(x4, target, nfin)
    dx3, da1, db1, _, dg_ffn1 = _ffn_bwd(dx4, x3, nf[1], a1, b1, w["gate1"], w["up1"], w["down1"], "ffn_bwd1")
    g["down1"], g["gate1"], g["up1"] = _ffn_wgrad(s1, da1, db1, dx4, h3, "ffn_wgrad1")
    dep = emit("ffn1", g)
    dos_and_stats = _attn_out_bwd(dx3, os_, sts, alpha, w["attn_out"], dep=dep)
    dos, dsts = dos_and_stats[:3], dos_and_stats[3:]
    g["attn_out"] = [_wgrad(m, dx3, f"attn_out_wgrad{i}") for i, m in enumerate((m0, m1, m2))]
    dqkvs = [_attn_bwd(qkvs[gi], dos[gi], sts[gi], dsts[gi], HEAD_GROUPS[gi], f"attn_bwd{gi}") for gi in range(3)]
    dx2, n0, n1, n2, h2b, dg_mix1 = _qkv_bwd(dqkvs, cos, sin, x2, nm[1], w["qkv"], dx3)
    g["qkv"] = [_wgrad_stack(h2b, n, f"qkv_wgrad{i}") for i, n in enumerate((n0, n1, n2))]
    dep = emit("attn", g)
    dx1, da0, db0, _, dg_ffn0 = _ffn_bwd(dx2, x1, nf[0], a0, b0, w["gate0"], w["up0"], w["down0"], "ffn_bwd0", dep=dep)
    g["down0"], g["gate0"], g["up0"] = _ffn_wgrad(s0, da0, db0, dx2, h1, "ffn_wgrad0")
    dep = emit("ffn0", g)
    dx0, z, dzp, du, h0b, dscale, dg_mix0 = _pool_bwd(dx1, x, nm[0], p, w["pool_in"], w["pool_grp"], pool_scale, w["pool_out"], dep=dep)
    g["pool_out"] = _wgrad(z, dx1, "pool_out_wgrad")
    g["pool_in"] = _wgrad(h0b, du, "pool_in_wgrad")
    g["pool_grp"] = _wgrad_pool_groups(p, dzp)
    emit("pool", g)

    small = jnp.concatenate([dg_mix0, dg_mix1, dg_ffn0, dg_ffn1, dg_final, dscale,
                             jnp.broadcast_to(loss, (1, D)), jnp.zeros((1, D), F32)], axis=0)
    return dx0, small


GROUPS = {"ffn1": ("down1", "gate1", "up1"), "attn": ("qkv", "attn_out"), "ffn0": ("down0", "gate0", "up0"),
          "pool": ("pool_in", "pool_out", "pool_grp")}


def _grad_blocks(group, g):
    blocks = {}
    if group == "pool":
        blocks["pool_in"] = g["pool_in"].reshape(N_DEV, D // N_DEV, D)
        blocks["pool_out"] = g["pool_out"].reshape(N_DEV, D // N_DEV, D)
        blocks["pool_grp"] = jnp.transpose(g["pool_grp"].reshape(4, N_DEV, POOL_GC // N_DEV, POOL_GC), (1, 0, 2, 3)).reshape(N_DEV, 4 * POOL_GC // N_DEV, POOL_GC)
    elif group == "attn":
        wo = jnp.concatenate([gw[:nh * HEAD_DIM] for gw, nh in zip(g["attn_out"], HEAD_GROUPS)], axis=0)
        blocks["attn_out"] = wo.reshape(N_DEV, D // N_DEV, D)
        wqkv = jnp.concatenate([g["qkv"][gi][k][:, :HEAD_GROUPS[gi] * HEAD_DIM] for k in range(3) for gi in range(3)], axis=1)
        blocks["qkv"] = jnp.transpose(wqkv.reshape(D, N_DEV, 3 * D // N_DEV), (1, 0, 2))
    else:
        for nm in GROUPS[group]:
            blocks[nm] = g[nm].reshape(N_DEV, F // N_DEV, D)
    return [blocks[nm] for nm in GROUPS[group]]


def kernel(x, norm_mix, norm_ffn, norm_final, pool_w_in, pool_w_group, pool_scale, pool_w_out, attn_w_qkv, attn_w_out, ffn_w_gate, ffn_w_up, ffn_w_down, loss_target, m_norm_mix, m_norm_ffn, m_norm_final, m_pool_w_in, m_pool_w_group, m_pool_scale, m_pool_w_out, m_attn_w_qkv, m_attn_w_out, m_ffn_w_gate, m_ffn_w_up, m_ffn_w_down, v_norm_mix, v_norm_ffn, v_norm_final, v_pool_w_in, v_pool_w_group, v_pool_scale, v_pool_w_out, v_attn_w_qkv, v_attn_w_out, v_ffn_w_gate, v_ffn_w_up, v_ffn_w_down):
    shard = {
        "pool_in": pool_w_in[0], "pool_grp": pool_w_group[0], "pool_out": pool_w_out[0],
        "qkv": attn_w_qkv[0], "attn_out": attn_w_out[0],
    }
    for l in range(2):
        shard[f"gate{l}"] = ffn_w_gate[l].T
        shard[f"up{l}"] = ffn_w_up[l].T
        shard[f"down{l}"] = ffn_w_down[l]
    shard = {k: v.astype(BF) for k, v in shard.items()}
    w0 = _layout_weights(dict(zip(LAYER0, _allgather([shard[k] for k in LAYER0], "weights_allgather"))))
    l1 = _exchange_start([shard[k] for k in LAYER1], "weights_start_layer1", same_block=True, dep=w0["pool_in"])

    def layer1(after):
        return _layout_weights(dict(zip(LAYER1, _exchange_wait(*l1[:5], after, "weights_wait_layer1"))))

    started = {}

    def emit(group, g):
        started[group] = _exchange_start(_grad_blocks(group, g), f"grads_start_{group}")
        return started[group][5]

    grad_x, small = _local_step(x[0], loss_target[0], w0, layer1, l1[5], norm_mix, norm_ffn, norm_final, pool_scale, emit)

    small_st = _exchange_start([small], "small_start", same_block=True, dep=started["pool"][5])
    received = {}
    after = small_st[5]
    for group, st in started.items():
        lands = _exchange_wait(*st[:5], after, f"grads_wait_{group}")
        received.update(zip(GROUPS[group], lands))
    small_all = _exchange_wait(*small_st[:5], after, "small_wait")[0]

    def upd(parts, wt, mt, vt, name):
        shape = wt.shape
        r2 = lambda t: t.reshape(parts.shape[1:])
        outs = _adamw(parts, r2(wt), r2(mt), r2(vt), name)
        return [o.reshape(shape) for o in outs]

    res = {}
    res["pool_w_in"] = upd(received["pool_in"], pool_w_in, m_pool_w_in, v_pool_w_in, "adamw_pool_in")
    res["pool_w_group"] = upd(received["pool_grp"], pool_w_group, m_pool_w_group, v_pool_w_group, "adamw_pool_grp")
    res["pool_w_out"] = upd(received["pool_out"], pool_w_out, m_pool_w_out, v_pool_w_out, "adamw_pool_out")
    res["attn_w_qkv"] = upd(received["qkv"], attn_w_qkv, m_attn_w_qkv, v_attn_w_qkv, "adamw_qkv")
    res["attn_w_out"] = upd(received["attn_out"], attn_w_out, m_attn_w_out, v_attn_w_out, "adamw_attn_out")
    for nm, wt, mt, vt in (("gate", ffn_w_gate, m_ffn_w_gate, v_ffn_w_gate), ("up", ffn_w_up, m_ffn_w_up, v_ffn_w_up)):
        per_layer = []
        for l in range(2):
            gt = _sum_parts(received[f"{nm}{l}"], f"sum_{nm}{l}").T
            per_layer.append(_adamw(gt[None], wt[l], mt[l], vt[l], f"adamw_{nm}{l}"))
        res[f"ffn_w_{nm}"] = [jnp.stack([per_layer[0][i], per_layer[1][i]]) for i in range(4)]
    per_layer = [_adamw(received[f"down{l}"], ffn_w_down[l], m_ffn_w_down[l], v_ffn_w_down[l], f"adamw_down{l}") for l in range(2)]
    res["ffn_w_down"] = [jnp.stack([per_layer[0][i], per_layer[1][i]]) for i in range(4)]

    small_w = jnp.concatenate([norm_mix, norm_ffn, norm_final[None], pool_scale, jnp.zeros((2, D), F32)], axis=0)
    small_m = jnp.concatenate([m_norm_mix, m_norm_ffn, m_norm_final[None], m_pool_scale, jnp.zeros((2, D), F32)], axis=0)
    small_v = jnp.concatenate([v_norm_mix, v_norm_ffn, v_norm_final[None], v_pool_scale, jnp.ones((2, D), F32)], axis=0)
    sg, sd, sm, sv = _adamw(small_all, small_w, small_m, small_v, "adamw_small")
    loss = sg[6, 0]
    res["norm_mix"] = [t[0:2] for t in (sg, sd, sm, sv)]
    res["norm_ffn"] = [t[2:4] for t in (sg, sd, sm, sv)]
    res["norm_final"] = [t[4] for t in (sg, sd, sm, sv)]
    res["pool_scale"] = [t[5:6] for t in (sg, sd, sm, sv)]

    order = ["norm_mix", "norm_ffn", "norm_final", "pool_w_in", "pool_w_group", "pool_scale", "pool_w_out",
             "attn_w_qkv", "attn_w_out", "ffn_w_gate", "ffn_w_up", "ffn_w_down"]
    return (loss, grad_x[None], *[res[k][0] for k in order], *[res[k][1] for k in order],
            *[res[k][2] for k in order], *[res[k][3] for k in order])
```

```python
import math

import jax
import jax.numpy as jnp
from jax import lax
from jax.experimental import pallas as pl
from jax.experimental.pallas import tpu as pltpu

D = 1024
F = 2816
N_DEV = 8
EPS = 1e-6
NEG_INF = -1e30
POOL_WINDOWS = (2, 4, 8, 16)
POOL_HALO = 16
POOL_GC = 256
HEAD_DIM = 64
HEAD_GROUPS = (6, 5, 5)
HEAD_OFFS = (0, 6, 11)
DILATIONS = (1, 4, 16)
ATT_W = 128
GROUP_LANES = 384
LSE_GROUP_LANE = 8
ROPE_THETA = 10000.0
ADAM_LR, ADAM_B1, ADAM_B2, ADAM_EPS, ADAM_WD, ADAM_STEP = 0.001, 0.9, 0.999, 1e-08, 0.01, 10

BF = jnp.bfloat16
F32 = jnp.float32
VMEM_LIMIT = 56 * 1024 * 1024
MESH = pl.DeviceIdType.MESH


def _params(*sem):
    return pltpu.CompilerParams(dimension_semantics=sem, vmem_limit_bytes=VMEM_LIMIT)


def _dot(a, b):
    return jnp.dot(a, b, preferred_element_type=F32)


def _dot_nt(a, b):
    return lax.dot_general(a, b, (((1,), (1,)), ((), ())), preferred_element_type=F32)


def _dot_tn(a, b):
    return lax.dot_general(a, b, (((0,), (0,)), ((), ())), preferred_element_type=F32)


def _rms_fwd(xv, g):
    r = lax.rsqrt(jnp.mean(xv * xv, axis=-1, keepdims=True) + EPS)
    return (xv * r) * g


def _rms_bwd(xv, g, dh):
    r = lax.rsqrt(jnp.mean(xv * xv, axis=-1, keepdims=True) + EPS)
    xhat = xv * r
    dg = jnp.sum(dh * xhat, axis=0, keepdims=True)
    dxh = dh * g
    dx = r * (dxh - xhat * jnp.mean(dxh * xhat, axis=-1, keepdims=True))
    return dx, dg


def _lane_col(tile, j):
    lane = lax.broadcasted_iota(jnp.int32, tile.shape, 1)
    return jnp.sum(jnp.where(lane == j, tile, 0.0), axis=-1, keepdims=True)


def _row_spec(tm, n):
    return pl.BlockSpec((tm, n), lambda i: (i, 0))


def _full_spec(shape):
    nd = len(shape)
    return pl.BlockSpec(shape, lambda *_: (0,) * nd)


def _after(dep, body, in_specs, args):
    if dep is None:
        return body, list(in_specs), list(args)

    def body_after(dep_ref, *refs):
        body(*refs)

    return body_after, [pl.BlockSpec(memory_space=pl.ANY)] + list(in_specs), [dep] + list(args)


def _pool_fwd(x, g, w_in, w_grp, scale, w_out, tm=512, dep=None):
    T = x.shape[0]
    n = tm + POOL_HALO

    def body(x_ref, g_ref, win_ref, wg_ref, sc_ref, wout_ref, x1_ref, p_ref, tail_ref, z_ref):
        i = pl.program_id(0)

        @pl.when(i == 0)
        def _():
            tail_ref[...] = jnp.zeros_like(tail_ref)

        u = _dot(_rms_fwd(x_ref[...], g_ref[...]).astype(BF), win_ref[...])
        pos = i * tm + lax.broadcasted_iota(jnp.int32, (tm, 1), 0)
        for g, w in enumerate(POOL_WINDOWS):
            sl = slice(g * POOL_GC, (g + 1) * POOL_GC)
            ug = u[:, sl]
            s = jnp.concatenate([tail_ref[:, sl], ug], axis=0)
            step = 1
            while step < w:
                s = s + pltpu.roll(s, step, 0)
                step *= 2
            cnt = jnp.minimum(pos + 1, w).astype(F32)
            pg = (s[POOL_HALO:, :] / cnt - ug).astype(BF)
            p_ref[:, sl] = pg
            z_ref[:, sl] = (_dot(pg, wg_ref[g]) * sc_ref[:, sl]).astype(BF)
        tail_ref[...] = u[tm - POOL_HALO:, :]
        x1_ref[...] = x_ref[...] + _dot(z_ref[...], wout_ref[...])

    body, in_specs, args = _after(
        dep, body,
        [_row_spec(tm, D), _full_spec((1, D)), _full_spec((D, D)), _full_spec((4, POOL_GC, POOL_GC)), _full_spec((1, D)),
         _full_spec((D, D))],
        [x, g, w_in, w_grp, scale, w_out])
    return pl.pallas_call(
        body, name="pool_fwd", grid=(T // tm,),
        in_specs=in_specs,
        out_specs=[_row_spec(tm, D), _row_spec(tm, D)],
        out_shape=[jax.ShapeDtypeStruct((T, D), F32), jax.ShapeDtypeStruct((T, D), BF)],
        scratch_shapes=[pltpu.VMEM((POOL_HALO, D), F32), pltpu.VMEM((tm, D), BF)],
        compiler_params=_params("arbitrary"),
    )(*args)


def _pool_bwd(dx1, x0, g0, p, w_in, w_grp, scale, w_out, tm=512, dep=None):
    T = x0.shape[0]
    nt = T // tm
    n = tm + POOL_HALO
    rev = lambda i: (nt - 1 - i, 0)

    def body(dx1_ref, x0_ref, g_ref, p_ref, win_ref, wg_ref, sc_ref, wout_ref,
             dx0_ref, z_ref, dzp_ref, du_ref, h0_ref, dsc_ref, dg_ref, head_ref):
        i = pl.program_id(0)

        @pl.when(i == 0)
        def _():
            head_ref[...] = jnp.zeros_like(head_ref)
            dsc_ref[...] = jnp.zeros_like(dsc_ref)
            dg_ref[...] = jnp.zeros_like(dg_ref)

        dx1v = dx1_ref[...]
        dz = _dot_nt(dx1v.astype(BF), wout_ref[...])
        pos = (nt - 1 - i) * tm + lax.broadcasted_iota(jnp.int32, (tm, 1), 0)
        for g, w in enumerate(POOL_WINDOWS):
            sl = slice(g * POOL_GC, (g + 1) * POOL_GC)
            zpre = _dot(p_ref[:, sl], wg_ref[g])
            dzg = dz[:, sl]
            dsc_ref[:, sl] += jnp.sum(dzg * zpre, axis=0, keepdims=True)
            z_ref[:, sl] = (zpre * sc_ref[:, sl]).astype(BF)
            dzp = (dzg * sc_ref[:, sl]).astype(BF)
            dzp_ref[:, sl] = dzp
            dp = _dot_nt(dzp, wg_ref[g])
            cnt = jnp.minimum(pos + 1, w).astype(F32)
            dpc = dp / cnt
            s = jnp.concatenate([dpc, head_ref[:, sl]], axis=0)
            step = 1
            while step < w:
                s = s + pltpu.roll(s, n - step, 0)
                step *= 2
            head_ref[:, sl] = dpc[:POOL_HALO, :]
            du_ref[:, sl] = (s[:tm, :] - dp).astype(BF)
        dh0 = _dot_nt(du_ref[...], win_ref[...])
        x0v = x0_ref[...]
        h0_ref[...] = _rms_fwd(x0v, g_ref[...]).astype(BF)
        dx, dg = _rms_bwd(x0v, g_ref[...], dh0)
        dx0_ref[...] = dx1v + dx
        dg_ref[...] += dg

    bf_rows = jax.ShapeDtypeStruct((T, D), BF)
    vec = jax.ShapeDtypeStruct((1, D), F32)
    body, in_specs, args = _after(
        dep, body,
        [pl.BlockSpec((tm, D), rev), pl.BlockSpec((tm, D), rev), _full_spec((1, D)), pl.BlockSpec((tm, D), rev),
         _full_spec((D, D)), _full_spec((4, POOL_GC, POOL_GC)), _full_spec((1, D)), _full_spec((D, D))],
        [dx1, x0, g0, p, w_in, w_grp, scale, w_out])
    return pl.pallas_call(
        body, name="pool_bwd", grid=(nt,),
        in_specs=in_specs,
        out_specs=[pl.BlockSpec((tm, D), rev)] * 5 + [_full_spec((1, D))] * 2,
        out_shape=[jax.ShapeDtypeStruct((T, D), F32), bf_rows, bf_rows, bf_rows, bf_rows, vec, vec],
        scratch_shapes=[pltpu.VMEM((POOL_HALO, D), F32)],
        compiler_params=_params("arbitrary"),
    )(*args)


def _ffn_fwd(x, g, wg_t, wu_t, wd, name, tm=512, fk=1408):
    T = x.shape[0]

    def body(x_ref, g_ref, wg_ref, wu_ref, wd_ref, xo_ref, a_ref, b_ref, s_ref, acc_ref, h_ref):
        k = pl.program_id(1)

        @pl.when(k == 0)
        def _():
            acc_ref[...] = jnp.zeros_like(acc_ref)
            h_ref[...] = _rms_fwd(x_ref[...], g_ref[...]).astype(BF)

        hv = h_ref[...]
        a = _dot_nt(hv, wg_ref[...])
        b = _dot_nt(hv, wu_ref[...])
        s = ((a * jax.nn.sigmoid(a)) * b).astype(BF)
        a_ref[...] = a.astype(BF)
        b_ref[...] = b.astype(BF)
        s_ref[...] = s
        acc_ref[...] += _dot(s, wd_ref[...])

        @pl.when(k == pl.num_programs(1) - 1)
        def _():
            xo_ref[...] = x_ref[...] + acc_ref[...]

    row = pl.BlockSpec((tm, D), lambda i, k: (i, 0))
    wsp = pl.BlockSpec((fk, D), lambda i, k: (k, 0))
    act = pl.BlockSpec((tm, fk), lambda i, k: (i, k))
    act_shape = jax.ShapeDtypeStruct((T, F), BF)
    return pl.pallas_call(
        body, name=name, grid=(T // tm, F // fk),
        in_specs=[row, pl.BlockSpec((1, D), lambda i, k: (0, 0)), wsp, wsp, wsp],
        out_specs=[row, act, act, act],
        out_shape=[jax.ShapeDtypeStruct((T, D), F32), act_shape, act_shape, act_shape],
        scratch_shapes=[pltpu.VMEM((tm, D), F32), pltpu.VMEM((tm, D), BF)],
        compiler_params=_params("parallel", "arbitrary"),
    )(x, g, wg_t, wu_t, wd)


def _ffn_bwd(dxo, x_in, g, a, b, wg_t, wu_t, wd, name, tm=512, fk=1408, dep=None):
    T = x_in.shape[0]

    def body(dxo_ref, x_ref, g_ref, a_ref, b_ref, wg_ref, wu_ref, wd_ref,
             dx_ref, da_ref, db_ref, h_ref, dy_ref, dg_ref, dh_ref):
        i = pl.program_id(0)
        k = pl.program_id(1)

        @pl.when(jnp.logical_and(i == 0, k == 0))
        def _():
            dg_ref[...] = jnp.zeros_like(dg_ref)

        @pl.when(k == 0)
        def _():
            h_ref[...] = _rms_fwd(x_ref[...], g_ref[...]).astype(BF)
            dy_ref[...] = dxo_ref[...].astype(BF)
            dh_ref[...] = jnp.zeros_like(dh_ref)

        ds = _dot_nt(dy_ref[...], wd_ref[...])
        av = a_ref[...].astype(F32)
        bv = b_ref[...].astype(F32)
        sig = jax.nn.sigmoid(av)
        db = (ds * (av * sig)).astype(BF)
        da = (ds * bv * (sig * (1.0 + av * (1.0 - sig)))).astype(BF)
        da_ref[...] = da
        db_ref[...] = db
        dh_ref[...] += _dot(da, wg_ref[...]) + _dot(db, wu_ref[...])

        @pl.when(k == pl.num_programs(1) - 1)
        def _():
            dx, dg = _rms_bwd(x_ref[...], g_ref[...], dh_ref[...])
            dx_ref[...] = dxo_ref[...] + dx
            dg_ref[...] += dg

    row = pl.BlockSpec((tm, D), lambda i, k: (i, 0))
    wsp = pl.BlockSpec((fk, D), lambda i, k: (k, 0))
    act = pl.BlockSpec((tm, fk), lambda i, k: (i, k))
    vec = pl.BlockSpec((1, D), lambda i, k: (0, 0))
    act_shape = jax.ShapeDtypeStruct((T, F), BF)
    body, in_specs, args = _after(dep, body, [row, row, vec, act, act, wsp, wsp, wsp], [dxo, x_in, g, a, b, wg_t, wu_t, wd])
    return pl.pallas_call(
        body, name=name, grid=(T // tm, F // fk),
        in_specs=in_specs,
        out_specs=[row, act, act, row, row, vec],
        out_shape=[jax.ShapeDtypeStruct((T, D), F32), act_shape, act_shape, jax.ShapeDtypeStruct((T, D), BF),
                   jax.ShapeDtypeStruct((T, D), BF), jax.ShapeDtypeStruct((1, D), F32)],
        scratch_shapes=[pltpu.VMEM((tm, D), F32)],
        compiler_params=_params("arbitrary", "arbitrary"),
    )(*args)


def _wgrad(a, b, name, tk=1024, mblk=None):
    T, M = a.shape
    N = b.shape[1]
    mblk = M if mblk is None else mblk

    def body(a_ref, b_ref, o_ref, acc_ref):
        t = pl.program_id(1)

        @pl.when(t == 0)
        def _():
            acc_ref[...] = jnp.zeros_like(acc_ref)

        acc_ref[...] += _dot_tn(a_ref[...].astype(BF), b_ref[...].astype(BF))

        @pl.when(t == pl.num_programs(1) - 1)
        def _():
            o_ref[...] = acc_ref[...].astype(BF)

    return pl.pallas_call(
        body, name=name, grid=(M // mblk, T // tk),
        in_specs=[pl.BlockSpec((tk, mblk), lambda m, t: (t, m)), pl.BlockSpec((tk, N), lambda m, t: (t, 0))],
        out_specs=pl.BlockSpec((mblk, N), lambda m, t: (m, 0)),
        out_shape=jax.ShapeDtypeStruct((M, N), BF),
        scratch_shapes=[pltpu.VMEM((mblk, N), F32)],
        compiler_params=_params("parallel", "arbitrary"),
    )(a, b)


def _wgrad_pool_groups(p, dzp, tk=512):
    T = p.shape[0]

    def body(p_ref, d_ref, o_ref, acc_ref):
        t = pl.program_id(1)

        @pl.when(t == 0)
        def _():
            acc_ref[...] = jnp.zeros_like(acc_ref)

        acc_ref[...] += _dot_tn(p_ref[...], d_ref[...])

        @pl.when(t == pl.num_programs(1) - 1)
        def _():
            o_ref[...] = acc_ref[...].astype(BF)

    blk = pl.BlockSpec((tk, POOL_GC), lambda g, t: (t, g))
    return pl.pallas_call(
        body, name="pool_wgrad_groups", grid=(4, T // tk),
        in_specs=[blk, blk],
        out_specs=pl.BlockSpec((None, POOL_GC, POOL_GC), lambda g, t: (g, 0, 0)),
        out_shape=jax.ShapeDtypeStruct((4, POOL_GC, POOL_GC), BF),
        scratch_shapes=[pltpu.VMEM((POOL_GC, POOL_GC), F32)],
        compiler_params=_params("parallel", "arbitrary"),
    )(p, dzp)


def _wgrad_stack(a, b3, name, tk=1024):
    T, M = a.shape
    N = b3.shape[2]

    def body(a_ref, b_ref, o_ref, acc_ref):
        t = pl.program_id(0)

        @pl.when(t == 0)
        def _():
            acc_ref[...] = jnp.zeros_like(acc_ref)

        av = a_ref[...]
        for w in range(3):
            acc_ref[w] += _dot_tn(av, b_ref[w])

        @pl.when(t == pl.num_programs(0) - 1)
        def _():
            o_ref[...] = acc_ref[...].astype(BF)

    return pl.pallas_call(
        body, name=name, grid=(T // tk,),
        in_specs=[_row_spec(tk, M), pl.BlockSpec((3, tk, N), lambda t: (0, t, 0))],
        out_specs=_full_spec((3, M, N)),
        out_shape=jax.ShapeDtypeStruct((3, M, N), BF),
        scratch_shapes=[pltpu.VMEM((3, M, N), F32)],
        compiler_params=_params("arbitrary"),
    )(a, b3)


def _rot_half(t):
    lane = lax.broadcasted_iota(jnp.int32, t.shape, 1)
    first = (lane % HEAD_DIM) < (HEAD_DIM // 2)
    return jnp.where(first, -pltpu.roll(t, 128 - HEAD_DIM // 2, 1), pltpu.roll(t, HEAD_DIM // 2, 1))


def _scatter_rows(dst_ref, scr_ref, d, cast):
    nc, rows, _ = scr_ref.shape
    n = rows // d
    for c in range(nc):
        sl = slice(c * 128, (c + 1) * 128)
        for r in range(d):
            src = scr_ref[c] if d == 1 else scr_ref.at[c][pl.ds(r, n, stride=d), :]
            dst_ref[r, :, sl] = src.astype(cast)


def _gather_rows(scr_ref, src_ref, d):
    nc, rows, _ = scr_ref.shape
    n = rows // d
    for c in range(nc):
        sl = slice(c * 128, (c + 1) * 128)
        for r in range(d):
            val = src_ref[r, :, sl].astype(F32)
            if d == 1:
                scr_ref[c] = val
            else:
                scr_ref.at[c][pl.ds(r, n, stride=d), :] = val


def _chunks_to_rows(scr_ref):
    nc = scr_ref.shape[0]
    return scr_ref[0] if nc == 1 else jnp.concatenate([scr_ref[c] for c in range(nc)], axis=1)


def _rows_to_chunks(scr_ref, val):
    for c in range(scr_ref.shape[0]):
        scr_ref[c] = val[:, c * 128:(c + 1) * 128]


def _rope(t, cv, sv, scale):
    return (t * cv + _rot_half(t) * sv) * scale


def _qkv_fwd(x, g, w3s, cos, sin, tm=512):
    T = x.shape[0]
    C = GROUP_LANES
    nc = C // 128

    def body(x_ref, g_ref, w0, w1, w2, cos_ref, sin_ref, o0, o1, o2, scr_ref):
        h = _rms_fwd(x_ref[...], g_ref[...]).astype(BF)
        cv = cos_ref[...]
        sv = sin_ref[...]
        for gi, (w_ref, o_ref, d) in enumerate(zip((w0, w1, w2), (o0, o1, o2), DILATIONS)):
            for w in range(3):
                t = _dot(h, w_ref[w])
                chunks = [t[:, c * 128:(c + 1) * 128] for c in range(nc)]
                if w < 2:
                    chunks = [_rope(tc, cv, sv, HEAD_DIM ** -0.5 if w == 0 else 1.0) for tc in chunks]
                if d == 1:
                    for c in range(nc):
                        o_ref[w, 0, :, c * 128:(c + 1) * 128] = chunks[c].astype(BF)
                else:
                    scr = scr_ref.at[gi * 3 + w]
                    for c in range(nc):
                        scr[c] = chunks[c]
                    _scatter_rows(o_ref.at[w], scr, d, BF)

    return pl.pallas_call(
        body, name="qkv_fwd", grid=(T // tm,),
        in_specs=[_row_spec(tm, D), _full_spec((1, D))] + [_full_spec((3, D, C))] * 3 + [_row_spec(tm, 128), _row_spec(tm, 128)],
        out_specs=[pl.BlockSpec((3, d, tm // d, C), lambda i: (0, 0, i, 0)) for d in DILATIONS],
        out_shape=[jax.ShapeDtypeStruct((3, d, T // d, C), BF) for d in DILATIONS],
        scratch_shapes=[pltpu.VMEM((9, nc, tm, 128), F32)],
        compiler_params=_params("parallel"),
    )(x, g, *w3s, cos, sin)


def _att_chunk(L):
    return min(L, 1024)


def _head_mask(shape, h):
    lane = lax.broadcasted_iota(jnp.int32, shape, 1)
    return (lane // HEAD_DIM) == (h % 2)


def _attn_fwd(qkv, nh, name):
    _, d, L, C = qkv.shape
    lc = _att_chunk(L)
    nblk = lc // ATT_W

    def body(q_ref, k_ref, kh_ref, v_ref, vh_ref, o_ref, st_ref):
        i = pl.program_id(1)
        qi = lax.broadcasted_iota(jnp.int32, (ATT_W, 2 * ATT_W), 0)
        kj = lax.broadcasted_iota(jnp.int32, (ATT_W, 2 * ATT_W), 1)
        band = jnp.logical_and(kj >= qi, kj <= qi + ATT_W)
        lane = lax.broadcasted_iota(jnp.int32, (ATT_W, 128), 1)

        def block(row0, kc, vc, mask):
            rows = pl.ds(row0, ATT_W)
            lses = []
            for hp in range(C // 128):
                sl = slice(hp * 128, (hp + 1) * 128)
                qp = q_ref[rows, sl]
                kp = kc[:, sl]
                vp = vc[:, sl]
                outs = []
                for h in range(2 * hp, min(2 * hp + 2, nh)):
                    hm = _head_mask(qp.shape, h)
                    s = _dot_nt(jnp.where(hm, qp, jnp.zeros_like(qp)), kp)
                    s = jnp.where(mask, s, NEG_INF)
                    m = jnp.max(s, axis=-1, keepdims=True)
                    e = jnp.exp(s - m)
                    den = jnp.sum(e, axis=-1, keepdims=True)
                    p = (e * pl.reciprocal(den)).astype(BF)
                    outs.append(_dot(p, vp))
                    lses.append(m + jnp.log(den))
                if len(outs) == 2:
                    o = jnp.where(_head_mask(outs[0].shape, 0), outs[0], outs[1])
                else:
                    o = jnp.where(_head_mask(outs[0].shape, 0), outs[0], 0.0)
                o_ref[rows, sl] = o.astype(BF)
            mm = lses[0]
            for l in lses[1:]:
                mm = jnp.maximum(mm, l)
            tot = jnp.exp(lses[0] - mm)
            for l in lses[1:]:
                tot = tot + jnp.exp(l - mm)
            tile = jnp.where(lane == LSE_GROUP_LANE, mm + jnp.log(tot) - math.log(nh), 0.0)
            for h, l in enumerate(lses):
                tile = jnp.where(lane == h, l, tile)
            st_ref[rows, :] = tile

        first_mask = jnp.logical_and(band, jnp.logical_or(kj >= ATT_W, i > 0))
        block(0, jnp.concatenate([kh_ref[...], k_ref[pl.ds(0, ATT_W), :]], axis=0),
              jnp.concatenate([vh_ref[...], v_ref[pl.ds(0, ATT_W), :]], axis=0), first_mask)

        if nblk > 1:
            def step(blk, carry):
                prev = pl.ds(pl.multiple_of((blk - 1) * ATT_W, ATT_W), 2 * ATT_W)
                block(pl.multiple_of(blk * ATT_W, ATT_W), k_ref[prev, :], v_ref[prev, :], band)
                return carry
            lax.fori_loop(1, nblk, step, 0)

    main = lambda w: pl.BlockSpec((None, None, lc, C), lambda r, i: (w, r, i, 0))
    halo = lambda w: pl.BlockSpec((None, None, ATT_W, C), lambda r, i: (w, r, jnp.maximum(i * nblk - 1, 0), 0))
    return pl.pallas_call(
        body, name=name, grid=(d, L // lc),
        in_specs=[main(0), main(1), halo(1), main(2), halo(2)],
        out_specs=[pl.BlockSpec((None, lc, C), lambda r, i: (r, i, 0)), pl.BlockSpec((None, lc, 128), lambda r, i: (r, i, 0))],
        out_shape=[jax.ShapeDtypeStruct((d, L, C), BF), jax.ShapeDtypeStruct((d, L, 128), F32)],
        compiler_params=_params("parallel", "arbitrary"),
    )(qkv, qkv, qkv, qkv, qkv)


def _attn_bwd(qkv, do, st, dst, nh, name):
    _, d, L, C = qkv.shape
    lc = _att_chunk(L)
    nblk = lc // ATT_W
    nchunk = L // lc

    def body(q_ref, qn_ref, k_ref, kh_ref, v_ref, vh_ref, do_ref, don_ref, st_ref, stn_ref, ds_ref, dsn_ref, o_ref):
        i = pl.program_id(1)
        qi = lax.broadcasted_iota(jnp.int32, (ATT_W, 2 * ATT_W), 0)
        kj = lax.broadcasted_iota(jnp.int32, (ATT_W, 2 * ATT_W), 1)
        band_q = jnp.logical_and(kj >= qi, kj <= qi + ATT_W)
        qa = lax.broadcasted_iota(jnp.int32, (2 * ATT_W, ATT_W), 0)
        kb = lax.broadcasted_iota(jnp.int32, (2 * ATT_W, ATT_W), 1)
        band_k = jnp.logical_and(qa >= kb, qa <= kb + ATT_W)

        def probs(qm, kp, lse, mask):
            s = _dot_nt(qm, kp)
            return jnp.where(mask, jnp.exp(s - lse), 0.0)

        def q_block(row0, kc, vc, mask):
            rows = pl.ds(row0, ATT_W)
            stv = st_ref[rows, :]
            dsv = ds_ref[rows, :]
            for hp in range(C // 128):
                sl = slice(hp * 128, (hp + 1) * 128)
                qp = q_ref[rows, sl]
                dop = do_ref[rows, sl]
                kp = kc[:, sl]
                vp = vc[:, sl]
                outs = []
                for h in range(2 * hp, min(2 * hp + 2, nh)):
                    hm = _head_mask(qp.shape, h)
                    p = probs(jnp.where(hm, qp, jnp.zeros_like(qp)), kp, _lane_col(stv, h), mask)
                    dp = _dot_nt(jnp.where(hm, dop, jnp.zeros_like(dop)), vp)
                    dsc = (p * (dp - _lane_col(dsv, h))).astype(BF)
                    outs.append(_dot(dsc, kp))
                if len(outs) == 2:
                    dq = jnp.where(_head_mask(outs[0].shape, 0), outs[0], outs[1])
                else:
                    dq = jnp.where(_head_mask(outs[0].shape, 0), outs[0], 0.0)
                o_ref[0, rows, sl] = dq

        def k_block(row0, qq, doo, stv, dsv, mask):
            rows = pl.ds(row0, ATT_W)
            for hp in range(C // 128):
                sl = slice(hp * 128, (hp + 1) * 128)
                qp = qq[:, sl]
                dop = doo[:, sl]
                kp = k_ref[rows, sl]
                vp = v_ref[rows, sl]
                dks, dvs = [], []
                for h in range(2 * hp, min(2 * hp + 2, nh)):
                    hm = _head_mask(qp.shape, h)
                    qm = jnp.where(hm, qp, jnp.zeros_like(qp))
                    dom = jnp.where(hm, dop, jnp.zeros_like(dop))
                    p = probs(qm, kp, _lane_col(stv, h), mask)
                    dp = _dot_nt(dom, vp)
                    dsc = (p * (dp - _lane_col(dsv, h))).astype(BF)
                    dks.append(_dot_tn(dsc, qm))
                    dvs.append(_dot_tn(p.astype(BF), dom))
                if len(dks) == 2:
                    o_ref[1, rows, sl] = dks[0] + dks[1]
                    o_ref[2, rows, sl] = dvs[0] + dvs[1]
                else:
                    o_ref[1, rows, sl] = dks[0]
                    o_ref[2, rows, sl] = dvs[0]

        first_mask = jnp.logical_and(band_q, jnp.logical_or(kj >= ATT_W, i > 0))
        q_block(0, jnp.concatenate([kh_ref[...], k_ref[pl.ds(0, ATT_W), :]], axis=0),
                jnp.concatenate([vh_ref[...], v_ref[pl.ds(0, ATT_W), :]], axis=0), first_mask)
        if nblk > 1:
            def q_step(blk, carry):
                prev = pl.ds(pl.multiple_of((blk - 1) * ATT_W, ATT_W), 2 * ATT_W)
                q_block(pl.multiple_of(blk * ATT_W, ATT_W), k_ref[prev, :], v_ref[prev, :], band_q)
                return carry
            lax.fori_loop(1, nblk, q_step, 0)

            def k_step(blk, carry):
                two = pl.ds(pl.multiple_of(blk * ATT_W, ATT_W), 2 * ATT_W)
                k_block(pl.multiple_of(blk * ATT_W, ATT_W), q_ref[two, :], do_ref[two, :], st_ref[two, :], ds_ref[two, :], band_k)
                return carry
            lax.fori_loop(0, nblk - 1, k_step, 0)

        last = pl.ds((nblk - 1) * ATT_W, ATT_W)
        last_mask = jnp.logical_and(band_k, jnp.logical_or(qa < ATT_W, i < nchunk - 1))
        k_block((nblk - 1) * ATT_W,
                jnp.concatenate([q_ref[last, :], qn_ref[...]], axis=0),
                jnp.concatenate([do_ref[last, :], don_ref[...]], axis=0),
                jnp.concatenate([st_ref[last, :], stn_ref[...]], axis=0),
                jnp.concatenate([ds_ref[last, :], dsn_ref[...]], axis=0), last_mask)

    nb_all = L // ATT_W
    main4 = lambda w: pl.BlockSpec((None, None, lc, C), lambda r, i: (w, r, i, 0))
    prev4 = lambda w: pl.BlockSpec((None, None, ATT_W, C), lambda r, i: (w, r, jnp.maximum(i * nblk - 1, 0), 0))
    next4 = lambda w: pl.BlockSpec((None, None, ATT_W, C), lambda r, i: (w, r, jnp.minimum((i + 1) * nblk, nb_all - 1), 0))
    main3 = lambda n: pl.BlockSpec((None, lc, n), lambda r, i: (r, i, 0))
    next3 = lambda n: pl.BlockSpec((None, ATT_W, n), lambda r, i: (r, jnp.minimum((i + 1) * nblk, nb_all - 1), 0))
    return pl.pallas_call(
        body, name=name, grid=(d, nchunk),
        in_specs=[main4(0), next4(0), main4(1), prev4(1), main4(2), prev4(2),
                  main3(C), next3(C), main3(128), next3(128), main3(128), next3(128)],
        out_specs=pl.BlockSpec((3, None, lc, C), lambda r, i: (0, r, i, 0)),
        out_shape=jax.ShapeDtypeStruct((3, d, L, C), F32),
        compiler_params=_params("parallel", "arbitrary"),
    )(qkv, qkv, qkv, qkv, qkv, qkv, do, do, st, st, dst, dst)


def _alpha_from(lse_nat):
    m = jnp.maximum(jnp.maximum(lse_nat[0], lse_nat[1]), lse_nat[2])
    e = [jnp.exp(l - m) for l in lse_nat]
    inv = 1.0 / (e[0] + e[1] + e[2])
    return [ei * inv for ei in e]


def _attn_out_fwd(x, os_, sts, wos, tm=512):
    T = x.shape[0]
    C = GROUP_LANES

    def body(x_ref, o0, o1, o2, s0, s1, s2, w0, w1, w2, xo_ref, m0, m1, m2, al_ref, oscr, sscr):
        o_refs, st_refs, w_refs, m_refs = (o0, o1, o2), (s0, s1, s2), (w0, w1, w2), (m0, m1, m2)
        lses = []
        for g, d in enumerate(DILATIONS):
            _gather_rows(sscr.at[g], st_refs[g], d)
            lses.append(_lane_col(sscr[g, 0], LSE_GROUP_LANE))
        alpha = _alpha_from(lses)
        y = x_ref[...]
        for g, d in enumerate(DILATIONS):
            _gather_rows(oscr, o_refs[g], d)
            mg = (_chunks_to_rows(oscr) * (3.0 * alpha[g])).astype(BF)
            m_refs[g][...] = mg
            y = y + _dot(mg, w_refs[g][...])
        xo_ref[...] = y
        lane = lax.broadcasted_iota(jnp.int32, (tm, 128), 1)
        al_ref[...] = jnp.where(lane == 0, alpha[0], jnp.where(lane == 1, alpha[1], jnp.where(lane == 2, alpha[2], 0.0)))

    o_specs = [pl.BlockSpec((d, tm // d, C), lambda i: (0, i, 0)) for d in DILATIONS]
    st_specs = [pl.BlockSpec((d, tm // d, 128), lambda i: (0, i, 0)) for d in DILATIONS]
    mshape = jax.ShapeDtypeStruct((T, C), BF)
    return pl.pallas_call(
        body, name="attn_out_fwd", grid=(T // tm,),
        in_specs=[_row_spec(tm, D)] + o_specs + st_specs + [_full_spec((C, D))] * 3,
        out_specs=[_row_spec(tm, D), _row_spec(tm, C), _row_spec(tm, C), _row_spec(tm, C), _row_spec(tm, 128)],
        out_shape=[jax.ShapeDtypeStruct((T, D), F32), mshape, mshape, mshape, jax.ShapeDtypeStruct((T, 128), F32)],
        scratch_shapes=[pltpu.VMEM((C // 128, tm, 128), F32), pltpu.VMEM((3, 1, tm, 128), F32)],
        compiler_params=_params("parallel"),
    )(x, *os_, *sts, *wos)


def _attn_out_bwd(dx, os_, sts, alpha, wos, tm=512, dep=None):
    T = dx.shape[0]
    C = GROUP_LANES

    def body(dx_ref, o0, o1, o2, s0, s1, s2, al_ref, w0, w1, w2, do0, do1, do2, ds0, ds1, ds2, dmscr, oscr, sscr, tscr):
        o_refs, st_refs, w_refs = (o0, o1, o2), (s0, s1, s2), (w0, w1, w2)
        do_refs, ds_refs = (do0, do1, do2), (ds0, ds1, ds2)
        dyb = dx_ref[...].astype(BF)
        alv = al_ref[...]
        alpha_g = [_lane_col(alv, g) for g in range(3)]
        dalpha = []
        for g, d in enumerate(DILATIONS):
            dm = _dot_nt(dyb, w_refs[g][...])
            _rows_to_chunks(dmscr.at[g], dm)
            _gather_rows(oscr.at[g], o_refs[g], d)
            _gather_rows(sscr.at[g], st_refs[g], d)
            dalpha.append(3.0 * jnp.sum(dm * _chunks_to_rows(oscr.at[g]), axis=-1, keepdims=True))
        mean_da = alpha_g[0] * dalpha[0] + alpha_g[1] * dalpha[1] + alpha_g[2] * dalpha[2]
        lane = lax.broadcasted_iota(jnp.int32, (tm, 128), 1)
        head_lane = lax.broadcasted_iota(jnp.int32, (tm, C), 1) // HEAD_DIM
        for g, d in enumerate(DILATIONS):
            nh = HEAD_GROUPS[g]
            dlse_g = alpha_g[g] * (dalpha[g] - mean_da)
            do_nat = _chunks_to_rows(dmscr.at[g]) * (3.0 * alpha_g[g])
            prod = do_nat * _chunks_to_rows(oscr.at[g])
            stv = sscr[g, 0]
            lse_g = _lane_col(stv, LSE_GROUP_LANE)
            tile = jnp.zeros((tm, 128), F32)
            for h in range(nh):
                delta = jnp.sum(jnp.where(head_lane == h, prod, 0.0), axis=-1, keepdims=True)
                dlse = dlse_g * jnp.exp(_lane_col(stv, h) - lse_g) * (1.0 / nh)
                tile = jnp.where(lane == h, delta - dlse, tile)
            _rows_to_chunks(dmscr.at[g], do_nat)
            _scatter_rows(do_refs[g], dmscr.at[g], d, BF)
            tscr[0] = tile
            _scatter_rows(ds_refs[g], tscr, d, F32)

    o_specs = [pl.BlockSpec((d, tm // d, C), lambda i: (0, i, 0)) for d in DILATIONS]
    st_specs = [pl.BlockSpec((d, tm // d, 128), lambda i: (0, i, 0)) for d in DILATIONS]
    body, in_specs, args = _after(
        dep, body, [_row_spec(tm, D)] + o_specs + st_specs + [_row_spec(tm, 128)] + [_full_spec((C, D))] * 3,
        [dx, *os_, *sts, alpha, *wos])
    return pl.pallas_call(
        body, name="attn_out_bwd", grid=(T // tm,),
        in_specs=in_specs,
        out_specs=o_specs + st_specs,
        out_shape=[jax.ShapeDtypeStruct((d, T // d, C), BF) for d in DILATIONS]
        + [jax.ShapeDtypeStruct((d, T // d, 128), F32) for d in DILATIONS],
        scratch_shapes=[pltpu.VMEM((3, C // 128, tm, 128), F32), pltpu.VMEM((3, C // 128, tm, 128), F32),
                        pltpu.VMEM((3, 1, tm, 128), F32), pltpu.VMEM((1, tm, 128), F32)],
        compiler_params=_params("parallel"),
    )(*args)


def _qkv_bwd(dqkvs, cos, sin, x2, g, w3s, dx3, tm=256):
    T = x2.shape[0]
    C = GROUP_LANES

    def body(dq0, dq1, dq2, cos_ref, sin_ref, x_ref, g_ref, w0, w1, w2, dx3_ref,
             dx_ref, n0, n1, n2, h_ref, dg_ref, scr, dh_ref):
        dq_refs, w_refs, n_refs = (dq0, dq1, dq2), (w0, w1, w2), (n0, n1, n2)

        @pl.when(pl.program_id(0) == 0)
        def _():
            dg_ref[...] = jnp.zeros_like(dg_ref)

        cv = cos_ref[...]
        sv = sin_ref[...]
        dh_ref[...] = jnp.zeros_like(dh_ref)
        for gi, d in enumerate(DILATIONS):
            for w in range(3):
                _gather_rows(scr, dq_refs[gi].at[w], d)
                if w < 2:
                    scale = HEAD_DIM ** -0.5 if w == 0 else 1.0
                    for c in range(C // 128):
                        t = scr[c]
                        scr[c] = (t * cv - _rot_half(t) * sv) * scale
                tb = _chunks_to_rows(scr).astype(BF)
                n_refs[gi][w] = tb
                dh_ref[...] += _dot_nt(tb, w_refs[gi][w])
        xv = x_ref[...]
        h_ref[...] = _rms_fwd(xv, g_ref[...]).astype(BF)
        dx, dg = _rms_bwd(xv, g_ref[...], dh_ref[...])
        dx_ref[...] = dx3_ref[...] + dx
        dg_ref[...] += dg

    dq_specs = [pl.BlockSpec((3, d, tm // d, C), lambda i: (0, 0, i, 0)) for d in DILATIONS]
    nat = pl.BlockSpec((3, tm, C), lambda i: (0, i, 0))
    nshape = jax.ShapeDtypeStruct((3, T, C), BF)
    return pl.pallas_call(
        body, name="qkv_bwd", grid=(T // tm,),
        in_specs=dq_specs + [_row_spec(tm, 128), _row_spec(tm, 128), _row_spec(tm, D), _full_spec((1, D))]
        + [_full_spec((3, D, C))] * 3 + [_row_spec(tm, D)],
        out_specs=[_row_spec(tm, D), nat, nat, nat, _row_spec(tm, D), _full_spec((1, D))],
        out_shape=[jax.ShapeDtypeStruct((T, D), F32), nshape, nshape, nshape, jax.ShapeDtypeStruct((T, D), BF),
                   jax.ShapeDtypeStruct((1, D), F32)],
        scratch_shapes=[pltpu.VMEM((C // 128, tm, 128), F32), pltpu.VMEM((tm, D), F32)],
        compiler_params=_params("arbitrary"),
    )(*dqkvs, cos, sin, x2, g, *w3s, dx3)


def _final_bwd(x, target, g, tm=512):
    T = x.shape[0]

    def body(x_ref, t_ref, g_ref, dx_ref, loss_ref, dg_ref):
        @pl.when(pl.program_id(0) == 0)
        def _():
            loss_ref[...] = jnp.zeros_like(loss_ref)
            dg_ref[...] = jnp.zeros_like(dg_ref)

        xv = x_ref[...]
        gv = g_ref[...]
        diff = _rms_fwd(xv, gv) - t_ref[...]
        loss_ref[...] += 0.5 * jnp.sum(jnp.mean(diff * diff, axis=-1, keepdims=True), axis=0, keepdims=True)
        dx, dg = _rms_bwd(xv, gv, diff * (1.0 / D))
        dx_ref[...] = dx
        dg_ref[...] += dg

    return pl.pallas_call(
        body, name="final_bwd", grid=(T // tm,),
        in_specs=[_row_spec(tm, D), _row_spec(tm, D), _full_spec((1, D))],
        out_specs=[_row_spec(tm, D), _full_spec((1, 1)), _full_spec((1, D))],
        out_shape=[jax.ShapeDtypeStruct((T, D), F32), jax.ShapeDtypeStruct((1, 1), F32), jax.ShapeDtypeStruct((1, D), F32)],
        compiler_params=_params("arbitrary"),
    )(x, target, g)


def _place():
    x, y, c = lax.axis_index("x"), lax.axis_index("y"), lax.axis_index("c")
    return x, y, c


def _allgather(arrs, name):
    n = len(arrs)

    def body(*refs):
        ins, outs = refs[:n], refs[n:2 * n]
        send_sems, recv_sems, local_sems = refs[2 * n:]
        x, y, c = _place()
        me, sibling = (x, y, c), (x, y, 1 - c)
        chips = [(1 - x, y), (x, 1 - y), (1 - x, 1 - y)]

        def slot(a, px, py, pc):
            return outs[a].at[4 * px + 2 * py + pc]

        def copy(a, k, block, to, src=None):
            return pltpu.make_async_remote_copy(
                src_ref=slot(a, *block) if src is None else src, dst_ref=slot(a, *block),
                send_sem=send_sems.at[a, k], recv_sem=recv_sems.at[a, k], device_id=to, device_id_type=MESH)

        mine = [pltpu.make_async_copy(ins[a], slot(a, *me), local_sems.at[a]) for a in range(n)]
        for cp in mine:
            cp.start()
        first = []
        for a in range(n):
            first.append(copy(a, 0, me, sibling, src=ins[a]))
            first += [copy(a, 1 + j, me, (*chip, c), src=ins[a]) for j, chip in enumerate(chips)]
        for cp in first:
            cp.start()
        passed = []
        for a in range(n):
            for j, chip in enumerate(chips):
                copy(a, 1 + j, (*chip, c), me).wait_recv()
                fwd = copy(a, 4 + j, (*chip, c), sibling)
                fwd.start()
                passed.append(fwd)
        for a in range(n):
            copy(a, 0, sibling, me).wait_recv()
            for j, chip in enumerate(chips):
                copy(a, 4 + j, (*chip, 1 - c), me).wait_recv()
        for cp in first + passed:
            cp.wait_send()
        for cp in mine:
            cp.wait()

    hbm = pl.BlockSpec(memory_space=pl.ANY)
    return pl.pallas_call(
        body, name=name,
        in_specs=[hbm] * n, out_specs=[hbm] * n,
        out_shape=[jax.ShapeDtypeStruct((N_DEV,) + a.shape, a.dtype) for a in arrs],
        scratch_shapes=[pltpu.SemaphoreType.DMA((n, 7)), pltpu.SemaphoreType.DMA((n, 7)), pltpu.SemaphoreType.DMA((n,))],
    )(*arrs)


def _peer(k):
    x, y, c = _place()
    px = 1 - x if k & 4 else x
    py = 1 - y if k & 2 else y
    pc = 1 - c if k & 1 else c
    return (px, py, pc), 4 * px + 2 * py + pc


HBM_SPEC = pl.BlockSpec(memory_space=pltpu.HBM)
SEM_SPEC = pl.BlockSpec(memory_space=pltpu.SEMAPHORE)
EFFECT = pltpu.SideEffectType.DATAFLOW_SIDE_EFFECTING


def _exchange_start(arrs, name, same_block=False, dep=None):
    n = len(arrs)
    n_dep = 0 if dep is None else 1

    def body(*refs):
        srcs, lands = refs[:n], refs[n:2 * n]
        send_sems, recv_sems, local_sems = refs[2 * n + n_dep:2 * n + n_dep + 3]
        token = refs[-1]
        x, y, c = _place()
        me = 4 * x + 2 * y + c
        block = (lambda a, j: srcs[a]) if same_block else (lambda a, j: srcs[a].at[j])
        for a in range(n):
            pltpu.make_async_copy(block(a, me), lands[a].at[me], local_sems.at[a]).start()
        for a in range(n):
            for k in range(1, N_DEV):
                to, to_idx = _peer(k)
                pltpu.make_async_remote_copy(
                    src_ref=block(a, to_idx), dst_ref=lands[a].at[me],
                    send_sem=send_sems.at[a * (N_DEV - 1) + k - 1], recv_sem=recv_sems.at[a * (N_DEV - 1) + k - 1], device_id=to, device_id_type=MESH).start()
        token[...] = jnp.zeros_like(token)

    land_shape = (lambda a: (N_DEV,) + a.shape) if same_block else (lambda a: a.shape)
    src_shapes = [pltpu.HBM(a.shape, a.dtype) for a in arrs]
    land_shapes = [pltpu.HBM(land_shape(a), a.dtype) for a in arrs]
    outs = pl.pallas_call(
        body, name=name,
        out_shape=(pltpu.SemaphoreType.DMA((n * (N_DEV - 1),)), pltpu.SemaphoreType.DMA((n * (N_DEV - 1),)),
                   pltpu.SemaphoreType.DMA((n,)), *src_shapes, *land_shapes, jax.ShapeDtypeStruct((8, 128), F32)),
        in_specs=[HBM_SPEC] * (2 * n) + [pl.BlockSpec(memory_space=pl.ANY)] * n_dep,
        out_specs=(SEM_SPEC, SEM_SPEC, SEM_SPEC, *([HBM_SPEC] * (2 * n)), pl.BlockSpec(memory_space=pltpu.VMEM)),
        input_output_aliases={i: 3 + i for i in range(2 * n)},
        compiler_params=pltpu.CompilerParams(has_side_effects=EFFECT),
    )(*[pltpu.with_memory_space_constraint(a, pltpu.HBM) for a in arrs],
      *[pltpu.with_memory_space_constraint(lax.empty(land_shape(a), a.dtype), pltpu.HBM) for a in arrs],
      *([] if dep is None else [dep]))
    return outs[0], outs[1], outs[2], outs[3:3 + n], outs[3 + n:3 + 2 * n], outs[3 + 2 * n]


def _exchange_wait(send_sems, recv_sems, local_sems, src_thru, land_thru, after, name):
    n = len(src_thru)
    same_block = src_thru[0].shape != land_thru[0].shape

    def body(*refs):
        srcs, lands = refs[:n], refs[n:2 * n]
        send_sems, recv_sems, local_sems = refs[2 * n:2 * n + 3]
        x, y, c = _place()
        me = 4 * x + 2 * y + c
        for a in range(n):
            pltpu.make_async_copy(srcs[a] if same_block else srcs[a].at[me], lands[a].at[me], local_sems.at[a]).wait()
            for k in range(1, N_DEV):
                frm, frm_idx = _peer(k)
                cp = pltpu.make_async_remote_copy(
                    src_ref=srcs[a] if same_block else srcs[a].at[frm_idx], dst_ref=lands[a].at[frm_idx],
                    send_sem=send_sems.at[a * (N_DEV - 1) + k - 1], recv_sem=recv_sems.at[a * (N_DEV - 1) + k - 1], device_id=frm, device_id_type=MESH)
                cp.wait_send()
                cp.wait_recv()

    outs = pl.pallas_call(
        body, name=name,
        out_shape=tuple(pltpu.HBM(a.shape, a.dtype) for a in (*src_thru, *land_thru)),
        in_specs=[HBM_SPEC] * (2 * n) + [SEM_SPEC, SEM_SPEC, SEM_SPEC, pl.BlockSpec(memory_space=pl.ANY)],
        out_specs=[HBM_SPEC] * (2 * n),
        input_output_aliases={i: i for i in range(2 * n)},
        compiler_params=pltpu.CompilerParams(has_side_effects=EFFECT),
    )(*src_thru, *land_thru, send_sems, recv_sems, local_sems, after)
    return outs[n:]


def _row_tile(rows):
    for t in (256, 176, 128, 8):
        if rows % t == 0:
            return t
    return rows


def _sum_parts(parts, name):
    K, R, C = parts.shape
    tr = _row_tile(R)

    def body(p_ref, o_ref):
        g = p_ref[0].astype(F32)
        for k in range(1, K):
            g = g + p_ref[k].astype(F32)
        o_ref[...] = g

    return pl.pallas_call(
        body, name=name, grid=(R // tr,),
        in_specs=[pl.BlockSpec((K, tr, C), lambda i: (0, i, 0))],
        out_specs=_row_spec(tr, C),
        out_shape=jax.ShapeDtypeStruct((R, C), F32),
        compiler_params=_params("parallel"),
    )(parts)


def _adamw(parts, w, m, v, name):
    K, R, C = parts.shape
    tr = _row_tile(R)

    def body(p_ref, w_ref, m_ref, v_ref, g_ref, d_ref, nm_ref, nv_ref):
        g = p_ref[0].astype(F32)
        for k in range(1, K):
            g = g + p_ref[k].astype(F32)
        nm = ADAM_B1 * m_ref[...] + (1.0 - ADAM_B1) * g
        nv = ADAM_B2 * v_ref[...] + (1.0 - ADAM_B2) * jnp.square(g)
        m_hat = nm / (1.0 - ADAM_B1 ** ADAM_STEP)
        v_hat = nv / (1.0 - ADAM_B2 ** ADAM_STEP)
        g_ref[...] = g
        d_ref[...] = -ADAM_LR * (m_hat / (jnp.sqrt(v_hat) + ADAM_EPS) + ADAM_WD * w_ref[...])
        nm_ref[...] = nm
        nv_ref[...] = nv

    blk = _row_spec(tr, C)
    shp = jax.ShapeDtypeStruct((R, C), F32)
    return pl.pallas_call(
        body, name=name, grid=(R // tr,),
        in_specs=[pl.BlockSpec((K, tr, C), lambda i: (0, i, 0)), blk, blk, blk],
        out_specs=[blk] * 4,
        out_shape=[shp] * 4,
        compiler_params=_params("parallel"),
    )(parts, w, m, v)


def _rope_tables(T):
    inv_freq = 1.0 / (ROPE_THETA ** (jnp.arange(0, HEAD_DIM, 2, dtype=F32) / HEAD_DIM))
    ang = jnp.arange(T, dtype=F32)[:, None] * inv_freq[None, :]
    cos, sin = jnp.cos(ang), jnp.sin(ang)
    return jnp.concatenate([cos] * 4, axis=-1), jnp.concatenate([sin] * 4, axis=-1)


def _pad_lanes(a, n):
    return jnp.pad(a, ((0, 0),) * (a.ndim - 1) + ((0, n - a.shape[-1]),))


LAYER0 = ("pool_in", "pool_grp", "pool_out", "gate0", "up0", "down0")
LAYER1 = ("qkv", "attn_out", "gate1", "up1", "down1")


def _layout_weights(gw):
    w = {}
    if "pool_in" in gw:
        w["pool_in"] = gw["pool_in"].reshape(D, D)
        w["pool_grp"] = jnp.transpose(gw["pool_grp"], (1, 0, 2, 3)).reshape(4, POOL_GC, POOL_GC)
        w["pool_out"] = gw["pool_out"].reshape(D, D)
    if "qkv" in gw:
        wqkv = jnp.transpose(gw["qkv"], (1, 0, 2)).reshape(D, 3 * D)
        wo = gw["attn_out"].reshape(D, D)
        w["qkv"], w["attn_out"] = [], []
        for nh, off in zip(HEAD_GROUPS, HEAD_OFFS):
            lo, n = off * HEAD_DIM, nh * HEAD_DIM
            w["qkv"].append(jnp.stack([_pad_lanes(wqkv[:, k * D + lo:k * D + lo + n], GROUP_LANES) for k in range(3)]))
            w["attn_out"].append(jnp.pad(wo[lo:lo + n], ((0, GROUP_LANES - n), (0, 0))))
    for nm in gw:
        if nm[:-1] in ("gate", "up", "down"):
            w[nm] = gw[nm].reshape(F, D)
    return w


def _local_step(x, target, w, layer1, dep0, norm_mix, norm_ffn, norm_final, pool_scale, emit):
    T = x.shape[0]
    cos, sin = _rope_tables(T)
    nm = [norm_mix[i:i + 1] for i in range(2)]
    nf = [norm_ffn[i:i + 1] for i in range(2)]
    nfin = norm_final.reshape(1, D)

    x1, p = _pool_fwd(x, nm[0], w["pool_in"], w["pool_grp"], pool_scale, w["pool_out"], dep=dep0)
    x2, a0, b0, s0 = _ffn_fwd(x1, nf[0], w["gate0"], w["up0"], w["down0"], "ffn_fwd0")
    w = {**w, **layer1(x2)}
    qkvs = _qkv_fwd(x2, nm[1], w["qkv"], cos, sin)
    att = [_attn_fwd(qkvs[g], HEAD_GROUPS[g], f"attn_fwd{g}") for g in range(3)]
    os_, sts = [a[0] for a in att], [a[1] for a in att]
    x3, m0, m1, m2, alpha = _attn_out_fwd(x2, os_, sts, w["attn_out"])
    x4, a1, b1, s1 = _ffn_fwd(x3, nf[1], w["gate1"], w["up1"], w["down1"], "ffn_fwd1")

    g = {}
    dx4, loss, dg_final = _final_bwd(x4, target, nfin)
    dx3, da1, db1, h3, dy4, dg_ffn1 = _ffn_bwd(dx4, x3, nf[1], a1, b1, w["gate1"], w["up1"], w["down1"], "ffn_bwd1")
    g["down1"] = _wgrad(s1, dy4, "down_wgrad1", mblk=F // 2)
    g["gate1"] = _wgrad(da1, h3, "gate_wgrad1", mblk=F // 2)
    g["up1"] = _wgrad(db1, h3, "up_wgrad1", mblk=F // 2)
    dep = emit("ffn1", g)
    dos_and_stats = _attn_out_bwd(dx3, os_, sts, alpha, w["attn_out"], dep=dep)
    dos, dsts = dos_and_stats[:3], dos_and_stats[3:]
    g["attn_out"] = [_wgrad(m, dx3, f"attn_out_wgrad{i}") for i, m in enumerate((m0, m1, m2))]
    dqkvs = [_attn_bwd(qkvs[gi], dos[gi], sts[gi], dsts[gi], HEAD_GROUPS[gi], f"attn_bwd{gi}") for gi in range(3)]
    dx2, n0, n1, n2, h2b, dg_mix1 = _qkv_bwd(dqkvs, cos, sin, x2, nm[1], w["qkv"], dx3)
    g["qkv"] = [_wgrad_stack(h2b, n, f"qkv_wgrad{i}") for i, n in enumerate((n0, n1, n2))]
    dep = emit("attn", g)
    dx1, da0, db0, h1, dy2, dg_ffn0 = _ffn_bwd(dx2, x1, nf[0], a0, b0, w["gate0"], w["up0"], w["down0"], "ffn_bwd0", dep=dep)
    g["down0"] = _wgrad(s0, dy2, "down_wgrad0", mblk=F // 2)
    g["gate0"] = _wgrad(da0, h1, "gate_wgrad0", mblk=F // 2)
    g["up0"] = _wgrad(db0, h1, "up_wgrad0", mblk=F // 2)
    dep = emit("ffn0", g)
    dx0, z, dzp, du, h0b, dscale, dg_mix0 = _pool_bwd(dx1, x, nm[0], p, w["pool_in"], w["pool_grp"], pool_scale, w["pool_out"], dep=dep)
    g["pool_out"] = _wgrad(z, dx1, "pool_out_wgrad")
    g["pool_in"] = _wgrad(h0b, du, "pool_in_wgrad")
    g["pool_grp"] = _wgrad_pool_groups(p, dzp)
    emit("pool", g)

    small = jnp.concatenate([dg_mix0, dg_mix1, dg_ffn0, dg_ffn1, dg_final, dscale,
                             jnp.broadcast_to(loss, (1, D)), jnp.zeros((1, D), F32)], axis=0)
    return dx0, small


GROUPS = {"ffn1": ("down1", "gate1", "up1"), "attn": ("qkv", "attn_out"), "ffn0": ("down0", "gate0", "up0"),
          "pool": ("pool_in", "pool_out", "pool_grp")}


def _grad_blocks(group, g):
    blocks = {}
    if group == "pool":
        blocks["pool_in"] = g["pool_in"].reshape(N_DEV, D // N_DEV, D)
        blocks["pool_out"] = g["pool_out"].reshape(N_DEV, D // N_DEV, D)
        blocks["pool_grp"] = jnp.transpose(g["pool_grp"].reshape(4, N_DEV, POOL_GC // N_DEV, POOL_GC), (1, 0, 2, 3)).reshape(N_DEV, 4 * POOL_GC // N_DEV, POOL_GC)
    elif group == "attn":
        wo = jnp.concatenate([gw[:nh * HEAD_DIM] for gw, nh in zip(g["attn_out"], HEAD_GROUPS)], axis=0)
        blocks["attn_out"] = wo.reshape(N_DEV, D // N_DEV, D)
        wqkv = jnp.concatenate([g["qkv"][gi][k][:, :HEAD_GROUPS[gi] * HEAD_DIM] for k in range(3) for gi in range(3)], axis=1)
        blocks["qkv"] = jnp.transpose(wqkv.reshape(D, N_DEV, 3 * D // N_DEV), (1, 0, 2))
    else:
        for nm in GROUPS[group]:
            blocks[nm] = g[nm].reshape(N_DEV, F // N_DEV, D)
    return [blocks[nm] for nm in GROUPS[group]]


def kernel(x, norm_mix, norm_ffn, norm_final, pool_w_in, pool_w_group, pool_scale, pool_w_out, attn_w_qkv, attn_w_out, ffn_w_gate, ffn_w_up, ffn_w_down, loss_target, m_norm_mix, m_norm_ffn, m_norm_final, m_pool_w_in, m_pool_w_group, m_pool_scale, m_pool_w_out, m_attn_w_qkv, m_attn_w_out, m_ffn_w_gate, m_ffn_w_up, m_ffn_w_down, v_norm_mix, v_norm_ffn, v_norm_final, v_pool_w_in, v_pool_w_group, v_pool_scale, v_pool_w_out, v_attn_w_qkv, v_attn_w_out, v_ffn_w_gate, v_ffn_w_up, v_ffn_w_down):
    shard = {
        "pool_in": pool_w_in[0], "pool_grp": pool_w_group[0], "pool_out": pool_w_out[0],
        "qkv": attn_w_qkv[0], "attn_out": attn_w_out[0],
    }
    for l in range(2):
        shard[f"gate{l}"] = ffn_w_gate[l].T
        shard[f"up{l}"] = ffn_w_up[l].T
        shard[f"down{l}"] = ffn_w_down[l]
    shard = {k: v.astype(BF) for k, v in shard.items()}
    w0 = _layout_weights(dict(zip(LAYER0, _allgather([shard[k] for k in LAYER0], "weights_allgather"))))
    l1 = _exchange_start([shard[k] for k in LAYER1], "weights_start_layer1", same_block=True, dep=w0["pool_in"])

    def layer1(after):
        return _layout_weights(dict(zip(LAYER1, _exchange_wait(*l1[:5], after, "weights_wait_layer1"))))

    started = {}

    def emit(group, g):
        started[group] = _exchange_start(_grad_blocks(group, g), f"grads_start_{group}")
        return started[group][5]

    grad_x, small = _local_step(x[0], loss_target[0], w0, layer1, l1[5], norm_mix, norm_ffn, norm_final, pool_scale, emit)

    small_st = _exchange_start([small], "small_start", same_block=True, dep=started["pool"][5])
    received = {}
    after = small_st[5]
    for group, st in started.items():
        lands = _exchange_wait(*st[:5], after, f"grads_wait_{group}")
        received.update(zip(GROUPS[group], lands))
    small_all = _exchange_wait(*small_st[:5], after, "small_wait")[0]

    def upd(parts, wt, mt, vt, name):
        shape = wt.shape
        r2 = lambda t: t.reshape(parts.shape[1:])
        outs = _adamw(parts, r2(wt), r2(mt), r2(vt), name)
        return [o.reshape(shape) for o in outs]

    res = {}
    res["pool_w_in"] = upd(received["pool_in"], pool_w_in, m_pool_w_in, v_pool_w_in, "adamw_pool_in")
    res["pool_w_group"] = upd(received["pool_grp"], pool_w_group, m_pool_w_group, v_pool_w_group, "adamw_pool_grp")
    res["pool_w_out"] = upd(received["pool_out"], pool_w_out, m_pool_w_out, v_pool_w_out, "adamw_pool_out")
    res["attn_w_qkv"] = upd(received["qkv"], attn_w_qkv, m_attn_w_qkv, v_attn_w_qkv, "adamw_qkv")
    res["attn_w_out"] = upd(received["attn_out"], attn_w_out, m_attn_w_out, v_attn_w_out, "adamw_attn_out")
    for nm, wt, mt, vt in (("gate", ffn_w_gate, m_ffn_w_gate, v_ffn_w_gate), ("up", ffn_w_up, m_ffn_w_up, v_ffn_w_up)):
        per_layer = []
        for l in range(2):
            gt = _sum_parts(received[f"{nm}{l}"], f"sum_{nm}{l}").T
            per_layer.append(_adamw(gt[None], wt[l], mt[l], vt[l], f"adamw_{nm}{l}"))
        res[f"ffn_w_{nm}"] = [jnp.stack([per_layer[0][i], per_layer[1][i]]) for i in range(4)]
    per_layer = [_adamw(received[f"down{l}"], ffn_w_down[l], m_ffn_w_down[l], v_ffn_w_down[l], f"adamw_down{l}") for l in range(2)]
    res["ffn_w_down"] = [jnp.stack([per_layer[0][i], per_layer[1][i]]) for i in range(4)]

    small_w = jnp.concatenate([norm_mix, norm_ffn, norm_final[None], pool_scale, jnp.zeros((2, D), F32)], axis=0)
    small_m = jnp.concatenate([m_norm_mix, m_norm_ffn, m_norm_final[None], m_pool_scale, jnp.zeros((2, D), F32)], axis=0)
    small_v = jnp.concatenate([v_norm_mix, v_norm_ffn, v_norm_final[None], v_pool_scale, jnp.ones((2, D), F32)], axis=0)
    sg, sd, sm, sv = _adamw(small_all, small_w, small_m, small_v, "adamw_small")
    loss = sg[6, 0]
    res["norm_mix"] = [t[0:2] for t in (sg, sd, sm, sv)]
    res["norm_ffn"] = [t[2:4] for t in (sg, sd, sm, sv)]
    res["norm_final"] = [t[4] for t in (sg, sd, sm, sv)]
    res["pool_scale"] = [t[5:6] for t in (sg, sd, sm, sv)]

    order = ["norm_mix", "norm_ffn", "norm_final", "pool_w_in", "pool_w_group", "pool_scale", "pool_w_out",
             "attn_w_qkv", "attn_w_out", "ffn_w_gate", "ffn_w_up", "ffn_w_down"]
    return (loss, grad_x[None], *[res[k][0] for k in order], *[res[k][1] for k in order],
            *[res[k][2] for k in order], *[res[k][3] for k in order])
```

```python
import math

import jax
import jax.numpy as jnp
from jax import lax
from jax.experimental import pallas as pl
from jax.experimental.pallas import tpu as pltpu

D = 1024
F = 2816
N_DEV = 8
EPS = 1e-6
NEG_INF = -1e30
POOL_WINDOWS = (2, 4, 8, 16)
POOL_HALO = 16
POOL_GC = 256
HEAD_DIM = 64
HEAD_GROUPS = (6, 5, 5)
HEAD_OFFS = (0, 6, 11)
DILATIONS = (1, 4, 16)
ATT_W = 128
GROUP_LANES = 384
LSE_GROUP_LANE = 8
ROPE_THETA = 10000.0
ADAM_LR, ADAM_B1, ADAM_B2, ADAM_EPS, ADAM_WD, ADAM_STEP = 0.001, 0.9, 0.999, 1e-08, 0.01, 10

BF = jnp.bfloat16
F32 = jnp.float32
VMEM_LIMIT = 56 * 1024 * 1024
MESH = pl.DeviceIdType.MESH


def _params(*sem):
    return pltpu.CompilerParams(dimension_semantics=sem, vmem_limit_bytes=VMEM_LIMIT)


def _dot(a, b):
    return jnp.dot(a, b, preferred_element_type=F32)


def _dot_nt(a, b):
    return lax.dot_general(a, b, (((1,), (1,)), ((), ())), preferred_element_type=F32)


def _dot_tn(a, b):
    return lax.dot_general(a, b, (((0,), (0,)), ((), ())), preferred_element_type=F32)


def _rms_fwd(xv, g):
    r = lax.rsqrt(jnp.mean(xv * xv, axis=-1, keepdims=True) + EPS)
    return (xv * r) * g


def _rms_bwd(xv, g, dh):
    r = lax.rsqrt(jnp.mean(xv * xv, axis=-1, keepdims=True) + EPS)
    xhat = xv * r
    dg = jnp.sum(dh * xhat, axis=0, keepdims=True)
    dxh = dh * g
    dx = r * (dxh - xhat * jnp.mean(dxh * xhat, axis=-1, keepdims=True))
    return dx, dg


def _lane_col(tile, j):
    lane = lax.broadcasted_iota(jnp.int32, tile.shape, 1)
    return jnp.sum(jnp.where(lane == j, tile, 0.0), axis=-1, keepdims=True)


def _row_spec(tm, n):
    return pl.BlockSpec((tm, n), lambda i: (i, 0))


def _full_spec(shape):
    nd = len(shape)
    return pl.BlockSpec(shape, lambda *_: (0,) * nd)


def _after(dep, body, in_specs, args):
    if dep is None:
        return body, list(in_specs), list(args)

    def body_after(dep_ref, *refs):
        body(*refs)

    return body_after, [pl.BlockSpec(memory_space=pl.ANY)] + list(in_specs), [dep] + list(args)


def _pool_fwd(x, g, w_in, w_grp, scale, w_out, tm=512, dep=None):
    T = x.shape[0]
    n = tm + POOL_HALO

    def body(x_ref, g_ref, win_ref, wg_ref, sc_ref, wout_ref, x1_ref, p_ref, tail_ref, z_ref):
        i = pl.program_id(0)

        @pl.when(i == 0)
        def _():
            tail_ref[...] = jnp.zeros_like(tail_ref)

        u = _dot(_rms_fwd(x_ref[...], g_ref[...]).astype(BF), win_ref[...])
        pos = i * tm + lax.broadcasted_iota(jnp.int32, (tm, 1), 0)
        for g, w in enumerate(POOL_WINDOWS):
            sl = slice(g * POOL_GC, (g + 1) * POOL_GC)
            ug = u[:, sl]
            s = jnp.concatenate([tail_ref[:, sl], ug], axis=0)
            step = 1
            while step < w:
                s = s + pltpu.roll(s, step, 0)
                step *= 2
            cnt = jnp.minimum(pos + 1, w).astype(F32)
            pg = (s[POOL_HALO:, :] / cnt - ug).astype(BF)
            p_ref[:, sl] = pg
            z_ref[:, sl] = (_dot(pg, wg_ref[g]) * sc_ref[:, sl]).astype(BF)
        tail_ref[...] = u[tm - POOL_HALO:, :]
        x1_ref[...] = x_ref[...] + _dot(z_ref[...], wout_ref[...])

    body, in_specs, args = _after(
        dep, body,
        [_row_spec(tm, D), _full_spec((1, D)), _full_spec((D, D)), _full_spec((4, POOL_GC, POOL_GC)), _full_spec((1, D)),
         _full_spec((D, D))],
        [x, g, w_in, w_grp, scale, w_out])
    return pl.pallas_call(
        body, name="pool_fwd", grid=(T // tm,),
        in_specs=in_specs,
        out_specs=[_row_spec(tm, D), _row_spec(tm, D)],
        out_shape=[jax.ShapeDtypeStruct((T, D), F32), jax.ShapeDtypeStruct((T, D), BF)],
        scratch_shapes=[pltpu.VMEM((POOL_HALO, D), F32), pltpu.VMEM((tm, D), BF)],
        compiler_params=_params("arbitrary"),
    )(*args)


def _pool_bwd(dx1, x0, g0, p, w_in, w_grp, scale, w_out, tm=512, dep=None):
    T = x0.shape[0]
    nt = T // tm
    n = tm + POOL_HALO
    rev = lambda i: (nt - 1 - i, 0)

    def body(dx1_ref, x0_ref, g_ref, p_ref, win_ref, wg_ref, sc_ref, wout_ref,
             dx0_ref, z_ref, dzp_ref, du_ref, h0_ref, dsc_ref, dg_ref, head_ref):
        i = pl.program_id(0)

        @pl.when(i == 0)
        def _():
            head_ref[...] = jnp.zeros_like(head_ref)
            dsc_ref[...] = jnp.zeros_like(dsc_ref)
            dg_ref[...] = jnp.zeros_like(dg_ref)

        dx1v = dx1_ref[...]
        dz = _dot_nt(dx1v.astype(BF), wout_ref[...])
        pos = (nt - 1 - i) * tm + lax.broadcasted_iota(jnp.int32, (tm, 1), 0)
        for g, w in enumerate(POOL_WINDOWS):
            sl = slice(g * POOL_GC, (g + 1) * POOL_GC)
            zpre = _dot(p_ref[:, sl], wg_ref[g])
            dzg = dz[:, sl]
            dsc_ref[:, sl] += jnp.sum(dzg * zpre, axis=0, keepdims=True)
            z_ref[:, sl] = (zpre * sc_ref[:, sl]).astype(BF)
            dzp = (dzg * sc_ref[:, sl]).astype(BF)
            dzp_ref[:, sl] = dzp
            dp = _dot_nt(dzp, wg_ref[g])
            cnt = jnp.minimum(pos + 1, w).astype(F32)
            dpc = dp / cnt
            s = jnp.concatenate([dpc, head_ref[:, sl]], axis=0)
            step = 1
            while step < w:
                s = s + pltpu.roll(s, n - step, 0)
                step *= 2
            head_ref[:, sl] = dpc[:POOL_HALO, :]
            du_ref[:, sl] = (s[:tm, :] - dp).astype(BF)
        dh0 = _dot_nt(du_ref[...], win_ref[...])
        x0v = x0_ref[...]
        h0_ref[...] = _rms_fwd(x0v, g_ref[...]).astype(BF)
        dx, dg = _rms_bwd(x0v, g_ref[...], dh0)
        dx0_ref[...] = dx1v + dx
        dg_ref[...] += dg

    bf_rows = jax.ShapeDtypeStruct((T, D), BF)
    vec = jax.ShapeDtypeStruct((1, D), F32)
    body, in_specs, args = _after(
        dep, body,
        [pl.BlockSpec((tm, D), rev), pl.BlockSpec((tm, D), rev), _full_spec((1, D)), pl.BlockSpec((tm, D), rev),
         _full_spec((D, D)), _full_spec((4, POOL_GC, POOL_GC)), _full_spec((1, D)), _full_spec((D, D))],
        [dx1, x0, g0, p, w_in, w_grp, scale, w_out])
    return pl.pallas_call(
        body, name="pool_bwd", grid=(nt,),
        in_specs=in_specs,
        out_specs=[pl.BlockSpec((tm, D), rev)] * 5 + [_full_spec((1, D))] * 2,
        out_shape=[jax.ShapeDtypeStruct((T, D), F32), bf_rows, bf_rows, bf_rows, bf_rows, vec, vec],
        scratch_shapes=[pltpu.VMEM((POOL_HALO, D), F32)],
        compiler_params=_params("arbitrary"),
    )(*args)


def _ffn_fwd(x, g, wg_t, wu_t, wd, name, tm=512, fk=1408):
    T = x.shape[0]

    def body(x_ref, g_ref, wg_ref, wu_ref, wd_ref, xo_ref, a_ref, b_ref, s_ref, acc_ref, h_ref):
        k = pl.program_id(1)

        @pl.when(k == 0)
        def _():
            acc_ref[...] = jnp.zeros_like(acc_ref)
            h_ref[...] = _rms_fwd(x_ref[...], g_ref[...]).astype(BF)

        hv = h_ref[...]
        a = _dot_nt(hv, wg_ref[...])
        b = _dot_nt(hv, wu_ref[...])
        s = ((a * jax.nn.sigmoid(a)) * b).astype(BF)
        a_ref[...] = a.astype(BF)
        b_ref[...] = b.astype(BF)
        s_ref[...] = s
        acc_ref[...] += _dot(s, wd_ref[...])

        @pl.when(k == pl.num_programs(1) - 1)
        def _():
            xo_ref[...] = x_ref[...] + acc_ref[...]

    row = pl.BlockSpec((tm, D), lambda i, k: (i, 0))
    wsp = pl.BlockSpec((fk, D), lambda i, k: (k, 0))
    act = pl.BlockSpec((tm, fk), lambda i, k: (i, k))
    act_shape = jax.ShapeDtypeStruct((T, F), BF)
    return pl.pallas_call(
        body, name=name, grid=(T // tm, F // fk),
        in_specs=[row, pl.BlockSpec((1, D), lambda i, k: (0, 0)), wsp, wsp, wsp],
        out_specs=[row, act, act, act],
        out_shape=[jax.ShapeDtypeStruct((T, D), F32), act_shape, act_shape, act_shape],
        scratch_shapes=[pltpu.VMEM((tm, D), F32), pltpu.VMEM((tm, D), BF)],
        compiler_params=_params("parallel", "arbitrary"),
    )(x, g, wg_t, wu_t, wd)


def _ffn_bwd(dxo, x_in, g, a, b, wg_t, wu_t, wd, name, tm=512, fk=1408, dep=None):
    T = x_in.shape[0]

    def body(dxo_ref, x_ref, g_ref, a_ref, b_ref, wg_ref, wu_ref, wd_ref,
             dx_ref, da_ref, db_ref, h_ref, dy_ref, dg_ref, dh_ref):
        i = pl.program_id(0)
        k = pl.program_id(1)

        @pl.when(jnp.logical_and(i == 0, k == 0))
        def _():
            dg_ref[...] = jnp.zeros_like(dg_ref)

        @pl.when(k == 0)
        def _():
            h_ref[...] = _rms_fwd(x_ref[...], g_ref[...]).astype(BF)
            dy_ref[...] = dxo_ref[...].astype(BF)
            dh_ref[...] = jnp.zeros_like(dh_ref)

        ds = _dot_nt(dy_ref[...], wd_ref[...])
        av = a_ref[...].astype(F32)
        bv = b_ref[...].astype(F32)
        sig = jax.nn.sigmoid(av)
        db = (ds * (av * sig)).astype(BF)
        da = (ds * bv * (sig * (1.0 + av * (1.0 - sig)))).astype(BF)
        da_ref[...] = da
        db_ref[...] = db
        dh_ref[...] += _dot(da, wg_ref[...]) + _dot(db, wu_ref[...])

        @pl.when(k == pl.num_programs(1) - 1)
        def _():
            dx, dg = _rms_bwd(x_ref[...], g_ref[...], dh_ref[...])
            dx_ref[...] = dxo_ref[...] + dx
            dg_ref[...] += dg

    row = pl.BlockSpec((tm, D), lambda i, k: (i, 0))
    wsp = pl.BlockSpec((fk, D), lambda i, k: (k, 0))
    act = pl.BlockSpec((tm, fk), lambda i, k: (i, k))
    vec = pl.BlockSpec((1, D), lambda i, k: (0, 0))
    act_shape = jax.ShapeDtypeStruct((T, F), BF)
    body, in_specs, args = _after(dep, body, [row, row, vec, act, act, wsp, wsp, wsp], [dxo, x_in, g, a, b, wg_t, wu_t, wd])
    return pl.pallas_call(
        body, name=name, grid=(T // tm, F // fk),
        in_specs=in_specs,
        out_specs=[row, act, act, row, row, vec],
        out_shape=[jax.ShapeDtypeStruct((T, D), F32), act_shape, act_shape, jax.ShapeDtypeStruct((T, D), BF),
                   jax.ShapeDtypeStruct((T, D), BF), jax.ShapeDtypeStruct((1, D), F32)],
        scratch_shapes=[pltpu.VMEM((tm, D), F32)],
        compiler_params=_params("arbitrary", "arbitrary"),
    )(*args)


def _wgrad(a, b, name, tk=1024, mblk=None):
    T, M = a.shape
    N = b.shape[1]
    mblk = M if mblk is None else mblk

    def body(a_ref, b_ref, o_ref, acc_ref):
        t = pl.program_id(1)

        @pl.when(t == 0)
        def _():
            acc_ref[...] = jnp.zeros_like(acc_ref)

        acc_ref[...] += _dot_tn(a_ref[...].astype(BF), b_ref[...].astype(BF))

        @pl.when(t == pl.num_programs(1) - 1)
        def _():
            o_ref[...] = acc_ref[...].astype(BF)

    return pl.pallas_call(
        body, name=name, grid=(M // mblk, T // tk),
        in_specs=[pl.BlockSpec((tk, mblk), lambda m, t: (t, m)), pl.BlockSpec((tk, N), lambda m, t: (t, 0))],
        out_specs=pl.BlockSpec((mblk, N), lambda m, t: (m, 0)),
        out_shape=jax.ShapeDtypeStruct((M, N), BF),
        scratch_shapes=[pltpu.VMEM((mblk, N), F32)],
        compiler_params=_params("parallel", "arbitrary"),
    )(a, b)


def _wgrad_multi(a_list, b, name, tk=1024):
    T, M = a_list[0].shape
    N = b.shape[1]
    n = len(a_list)

    def body(*refs):
        a_refs, b_ref, o_refs, acc_refs = refs[:n], refs[n], refs[n + 1:2 * n + 1], refs[2 * n + 1:]
        t = pl.program_id(0)

        @pl.when(t == 0)
        def _():
            for acc in acc_refs:
                acc[...] = jnp.zeros_like(acc)

        bv = b_ref[...].astype(BF)
        for a_ref, acc in zip(a_refs, acc_refs):
            acc[...] += _dot_tn(a_ref[...], bv)

        @pl.when(t == pl.num_programs(0) - 1)
        def _():
            for o_ref, acc in zip(o_refs, acc_refs):
                o_ref[...] = acc[...].astype(BF)

    return pl.pallas_call(
        body, name=name, grid=(T // tk,),
        in_specs=[_row_spec(tk, M)] * n + [_row_spec(tk, N)],
        out_specs=[_full_spec((M, N))] * n,
        out_shape=[jax.ShapeDtypeStruct((M, N), BF)] * n,
        scratch_shapes=[pltpu.VMEM((M, N), F32)] * n,
        compiler_params=_params("arbitrary"),
    )(*a_list, b)


def _wgrad_pool_groups(p, dzp, tk=2048):
    T = p.shape[0]

    def body(p_ref, d_ref, o_ref, acc_ref):
        t = pl.program_id(1)

        @pl.when(t == 0)
        def _():
            acc_ref[...] = jnp.zeros_like(acc_ref)

        acc_ref[...] += _dot_tn(p_ref[...], d_ref[...])

        @pl.when(t == pl.num_programs(1) - 1)
        def _():
            o_ref[...] = acc_ref[...].astype(BF)

    blk = pl.BlockSpec((tk, POOL_GC), lambda g, t: (t, g))
    return pl.pallas_call(
        body, name="pool_wgrad_groups", grid=(4, T // tk),
        in_specs=[blk, blk],
        out_specs=pl.BlockSpec((None, POOL_GC, POOL_GC), lambda g, t: (g, 0, 0)),
        out_shape=jax.ShapeDtypeStruct((4, POOL_GC, POOL_GC), BF),
        scratch_shapes=[pltpu.VMEM((POOL_GC, POOL_GC), F32)],
        compiler_params=_params("parallel", "arbitrary"),
    )(p, dzp)


def _wgrad_stack(a, b3, name, tk=1024):
    T, M = a.shape
    N = b3.shape[2]

    def body(a_ref, b_ref, o_ref, acc_ref):
        t = pl.program_id(0)

        @pl.when(t == 0)
        def _():
            acc_ref[...] = jnp.zeros_like(acc_ref)

        av = a_ref[...]
        for w in range(3):
            acc_ref[w] += _dot_tn(av, b_ref[w])

        @pl.when(t == pl.num_programs(0) - 1)
        def _():
            o_ref[...] = acc_ref[...].astype(BF)

    return pl.pallas_call(
        body, name=name, grid=(T // tk,),
        in_specs=[_row_spec(tk, M), pl.BlockSpec((3, tk, N), lambda t: (0, t, 0))],
        out_specs=_full_spec((3, M, N)),
        out_shape=jax.ShapeDtypeStruct((3, M, N), BF),
        scratch_shapes=[pltpu.VMEM((3, M, N), F32)],
        compiler_params=_params("arbitrary"),
    )(a, b3)


def _rot_half(t):
    lane = lax.broadcasted_iota(jnp.int32, t.shape, 1)
    first = (lane % HEAD_DIM) < (HEAD_DIM // 2)
    return jnp.where(first, -pltpu.roll(t, 128 - HEAD_DIM // 2, 1), pltpu.roll(t, HEAD_DIM // 2, 1))


def _scatter_rows(dst_ref, scr_ref, d, cast):
    nc, rows, _ = scr_ref.shape
    n = rows // d
    for c in range(nc):
        sl = slice(c * 128, (c + 1) * 128)
        for r in range(d):
            src = scr_ref[c] if d == 1 else scr_ref.at[c][pl.ds(r, n, stride=d), :]
            dst_ref[r, :, sl] = src.astype(cast)


def _gather_rows(scr_ref, src_ref, d):
    nc, rows, _ = scr_ref.shape
    n = rows // d
    for c in range(nc):
        sl = slice(c * 128, (c + 1) * 128)
        for r in range(d):
            val = src_ref[r, :, sl].astype(F32)
            if d == 1:
                scr_ref[c] = val
            else:
                scr_ref.at[c][pl.ds(r, n, stride=d), :] = val


def _chunks_to_rows(scr_ref):
    nc = scr_ref.shape[0]
    return scr_ref[0] if nc == 1 else jnp.concatenate([scr_ref[c] for c in range(nc)], axis=1)


def _rows_to_chunks(scr_ref, val):
    for c in range(scr_ref.shape[0]):
        scr_ref[c] = val[:, c * 128:(c + 1) * 128]


def _rope(t, cv, sv, scale):
    return (t * cv + _rot_half(t) * sv) * scale


def _qkv_fwd(x, g, w3s, cos, sin, tm=512):
    T = x.shape[0]
    C = GROUP_LANES
    nc = C // 128

    def body(x_ref, g_ref, w0, w1, w2, cos_ref, sin_ref, o0, o1, o2, scr_ref):
        h = _rms_fwd(x_ref[...], g_ref[...]).astype(BF)
        cv = cos_ref[...]
        sv = sin_ref[...]
        for gi, (w_ref, o_ref, d) in enumerate(zip((w0, w1, w2), (o0, o1, o2), DILATIONS)):
            for w in range(3):
                t = _dot(h, w_ref[w])
                chunks = [t[:, c * 128:(c + 1) * 128] for c in range(nc)]
                if w < 2:
                    chunks = [_rope(tc, cv, sv, HEAD_DIM ** -0.5 if w == 0 else 1.0) for tc in chunks]
                if d == 1:
                    for c in range(nc):
                        o_ref[w, 0, :, c * 128:(c + 1) * 128] = chunks[c].astype(BF)
                else:
                    scr = scr_ref.at[gi * 3 + w]
                    for c in range(nc):
                        scr[c] = chunks[c]
                    _scatter_rows(o_ref.at[w], scr, d, BF)

    return pl.pallas_call(
        body, name="qkv_fwd", grid=(T // tm,),
        in_specs=[_row_spec(tm, D), _full_spec((1, D))] + [_full_spec((3, D, C))] * 3 + [_row_spec(tm, 128), _row_spec(tm, 128)],
        out_specs=[pl.BlockSpec((3, d, tm // d, C), lambda i: (0, 0, i, 0)) for d in DILATIONS],
        out_shape=[jax.ShapeDtypeStruct((3, d, T // d, C), BF) for d in DILATIONS],
        scratch_shapes=[pltpu.VMEM((9, nc, tm, 128), F32)],
        compiler_params=_params("parallel"),
    )(x, g, *w3s, cos, sin)


def _att_chunk(L):
    return min(L, 1024)


def _head_mask(shape, h):
    lane = lax.broadcasted_iota(jnp.int32, shape, 1)
    return (lane // HEAD_DIM) == (h % 2)


def _attn_fwd(qkv, nh, name):
    _, d, L, C = qkv.shape
    lc = _att_chunk(L)
    nblk = lc // ATT_W

    def body(q_ref, k_ref, kh_ref, v_ref, vh_ref, o_ref, st_ref):
        i = pl.program_id(1)
        qi = lax.broadcasted_iota(jnp.int32, (ATT_W, 2 * ATT_W), 0)
        kj = lax.broadcasted_iota(jnp.int32, (ATT_W, 2 * ATT_W), 1)
        band = jnp.logical_and(kj >= qi, kj <= qi + ATT_W)
        lane = lax.broadcasted_iota(jnp.int32, (ATT_W, 128), 1)

        def block(row0, kc, vc, mask):
            rows = pl.ds(row0, ATT_W)
            lses = []
            for hp in range(C // 128):
                sl = slice(hp * 128, (hp + 1) * 128)
                qp = q_ref[rows, sl]
                kp = kc[:, sl]
                vp = vc[:, sl]
                outs = []
                for h in range(2 * hp, min(2 * hp + 2, nh)):
                    hm = _head_mask(qp.shape, h)
                    s = _dot_nt(jnp.where(hm, qp, jnp.zeros_like(qp)), kp)
                    s = jnp.where(mask, s, NEG_INF)
                    m = jnp.max(s, axis=-1, keepdims=True)
                    e = jnp.exp(s - m)
                    den = jnp.sum(e, axis=-1, keepdims=True)
                    p = (e * pl.reciprocal(den)).astype(BF)
                    outs.append(_dot(p, vp))
                    lses.append(m + jnp.log(den))
                if len(outs) == 2:
                    o = jnp.where(_head_mask(outs[0].shape, 0), outs[0], outs[1])
                else:
                    o = jnp.where(_head_mask(outs[0].shape, 0), outs[0], 0.0)
                o_ref[rows, sl] = o.astype(BF)
            mm = lses[0]
            for l in lses[1:]:
                mm = jnp.maximum(mm, l)
            tot = jnp.exp(lses[0] - mm)
            for l in lses[1:]:
                tot = tot + jnp.exp(l - mm)
            tile = jnp.where(lane == LSE_GROUP_LANE, mm + jnp.log(tot) - math.log(nh), 0.0)
            for h, l in enumerate(lses):
                tile = jnp.where(lane == h, l, tile)
            st_ref[rows, :] = tile

        first_mask = jnp.logical_and(band, jnp.logical_or(kj >= ATT_W, i > 0))
        block(0, jnp.concatenate([kh_ref[...], k_ref[pl.ds(0, ATT_W), :]], axis=0),
              jnp.concatenate([vh_ref[...], v_ref[pl.ds(0, ATT_W), :]], axis=0), first_mask)

        if nblk > 1:
            def step(blk, carry):
                prev = pl.ds(pl.multiple_of((blk - 1) * ATT_W, ATT_W), 2 * ATT_W)
                block(pl.multiple_of(blk * ATT_W, ATT_W), k_ref[prev, :], v_ref[prev, :], band)
                return carry
            lax.fori_loop(1, nblk, step, 0, unroll=True)

    main = lambda w: pl.BlockSpec((None, None, lc, C), lambda r, i: (w, r, i, 0))
    halo = lambda w: pl.BlockSpec((None, None, ATT_W, C), lambda r, i: (w, r, jnp.maximum(i * nblk - 1, 0), 0))
    return pl.pallas_call(
        body, name=name, grid=(d, L // lc),
        in_specs=[main(0), main(1), halo(1), main(2), halo(2)],
        out_specs=[pl.BlockSpec((None, lc, C), lambda r, i: (r, i, 0)), pl.BlockSpec((None, lc, 128), lambda r, i: (r, i, 0))],
        out_shape=[jax.ShapeDtypeStruct((d, L, C), BF), jax.ShapeDtypeStruct((d, L, 128), F32)],
        compiler_params=_params("parallel", "arbitrary"),
    )(qkv, qkv, qkv, qkv, qkv)


def _attn_bwd(qkv, do, st, dst, nh, name):
    _, d, L, C = qkv.shape
    lc = _att_chunk(L)
    nblk = lc // ATT_W
    nchunk = L // lc

    def body(q_ref, qn_ref, k_ref, kh_ref, v_ref, vh_ref, do_ref, don_ref, st_ref, stn_ref, ds_ref, dsn_ref, o_ref):
        i = pl.program_id(1)
        qi = lax.broadcasted_iota(jnp.int32, (ATT_W, 2 * ATT_W), 0)
        kj = lax.broadcasted_iota(jnp.int32, (ATT_W, 2 * ATT_W), 1)
        band_q = jnp.logical_and(kj >= qi, kj <= qi + ATT_W)
        qa = lax.broadcasted_iota(jnp.int32, (2 * ATT_W, ATT_W), 0)
        kb = lax.broadcasted_iota(jnp.int32, (2 * ATT_W, ATT_W), 1)
        band_k = jnp.logical_and(qa >= kb, qa <= kb + ATT_W)

        def probs(qm, kp, lse, mask):
            s = _dot_nt(qm, kp)
            return jnp.where(mask, jnp.exp(s - lse), 0.0)

        def q_block(row0, kc, vc, mask):
            rows = pl.ds(row0, ATT_W)
            stv = st_ref[rows, :]
            dsv = ds_ref[rows, :]
            for hp in range(C // 128):
                sl = slice(hp * 128, (hp + 1) * 128)
                qp = q_ref[rows, sl]
                dop = do_ref[rows, sl]
                kp = kc[:, sl]
                vp = vc[:, sl]
                outs = []
                for h in range(2 * hp, min(2 * hp + 2, nh)):
                    hm = _head_mask(qp.shape, h)
                    p = probs(jnp.where(hm, qp, jnp.zeros_like(qp)), kp, _lane_col(stv, h), mask)
                    dp = _dot_nt(jnp.where(hm, dop, jnp.zeros_like(dop)), vp)
                    dsc = (p * (dp - _lane_col(dsv, h))).astype(BF)
                    outs.append(_dot(dsc, kp))
                if len(outs) == 2:
                    dq = jnp.where(_head_mask(outs[0].shape, 0), outs[0], outs[1])
                else:
                    dq = jnp.where(_head_mask(outs[0].shape, 0), outs[0], 0.0)
                o_ref[0, rows, sl] = dq

        def k_block(row0, qq, doo, stv, dsv, mask):
            rows = pl.ds(row0, ATT_W)
            for hp in range(C // 128):
                sl = slice(hp * 128, (hp + 1) * 128)
                qp = qq[:, sl]
                dop = doo[:, sl]
                kp = k_ref[rows, sl]
                vp = v_ref[rows, sl]
                dks, dvs = [], []
                for h in range(2 * hp, min(2 * hp + 2, nh)):
                    hm = _head_mask(qp.shape, h)
                    qm = jnp.where(hm, qp, jnp.zeros_like(qp))
                    dom = jnp.where(hm, dop, jnp.zeros_like(dop))
                    p = probs(qm, kp, _lane_col(stv, h), mask)
                    dp = _dot_nt(dom, vp)
                    dsc = (p * (dp - _lane_col(dsv, h))).astype(BF)
                    dks.append(_dot_tn(dsc, qm))
                    dvs.append(_dot_tn(p.astype(BF), dom))
                if len(dks) == 2:
                    o_ref[1, rows, sl] = dks[0] + dks[1]
                    o_ref[2, rows, sl] = dvs[0] + dvs[1]
                else:
                    o_ref[1, rows, sl] = dks[0]
                    o_ref[2, rows, sl] = dvs[0]

        first_mask = jnp.logical_and(band_q, jnp.logical_or(kj >= ATT_W, i > 0))
        q_block(0, jnp.concatenate([kh_ref[...], k_ref[pl.ds(0, ATT_W), :]], axis=0),
                jnp.concatenate([vh_ref[...], v_ref[pl.ds(0, ATT_W), :]], axis=0), first_mask)
        if nblk > 1:
            def q_step(blk, carry):
                prev = pl.ds(pl.multiple_of((blk - 1) * ATT_W, ATT_W), 2 * ATT_W)
                q_block(pl.multiple_of(blk * ATT_W, ATT_W), k_ref[prev, :], v_ref[prev, :], band_q)
                return carry
            lax.fori_loop(1, nblk, q_step, 0, unroll=True)

            def k_step(blk, carry):
                two = pl.ds(pl.multiple_of(blk * ATT_W, ATT_W), 2 * ATT_W)
                k_block(pl.multiple_of(blk * ATT_W, ATT_W), q_ref[two, :], do_ref[two, :], st_ref[two, :], ds_ref[two, :], band_k)
                return carry
            lax.fori_loop(0, nblk - 1, k_step, 0, unroll=True)

        last = pl.ds((nblk - 1) * ATT_W, ATT_W)
        last_mask = jnp.logical_and(band_k, jnp.logical_or(qa < ATT_W, i < nchunk - 1))
        k_block((nblk - 1) * ATT_W,
                jnp.concatenate([q_ref[last, :], qn_ref[...]], axis=0),
                jnp.concatenate([do_ref[last, :], don_ref[...]], axis=0),
                jnp.concatenate([st_ref[last, :], stn_ref[...]], axis=0),
                jnp.concatenate([ds_ref[last, :], dsn_ref[...]], axis=0), last_mask)

    nb_all = L // ATT_W
    main4 = lambda w: pl.BlockSpec((None, None, lc, C), lambda r, i: (w, r, i, 0))
    prev4 = lambda w: pl.BlockSpec((None, None, ATT_W, C), lambda r, i: (w, r, jnp.maximum(i * nblk - 1, 0), 0))
    next4 = lambda w: pl.BlockSpec((None, None, ATT_W, C), lambda r, i: (w, r, jnp.minimum((i + 1) * nblk, nb_all - 1), 0))
    main3 = lambda n: pl.BlockSpec((None, lc, n), lambda r, i: (r, i, 0))
    next3 = lambda n: pl.BlockSpec((None, ATT_W, n), lambda r, i: (r, jnp.minimum((i + 1) * nblk, nb_all - 1), 0))
    return pl.pallas_call(
        body, name=name, grid=(d, nchunk),
        in_specs=[main4(0), next4(0), main4(1), prev4(1), main4(2), prev4(2),
                  main3(C), next3(C), main3(128), next3(128), main3(128), next3(128)],
        out_specs=pl.BlockSpec((3, None, lc, C), lambda r, i: (0, r, i, 0)),
        out_shape=jax.ShapeDtypeStruct((3, d, L, C), F32),
        compiler_params=_params("parallel", "arbitrary"),
    )(qkv, qkv, qkv, qkv, qkv, qkv, do, do, st, st, dst, dst)


def _alpha_from(lse_nat):
    m = jnp.maximum(jnp.maximum(lse_nat[0], lse_nat[1]), lse_nat[2])
    e = [jnp.exp(l - m) for l in lse_nat]
    inv = 1.0 / (e[0] + e[1] + e[2])
    return [ei * inv for ei in e]


def _attn_out_fwd(x, os_, sts, wos, tm=512):
    T = x.shape[0]
    C = GROUP_LANES

    def body(x_ref, o0, o1, o2, s0, s1, s2, w0, w1, w2, xo_ref, m0, m1, m2, al_ref, oscr, sscr):
        o_refs, st_refs, w_refs, m_refs = (o0, o1, o2), (s0, s1, s2), (w0, w1, w2), (m0, m1, m2)
        lses = []
        for g, d in enumerate(DILATIONS):
            _gather_rows(sscr.at[g], st_refs[g], d)
            lses.append(_lane_col(sscr[g, 0], LSE_GROUP_LANE))
        alpha = _alpha_from(lses)
        y = x_ref[...]
        for g, d in enumerate(DILATIONS):
            _gather_rows(oscr, o_refs[g], d)
            mg = (_chunks_to_rows(oscr) * (3.0 * alpha[g])).astype(BF)
            m_refs[g][...] = mg
            y = y + _dot(mg, w_refs[g][...])
        xo_ref[...] = y
        lane = lax.broadcasted_iota(jnp.int32, (tm, 128), 1)
        al_ref[...] = jnp.where(lane == 0, alpha[0], jnp.where(lane == 1, alpha[1], jnp.where(lane == 2, alpha[2], 0.0)))

    o_specs = [pl.BlockSpec((d, tm // d, C), lambda i: (0, i, 0)) for d in DILATIONS]
    st_specs = [pl.BlockSpec((d, tm // d, 128), lambda i: (0, i, 0)) for d in DILATIONS]
    mshape = jax.ShapeDtypeStruct((T, C), BF)
    return pl.pallas_call(
        body, name="attn_out_fwd", grid=(T // tm,),
        in_specs=[_row_spec(tm, D)] + o_specs + st_specs + [_full_spec((C, D))] * 3,
        out_specs=[_row_spec(tm, D), _row_spec(tm, C), _row_spec(tm, C), _row_spec(tm, C), _row_spec(tm, 128)],
        out_shape=[jax.ShapeDtypeStruct((T, D), F32), mshape, mshape, mshape, jax.ShapeDtypeStruct((T, 128), F32)],
        scratch_shapes=[pltpu.VMEM((C // 128, tm, 128), F32), pltpu.VMEM((3, 1, tm, 128), F32)],
        compiler_params=_params("parallel"),
    )(x, *os_, *sts, *wos)


def _attn_out_bwd(dx, os_, sts, alpha, wos, tm=512, dep=None):
    T = dx.shape[0]
    C = GROUP_LANES

    def body(dx_ref, o0, o1, o2, s0, s1, s2, al_ref, w0, w1, w2, do0, do1, do2, ds0, ds1, ds2, dmscr, oscr, sscr, tscr):
        o_refs, st_refs, w_refs = (o0, o1, o2), (s0, s1, s2), (w0, w1, w2)
        do_refs, ds_refs = (do0, do1, do2), (ds0, ds1, ds2)
        dyb = dx_ref[...].astype(BF)
        alv = al_ref[...]
        alpha_g = [_lane_col(alv, g) for g in range(3)]
        dalpha = []
        for g, d in enumerate(DILATIONS):
            dm = _dot_nt(dyb, w_refs[g][...])
            _rows_to_chunks(dmscr.at[g], dm)
            _gather_rows(oscr.at[g], o_refs[g], d)
            _gather_rows(sscr.at[g], st_refs[g], d)
            dalpha.append(3.0 * jnp.sum(dm * _chunks_to_rows(oscr.at[g]), axis=-1, keepdims=True))
        mean_da = alpha_g[0] * dalpha[0] + alpha_g[1] * dalpha[1] + alpha_g[2] * dalpha[2]
        lane = lax.broadcasted_iota(jnp.int32, (tm, 128), 1)
        head_lane = lax.broadcasted_iota(jnp.int32, (tm, C), 1) // HEAD_DIM
        for g, d in enumerate(DILATIONS):
            nh = HEAD_GROUPS[g]
            dlse_g = alpha_g[g] * (dalpha[g] - mean_da)
            do_nat = _chunks_to_rows(dmscr.at[g]) * (3.0 * alpha_g[g])
            prod = do_nat * _chunks_to_rows(oscr.at[g])
            stv = sscr[g, 0]
            lse_g = _lane_col(stv, LSE_GROUP_LANE)
            tile = jnp.zeros((tm, 128), F32)
            for h in range(nh):
                delta = jnp.sum(jnp.where(head_lane == h, prod, 0.0), axis=-1, keepdims=True)
                dlse = dlse_g * jnp.exp(_lane_col(stv, h) - lse_g) * (1.0 / nh)
                tile = jnp.where(lane == h, delta - dlse, tile)
            _rows_to_chunks(dmscr.at[g], do_nat)
            _scatter_rows(do_refs[g], dmscr.at[g], d, BF)
            tscr[0] = tile
            _scatter_rows(ds_refs[g], tscr, d, F32)

    o_specs = [pl.BlockSpec((d, tm // d, C), lambda i: (0, i, 0)) for d in DILATIONS]
    st_specs = [pl.BlockSpec((d, tm // d, 128), lambda i: (0, i, 0)) for d in DILATIONS]
    body, in_specs, args = _after(
        dep, body, [_row_spec(tm, D)] + o_specs + st_specs + [_row_spec(tm, 128)] + [_full_spec((C, D))] * 3,
        [dx, *os_, *sts, alpha, *wos])
    return pl.pallas_call(
        body, name="attn_out_bwd", grid=(T // tm,),
        in_specs=in_specs,
        out_specs=o_specs + st_specs,
        out_shape=[jax.ShapeDtypeStruct((d, T // d, C), BF) for d in DILATIONS]
        + [jax.ShapeDtypeStruct((d, T // d, 128), F32) for d in DILATIONS],
        scratch_shapes=[pltpu.VMEM((3, C // 128, tm, 128), F32), pltpu.VMEM((3, C // 128, tm, 128), F32),
                        pltpu.VMEM((3, 1, tm, 128), F32), pltpu.VMEM((1, tm, 128), F32)],
        compiler_params=_params("parallel"),
    )(*args)


def _qkv_bwd(dqkvs, cos, sin, x2, g, w3s, dx3, tm=256):
    T = x2.shape[0]
    C = GROUP_LANES

    def body(dq0, dq1, dq2, cos_ref, sin_ref, x_ref, g_ref, w0, w1, w2, dx3_ref,
             dx_ref, n0, n1, n2, h_ref, dg_ref, scr, dh_ref):
        dq_refs, w_refs, n_refs = (dq0, dq1, dq2), (w0, w1, w2), (n0, n1, n2)

        @pl.when(pl.program_id(0) == 0)
        def _():
            dg_ref[...] = jnp.zeros_like(dg_ref)

        cv = cos_ref[...]
        sv = sin_ref[...]
        dh_ref[...] = jnp.zeros_like(dh_ref)
        for gi, d in enumerate(DILATIONS):
            for w in range(3):
                _gather_rows(scr, dq_refs[gi].at[w], d)
                if w < 2:
                    scale = HEAD_DIM ** -0.5 if w == 0 else 1.0
                    for c in range(C // 128):
                        t = scr[c]
                        scr[c] = (t * cv - _rot_half(t) * sv) * scale
                tb = _chunks_to_rows(scr).astype(BF)
                n_refs[gi][w] = tb
                dh_ref[...] += _dot_nt(tb, w_refs[gi][w])
        xv = x_ref[...]
        h_ref[...] = _rms_fwd(xv, g_ref[...]).astype(BF)
        dx, dg = _rms_bwd(xv, g_ref[...], dh_ref[...])
        dx_ref[...] = dx3_ref[...] + dx
        dg_ref[...] += dg

    dq_specs = [pl.BlockSpec((3, d, tm // d, C), lambda i: (0, 0, i, 0)) for d in DILATIONS]
    nat = pl.BlockSpec((3, tm, C), lambda i: (0, i, 0))
    nshape = jax.ShapeDtypeStruct((3, T, C), BF)
    return pl.pallas_call(
        body, name="qkv_bwd", grid=(T // tm,),
        in_specs=dq_specs + [_row_spec(tm, 128), _row_spec(tm, 128), _row_spec(tm, D), _full_spec((1, D))]
        + [_full_spec((3, D, C))] * 3 + [_row_spec(tm, D)],
        out_specs=[_row_spec(tm, D), nat, nat, nat, _row_spec(tm, D), _full_spec((1, D))],
        out_shape=[jax.ShapeDtypeStruct((T, D), F32), nshape, nshape, nshape, jax.ShapeDtypeStruct((T, D), BF),
                   jax.ShapeDtypeStruct((1, D), F32)],
        scratch_shapes=[pltpu.VMEM((C // 128, tm, 128), F32), pltpu.VMEM((tm, D), F32)],
        compiler_params=_params("arbitrary"),
    )(*dqkvs, cos, sin, x2, g, *w3s, dx3)


def _final_bwd(x, target, g, tm=512):
    T = x.shape[0]

    def body(x_ref, t_ref, g_ref, dx_ref, loss_ref, dg_ref):
        @pl.when(pl.program_id(0) == 0)
        def _():
            loss_ref[...] = jnp.zeros_like(loss_ref)
            dg_ref[...] = jnp.zeros_like(dg_ref)

        xv = x_ref[...]
        gv = g_ref[...]
        diff = _rms_fwd(xv, gv) - t_ref[...]
        loss_ref[...] += 0.5 * jnp.sum(jnp.mean(diff * diff, axis=-1, keepdims=True), axis=0, keepdims=True)
        dx, dg = _rms_bwd(xv, gv, diff * (1.0 / D))
        dx_ref[...] = dx
        dg_ref[...] += dg

    return pl.pallas_call(
        body, name="final_bwd", grid=(T // tm,),
        in_specs=[_row_spec(tm, D), _row_spec(tm, D), _full_spec((1, D))],
        out_specs=[_row_spec(tm, D), _full_spec((1, 1)), _full_spec((1, D))],
        out_shape=[jax.ShapeDtypeStruct((T, D), F32), jax.ShapeDtypeStruct((1, 1), F32), jax.ShapeDtypeStruct((1, D), F32)],
        compiler_params=_params("arbitrary"),
    )(x, target, g)


def _place():
    x, y, c = lax.axis_index("x"), lax.axis_index("y"), lax.axis_index("c")
    return x, y, c


def _allgather(arrs, name):
    n = len(arrs)

    def body(*refs):
        ins, outs = refs[:n], refs[n:2 * n]
        send_sems, recv_sems, local_sems = refs[2 * n:]
        x, y, c = _place()
        me, sibling = (x, y, c), (x, y, 1 - c)
        chips = [(1 - x, y), (x, 1 - y), (1 - x, 1 - y)]

        def slot(a, px, py, pc):
            return outs[a].at[4 * px + 2 * py + pc]

        def copy(a, k, block, to, src=None):
            return pltpu.make_async_remote_copy(
                src_ref=slot(a, *block) if src is None else src, dst_ref=slot(a, *block),
                send_sem=send_sems.at[a, k], recv_sem=recv_sems.at[a, k], device_id=to, device_id_type=MESH)

        mine = [pltpu.make_async_copy(ins[a], slot(a, *me), local_sems.at[a]) for a in range(n)]
        for cp in mine:
            cp.start()
        first = []
        for a in range(n):
            first.append(copy(a, 0, me, sibling, src=ins[a]))
            first += [copy(a, 1 + j, me, (*chip, c), src=ins[a]) for j, chip in enumerate(chips)]
        for cp in first:
            cp.start()
        passed = []
        for a in range(n):
            for j, chip in enumerate(chips):
                copy(a, 1 + j, (*chip, c), me).wait_recv()
                fwd = copy(a, 4 + j, (*chip, c), sibling)
                fwd.start()
                passed.append(fwd)
        for a in range(n):
            copy(a, 0, sibling, me).wait_recv()
            for j, chip in enumerate(chips):
                copy(a, 4 + j, (*chip, 1 - c), me).wait_recv()
        for cp in first + passed:
            cp.wait_send()
        for cp in mine:
            cp.wait()

    hbm = pl.BlockSpec(memory_space=pl.ANY)
    return pl.pallas_call(
        body, name=name,
        in_specs=[hbm] * n, out_specs=[hbm] * n,
        out_shape=[jax.ShapeDtypeStruct((N_DEV,) + a.shape, a.dtype) for a in arrs],
        scratch_shapes=[pltpu.SemaphoreType.DMA((n, 7)), pltpu.SemaphoreType.DMA((n, 7)), pltpu.SemaphoreType.DMA((n,))],
    )(*arrs)


def _peer(k):
    x, y, c = _place()
    px = 1 - x if k & 4 else x
    py = 1 - y if k & 2 else y
    pc = 1 - c if k & 1 else c
    return (px, py, pc), 4 * px + 2 * py + pc


HBM_SPEC = pl.BlockSpec(memory_space=pltpu.HBM)
SEM_SPEC = pl.BlockSpec(memory_space=pltpu.SEMAPHORE)
EFFECT = pltpu.SideEffectType.DATAFLOW_SIDE_EFFECTING


def _exchange_start(arrs, name, same_block=False, dep=None):
    n = len(arrs)
    n_dep = 0 if dep is None else 1

    def body(*refs):
        srcs, lands = refs[:n], refs[n:2 * n]
        send_sems, recv_sems, local_sems = refs[2 * n + n_dep:2 * n + n_dep + 3]
        token = refs[-1]
        x, y, c = _place()
        me = 4 * x + 2 * y + c
        block = (lambda a, j: srcs[a]) if same_block else (lambda a, j: srcs[a].at[j])
        for a in range(n):
            pltpu.make_async_copy(block(a, me), lands[a].at[me], local_sems.at[a]).start()
        for a in range(n):
            for k in range(1, N_DEV):
                to, to_idx = _peer(k)
                pltpu.make_async_remote_copy(
                    src_ref=block(a, to_idx), dst_ref=lands[a].at[me],
                    send_sem=send_sems.at[a * (N_DEV - 1) + k - 1], recv_sem=recv_sems.at[a * (N_DEV - 1) + k - 1], device_id=to, device_id_type=MESH).start()
        token[...] = jnp.zeros_like(token)

    land_shape = (lambda a: (N_DEV,) + a.shape) if same_block else (lambda a: a.shape)
    src_shapes = [pltpu.HBM(a.shape, a.dtype) for a in arrs]
    land_shapes = [pltpu.HBM(land_shape(a), a.dtype) for a in arrs]
    outs = pl.pallas_call(
        body, name=name,
        out_shape=(pltpu.SemaphoreType.DMA((n * (N_DEV - 1),)), pltpu.SemaphoreType.DMA((n * (N_DEV - 1),)),
                   pltpu.SemaphoreType.DMA((n,)), *src_shapes, *land_shapes, jax.ShapeDtypeStruct((8, 128), F32)),
        in_specs=[HBM_SPEC] * (2 * n) + [pl.BlockSpec(memory_space=pl.ANY)] * n_dep,
        out_specs=(SEM_SPEC, SEM_SPEC, SEM_SPEC, *([HBM_SPEC] * (2 * n)), pl.BlockSpec(memory_space=pltpu.VMEM)),
        input_output_aliases={i: 3 + i for i in range(2 * n)},
        compiler_params=pltpu.CompilerParams(has_side_effects=EFFECT),
    )(*[pltpu.with_memory_space_constraint(a, pltpu.HBM) for a in arrs],
      *[pltpu.with_memory_space_constraint(lax.empty(land_shape(a), a.dtype), pltpu.HBM) for a in arrs],
      *([] if dep is None else [dep]))
    return outs[0], outs[1], outs[2], outs[3:3 + n], outs[3 + n:3 + 2 * n], outs[3 + 2 * n]


def _exchange_wait(send_sems, recv_sems, local_sems, src_thru, land_thru, after, name):
    n = len(src_thru)
    after = list(after) if isinstance(after, (list, tuple)) else [after]
    same_block = src_thru[0].shape != land_thru[0].shape

    def body(*refs):
        srcs, lands = refs[:n], refs[n:2 * n]
        send_sems, recv_sems, local_sems = refs[2 * n:2 * n + 3]
        x, y, c = _place()
        me = 4 * x + 2 * y + c
        for a in range(n):
            pltpu.make_async_copy(srcs[a] if same_block else srcs[a].at[me], lands[a].at[me], local_sems.at[a]).wait()
            for k in range(1, N_DEV):
                frm, frm_idx = _peer(k)
                cp = pltpu.make_async_remote_copy(
                    src_ref=srcs[a] if same_block else srcs[a].at[frm_idx], dst_ref=lands[a].at[frm_idx],
                    send_sem=send_sems.at[a * (N_DEV - 1) + k - 1], recv_sem=recv_sems.at[a * (N_DEV - 1) + k - 1], device_id=frm, device_id_type=MESH)
                cp.wait_send()
                cp.wait_recv()

    outs = pl.pallas_call(
        body, name=name,
        out_shape=tuple(pltpu.HBM(a.shape, a.dtype) for a in (*src_thru, *land_thru)),
        in_specs=[HBM_SPEC] * (2 * n) + [SEM_SPEC, SEM_SPEC, SEM_SPEC] + [pl.BlockSpec(memory_space=pl.ANY)] * len(after),
        out_specs=[HBM_SPEC] * (2 * n),
        input_output_aliases={i: i for i in range(2 * n)},
        compiler_params=pltpu.CompilerParams(has_side_effects=EFFECT),
    )(*src_thru, *land_thru, send_sems, recv_sems, local_sems, *after)
    return outs[n:]


def _row_tile(rows):
    for t in (256, 176, 128, 8):
        if rows % t == 0:
            return t
    return rows


def _adamw(parts, w, m, v, name, transposed=False):
    K = parts.shape[0]
    R, C = w.shape
    tr = 256 if transposed else _row_tile(R)

    def body(p_ref, w_ref, m_ref, v_ref, g_ref, d_ref, nm_ref, nv_ref):
        g = p_ref[0].astype(F32)
        for k in range(1, K):
            g = g + p_ref[k].astype(F32)
        if transposed:
            g = g.T
        nm = ADAM_B1 * m_ref[...] + (1.0 - ADAM_B1) * g
        nv = ADAM_B2 * v_ref[...] + (1.0 - ADAM_B2) * jnp.square(g)
        m_hat = nm / (1.0 - ADAM_B1 ** ADAM_STEP)
        v_hat = nv / (1.0 - ADAM_B2 ** ADAM_STEP)
        g_ref[...] = g
        d_ref[...] = -ADAM_LR * (m_hat / (jnp.sqrt(v_hat) + ADAM_EPS) + ADAM_WD * w_ref[...])
        nm_ref[...] = nm
        nv_ref[...] = nv

    blk = _row_spec(tr, C)
    shp = jax.ShapeDtypeStruct((R, C), F32)
    parts_spec = pl.BlockSpec((K, C, tr), lambda i: (0, 0, i)) if transposed else pl.BlockSpec((K, tr, C), lambda i: (0, i, 0))
    return pl.pallas_call(
        body, name=name, grid=(R // tr,),
        in_specs=[parts_spec, blk, blk, blk],
        out_specs=[blk] * 4,
        out_shape=[shp] * 4,
        compiler_params=_params("parallel"),
    )(parts, w, m, v)


def _rope_tables(T):
    inv_freq = 1.0 / (ROPE_THETA ** (jnp.arange(0, HEAD_DIM, 2, dtype=F32) / HEAD_DIM))
    ang = jnp.arange(T, dtype=F32)[:, None] * inv_freq[None, :]
    cos, sin = jnp.cos(ang), jnp.sin(ang)
    return jnp.concatenate([cos] * 4, axis=-1), jnp.concatenate([sin] * 4, axis=-1)


def _pad_lanes(a, n):
    return jnp.pad(a, ((0, 0),) * (a.ndim - 1) + ((0, n - a.shape[-1]),))


LAYER0 = ("pool_in", "pool_grp", "pool_out", "gate0", "up0", "down0")
LAYER1 = ("qkv", "attn_out", "gate1", "up1", "down1")


def _layout_weights(gw):
    w = {}
    if "pool_in" in gw:
        w["pool_in"] = gw["pool_in"].reshape(D, D)
        w["pool_grp"] = jnp.transpose(gw["pool_grp"], (1, 0, 2, 3)).reshape(4, POOL_GC, POOL_GC)
        w["pool_out"] = gw["pool_out"].reshape(D, D)
    if "qkv" in gw:
        wqkv = jnp.transpose(gw["qkv"], (1, 0, 2)).reshape(D, 3 * D)
        wo = gw["attn_out"].reshape(D, D)
        w["qkv"], w["attn_out"] = [], []
        for nh, off in zip(HEAD_GROUPS, HEAD_OFFS):
            lo, n = off * HEAD_DIM, nh * HEAD_DIM
            w["qkv"].append(jnp.stack([_pad_lanes(wqkv[:, k * D + lo:k * D + lo + n], GROUP_LANES) for k in range(3)]))
            w["attn_out"].append(jnp.pad(wo[lo:lo + n], ((0, GROUP_LANES - n), (0, 0))))
    for nm in gw:
        if nm[:-1] in ("gate", "up", "down"):
            w[nm] = gw[nm].reshape(F, D)
    return w


def _local_step(x, target, w, layer1, dep0, norm_mix, norm_ffn, norm_final, pool_scale, emit):
    T = x.shape[0]
    cos, sin = _rope_tables(T)
    nm = [norm_mix[i:i + 1] for i in range(2)]
    nf = [norm_ffn[i:i + 1] for i in range(2)]
    nfin = norm_final.reshape(1, D)

    x1, p = _pool_fwd(x, nm[0], w["pool_in"], w["pool_grp"], pool_scale, w["pool_out"], dep=dep0)
    x2, a0, b0, s0 = _ffn_fwd(x1, nf[0], w["gate0"], w["up0"], w["down0"], "ffn_fwd0")
    w = {**w, **layer1(x2)}
    qkvs = _qkv_fwd(x2, nm[1], w["qkv"], cos, sin)
    att = [_attn_fwd(qkvs[g], HEAD_GROUPS[g], f"attn_fwd{g}") for g in range(3)]
    os_, sts = [a[0] for a in att], [a[1] for a in att]
    x3, m0, m1, m2, alpha = _attn_out_fwd(x2, os_, sts, w["attn_out"])
    x4, a1, b1, s1 = _ffn_fwd(x3, nf[1], w["gate1"], w["up1"], w["down1"], "ffn_fwd1")

    g = {}
    dx4, loss, dg_final = _final_bwd(x4, target, nfin)
    dx3, da1, db1, h3, dy4, dg_ffn1 = _ffn_bwd(dx4, x3, nf[1], a1, b1, w["gate1"], w["up1"], w["down1"], "ffn_bwd1")
    g["down1"] = _wgrad(s1, dy4, "down_wgrad1", mblk=F // 2)
    g["gate1"] = _wgrad(da1, h3, "gate_wgrad1", mblk=F // 2)
    g["up1"] = _wgrad(db1, h3, "up_wgrad1", mblk=F // 2)
    dep = emit("ffn1", g)
    dos_and_stats = _attn_out_bwd(dx3, os_, sts, alpha, w["attn_out"], dep=dep)
    dos, dsts = dos_and_stats[:3], dos_and_stats[3:]
    g["attn_out"] = _wgrad_multi([m0, m1, m2], dx3, "attn_out_wgrad")
    dqkvs = [_attn_bwd(qkvs[gi], dos[gi], sts[gi], dsts[gi], HEAD_GROUPS[gi], f"attn_bwd{gi}") for gi in range(3)]
    dx2, n0, n1, n2, h2b, dg_mix1 = _qkv_bwd(dqkvs, cos, sin, x2, nm[1], w["qkv"], dx3)
    g["qkv"] = [_wgrad_stack(h2b, n, f"qkv_wgrad{i}") for i, n in enumerate((n0, n1, n2))]
    dep = emit("attn", g)
    dx1, da0, db0, h1, dy2, dg_ffn0 = _ffn_bwd(dx2, x1, nf[0], a0, b0, w["gate0"], w["up0"], w["down0"], "ffn_bwd0", dep=dep)
    g["down0"] = _wgrad(s0, dy2, "down_wgrad0", mblk=F // 2)
    g["gate0"] = _wgrad(da0, h1, "gate_wgrad0", mblk=F // 2)
    g["up0"] = _wgrad(db0, h1, "up_wgrad0", mblk=F // 2)
    dep = emit("ffn0", g)
    dx0, z, dzp, du, h0b, dscale, dg_mix0 = _pool_bwd(dx1, x, nm[0], p, w["pool_in"], w["pool_grp"], pool_scale, w["pool_out"], dep=dep)
    g["pool_out"] = _wgrad(z, dx1, "pool_out_wgrad")
    g["pool_in"] = _wgrad(h0b, du, "pool_in_wgrad")
    g["pool_grp"] = _wgrad_pool_groups(p, dzp)
    emit("pool", g)

    small = jnp.concatenate([dg_mix0, dg_mix1, dg_ffn0, dg_ffn1, dg_final, dscale,
                             jnp.broadcast_to(loss, (1, D)), jnp.zeros((1, D), F32)], axis=0)
    return dx0, small


GROUPS = {"ffn1": ("down1", "gate1", "up1"), "attn": ("qkv", "attn_out"), "ffn0": ("down0", "gate0", "up0"),
          "pool": ("pool_in", "pool_out", "pool_grp")}


def _grad_blocks(group, g):
    blocks = {}
    if group == "pool":
        blocks["pool_in"] = g["pool_in"].reshape(N_DEV, D // N_DEV, D)
        blocks["pool_out"] = g["pool_out"].reshape(N_DEV, D // N_DEV, D)
        blocks["pool_grp"] = jnp.transpose(g["pool_grp"].reshape(4, N_DEV, POOL_GC // N_DEV, POOL_GC), (1, 0, 2, 3)).reshape(N_DEV, 4 * POOL_GC // N_DEV, POOL_GC)
    elif group == "attn":
        wo = jnp.concatenate([gw[:nh * HEAD_DIM] for gw, nh in zip(g["attn_out"], HEAD_GROUPS)], axis=0)
        blocks["attn_out"] = wo.reshape(N_DEV, D // N_DEV, D)
        wqkv = jnp.concatenate([g["qkv"][gi][k][:, :HEAD_GROUPS[gi] * HEAD_DIM] for k in range(3) for gi in range(3)], axis=1)
        blocks["qkv"] = jnp.transpose(wqkv.reshape(D, N_DEV, 3 * D // N_DEV), (1, 0, 2))
    else:
        for nm in GROUPS[group]:
            blocks[nm] = g[nm].reshape(N_DEV, F // N_DEV, D)
    return [blocks[nm] for nm in GROUPS[group]]


def kernel(x, norm_mix, norm_ffn, norm_final, pool_w_in, pool_w_group, pool_scale, pool_w_out, attn_w_qkv, attn_w_out, ffn_w_gate, ffn_w_up, ffn_w_down, loss_target, m_norm_mix, m_norm_ffn, m_norm_final, m_pool_w_in, m_pool_w_group, m_pool_scale, m_pool_w_out, m_attn_w_qkv, m_attn_w_out, m_ffn_w_gate, m_ffn_w_up, m_ffn_w_down, v_norm_mix, v_norm_ffn, v_norm_final, v_pool_w_in, v_pool_w_group, v_pool_scale, v_pool_w_out, v_attn_w_qkv, v_attn_w_out, v_ffn_w_gate, v_ffn_w_up, v_ffn_w_down):
    shard = {
        "pool_in": pool_w_in[0], "pool_grp": pool_w_group[0], "pool_out": pool_w_out[0],
        "qkv": attn_w_qkv[0], "attn_out": attn_w_out[0],
    }
    for l in range(2):
        shard[f"gate{l}"] = ffn_w_gate[l].T
        shard[f"up{l}"] = ffn_w_up[l].T
        shard[f"down{l}"] = ffn_w_down[l]
    shard = {k: v.astype(BF) for k, v in shard.items()}
    w0 = _layout_weights(dict(zip(LAYER0, _allgather([shard[k] for k in LAYER0], "weights_allgather"))))
    l1 = _exchange_start([shard[k] for k in LAYER1], "weights_start_layer1", same_block=True, dep=w0["pool_in"])

    def layer1(after):
        return _layout_weights(dict(zip(LAYER1, _exchange_wait(*l1[:5], after, "weights_wait_layer1"))))

    started = {}

    def emit(group, g):
        started[group] = _exchange_start(_grad_blocks(group, g), f"grads_start_{group}")
        return started[group][5]

    grad_x, small = _local_step(x[0], loss_target[0], w0, layer1, l1[5], norm_mix, norm_ffn, norm_final, pool_scale, emit)

    small_st = _exchange_start([small], "small_start", same_block=True, dep=started["pool"][5])

    def finish(group, after):
        lands = _exchange_wait(*started[group][:5], after, f"grads_wait_{group}")
        return dict(zip(GROUPS[group], lands))

    def upd(parts, wt, mt, vt, name):
        shape = wt.shape
        r2 = lambda t: t.reshape(parts.shape[1:])
        outs = _adamw(parts, r2(wt), r2(mt), r2(vt), name)
        return [o.reshape(shape) for o in outs]

    def ffn_layer(rcv, l):
        out = {}
        for nm, wt, mt, vt in (("gate", ffn_w_gate, m_ffn_w_gate, v_ffn_w_gate), ("up", ffn_w_up, m_ffn_w_up, v_ffn_w_up)):
            out[nm] = _adamw(rcv[f"{nm}{l}"], wt[l], mt[l], vt[l], f"adamw_{nm}{l}", transposed=True)
        out["down"] = _adamw(rcv[f"down{l}"], ffn_w_down[l], m_ffn_w_down[l], v_ffn_w_down[l], f"adamw_down{l}")
        return out

    res = {}
    ffn1 = ffn_layer(finish("ffn1", small_st[5]), 1)
    rcv = finish("attn", [ffn1[nm][0] for nm in ffn1])
    res["attn_w_qkv"] = upd(rcv["qkv"], attn_w_qkv, m_attn_w_qkv, v_attn_w_qkv, "adamw_qkv")
    res["attn_w_out"] = upd(rcv["attn_out"], attn_w_out, m_attn_w_out, v_attn_w_out, "adamw_attn_out")
    ffn0 = ffn_layer(finish("ffn0", [res["attn_w_qkv"][0], res["attn_w_out"][0]]), 0)
    rcv = finish("pool", [ffn0[nm][0] for nm in ffn0])
    res["pool_w_in"] = upd(rcv["pool_in"], pool_w_in, m_pool_w_in, v_pool_w_in, "adamw_pool_in")
    res["pool_w_group"] = upd(rcv["pool_grp"], pool_w_group, m_pool_w_group, v_pool_w_group, "adamw_pool_grp")
    res["pool_w_out"] = upd(rcv["pool_out"], pool_w_out, m_pool_w_out, v_pool_w_out, "adamw_pool_out")
    for nm in ("gate", "up", "down"):
        res[f"ffn_w_{nm}"] = [jnp.stack([ffn0[nm][i], ffn1[nm][i]]) for i in range(4)]
    small_all = _exchange_wait(*small_st[:5], [res[k][0] for k in ("pool_w_in", "pool_w_group", "pool_w_out")], "small_wait")[0]

    small_w = jnp.concatenate([norm_mix, norm_ffn, norm_final[None], pool_scale, jnp.zeros((2, D), F32)], axis=0)
    small_m = jnp.concatenate([m_norm_mix, m_norm_ffn, m_norm_final[None], m_pool_scale, jnp.zeros((2, D), F32)], axis=0)
    small_v = jnp.concatenate([v_norm_mix, v_norm_ffn, v_norm_final[None], v_pool_scale, jnp.ones((2, D), F32)], axis=0)
    sg, sd, sm, sv = _adamw(small_all, small_w, small_m, small_v, "adamw_small")
    loss = sg[6, 0]
    res["norm_mix"] = [t[0:2] for t in (sg, sd, sm, sv)]
    res["norm_ffn"] = [t[2:4] for t in (sg, sd, sm, sv)]
    res["norm_final"] = [t[4] for t in (sg, sd, sm, sv)]
    res["pool_scale"] = [t[5:6] for t in (sg, sd, sm, sv)]

    order = ["norm_mix", "norm_ffn", "norm_final", "pool_w_in", "pool_w_group", "pool_scale", "pool_w_out",
             "attn_w_qkv", "attn_w_out", "ffn_w_gate", "ffn_w_up", "ffn_w_down"]
    return (loss, grad_x[None], *[res[k][0] for k in order], *[res[k][1] for k in order],
            *[res[k][2] for k in order], *[res[k][3] for k in order])
```

```python
import math

import jax
import jax.numpy as jnp
from jax import lax
from jax.experimental import pallas as pl
from jax.experimental.pallas import tpu as pltpu

D = 1024
F = 2816
N_DEV = 8
EPS = 1e-6
NEG_INF = -1e30
POOL_WINDOWS = (2, 4, 8, 16)
POOL_HALO = 16
POOL_GC = 256
HEAD_DIM = 64
HEAD_GROUPS = (6, 5, 5)
HEAD_OFFS = (0, 6, 11)
DILATIONS = (1, 4, 16)
ATT_W = 128
GROUP_LANES = 384
LSE_GROUP_LANE = 8
ROPE_THETA = 10000.0
ADAM_LR, ADAM_B1, ADAM_B2, ADAM_EPS, ADAM_WD, ADAM_STEP = 0.001, 0.9, 0.999, 1e-08, 0.01, 10

BF = jnp.bfloat16
F32 = jnp.float32
VMEM_LIMIT = 56 * 1024 * 1024
MESH = pl.DeviceIdType.MESH


def _params(*sem):
    return pltpu.CompilerParams(dimension_semantics=sem, vmem_limit_bytes=VMEM_LIMIT)


def _dot(a, b):
    return jnp.dot(a, b, preferred_element_type=F32)


def _dot_nt(a, b):
    return lax.dot_general(a, b, (((1,), (1,)), ((), ())), preferred_element_type=F32)


def _dot_tn(a, b):
    return lax.dot_general(a, b, (((0,), (0,)), ((), ())), preferred_element_type=F32)


def _rms_fwd(xv, g):
    r = lax.rsqrt(jnp.mean(xv * xv, axis=-1, keepdims=True) + EPS)
    return (xv * r) * g


def _rms_bwd(xv, g, dh):
    r = lax.rsqrt(jnp.mean(xv * xv, axis=-1, keepdims=True) + EPS)
    xhat = xv * r
    dg = jnp.sum(dh * xhat, axis=0, keepdims=True)
    dxh = dh * g
    dx = r * (dxh - xhat * jnp.mean(dxh * xhat, axis=-1, keepdims=True))
    return dx, dg


def _lane_col(tile, j):
    lane = lax.broadcasted_iota(jnp.int32, tile.shape, 1)
    return jnp.sum(jnp.where(lane == j, tile, 0.0), axis=-1, keepdims=True)


def _row_spec(tm, n):
    return pl.BlockSpec((tm, n), lambda i: (i, 0))


def _full_spec(shape):
    nd = len(shape)
    return pl.BlockSpec(shape, lambda *_: (0,) * nd)


def _after(dep, body, in_specs, args):
    if dep is None:
        return body, list(in_specs), list(args)

    def body_after(dep_ref, *refs):
        body(*refs)

    return body_after, [pl.BlockSpec(memory_space=pl.ANY)] + list(in_specs), [dep] + list(args)


def _pool_fwd(x, g, w_in, w_grp, scale, w_out, gather, tm=512, dep=None):
    T = x.shape[0]
    nt = T // tm
    n = tm + POOL_HALO
    ng = len(gather)

    def body(x_ref, g_ref, win_ref, wg_ref, sc_ref, wout_ref, *refs):
        shard_refs, (x1_ref, p_ref), full_refs = refs[:ng], refs[ng:ng + 2], refs[ng + 2:2 * ng + 2]
        tail_ref, z_ref = refs[2 * ng + 2:2 * ng + 4]
        begin, finish = _gather_phases(shard_refs, full_refs, *refs[2 * ng + 4:])
        i = pl.program_id(0)

        @pl.when(i == 0)
        def _():
            begin()
            tail_ref[...] = jnp.zeros_like(tail_ref)

        u = _dot(_rms_fwd(x_ref[...], g_ref[...]).astype(BF), win_ref[...])
        pos = i * tm + lax.broadcasted_iota(jnp.int32, (tm, 1), 0)
        for g, w in enumerate(POOL_WINDOWS):
            sl = slice(g * POOL_GC, (g + 1) * POOL_GC)
            ug = u[:, sl]
            s = jnp.concatenate([tail_ref[:, sl], ug], axis=0)
            step = 1
            while step < w:
                s = s + pltpu.roll(s, step, 0)
                step *= 2
            cnt = jnp.minimum(pos + 1, w).astype(F32)
            pg = (s[POOL_HALO:, :] / cnt - ug).astype(BF)
            p_ref[:, sl] = pg
            z_ref[:, sl] = (_dot(pg, wg_ref[g]) * sc_ref[:, sl]).astype(BF)
        tail_ref[...] = u[tm - POOL_HALO:, :]
        x1_ref[...] = x_ref[...] + _dot(z_ref[...], wout_ref[...])

        @pl.when(i == nt - 1)
        def _():
            finish()

    hbm = pl.BlockSpec(memory_space=pl.ANY)
    body, in_specs, args = _after(
        dep, body,
        [_row_spec(tm, D), _full_spec((1, D)), _full_spec((D, D)), _full_spec((4, POOL_GC, POOL_GC)), _full_spec((1, D)),
         _full_spec((D, D))] + [hbm] * ng,
        [x, g, w_in, w_grp, scale, w_out, *gather])
    return pl.pallas_call(
        body, name="pool_fwd", grid=(nt,),
        in_specs=in_specs,
        out_specs=[_row_spec(tm, D), _row_spec(tm, D)] + [hbm] * ng,
        out_shape=[jax.ShapeDtypeStruct((T, D), F32), jax.ShapeDtypeStruct((T, D), BF)]
        + [jax.ShapeDtypeStruct((N_DEV,) + a.shape, a.dtype) for a in gather],
        scratch_shapes=[pltpu.VMEM((POOL_HALO, D), F32), pltpu.VMEM((tm, D), BF)] + _gather_scratch(ng),
        compiler_params=_params("arbitrary"),
    )(*args)


def _pool_bwd(dx1, x0, g0, p, w_in, w_grp, scale, w_out, tm=512, dep=None):
    T = x0.shape[0]
    nt = T // tm
    n = tm + POOL_HALO
    rev = lambda i: (nt - 1 - i, 0)

    def body(dx1_ref, x0_ref, g_ref, p_ref, win_ref, wg_ref, sc_ref, wout_ref,
             dx0_ref, z_ref, dzp_ref, du_ref, h0_ref, dsc_ref, dg_ref, head_ref):
        i = pl.program_id(0)

        @pl.when(i == 0)
        def _():
            head_ref[...] = jnp.zeros_like(head_ref)
            dsc_ref[...] = jnp.zeros_like(dsc_ref)
            dg_ref[...] = jnp.zeros_like(dg_ref)

        dx1v = dx1_ref[...]
        dz = _dot_nt(dx1v.astype(BF), wout_ref[...])
        pos = (nt - 1 - i) * tm + lax.broadcasted_iota(jnp.int32, (tm, 1), 0)
        for g, w in enumerate(POOL_WINDOWS):
            sl = slice(g * POOL_GC, (g + 1) * POOL_GC)
            zpre = _dot(p_ref[:, sl], wg_ref[g])
            dzg = dz[:, sl]
            dsc_ref[:, sl] += jnp.sum(dzg * zpre, axis=0, keepdims=True)
            z_ref[:, sl] = (zpre * sc_ref[:, sl]).astype(BF)
            dzp = (dzg * sc_ref[:, sl]).astype(BF)
            dzp_ref[:, sl] = dzp
            dp = _dot_nt(dzp, wg_ref[g])
            cnt = jnp.minimum(pos + 1, w).astype(F32)
            dpc = dp / cnt
            s = jnp.concatenate([dpc, head_ref[:, sl]], axis=0)
            step = 1
            while step < w:
                s = s + pltpu.roll(s, n - step, 0)
                step *= 2
            head_ref[:, sl] = dpc[:POOL_HALO, :]
            du_ref[:, sl] = (s[:tm, :] - dp).astype(BF)
        dh0 = _dot_nt(du_ref[...], win_ref[...])
        x0v = x0_ref[...]
        h0_ref[...] = _rms_fwd(x0v, g_ref[...]).astype(BF)
        dx, dg = _rms_bwd(x0v, g_ref[...], dh0)
        dx0_ref[...] = dx1v + dx
        dg_ref[...] += dg

    bf_rows = jax.ShapeDtypeStruct((T, D), BF)
    vec = jax.ShapeDtypeStruct((1, D), F32)
    body, in_specs, args = _after(
        dep, body,
        [pl.BlockSpec((tm, D), rev), pl.BlockSpec((tm, D), rev), _full_spec((1, D)), pl.BlockSpec((tm, D), rev),
         _full_spec((D, D)), _full_spec((4, POOL_GC, POOL_GC)), _full_spec((1, D)), _full_spec((D, D))],
        [dx1, x0, g0, p, w_in, w_grp, scale, w_out])
    return pl.pallas_call(
        body, name="pool_bwd", grid=(nt,),
        in_specs=in_specs,
        out_specs=[pl.BlockSpec((tm, D), rev)] * 5 + [_full_spec((1, D))] * 2,
        out_shape=[jax.ShapeDtypeStruct((T, D), F32), bf_rows, bf_rows, bf_rows, bf_rows, vec, vec],
        scratch_shapes=[pltpu.VMEM((POOL_HALO, D), F32)],
        compiler_params=_params("arbitrary"),
    )(*args)


def _ffn_fwd(x, g, wg_t, wu_t, wd, name, gather=(), tm=512, fk=1408):
    T = x.shape[0]
    ng = len(gather)
    ni, nk = T // tm, F // fk

    def body(x_ref, g_ref, wg_ref, wu_ref, wd_ref, *refs):
        shard_refs, (xo_ref, a_ref, b_ref, s_ref), full_refs = refs[:ng], refs[ng:ng + 4], refs[ng + 4:2 * ng + 4]
        acc_ref, h_ref = refs[2 * ng + 4:2 * ng + 6]
        begin, finish = _gather_phases(shard_refs, full_refs, *refs[2 * ng + 6:])
        i = pl.program_id(0)
        k = pl.program_id(1)

        @pl.when(jnp.logical_and(i == 0, k == 0))
        def _():
            begin()

        @pl.when(k == 0)
        def _():
            acc_ref[...] = jnp.zeros_like(acc_ref)
            h_ref[...] = _rms_fwd(x_ref[...], g_ref[...]).astype(BF)

        hv = h_ref[...]
        a = _dot_nt(hv, wg_ref[...])
        b = _dot_nt(hv, wu_ref[...])
        s = ((a * jax.nn.sigmoid(a)) * b).astype(BF)
        a_ref[...] = a.astype(BF)
        b_ref[...] = b.astype(BF)
        s_ref[...] = s
        acc_ref[...] += _dot(s, wd_ref[...])

        @pl.when(k == nk - 1)
        def _():
            xo_ref[...] = x_ref[...] + acc_ref[...]

        @pl.when(jnp.logical_and(i == ni - 1, k == nk - 1))
        def _():
            finish()

    row = pl.BlockSpec((tm, D), lambda i, k: (i, 0))
    wsp = pl.BlockSpec((fk, D), lambda i, k: (k, 0))
    act = pl.BlockSpec((tm, fk), lambda i, k: (i, k))
    hbm = pl.BlockSpec(memory_space=pl.ANY)
    act_shape = jax.ShapeDtypeStruct((T, F), BF)
    return pl.pallas_call(
        body, name=name, grid=(ni, nk),
        in_specs=[row, pl.BlockSpec((1, D), lambda i, k: (0, 0)), wsp, wsp, wsp] + [hbm] * ng,
        out_specs=[row, act, act, act] + [hbm] * ng,
        out_shape=[jax.ShapeDtypeStruct((T, D), F32), act_shape, act_shape, act_shape]
        + [jax.ShapeDtypeStruct((N_DEV,) + a.shape, a.dtype) for a in gather],
        scratch_shapes=[pltpu.VMEM((tm, D), F32), pltpu.VMEM((tm, D), BF)] + _gather_scratch(ng),
        compiler_params=_params("arbitrary", "arbitrary"),
    )(x, g, wg_t, wu_t, wd, *gather)


def _ffn_bwd(dxo, x_in, g, a, b, wg_t, wu_t, wd, name, tm=512, fk=1408, dep=None):
    T = x_in.shape[0]

    def body(dxo_ref, x_ref, g_ref, a_ref, b_ref, wg_ref, wu_ref, wd_ref,
             dx_ref, da_ref, db_ref, h_ref, dy_ref, dg_ref, dh_ref):
        i = pl.program_id(0)
        k = pl.program_id(1)

        @pl.when(jnp.logical_and(i == 0, k == 0))
        def _():
            dg_ref[...] = jnp.zeros_like(dg_ref)

        @pl.when(k == 0)
        def _():
            h_ref[...] = _rms_fwd(x_ref[...], g_ref[...]).astype(BF)
            dy_ref[...] = dxo_ref[...].astype(BF)
            dh_ref[...] = jnp.zeros_like(dh_ref)

        ds = _dot_nt(dy_ref[...], wd_ref[...])
        av = a_ref[...].astype(F32)
        bv = b_ref[...].astype(F32)
        sig = jax.nn.sigmoid(av)
        db = (ds * (av * sig)).astype(BF)
        da = (ds * bv * (sig * (1.0 + av * (1.0 - sig)))).astype(BF)
        da_ref[...] = da
        db_ref[...] = db
        dh_ref[...] += _dot(da, wg_ref[...]) + _dot(db, wu_ref[...])

        @pl.when(k == pl.num_programs(1) - 1)
        def _():
            dx, dg = _rms_bwd(x_ref[...], g_ref[...], dh_ref[...])
            dx_ref[...] = dxo_ref[...] + dx
            dg_ref[...] += dg

    row = pl.BlockSpec((tm, D), lambda i, k: (i, 0))
    wsp = pl.BlockSpec((fk, D), lambda i, k: (k, 0))
    act = pl.BlockSpec((tm, fk), lambda i, k: (i, k))
    vec = pl.BlockSpec((1, D), lambda i, k: (0, 0))
    act_shape = jax.ShapeDtypeStruct((T, F), BF)
    body, in_specs, args = _after(dep, body, [row, row, vec, act, act, wsp, wsp, wsp], [dxo, x_in, g, a, b, wg_t, wu_t, wd])
    return pl.pallas_call(
        body, name=name, grid=(T // tm, F // fk),
        in_specs=in_specs,
        out_specs=[row, act, act, row, row, vec],
        out_shape=[jax.ShapeDtypeStruct((T, D), F32), act_shape, act_shape, jax.ShapeDtypeStruct((T, D), BF),
                   jax.ShapeDtypeStruct((T, D), BF), jax.ShapeDtypeStruct((1, D), F32)],
        scratch_shapes=[pltpu.VMEM((tm, D), F32)],
        compiler_params=_params("arbitrary", "arbitrary"),
    )(*args)


def _wgrad(a, b, name, tk=1024, mblk=None):
    T, M = a.shape
    N = b.shape[1]
    mblk = M if mblk is None else mblk

    def body(a_ref, b_ref, o_ref, acc_ref):
        t = pl.program_id(1)

        @pl.when(t == 0)
        def _():
            acc_ref[...] = jnp.zeros_like(acc_ref)

        acc_ref[...] += _dot_tn(a_ref[...].astype(BF), b_ref[...].astype(BF))

        @pl.when(t == pl.num_programs(1) - 1)
        def _():
            o_ref[...] = acc_ref[...].astype(BF)

    return pl.pallas_call(
        body, name=name, grid=(M // mblk, T // tk),
        in_specs=[pl.BlockSpec((tk, mblk), lambda m, t: (t, m)), pl.BlockSpec((tk, N), lambda m, t: (t, 0))],
        out_specs=pl.BlockSpec((mblk, N), lambda m, t: (m, 0)),
        out_shape=jax.ShapeDtypeStruct((M, N), BF),
        scratch_shapes=[pltpu.VMEM((mblk, N), F32)],
        compiler_params=_params("parallel", "arbitrary"),
    )(a, b)


def _wgrad_multi(a_list, b, name, tk=1024):
    T, M = a_list[0].shape
    N = b.shape[1]
    n = len(a_list)

    def body(*refs):
        a_refs, b_ref, o_refs, acc_refs = refs[:n], refs[n], refs[n + 1:2 * n + 1], refs[2 * n + 1:]
        t = pl.program_id(0)

        @pl.when(t == 0)
        def _():
            for acc in acc_refs:
                acc[...] = jnp.zeros_like(acc)

        bv = b_ref[...].astype(BF)
        for a_ref, acc in zip(a_refs, acc_refs):
            acc[...] += _dot_tn(a_ref[...], bv)

        @pl.when(t == pl.num_programs(0) - 1)
        def _():
            for o_ref, acc in zip(o_refs, acc_refs):
                o_ref[...] = acc[...].astype(BF)

    return pl.pallas_call(
        body, name=name, grid=(T // tk,),
        in_specs=[_row_spec(tk, M)] * n + [_row_spec(tk, N)],
        out_specs=[_full_spec((M, N))] * n,
        out_shape=[jax.ShapeDtypeStruct((M, N), BF)] * n,
        scratch_shapes=[pltpu.VMEM((M, N), F32)] * n,
        compiler_params=_params("arbitrary"),
    )(*a_list, b)


def _wgrad_pool_groups(p, dzp, tk=2048):
    T = p.shape[0]

    def body(p_ref, d_ref, o_ref, acc_ref):
        t = pl.program_id(1)

        @pl.when(t == 0)
        def _():
            acc_ref[...] = jnp.zeros_like(acc_ref)

        acc_ref[...] += _dot_tn(p_ref[...], d_ref[...])

        @pl.when(t == pl.num_programs(1) - 1)
        def _():
            o_ref[...] = acc_ref[...].astype(BF)

    blk = pl.BlockSpec((tk, POOL_GC), lambda g, t: (t, g))
    return pl.pallas_call(
        body, name="pool_wgrad_groups", grid=(4, T // tk),
        in_specs=[blk, blk],
        out_specs=pl.BlockSpec((None, POOL_GC, POOL_GC), lambda g, t: (g, 0, 0)),
        out_shape=jax.ShapeDtypeStruct((4, POOL_GC, POOL_GC), BF),
        scratch_shapes=[pltpu.VMEM((POOL_GC, POOL_GC), F32)],
        compiler_params=_params("parallel", "arbitrary"),
    )(p, dzp)


def _wgrad_stack(a, b3, name, tk=1024):
    T, M = a.shape
    N = b3.shape[2]

    def body(a_ref, b_ref, o_ref, acc_ref):
        t = pl.program_id(0)

        @pl.when(t == 0)
        def _():
            acc_ref[...] = jnp.zeros_like(acc_ref)

        av = a_ref[...]
        for w in range(3):
            acc_ref[w] += _dot_tn(av, b_ref[w])

        @pl.when(t == pl.num_programs(0) - 1)
        def _():
            o_ref[...] = acc_ref[...].astype(BF)

    return pl.pallas_call(
        body, name=name, grid=(T // tk,),
        in_specs=[_row_spec(tk, M), pl.BlockSpec((3, tk, N), lambda t: (0, t, 0))],
        out_specs=_full_spec((3, M, N)),
        out_shape=jax.ShapeDtypeStruct((3, M, N), BF),
        scratch_shapes=[pltpu.VMEM((3, M, N), F32)],
        compiler_params=_params("arbitrary"),
    )(a, b3)


def _rot_half(t):
    lane = lax.broadcasted_iota(jnp.int32, t.shape, 1)
    first = (lane % HEAD_DIM) < (HEAD_DIM // 2)
    return jnp.where(first, -pltpu.roll(t, 128 - HEAD_DIM // 2, 1), pltpu.roll(t, HEAD_DIM // 2, 1))


def _scatter_rows(dst_ref, scr_ref, d, cast):
    nc, rows, _ = scr_ref.shape
    n = rows // d
    for c in range(nc):
        sl = slice(c * 128, (c + 1) * 128)
        for r in range(d):
            src = scr_ref[c] if d == 1 else scr_ref.at[c][pl.ds(r, n, stride=d), :]
            dst_ref[r, :, sl] = src.astype(cast)


def _gather_rows(scr_ref, src_ref, d):
    nc, rows, _ = scr_ref.shape
    n = rows // d
    for c in range(nc):
        sl = slice(c * 128, (c + 1) * 128)
        for r in range(d):
            val = src_ref[r, :, sl].astype(F32)
            if d == 1:
                scr_ref[c] = val
            else:
                scr_ref.at[c][pl.ds(r, n, stride=d), :] = val


def _chunks_to_rows(scr_ref):
    nc = scr_ref.shape[0]
    return scr_ref[0] if nc == 1 else jnp.concatenate([scr_ref[c] for c in range(nc)], axis=1)


def _rows_to_chunks(scr_ref, val):
    for c in range(scr_ref.shape[0]):
        scr_ref[c] = val[:, c * 128:(c + 1) * 128]


def _rope(t, cv, sv, scale):
    return (t * cv + _rot_half(t) * sv) * scale


def _qkv_fwd(x, g, w3s, cos, sin, tm=512):
    T = x.shape[0]
    C = GROUP_LANES
    nc = C // 128

    def body(x_ref, g_ref, w0, w1, w2, cos_ref, sin_ref, o0, o1, o2, scr_ref):
        h = _rms_fwd(x_ref[...], g_ref[...]).astype(BF)
        cv = cos_ref[...]
        sv = sin_ref[...]
        for gi, (w_ref, o_ref, d) in enumerate(zip((w0, w1, w2), (o0, o1, o2), DILATIONS)):
            for w in range(3):
                t = _dot(h, w_ref[w])
                chunks = [t[:, c * 128:(c + 1) * 128] for c in range(nc)]
                if w < 2:
                    chunks = [_rope(tc, cv, sv, HEAD_DIM ** -0.5 if w == 0 else 1.0) for tc in chunks]
                if d == 1:
                    for c in range(nc):
                        o_ref[w, 0, :, c * 128:(c + 1) * 128] = chunks[c].astype(BF)
                else:
                    scr = scr_ref.at[gi * 3 + w]
                    for c in range(nc):
                        scr[c] = chunks[c]
                    _scatter_rows(o_ref.at[w], scr, d, BF)

    return pl.pallas_call(
        body, name="qkv_fwd", grid=(T // tm,),
        in_specs=[_row_spec(tm, D), _full_spec((1, D))] + [_full_spec((3, D, C))] * 3 + [_row_spec(tm, 128), _row_spec(tm, 128)],
        out_specs=[pl.BlockSpec((3, d, tm // d, C), lambda i: (0, 0, i, 0)) for d in DILATIONS],
        out_shape=[jax.ShapeDtypeStruct((3, d, T // d, C), BF) for d in DILATIONS],
        scratch_shapes=[pltpu.VMEM((9, nc, tm, 128), F32)],
        compiler_params=_params("parallel"),
    )(x, g, *w3s, cos, sin)


def _att_chunk(L):
    return min(L, 1024)


def _head_mask(shape, h):
    lane = lax.broadcasted_iota(jnp.int32, shape, 1)
    return (lane // HEAD_DIM) == (h % 2)


def _attn_fwd(qkv, nh, name):
    _, d, L, C = qkv.shape
    lc = _att_chunk(L)
    nblk = lc // ATT_W

    def body(q_ref, k_ref, kh_ref, v_ref, vh_ref, o_ref, st_ref):
        i = pl.program_id(1)
        qi = lax.broadcasted_iota(jnp.int32, (ATT_W, 2 * ATT_W), 0)
        kj = lax.broadcasted_iota(jnp.int32, (ATT_W, 2 * ATT_W), 1)
        band = jnp.logical_and(kj >= qi, kj <= qi + ATT_W)
        lane = lax.broadcasted_iota(jnp.int32, (ATT_W, 128), 1)

        def block(row0, kc, vc, mask):
            rows = pl.ds(row0, ATT_W)
            lses = []
            for hp in range(C // 128):
                sl = slice(hp * 128, (hp + 1) * 128)
                qp = q_ref[rows, sl]
                kp = kc[:, sl]
                vp = vc[:, sl]
                outs = []
                for h in range(2 * hp, min(2 * hp + 2, nh)):
                    hm = _head_mask(qp.shape, h)
                    s = _dot_nt(jnp.where(hm, qp, jnp.zeros_like(qp)), kp)
                    s = jnp.where(mask, s, NEG_INF)
                    m = jnp.max(s, axis=-1, keepdims=True)
                    e = jnp.exp(s - m)
                    den = jnp.sum(e, axis=-1, keepdims=True)
                    p = (e * pl.reciprocal(den)).astype(BF)
                    outs.append(_dot(p, vp))
                    lses.append(m + jnp.log(den))
                if len(outs) == 2:
                    o = jnp.where(_head_mask(outs[0].shape, 0), outs[0], outs[1])
                else:
                    o = jnp.where(_head_mask(outs[0].shape, 0), outs[0], 0.0)
                o_ref[rows, sl] = o.astype(BF)
            mm = lses[0]
            for l in lses[1:]:
                mm = jnp.maximum(mm, l)
            tot = jnp.exp(lses[0] - mm)
            for l in lses[1:]:
                tot = tot + jnp.exp(l - mm)
            tile = jnp.where(lane == LSE_GROUP_LANE, mm + jnp.log(tot) - math.log(nh), 0.0)
            for h, l in enumerate(lses):
                tile = jnp.where(lane == h, l, tile)
            st_ref[rows, :] = tile

        first_mask = jnp.logical_and(band, jnp.logical_or(kj >= ATT_W, i > 0))
        block(0, jnp.concatenate([kh_ref[...], k_ref[pl.ds(0, ATT_W), :]], axis=0),
              jnp.concatenate([vh_ref[...], v_ref[pl.ds(0, ATT_W), :]], axis=0), first_mask)

        if nblk > 1:
            def step(blk, carry):
                prev = pl.ds(pl.multiple_of((blk - 1) * ATT_W, ATT_W), 2 * ATT_W)
                block(pl.multiple_of(blk * ATT_W, ATT_W), k_ref[prev, :], v_ref[prev, :], band)
                return carry
            lax.fori_loop(1, nblk, step, 0, unroll=True)

    main = lambda w: pl.BlockSpec((None, None, lc, C), lambda r, i: (w, r, i, 0))
    halo = lambda w: pl.BlockSpec((None, None, ATT_W, C), lambda r, i: (w, r, jnp.maximum(i * nblk - 1, 0), 0))
    return pl.pallas_call(
        body, name=name, grid=(d, L // lc),
        in_specs=[main(0), main(1), halo(1), main(2), halo(2)],
        out_specs=[pl.BlockSpec((None, lc, C), lambda r, i: (r, i, 0)), pl.BlockSpec((None, lc, 128), lambda r, i: (r, i, 0))],
        out_shape=[jax.ShapeDtypeStruct((d, L, C), BF), jax.ShapeDtypeStruct((d, L, 128), F32)],
        compiler_params=_params("parallel", "arbitrary"),
    )(qkv, qkv, qkv, qkv, qkv)


def _attn_bwd(qkv, do, st, dst, nh, name):
    _, d, L, C = qkv.shape
    lc = _att_chunk(L)
    nblk = lc // ATT_W
    nchunk = L // lc

    def body(q_ref, qn_ref, k_ref, kh_ref, v_ref, vh_ref, do_ref, don_ref, st_ref, stn_ref, ds_ref, dsn_ref, o_ref,
             p_scr, dsc_scr):
        i = pl.program_id(1)
        qi = lax.broadcasted_iota(jnp.int32, (ATT_W, 2 * ATT_W), 0)
        kj = lax.broadcasted_iota(jnp.int32, (ATT_W, 2 * ATT_W), 1)
        band_q = jnp.logical_and(kj >= qi, kj <= qi + ATT_W)
        qa = lax.broadcasted_iota(jnp.int32, (2 * ATT_W, ATT_W), 0)
        kb = lax.broadcasted_iota(jnp.int32, (2 * ATT_W, ATT_W), 1)
        band_k = jnp.logical_and(qa >= kb, qa <= kb + ATT_W)

        def probs(qm, kp, lse, mask):
            s = _dot_nt(qm, kp)
            return jnp.where(mask, jnp.exp(s - lse), 0.0)

        def q_block(blk, kc, vc, mask):
            rows = pl.ds(blk * ATT_W, ATT_W)
            stv = st_ref[rows, :]
            dsv = ds_ref[rows, :]
            for hp in range(C // 128):
                sl = slice(hp * 128, (hp + 1) * 128)
                qp = q_ref[rows, sl]
                dop = do_ref[rows, sl]
                kp = kc[:, sl]
                vp = vc[:, sl]
                outs = []
                for h in range(2 * hp, min(2 * hp + 2, nh)):
                    hm = _head_mask(qp.shape, h)
                    p = probs(jnp.where(hm, qp, jnp.zeros_like(qp)), kp, _lane_col(stv, h), mask)
                    dp = _dot_nt(jnp.where(hm, dop, jnp.zeros_like(dop)), vp)
                    dsc = (p * (dp - _lane_col(dsv, h))).astype(BF)
                    p_scr[blk, h] = p.astype(BF)
                    dsc_scr[blk, h] = dsc
                    outs.append(_dot(dsc, kp))
                if len(outs) == 2:
                    dq = jnp.where(_head_mask(outs[0].shape, 0), outs[0], outs[1])
                else:
                    dq = jnp.where(_head_mask(outs[0].shape, 0), outs[0], 0.0)
                o_ref[0, rows, sl] = dq

        def k_block_kept(m):
            rows = pl.ds(m * ATT_W, ATT_W)
            two = pl.ds(m * ATT_W, 2 * ATT_W)
            for hp in range(C // 128):
                sl = slice(hp * 128, (hp + 1) * 128)
                qp = q_ref[two, sl]
                dop = do_ref[two, sl]
                dk, dv = None, None
                for h in range(2 * hp, min(2 * hp + 2, nh)):
                    hm = _head_mask(qp.shape, h)
                    qm = jnp.where(hm, qp, jnp.zeros_like(qp))
                    dom = jnp.where(hm, dop, jnp.zeros_like(dop))
                    dsc = jnp.concatenate([dsc_scr[m, h, :, ATT_W:], dsc_scr[m + 1, h, :, :ATT_W]], axis=0)
                    p = jnp.concatenate([p_scr[m, h, :, ATT_W:], p_scr[m + 1, h, :, :ATT_W]], axis=0)
                    dk_h = _dot_tn(dsc, qm)
                    dv_h = _dot_tn(p, dom)
                    dk = dk_h if dk is None else dk + dk_h
                    dv = dv_h if dv is None else dv + dv_h
                o_ref[1, rows, sl] = dk
                o_ref[2, rows, sl] = dv

        def k_block(row0, qq, doo, stv, dsv, mask):
            rows = pl.ds(row0, ATT_W)
            for hp in range(C // 128):
                sl = slice(hp * 128, (hp + 1) * 128)
                qp = qq[:, sl]
                dop = doo[:, sl]
                kp = k_ref[rows, sl]
                vp = v_ref[rows, sl]
                dks, dvs = [], []
                for h in range(2 * hp, min(2 * hp + 2, nh)):
                    hm = _head_mask(qp.shape, h)
                    qm = jnp.where(hm, qp, jnp.zeros_like(qp))
                    dom = jnp.where(hm, dop, jnp.zeros_like(dop))
                    p = probs(qm, kp, _lane_col(stv, h), mask)
                    dp = _dot_nt(dom, vp)
                    dsc = (p * (dp - _lane_col(dsv, h))).astype(BF)
                    dks.append(_dot_tn(dsc, qm))
                    dvs.append(_dot_tn(p.astype(BF), dom))
                if len(dks) == 2:
                    o_ref[1, rows, sl] = dks[0] + dks[1]
                    o_ref[2, rows, sl] = dvs[0] + dvs[1]
                else:
                    o_ref[1, rows, sl] = dks[0]
                    o_ref[2, rows, sl] = dvs[0]

        first_mask = jnp.logical_and(band_q, jnp.logical_or(kj >= ATT_W, i > 0))
        q_block(0, jnp.concatenate([kh_ref[...], k_ref[pl.ds(0, ATT_W), :]], axis=0),
                jnp.concatenate([vh_ref[...], v_ref[pl.ds(0, ATT_W), :]], axis=0), first_mask)
        for blk in range(1, nblk):
            prev = pl.ds((blk - 1) * ATT_W, 2 * ATT_W)
            q_block(blk, k_ref[prev, :], v_ref[prev, :], band_q)
        for m in range(nblk - 1):
            k_block_kept(m)

        last = pl.ds((nblk - 1) * ATT_W, ATT_W)
        last_mask = jnp.logical_and(band_k, jnp.logical_or(qa < ATT_W, i < nchunk - 1))
        k_block((nblk - 1) * ATT_W,
                jnp.concatenate([q_ref[last, :], qn_ref[...]], axis=0),
                jnp.concatenate([do_ref[last, :], don_ref[...]], axis=0),
                jnp.concatenate([st_ref[last, :], stn_ref[...]], axis=0),
                jnp.concatenate([ds_ref[last, :], dsn_ref[...]], axis=0), last_mask)

    nb_all = L // ATT_W
    main4 = lambda w: pl.BlockSpec((None, None, lc, C), lambda r, i: (w, r, i, 0))
    prev4 = lambda w: pl.BlockSpec((None, None, ATT_W, C), lambda r, i: (w, r, jnp.maximum(i * nblk - 1, 0), 0))
    next4 = lambda w: pl.BlockSpec((None, None, ATT_W, C), lambda r, i: (w, r, jnp.minimum((i + 1) * nblk, nb_all - 1), 0))
    main3 = lambda n: pl.BlockSpec((None, lc, n), lambda r, i: (r, i, 0))
    next3 = lambda n: pl.BlockSpec((None, ATT_W, n), lambda r, i: (r, jnp.minimum((i + 1) * nblk, nb_all - 1), 0))
    return pl.pallas_call(
        body, name=name, grid=(d, nchunk),
        in_specs=[main4(0), next4(0), main4(1), prev4(1), main4(2), prev4(2),
                  main3(C), next3(C), main3(128), next3(128), main3(128), next3(128)],
        out_specs=pl.BlockSpec((3, None, lc, C), lambda r, i: (0, r, i, 0)),
        out_shape=jax.ShapeDtypeStruct((3, d, L, C), F32),
        scratch_shapes=[pltpu.VMEM((nblk, nh, ATT_W, 2 * ATT_W), BF)] * 2,
        compiler_params=_params("parallel", "arbitrary"),
    )(qkv, qkv, qkv, qkv, qkv, qkv, do, do, st, st, dst, dst)


def _alpha_from(lse_nat):
    m = jnp.maximum(jnp.maximum(lse_nat[0], lse_nat[1]), lse_nat[2])
    e = [jnp.exp(l - m) for l in lse_nat]
    inv = 1.0 / (e[0] + e[1] + e[2])
    return [ei * inv for ei in e]


def _attn_out_fwd(x, os_, sts, wos, tm=512):
    T = x.shape[0]
    C = GROUP_LANES

    def body(x_ref, o0, o1, o2, s0, s1, s2, w0, w1, w2, xo_ref, m0, m1, m2, al_ref, oscr, sscr):
        o_refs, st_refs, w_refs, m_refs = (o0, o1, o2), (s0, s1, s2), (w0, w1, w2), (m0, m1, m2)
        lses = []
        for g, d in enumerate(DILATIONS):
            _gather_rows(sscr.at[g], st_refs[g], d)
            lses.append(_lane_col(sscr[g, 0], LSE_GROUP_LANE))
        alpha = _alpha_from(lses)
        y = x_ref[...]
        for g, d in enumerate(DILATIONS):
            _gather_rows(oscr, o_refs[g], d)
            mg = (_chunks_to_rows(oscr) * (3.0 * alpha[g])).astype(BF)
            m_refs[g][...] = mg
            y = y + _dot(mg, w_refs[g][...])
        xo_ref[...] = y
        lane = lax.broadcasted_iota(jnp.int32, (tm, 128), 1)
        al_ref[...] = jnp.where(lane == 0, alpha[0], jnp.where(lane == 1, alpha[1], jnp.where(lane == 2, alpha[2], 0.0)))

    o_specs = [pl.BlockSpec((d, tm // d, C), lambda i: (0, i, 0)) for d in DILATIONS]
    st_specs = [pl.BlockSpec((d, tm // d, 128), lambda i: (0, i, 0)) for d in DILATIONS]
    mshape = jax.ShapeDtypeStruct((T, C), BF)
    return pl.pallas_call(
        body, name="attn_out_fwd", grid=(T // tm,),
        in_specs=[_row_spec(tm, D)] + o_specs + st_specs + [_full_spec((C, D))] * 3,
        out_specs=[_row_spec(tm, D), _row_spec(tm, C), _row_spec(tm, C), _row_spec(tm, C), _row_spec(tm, 128)],
        out_shape=[jax.ShapeDtypeStruct((T, D), F32), mshape, mshape, mshape, jax.ShapeDtypeStruct((T, 128), F32)],
        scratch_shapes=[pltpu.VMEM((C // 128, tm, 128), F32), pltpu.VMEM((3, 1, tm, 128), F32)],
        compiler_params=_params("parallel"),
    )(x, *os_, *sts, *wos)


def _attn_out_bwd(dx, os_, sts, alpha, wos, tm=512, dep=None):
    T = dx.shape[0]
    C = GROUP_LANES

    def body(dx_ref, o0, o1, o2, s0, s1, s2, al_ref, w0, w1, w2, do0, do1, do2, ds0, ds1, ds2, dmscr, oscr, sscr, tscr):
        o_refs, st_refs, w_refs = (o0, o1, o2), (s0, s1, s2), (w0, w1, w2)
        do_refs, ds_refs = (do0, do1, do2), (ds0, ds1, ds2)
        dyb = dx_ref[...].astype(BF)
        alv = al_ref[...]
        alpha_g = [_lane_col(alv, g) for g in range(3)]
        dalpha = []
        for g, d in enumerate(DILATIONS):
            dm = _dot_nt(dyb, w_refs[g][...])
            _rows_to_chunks(dmscr.at[g], dm)
            _gather_rows(oscr.at[g], o_refs[g], d)
            _gather_rows(sscr.at[g], st_refs[g], d)
            dalpha.append(3.0 * jnp.sum(dm * _chunks_to_rows(oscr.at[g]), axis=-1, keepdims=True))
        mean_da = alpha_g[0] * dalpha[0] + alpha_g[1] * dalpha[1] + alpha_g[2] * dalpha[2]
        lane = lax.broadcasted_iota(jnp.int32, (tm, 128), 1)
        head_lane = lax.broadcasted_iota(jnp.int32, (tm, C), 1) // HEAD_DIM
        for g, d in enumerate(DILATIONS):
            nh = HEAD_GROUPS[g]
            dlse_g = alpha_g[g] * (dalpha[g] - mean_da)
            do_nat = _chunks_to_rows(dmscr.at[g]) * (3.0 * alpha_g[g])
            prod = do_nat * _chunks_to_rows(oscr.at[g])
            stv = sscr[g, 0]
            lse_g = _lane_col(stv, LSE_GROUP_LANE)
            tile = jnp.zeros((tm, 128), F32)
            for h in range(nh):
                delta = jnp.sum(jnp.where(head_lane == h, prod, 0.0), axis=-1, keepdims=True)
                dlse = dlse_g * jnp.exp(_lane_col(stv, h) - lse_g) * (1.0 / nh)
                tile = jnp.where(lane == h, delta - dlse, tile)
            _rows_to_chunks(dmscr.at[g], do_nat)
            _scatter_rows(do_refs[g], dmscr.at[g], d, BF)
            tscr[0] = tile
            _scatter_rows(ds_refs[g], tscr, d, F32)

    o_specs = [pl.BlockSpec((d, tm // d, C), lambda i: (0, i, 0)) for d in DILATIONS]
    st_specs = [pl.BlockSpec((d, tm // d, 128), lambda i: (0, i, 0)) for d in DILATIONS]
    body, in_specs, args = _after(
        dep, body, [_row_spec(tm, D)] + o_specs + st_specs + [_row_spec(tm, 128)] + [_full_spec((C, D))] * 3,
        [dx, *os_, *sts, alpha, *wos])
    return pl.pallas_call(
        body, name="attn_out_bwd", grid=(T // tm,),
        in_specs=in_specs,
        out_specs=o_specs + st_specs,
        out_shape=[jax.ShapeDtypeStruct((d, T // d, C), BF) for d in DILATIONS]
        + [jax.ShapeDtypeStruct((d, T // d, 128), F32) for d in DILATIONS],
        scratch_shapes=[pltpu.VMEM((3, C // 128, tm, 128), F32), pltpu.VMEM((3, C // 128, tm, 128), F32),
                        pltpu.VMEM((3, 1, tm, 128), F32), pltpu.VMEM((1, tm, 128), F32)],
        compiler_params=_params("parallel"),
    )(*args)


def _qkv_bwd(dqkvs, cos, sin, x2, g, w3s, dx3, tm=256):
    T = x2.shape[0]
    C = GROUP_LANES

    def body(dq0, dq1, dq2, cos_ref, sin_ref, x_ref, g_ref, w0, w1, w2, dx3_ref,
             dx_ref, n0, n1, n2, h_ref, dg_ref, scr, dh_ref):
        dq_refs, w_refs, n_refs = (dq0, dq1, dq2), (w0, w1, w2), (n0, n1, n2)

        @pl.when(pl.program_id(0) == 0)
        def _():
            dg_ref[...] = jnp.zeros_like(dg_ref)

        cv = cos_ref[...]
        sv = sin_ref[...]
        dh_ref[...] = jnp.zeros_like(dh_ref)
        for gi, d in enumerate(DILATIONS):
            for w in range(3):
                _gather_rows(scr, dq_refs[gi].at[w], d)
                if w < 2:
                    scale = HEAD_DIM ** -0.5 if w == 0 else 1.0
                    for c in range(C // 128):
                        t = scr[c]
                        scr[c] = (t * cv - _rot_half(t) * sv) * scale
                tb = _chunks_to_rows(scr).astype(BF)
                n_refs[gi][w] = tb
                dh_ref[...] += _dot_nt(tb, w_refs[gi][w])
        xv = x_ref[...]
        h_ref[...] = _rms_fwd(xv, g_ref[...]).astype(BF)
        dx, dg = _rms_bwd(xv, g_ref[...], dh_ref[...])
        dx_ref[...] = dx3_ref[...] + dx
        dg_ref[...] += dg

    dq_specs = [pl.BlockSpec((3, d, tm // d, C), lambda i: (0, 0, i, 0)) for d in DILATIONS]
    nat = pl.BlockSpec((3, tm, C), lambda i: (0, i, 0))
    nshape = jax.ShapeDtypeStruct((3, T, C), BF)
    return pl.pallas_call(
        body, name="qkv_bwd", grid=(T // tm,),
        in_specs=dq_specs + [_row_spec(tm, 128), _row_spec(tm, 128), _row_spec(tm, D), _full_spec((1, D))]
        + [_full_spec((3, D, C))] * 3 + [_row_spec(tm, D)],
        out_specs=[_row_spec(tm, D), nat, nat, nat, _row_spec(tm, D), _full_spec((1, D))],
        out_shape=[jax.ShapeDtypeStruct((T, D), F32), nshape, nshape, nshape, jax.ShapeDtypeStruct((T, D), BF),
                   jax.ShapeDtypeStruct((1, D), F32)],
        scratch_shapes=[pltpu.VMEM((C // 128, tm, 128), F32), pltpu.VMEM((tm, D), F32)],
        compiler_params=_params("arbitrary"),
    )(*dqkvs, cos, sin, x2, g, *w3s, dx3)


def _final_bwd(x, target, g, tm=512):
    T = x.shape[0]

    def body(x_ref, t_ref, g_ref, dx_ref, loss_ref, dg_ref):
        @pl.when(pl.program_id(0) == 0)
        def _():
            loss_ref[...] = jnp.zeros_like(loss_ref)
            dg_ref[...] = jnp.zeros_like(dg_ref)

        xv = x_ref[...]
        gv = g_ref[...]
        diff = _rms_fwd(xv, gv) - t_ref[...]
        loss_ref[...] += 0.5 * jnp.sum(jnp.mean(diff * diff, axis=-1, keepdims=True), axis=0, keepdims=True)
        dx, dg = _rms_bwd(xv, gv, diff * (1.0 / D))
        dx_ref[...] = dx
        dg_ref[...] += dg

    return pl.pallas_call(
        body, name="final_bwd", grid=(T // tm,),
        in_specs=[_row_spec(tm, D), _row_spec(tm, D), _full_spec((1, D))],
        out_specs=[_row_spec(tm, D), _full_spec((1, 1)), _full_spec((1, D))],
        out_shape=[jax.ShapeDtypeStruct((T, D), F32), jax.ShapeDtypeStruct((1, 1), F32), jax.ShapeDtypeStruct((1, D), F32)],
        compiler_params=_params("arbitrary"),
    )(x, target, g)


def _place():
    x, y, c = lax.axis_index("x"), lax.axis_index("y"), lax.axis_index("c")
    return x, y, c


def _gather_phases(ins, outs, *sems):
    n = len(ins)
    if n == 0:
        return (lambda: None), (lambda: None)
    send_sems, recv_sems, local_sems = sems
    x, y, c = _place()
    me, sibling = (x, y, c), (x, y, 1 - c)
    chips = [(1 - x, y), (x, 1 - y), (1 - x, 1 - y)]

    def slot(a, px, py, pc):
        return outs[a].at[4 * px + 2 * py + pc]

    def copy(a, k, block, to, src=None):
        return pltpu.make_async_remote_copy(
            src_ref=slot(a, *block) if src is None else src, dst_ref=slot(a, *block),
            send_sem=send_sems.at[a, k], recv_sem=recv_sems.at[a, k], device_id=to, device_id_type=MESH)

    def mine(a):
        return pltpu.make_async_copy(ins[a], slot(a, *me), local_sems.at[a])

    def first(a):
        return [copy(a, 0, me, sibling, src=ins[a])] + [copy(a, 1 + j, me, (*chip, c), src=ins[a]) for j, chip in enumerate(chips)]

    def begin():
        for a in range(n):
            mine(a).start()
        for a in range(n):
            for cp in first(a):
                cp.start()

    def finish():
        passed = []
        for a in range(n):
            for j, chip in enumerate(chips):
                copy(a, 1 + j, (*chip, c), me).wait_recv()
                fwd = copy(a, 4 + j, (*chip, c), sibling)
                fwd.start()
                passed.append(fwd)
        for a in range(n):
            copy(a, 0, sibling, me).wait_recv()
            for j, chip in enumerate(chips):
                copy(a, 4 + j, (*chip, 1 - c), me).wait_recv()
        for a in range(n):
            for cp in first(a):
                cp.wait_send()
        for cp in passed:
            cp.wait_send()
        for a in range(n):
            mine(a).wait()

    return begin, finish


def _gather_scratch(n):
    return [pltpu.SemaphoreType.DMA((n, 7)), pltpu.SemaphoreType.DMA((n, 7)), pltpu.SemaphoreType.DMA((n,))] if n else []


def _allgather(arrs, name):
    n = len(arrs)

    def body(*refs):
        begin, finish = _gather_phases(refs[:n], refs[n:2 * n], *refs[2 * n:])
        begin()
        finish()

    hbm = pl.BlockSpec(memory_space=pl.ANY)
    return pl.pallas_call(
        body, name=name,
        in_specs=[hbm] * n, out_specs=[hbm] * n,
        out_shape=[jax.ShapeDtypeStruct((N_DEV,) + a.shape, a.dtype) for a in arrs],
        scratch_shapes=_gather_scratch(n),
    )(*arrs)


def _peer(k):
    x, y, c = _place()
    px = 1 - x if k & 4 else x
    py = 1 - y if k & 2 else y
    pc = 1 - c if k & 1 else c
    return (px, py, pc), 4 * px + 2 * py + pc


HBM_SPEC = pl.BlockSpec(memory_space=pltpu.HBM)
SEM_SPEC = pl.BlockSpec(memory_space=pltpu.SEMAPHORE)
EFFECT = pltpu.SideEffectType.DATAFLOW_SIDE_EFFECTING


def _exchange_start(arrs, name, same_block=False, dep=None):
    n = len(arrs)
    n_dep = 0 if dep is None else 1

    def body(*refs):
        srcs, lands = refs[:n], refs[n:2 * n]
        send_sems, recv_sems, local_sems = refs[2 * n + n_dep:2 * n + n_dep + 3]
        token = refs[-1]
        x, y, c = _place()
        me = 4 * x + 2 * y + c
        block = (lambda a, j: srcs[a]) if same_block else (lambda a, j: srcs[a].at[j])
        for a in range(n):
            pltpu.make_async_copy(block(a, me), lands[a].at[me], local_sems.at[a]).start()
        for a in range(n):
            for k in range(1, N_DEV):
                to, to_idx = _peer(k)
                pltpu.make_async_remote_copy(
                    src_ref=block(a, to_idx), dst_ref=lands[a].at[me],
                    send_sem=send_sems.at[a * (N_DEV - 1) + k - 1], recv_sem=recv_sems.at[a * (N_DEV - 1) + k - 1], device_id=to, device_id_type=MESH).start()
        token[...] = jnp.zeros_like(token)

    land_shape = (lambda a: (N_DEV,) + a.shape) if same_block else (lambda a: a.shape)
    src_shapes = [pltpu.HBM(a.shape, a.dtype) for a in arrs]
    land_shapes = [pltpu.HBM(land_shape(a), a.dtype) for a in arrs]
    outs = pl.pallas_call(
        body, name=name,
        out_shape=(pltpu.SemaphoreType.DMA((n * (N_DEV - 1),)), pltpu.SemaphoreType.DMA((n * (N_DEV - 1),)),
                   pltpu.SemaphoreType.DMA((n,)), *src_shapes, *land_shapes, jax.ShapeDtypeStruct((8, 128), F32)),
        in_specs=[HBM_SPEC] * (2 * n) + [pl.BlockSpec(memory_space=pl.ANY)] * n_dep,
        out_specs=(SEM_SPEC, SEM_SPEC, SEM_SPEC, *([HBM_SPEC] * (2 * n)), pl.BlockSpec(memory_space=pltpu.VMEM)),
        input_output_aliases={i: 3 + i for i in range(2 * n)},
        compiler_params=pltpu.CompilerParams(has_side_effects=EFFECT),
    )(*[pltpu.with_memory_space_constraint(a, pltpu.HBM) for a in arrs],
      *[pltpu.with_memory_space_constraint(lax.empty(land_shape(a), a.dtype), pltpu.HBM) for a in arrs],
      *([] if dep is None else [dep]))
    return outs[0], outs[1], outs[2], outs[3:3 + n], outs[3 + n:3 + 2 * n], outs[3 + 2 * n]


def _exchange_wait(send_sems, recv_sems, local_sems, src_thru, land_thru, after, name):
    n = len(src_thru)
    after = list(after) if isinstance(after, (list, tuple)) else [after]
    same_block = src_thru[0].shape != land_thru[0].shape

    def body(*refs):
        srcs, lands = refs[:n], refs[n:2 * n]
        send_sems, recv_sems, local_sems = refs[2 * n:2 * n + 3]
        x, y, c = _place()
        me = 4 * x + 2 * y + c
        for a in range(n):
            pltpu.make_async_copy(srcs[a] if same_block else srcs[a].at[me], lands[a].at[me], local_sems.at[a]).wait()
            for k in range(1, N_DEV):
                frm, frm_idx = _peer(k)
                cp = pltpu.make_async_remote_copy(
                    src_ref=srcs[a] if same_block else srcs[a].at[frm_idx], dst_ref=lands[a].at[frm_idx],
                    send_sem=send_sems.at[a * (N_DEV - 1) + k - 1], recv_sem=recv_sems.at[a * (N_DEV - 1) + k - 1], device_id=frm, device_id_type=MESH)
                cp.wait_send()
                cp.wait_recv()

    outs = pl.pallas_call(
        body, name=name,
        out_shape=tuple(pltpu.HBM(a.shape, a.dtype) for a in (*src_thru, *land_thru)),
        in_specs=[HBM_SPEC] * (2 * n) + [SEM_SPEC, SEM_SPEC, SEM_SPEC] + [pl.BlockSpec(memory_space=pl.ANY)] * len(after),
        out_specs=[HBM_SPEC] * (2 * n),
        input_output_aliases={i: i for i in range(2 * n)},
        compiler_params=pltpu.CompilerParams(has_side_effects=EFFECT),
    )(*src_thru, *land_thru, send_sems, recv_sems, local_sems, *after)
    return outs[n:]


def _row_tile(rows):
    for t in (256, 176, 128, 8):
        if rows % t == 0:
            return t
    return rows


def _adamw(parts, w, m, v, name, transposed=False):
    K = parts.shape[0]
    R, C = w.shape
    tr = 256 if transposed else _row_tile(R)

    def body(p_ref, w_ref, m_ref, v_ref, g_ref, d_ref, nm_ref, nv_ref):
        g = p_ref[0].astype(F32)
        for k in range(1, K):
            g = g + p_ref[k].astype(F32)
        if transposed:
            g = g.T
        nm = ADAM_B1 * m_ref[...] + (1.0 - ADAM_B1) * g
        nv = ADAM_B2 * v_ref[...] + (1.0 - ADAM_B2) * jnp.square(g)
        m_hat = nm / (1.0 - ADAM_B1 ** ADAM_STEP)
        v_hat = nv / (1.0 - ADAM_B2 ** ADAM_STEP)
        g_ref[...] = g
        d_ref[...] = -ADAM_LR * (m_hat / (jnp.sqrt(v_hat) + ADAM_EPS) + ADAM_WD * w_ref[...])
        nm_ref[...] = nm
        nv_ref[...] = nv

    blk = _row_spec(tr, C)
    shp = jax.ShapeDtypeStruct((R, C), F32)
    parts_spec = pl.BlockSpec((K, C, tr), lambda i: (0, 0, i)) if transposed else pl.BlockSpec((K, tr, C), lambda i: (0, i, 0))
    return pl.pallas_call(
        body, name=name, grid=(R // tr,),
        in_specs=[parts_spec, blk, blk, blk],
        out_specs=[blk] * 4,
        out_shape=[shp] * 4,
        compiler_params=_params("parallel"),
    )(parts, w, m, v)


def _rope_tables(T):
    inv_freq = 1.0 / (ROPE_THETA ** (jnp.arange(0, HEAD_DIM, 2, dtype=F32) / HEAD_DIM))
    ang = jnp.arange(T, dtype=F32)[:, None] * inv_freq[None, :]
    cos, sin = jnp.cos(ang), jnp.sin(ang)
    return jnp.concatenate([cos] * 4, axis=-1), jnp.concatenate([sin] * 4, axis=-1)


def _pad_lanes(a, n):
    return jnp.pad(a, ((0, 0),) * (a.ndim - 1) + ((0, n - a.shape[-1]),))


POOL = ("pool_in", "pool_grp", "pool_out")
FFN0 = ("gate0", "up0", "down0")
LAYER1 = ("qkv", "attn_out", "gate1", "up1", "down1")


def _layout_weights(gw):
    w = {}
    if "pool_in" in gw:
        w["pool_in"] = gw["pool_in"].reshape(D, D)
        w["pool_grp"] = jnp.transpose(gw["pool_grp"], (1, 0, 2, 3)).reshape(4, POOL_GC, POOL_GC)
        w["pool_out"] = gw["pool_out"].reshape(D, D)
    if "qkv" in gw:
        wqkv = jnp.transpose(gw["qkv"], (1, 0, 2)).reshape(D, 3 * D)
        wo = gw["attn_out"].reshape(D, D)
        w["qkv"], w["attn_out"] = [], []
        for nh, off in zip(HEAD_GROUPS, HEAD_OFFS):
            lo, n = off * HEAD_DIM, nh * HEAD_DIM
            w["qkv"].append(jnp.stack([_pad_lanes(wqkv[:, k * D + lo:k * D + lo + n], GROUP_LANES) for k in range(3)]))
            w["attn_out"].append(jnp.pad(wo[lo:lo + n], ((0, GROUP_LANES - n), (0, 0))))
    for nm in gw:
        if nm[:-1] in ("gate", "up", "down"):
            w[nm] = gw[nm].reshape(F, D)
    return w


def _local_step(x, target, w, ffn0_shards, layer1_shards, norm_mix, norm_ffn, norm_final, pool_scale, emit):
    T = x.shape[0]
    cos, sin = _rope_tables(T)
    nm = [norm_mix[i:i + 1] for i in range(2)]
    nf = [norm_ffn[i:i + 1] for i in range(2)]
    nfin = norm_final.reshape(1, D)

    x1, p, *ffn0 = _pool_fwd(x, nm[0], w["pool_in"], w["pool_grp"], pool_scale, w["pool_out"], ffn0_shards)
    w = {**w, **_layout_weights(dict(zip(FFN0, ffn0)))}
    x2, a0, b0, s0, *layer1 = _ffn_fwd(x1, nf[0], w["gate0"], w["up0"], w["down0"], "ffn_fwd0", gather=layer1_shards)
    w = {**w, **_layout_weights(dict(zip(LAYER1, layer1)))}
    qkvs = _qkv_fwd(x2, nm[1], w["qkv"], cos, sin)
    att = [_attn_fwd(qkvs[g], HEAD_GROUPS[g], f"attn_fwd{g}") for g in range(3)]
    os_, sts = [a[0] for a in att], [a[1] for a in att]
    x3, m0, m1, m2, alpha = _attn_out_fwd(x2, os_, sts, w["attn_out"])
    x4, a1, b1, s1 = _ffn_fwd(x3, nf[1], w["gate1"], w["up1"], w["down1"], "ffn_fwd1")

    g = {}
    dx4, loss, dg_final = _final_bwd(x4, target, nfin)
    dx3, da1, db1, h3, dy4, dg_ffn1 = _ffn_bwd(dx4, x3, nf[1], a1, b1, w["gate1"], w["up1"], w["down1"], "ffn_bwd1")
    g["down1"] = _wgrad(s1, dy4, "down_wgrad1", mblk=F // 2)
    g["gate1"] = _wgrad(da1, h3, "gate_wgrad1", mblk=F // 2)
    g["up1"] = _wgrad(db1, h3, "up_wgrad1", mblk=F // 2)
    dep = emit("ffn1", g)
    dos_and_stats = _attn_out_bwd(dx3, os_, sts, alpha, w["attn_out"], dep=dep)
    dos, dsts = dos_and_stats[:3], dos_and_stats[3:]
    g["attn_out"] = _wgrad_multi([m0, m1, m2], dx3, "attn_out_wgrad")
    dqkvs = [_attn_bwd(qkvs[gi], dos[gi], sts[gi], dsts[gi], HEAD_GROUPS[gi], f"attn_bwd{gi}") for gi in range(3)]
    dx2, n0, n1, n2, h2b, dg_mix1 = _qkv_bwd(dqkvs, cos, sin, x2, nm[1], w["qkv"], dx3)
    g["qkv"] = [_wgrad_stack(h2b, n, f"qkv_wgrad{i}") for i, n in enumerate((n0, n1, n2))]
    dep = emit("attn", g)
    dx1, da0, db0, h1, dy2, dg_ffn0 = _ffn_bwd(dx2, x1, nf[0], a0, b0, w["gate0"], w["up0"], w["down0"], "ffn_bwd0", dep=dep)
    g["down0"] = _wgrad(s0, dy2, "down_wgrad0", mblk=F // 2)
    g["gate0"] = _wgrad(da0, h1, "gate_wgrad0", mblk=F // 2)
    g["up0"] = _wgrad(db0, h1, "up_wgrad0", mblk=F // 2)
    dep = emit("ffn0", g)
    dx0, z, dzp, du, h0b, dscale, dg_mix0 = _pool_bwd(dx1, x, nm[0], p, w["pool_in"], w["pool_grp"], pool_scale, w["pool_out"], dep=dep)
    g["pool_out"] = _wgrad(z, dx1, "pool_out_wgrad")
    g["pool_in"] = _wgrad(h0b, du, "pool_in_wgrad")
    g["pool_grp"] = _wgrad_pool_groups(p, dzp)
    emit("pool", g)

    small = jnp.concatenate([dg_mix0, dg_mix1, dg_ffn0, dg_ffn1, dg_final, dscale,
                             jnp.broadcast_to(loss, (1, D)), jnp.zeros((1, D), F32)], axis=0)
    return dx0, small


GROUPS = {"ffn1": ("down1", "gate1", "up1"), "attn": ("qkv", "attn_out"), "ffn0": ("down0", "gate0", "up0"),
          "pool": ("pool_in", "pool_out", "pool_grp")}


def _grad_blocks(group, g):
    blocks = {}
    if group == "pool":
        blocks["pool_in"] = g["pool_in"].reshape(N_DEV, D // N_DEV, D)
        blocks["pool_out"] = g["pool_out"].reshape(N_DEV, D // N_DEV, D)
        blocks["pool_grp"] = jnp.transpose(g["pool_grp"].reshape(4, N_DEV, POOL_GC // N_DEV, POOL_GC), (1, 0, 2, 3)).reshape(N_DEV, 4 * POOL_GC // N_DEV, POOL_GC)
    elif group == "attn":
        wo = jnp.concatenate([gw[:nh * HEAD_DIM] for gw, nh in zip(g["attn_out"], HEAD_GROUPS)], axis=0)
        blocks["attn_out"] = wo.reshape(N_DEV, D // N_DEV, D)
        wqkv = jnp.concatenate([g["qkv"][gi][k][:, :HEAD_GROUPS[gi] * HEAD_DIM] for k in range(3) for gi in range(3)], axis=1)
        blocks["qkv"] = jnp.transpose(wqkv.reshape(D, N_DEV, 3 * D // N_DEV), (1, 0, 2))
    else:
        for nm in GROUPS[group]:
            blocks[nm] = g[nm].reshape(N_DEV, F // N_DEV, D)
    return [blocks[nm] for nm in GROUPS[group]]


def kernel(x, norm_mix, norm_ffn, norm_final, pool_w_in, pool_w_group, pool_scale, pool_w_out, attn_w_qkv, attn_w_out, ffn_w_gate, ffn_w_up, ffn_w_down, loss_target, m_norm_mix, m_norm_ffn, m_norm_final, m_pool_w_in, m_pool_w_group, m_pool_scale, m_pool_w_out, m_attn_w_qkv, m_attn_w_out, m_ffn_w_gate, m_ffn_w_up, m_ffn_w_down, v_norm_mix, v_norm_ffn, v_norm_final, v_pool_w_in, v_pool_w_group, v_pool_scale, v_pool_w_out, v_attn_w_qkv, v_attn_w_out, v_ffn_w_gate, v_ffn_w_up, v_ffn_w_down):
    shard = {
        "pool_in": pool_w_in[0], "pool_grp": pool_w_group[0], "pool_out": pool_w_out[0],
        "qkv": attn_w_qkv[0], "attn_out": attn_w_out[0],
    }
    for l in range(2):
        shard[f"gate{l}"] = ffn_w_gate[l].T
        shard[f"up{l}"] = ffn_w_up[l].T
        shard[f"down{l}"] = ffn_w_down[l]
    shard = {k: v.astype(BF) for k, v in shard.items()}
    w0 = _layout_weights(dict(zip(POOL, _allgather([shard[k] for k in POOL], "weights_allgather"))))

    started = {}

    def emit(group, g):
        started[group] = _exchange_start(_grad_blocks(group, g), f"grads_start_{group}")
        return started[group][5]

    grad_x, small = _local_step(x[0], loss_target[0], w0, [shard[k] for k in FFN0], [shard[k] for k in LAYER1], norm_mix, norm_ffn, norm_final, pool_scale, emit)

    small_st = _exchange_start([small], "small_start", same_block=True, dep=started["pool"][5])

    def finish(group, after):
        lands = _exchange_wait(*started[group][:5], after, f"grads_wait_{group}")
        return dict(zip(GROUPS[group], lands))

    def upd(parts, wt, mt, vt, name):
        shape = wt.shape
        r2 = lambda t: t.reshape(parts.shape[1:])
        outs = _adamw(parts, r2(wt), r2(mt), r2(vt), name)
        return [o.reshape(shape) for o in outs]

    def ffn_layer(rcv, l):
        out = {}
        for nm, wt, mt, vt in (("gate", ffn_w_gate, m_ffn_w_gate, v_ffn_w_gate), ("up", ffn_w_up, m_ffn_w_up, v_ffn_w_up)):
            out[nm] = _adamw(rcv[f"{nm}{l}"], wt[l], mt[l], vt[l], f"adamw_{nm}{l}", transposed=True)
        out["down"] = _adamw(rcv[f"down{l}"], ffn_w_down[l], m_ffn_w_down[l], v_ffn_w_down[l], f"adamw_down{l}")
        return out

    res = {}
    ffn1 = ffn_layer(finish("ffn1", small_st[5]), 1)
    rcv = finish("attn", [ffn1[nm][0] for nm in ffn1])
    res["attn_w_qkv"] = upd(rcv["qkv"], attn_w_qkv, m_attn_w_qkv, v_attn_w_qkv, "adamw_qkv")
    res["attn_w_out"] = upd(rcv["attn_out"], attn_w_out, m_attn_w_out, v_attn_w_out, "adamw_attn_out")
    ffn0 = ffn_layer(finish("ffn0", [res["attn_w_qkv"][0], res["attn_w_out"][0]]), 0)
    rcv = finish("pool", [ffn0[nm][0] for nm in ffn0])
    res["pool_w_in"] = upd(rcv["pool_in"], pool_w_in, m_pool_w_in, v_pool_w_in, "adamw_pool_in")
    res["pool_w_group"] = upd(rcv["pool_grp"], pool_w_group, m_pool_w_group, v_pool_w_group, "adamw_pool_grp")
    res["pool_w_out"] = upd(rcv["pool_out"], pool_w_out, m_pool_w_out, v_pool_w_out, "adamw_pool_out")
    for nm in ("gate", "up", "down"):
        res[f"ffn_w_{nm}"] = [jnp.stack([ffn0[nm][i], ffn1[nm][i]]) for i in range(4)]
    small_all = _exchange_wait(*small_st[:5], [res[k][0] for k in ("pool_w_in", "pool_w_group", "pool_w_out")], "small_wait")[0]

    small_w = jnp.concatenate([norm_mix, norm_ffn, norm_final[None], pool_scale, jnp.zeros((2, D), F32)], axis=0)
    small_m = jnp.concatenate([m_norm_mix, m_norm_ffn, m_norm_final[None], m_pool_scale, jnp.zeros((2, D), F32)], axis=0)
    small_v = jnp.concatenate([v_norm_mix, v_norm_ffn, v_norm_final[None], v_pool_scale, jnp.ones((2, D), F32)], axis=0)
    sg, sd, sm, sv = _adamw(small_all, small_w, small_m, small_v, "adamw_small")
    loss = sg[6, 0]
    res["norm_mix"] = [t[0:2] for t in (sg, sd, sm, sv)]
    res["norm_ffn"] = [t[2:4] for t in (sg, sd, sm, sv)]
    res["norm_final"] = [t[4] for t in (sg, sd, sm, sv)]
    res["pool_scale"] = [t[5:6] for t in (sg, sd, sm, sv)]

    order = ["norm_mix", "norm_ffn", "norm_final", "pool_w_in", "pool_w_group", "pool_scale", "pool_w_out",
             "attn_w_qkv", "attn_w_out", "ffn_w_gate", "ffn_w_up", "ffn_w_down"]
    return (loss, grad_x[None], *[res[k][0] for k in order], *[res[k][1] for k in order],
            *[res[k][2] for k in order], *[res[k][3] for k in order])
```

```python
import math

import jax
import jax.numpy as jnp
from jax import lax
from jax.experimental import pallas as pl
from jax.experimental.pallas import tpu as pltpu

D = 1024
F = 2816
N_DEV = 8
EPS = 1e-6
NEG_INF = -1e30
POOL_WINDOWS = (2, 4, 8, 16)
POOL_HALO = 16
POOL_GC = 256
HEAD_DIM = 64
HEAD_GROUPS = (6, 5, 5)
HEAD_OFFS = (0, 6, 11)
DILATIONS = (1, 4, 16)
ATT_W = 128
GROUP_LANES = 384
LSE_GROUP_LANE = 8
ROPE_THETA = 10000.0
ADAM_LR, ADAM_B1, ADAM_B2, ADAM_EPS, ADAM_WD, ADAM_STEP = 0.001, 0.9, 0.999, 1e-08, 0.01, 10

BF = jnp.bfloat16
F32 = jnp.float32
VMEM_LIMIT = 56 * 1024 * 1024
MESH = pl.DeviceIdType.MESH


def _params(*sem):
    return pltpu.CompilerParams(dimension_semantics=sem, vmem_limit_bytes=VMEM_LIMIT)


def _dot(a, b):
    return jnp.dot(a, b, preferred_element_type=F32)


def _dot_nt(a, b):
    return lax.dot_general(a, b, (((1,), (1,)), ((), ())), preferred_element_type=F32)


def _dot_tn(a, b):
    return lax.dot_general(a, b, (((0,), (0,)), ((), ())), preferred_element_type=F32)


def _rms_fwd(xv, g):
    r = lax.rsqrt(jnp.mean(xv * xv, axis=-1, keepdims=True) + EPS)
    return (xv * r) * g


def _rms_bwd(xv, g, dh):
    r = lax.rsqrt(jnp.mean(xv * xv, axis=-1, keepdims=True) + EPS)
    xhat = xv * r
    dg = jnp.sum(dh * xhat, axis=0, keepdims=True)
    dxh = dh * g
    dx = r * (dxh - xhat * jnp.mean(dxh * xhat, axis=-1, keepdims=True))
    return dx, dg


def _lane_col(tile, j):
    lane = lax.broadcasted_iota(jnp.int32, tile.shape, 1)
    return jnp.sum(jnp.where(lane == j, tile, 0.0), axis=-1, keepdims=True)


def _row_spec(tm, n):
    return pl.BlockSpec((tm, n), lambda i: (i, 0))


def _full_spec(shape):
    nd = len(shape)
    return pl.BlockSpec(shape, lambda *_: (0,) * nd)


def _after(dep, body, in_specs, args):
    if dep is None:
        return body, list(in_specs), list(args)

    def body_after(dep_ref, *refs):
        body(*refs)

    return body_after, [pl.BlockSpec(memory_space=pl.ANY)] + list(in_specs), [dep] + list(args)


def _pool_fwd(x, g, w_in, w_grp, scale, w_out, gather, tm=512, dep=None):
    T = x.shape[0]
    nt = T // tm
    n = tm + POOL_HALO
    ng = len(gather)

    def body(x_ref, g_ref, win_ref, wg_ref, sc_ref, wout_ref, *refs):
        shard_refs, (x1_ref, p_ref), full_refs = refs[:ng], refs[ng:ng + 2], refs[ng + 2:2 * ng + 2]
        tail_ref, z_ref = refs[2 * ng + 2:2 * ng + 4]
        begin, finish = _gather_phases(shard_refs, full_refs, *refs[2 * ng + 4:])
        i = pl.program_id(0)

        @pl.when(i == 0)
        def _():
            begin()
            tail_ref[...] = jnp.zeros_like(tail_ref)

        u = _dot(_rms_fwd(x_ref[...], g_ref[...]).astype(BF), win_ref[...])
        pos = i * tm + lax.broadcasted_iota(jnp.int32, (tm, 1), 0)
        for g, w in enumerate(POOL_WINDOWS):
            sl = slice(g * POOL_GC, (g + 1) * POOL_GC)
            ug = u[:, sl]
            s = jnp.concatenate([tail_ref[:, sl], ug], axis=0)
            step = 1
            while step < w:
                s = s + pltpu.roll(s, step, 0)
                step *= 2
            cnt = jnp.minimum(pos + 1, w).astype(F32)
            pg = (s[POOL_HALO:, :] / cnt - ug).astype(BF)
            p_ref[:, sl] = pg
            z_ref[:, sl] = (_dot(pg, wg_ref[g]) * sc_ref[:, sl]).astype(BF)
        tail_ref[...] = u[tm - POOL_HALO:, :]
        x1_ref[...] = x_ref[...] + _dot(z_ref[...], wout_ref[...])

        @pl.when(i == nt - 1)
        def _():
            finish()

    hbm = pl.BlockSpec(memory_space=pl.ANY)
    body, in_specs, args = _after(
        dep, body,
        [_row_spec(tm, D), _full_spec((1, D)), _full_spec((D, D)), _full_spec((4, POOL_GC, POOL_GC)), _full_spec((1, D)),
         _full_spec((D, D))] + [hbm] * ng,
        [x, g, w_in, w_grp, scale, w_out, *gather])
    return pl.pallas_call(
        body, name="pool_fwd", grid=(nt,),
        in_specs=in_specs,
        out_specs=[_row_spec(tm, D), _row_spec(tm, D)] + [hbm] * ng,
        out_shape=[jax.ShapeDtypeStruct((T, D), F32), jax.ShapeDtypeStruct((T, D), BF)]
        + [jax.ShapeDtypeStruct((N_DEV,) + a.shape, a.dtype) for a in gather],
        scratch_shapes=[pltpu.VMEM((POOL_HALO, D), F32), pltpu.VMEM((tm, D), BF)] + _gather_scratch(ng),
        compiler_params=_params("arbitrary"),
    )(*args)


def _pool_bwd(dx1, x0, g0, p, w_in, w_grp, scale, w_out, tm=512, dep=None):
    T = x0.shape[0]
    nt = T // tm
    n = tm + POOL_HALO
    rev = lambda i: (nt - 1 - i, 0)

    def body(dx1_ref, x0_ref, g_ref, p_ref, win_ref, wg_ref, sc_ref, wout_ref,
             dx0_ref, z_ref, dzp_ref, du_ref, h0_ref, dsc_ref, dg_ref, head_ref):
        i = pl.program_id(0)

        @pl.when(i == 0)
        def _():
            head_ref[...] = jnp.zeros_like(head_ref)
            dsc_ref[...] = jnp.zeros_like(dsc_ref)
            dg_ref[...] = jnp.zeros_like(dg_ref)

        dx1v = dx1_ref[...]
        dz = _dot_nt(dx1v.astype(BF), wout_ref[...])
        pos = (nt - 1 - i) * tm + lax.broadcasted_iota(jnp.int32, (tm, 1), 0)
        for g, w in enumerate(POOL_WINDOWS):
            sl = slice(g * POOL_GC, (g + 1) * POOL_GC)
            zpre = _dot(p_ref[:, sl], wg_ref[g])
            dzg = dz[:, sl]
            dsc_ref[:, sl] += jnp.sum(dzg * zpre, axis=0, keepdims=True)
            z_ref[:, sl] = (zpre * sc_ref[:, sl]).astype(BF)
            dzp = (dzg * sc_ref[:, sl]).astype(BF)
            dzp_ref[:, sl] = dzp
            dp = _dot_nt(dzp, wg_ref[g])
            cnt = jnp.minimum(pos + 1, w).astype(F32)
            dpc = dp / cnt
            s = jnp.concatenate([dpc, head_ref[:, sl]], axis=0)
            step = 1
            while step < w:
                s = s + pltpu.roll(s, n - step, 0)
                step *= 2
            head_ref[:, sl] = dpc[:POOL_HALO, :]
            du_ref[:, sl] = (s[:tm, :] - dp).astype(BF)
        dh0 = _dot_nt(du_ref[...], win_ref[...])
        x0v = x0_ref[...]
        h0_ref[...] = _rms_fwd(x0v, g_ref[...]).astype(BF)
        dx, dg = _rms_bwd(x0v, g_ref[...], dh0)
        dx0_ref[...] = dx1v + dx
        dg_ref[...] += dg

    bf_rows = jax.ShapeDtypeStruct((T, D), BF)
    vec = jax.ShapeDtypeStruct((1, D), F32)
    body, in_specs, args = _after(
        dep, body,
        [pl.BlockSpec((tm, D), rev), pl.BlockSpec((tm, D), rev), _full_spec((1, D)), pl.BlockSpec((tm, D), rev),
         _full_spec((D, D)), _full_spec((4, POOL_GC, POOL_GC)), _full_spec((1, D)), _full_spec((D, D))],
        [dx1, x0, g0, p, w_in, w_grp, scale, w_out])
    return pl.pallas_call(
        body, name="pool_bwd", grid=(nt,),
        in_specs=in_specs,
        out_specs=[pl.BlockSpec((tm, D), rev)] * 5 + [_full_spec((1, D))] * 2,
        out_shape=[jax.ShapeDtypeStruct((T, D), F32), bf_rows, bf_rows, bf_rows, bf_rows, vec, vec],
        scratch_shapes=[pltpu.VMEM((POOL_HALO, D), F32)],
        compiler_params=_params("arbitrary"),
    )(*args)


def _loss_head(xv, tv, gv):
    diff = _rms_fwd(xv, gv) - tv
    loss = 0.5 * jnp.sum(jnp.mean(diff * diff, axis=-1, keepdims=True), axis=0, keepdims=True)
    dx, dg = _rms_bwd(xv, gv, diff * (1.0 / D))
    return loss, dx, dg


def _ffn_fwd(x, g, wg_t, wu_t, wd, name, gather=(), head=None, tm=512, fk=1408):
    T = x.shape[0]
    ng = len(gather)
    nh_in, nh_out = (2, 2) if head is not None else (0, 0)
    ni, nk = T // tm, F // fk

    def body(x_ref, g_ref, wg_ref, wu_ref, wd_ref, *refs):
        head_in, refs = refs[:nh_in], refs[nh_in:]
        shard_refs, (xo_ref, a_ref, b_ref, s_ref), full_refs = refs[:ng], refs[ng:ng + 4], refs[ng + 4:2 * ng + 4]
        refs = refs[2 * ng + 4:]
        head_out, refs = refs[:nh_out], refs[nh_out:]
        acc_ref, h_ref = refs[:2]
        begin, finish = _gather_phases(shard_refs, full_refs, *refs[2:])
        i = pl.program_id(0)
        k = pl.program_id(1)

        @pl.when(jnp.logical_and(i == 0, k == 0))
        def _():
            begin()
            for r in head_out:
                r[...] = jnp.zeros_like(r)

        @pl.when(k == 0)
        def _():
            acc_ref[...] = jnp.zeros_like(acc_ref)
            h_ref[...] = _rms_fwd(x_ref[...], g_ref[...]).astype(BF)

        hv = h_ref[...]
        a = _dot_nt(hv, wg_ref[...])
        b = _dot_nt(hv, wu_ref[...])
        s = ((a * jax.nn.sigmoid(a)) * b).astype(BF)
        a_ref[...] = a.astype(BF)
        b_ref[...] = b.astype(BF)
        s_ref[...] = s
        acc_ref[...] += _dot(s, wd_ref[...])

        @pl.when(k == nk - 1)
        def _():
            xo = x_ref[...] + acc_ref[...]
            if head is None:
                xo_ref[...] = xo
            else:
                loss, dx, dg = _loss_head(xo, head_in[0][...], head_in[1][...])
                xo_ref[...] = dx
                head_out[0][...] += loss
                head_out[1][...] += dg

        @pl.when(jnp.logical_and(i == ni - 1, k == nk - 1))
        def _():
            finish()

    row = pl.BlockSpec((tm, D), lambda i, k: (i, 0))
    wsp = pl.BlockSpec((fk, D), lambda i, k: (k, 0))
    act = pl.BlockSpec((tm, fk), lambda i, k: (i, k))
    hbm = pl.BlockSpec(memory_space=pl.ANY)
    act_shape = jax.ShapeDtypeStruct((T, F), BF)
    vec = pl.BlockSpec((1, D), lambda i, k: (0, 0))
    one = pl.BlockSpec((1, 1), lambda i, k: (0, 0))
    return pl.pallas_call(
        body, name=name, grid=(ni, nk),
        in_specs=[row, vec, wsp, wsp, wsp] + [row, vec][:nh_in] + [hbm] * ng,
        out_specs=[row, act, act, act] + [hbm] * ng + [one, vec][:nh_out],
        out_shape=[jax.ShapeDtypeStruct((T, D), F32), act_shape, act_shape, act_shape]
        + [jax.ShapeDtypeStruct((N_DEV,) + a.shape, a.dtype) for a in gather]
        + [jax.ShapeDtypeStruct((1, 1), F32), jax.ShapeDtypeStruct((1, D), F32)][:nh_out],
        scratch_shapes=[pltpu.VMEM((tm, D), F32), pltpu.VMEM((tm, D), BF)] + _gather_scratch(ng),
        compiler_params=_params("arbitrary", "arbitrary"),
    )(x, g, wg_t, wu_t, wd, *(head or ()), *gather)


def _ffn_bwd(dxo, x_in, g, a, b, wg_t, wu_t, wd, name, tm=512, fk=1408, dep=None):
    T = x_in.shape[0]

    def body(dxo_ref, x_ref, g_ref, a_ref, b_ref, wg_ref, wu_ref, wd_ref,
             dx_ref, da_ref, db_ref, h_ref, dy_ref, dg_ref, dh_ref):
        i = pl.program_id(0)
        k = pl.program_id(1)

        @pl.when(jnp.logical_and(i == 0, k == 0))
        def _():
            dg_ref[...] = jnp.zeros_like(dg_ref)

        @pl.when(k == 0)
        def _():
            h_ref[...] = _rms_fwd(x_ref[...], g_ref[...]).astype(BF)
            dy_ref[...] = dxo_ref[...].astype(BF)
            dh_ref[...] = jnp.zeros_like(dh_ref)

        ds = _dot_nt(dy_ref[...], wd_ref[...])
        av = a_ref[...].astype(F32)
        bv = b_ref[...].astype(F32)
        sig = jax.nn.sigmoid(av)
        db = (ds * (av * sig)).astype(BF)
        da = (ds * bv * (sig * (1.0 + av * (1.0 - sig)))).astype(BF)
        da_ref[...] = da
        db_ref[...] = db
        dh_ref[...] += _dot(da, wg_ref[...]) + _dot(db, wu_ref[...])

        @pl.when(k == pl.num_programs(1) - 1)
        def _():
            dx, dg = _rms_bwd(x_ref[...], g_ref[...], dh_ref[...])
            dx_ref[...] = dxo_ref[...] + dx
            dg_ref[...] += dg

    row = pl.BlockSpec((tm, D), lambda i, k: (i, 0))
    wsp = pl.BlockSpec((fk, D), lambda i, k: (k, 0))
    act = pl.BlockSpec((tm, fk), lambda i, k: (i, k))
    vec = pl.BlockSpec((1, D), lambda i, k: (0, 0))
    act_shape = jax.ShapeDtypeStruct((T, F), BF)
    body, in_specs, args = _after(dep, body, [row, row, vec, act, act, wsp, wsp, wsp], [dxo, x_in, g, a, b, wg_t, wu_t, wd])
    return pl.pallas_call(
        body, name=name, grid=(T // tm, F // fk),
        in_specs=in_specs,
        out_specs=[row, act, act, row, row, vec],
        out_shape=[jax.ShapeDtypeStruct((T, D), F32), act_shape, act_shape, jax.ShapeDtypeStruct((T, D), BF),
                   jax.ShapeDtypeStruct((T, D), BF), jax.ShapeDtypeStruct((1, D), F32)],
        scratch_shapes=[pltpu.VMEM((tm, D), F32)],
        compiler_params=_params("arbitrary", "arbitrary"),
    )(*args)


def _wgrad(a, b, name, tk=1024, mblk=None):
    T, M = a.shape
    N = b.shape[1]
    mblk = M if mblk is None else mblk

    def body(a_ref, b_ref, o_ref, acc_ref):
        t = pl.program_id(1)

        @pl.when(t == 0)
        def _():
            acc_ref[...] = jnp.zeros_like(acc_ref)

        acc_ref[...] += _dot_tn(a_ref[...].astype(BF), b_ref[...].astype(BF))

        @pl.when(t == pl.num_programs(1) - 1)
        def _():
            o_ref[...] = acc_ref[...].astype(BF)

    return pl.pallas_call(
        body, name=name, grid=(M // mblk, T // tk),
        in_specs=[pl.BlockSpec((tk, mblk), lambda m, t: (t, m)), pl.BlockSpec((tk, N), lambda m, t: (t, 0))],
        out_specs=pl.BlockSpec((mblk, N), lambda m, t: (m, 0)),
        out_shape=jax.ShapeDtypeStruct((M, N), BF),
        scratch_shapes=[pltpu.VMEM((mblk, N), F32)],
        compiler_params=_params("parallel", "arbitrary"),
    )(a, b)


def _wgrad_multi(a_list, b, name, tk=1024):
    T, M = a_list[0].shape
    N = b.shape[1]
    n = len(a_list)

    def body(*refs):
        a_refs, b_ref, o_refs, acc_refs = refs[:n], refs[n], refs[n + 1:2 * n + 1], refs[2 * n + 1:]
        t = pl.program_id(0)

        @pl.when(t == 0)
        def _():
            for acc in acc_refs:
                acc[...] = jnp.zeros_like(acc)

        bv = b_ref[...].astype(BF)
        for a_ref, acc in zip(a_refs, acc_refs):
            acc[...] += _dot_tn(a_ref[...], bv)

        @pl.when(t == pl.num_programs(0) - 1)
        def _():
            for o_ref, acc in zip(o_refs, acc_refs):
                o_ref[...] = acc[...].astype(BF)

    return pl.pallas_call(
        body, name=name, grid=(T // tk,),
        in_specs=[_row_spec(tk, M)] * n + [_row_spec(tk, N)],
        out_specs=[_full_spec((M, N))] * n,
        out_shape=[jax.ShapeDtypeStruct((M, N), BF)] * n,
        scratch_shapes=[pltpu.VMEM((M, N), F32)] * n,
        compiler_params=_params("arbitrary"),
    )(*a_list, b)


def _wgrad_pool_groups(p, dzp, tk=2048):
    T = p.shape[0]

    def body(p_ref, d_ref, o_ref, acc_ref):
        t = pl.program_id(1)

        @pl.when(t == 0)
        def _():
            acc_ref[...] = jnp.zeros_like(acc_ref)

        acc_ref[...] += _dot_tn(p_ref[...], d_ref[...])

        @pl.when(t == pl.num_programs(1) - 1)
        def _():
            o_ref[...] = acc_ref[...].astype(BF)

    blk = pl.BlockSpec((tk, POOL_GC), lambda g, t: (t, g))
    return pl.pallas_call(
        body, name="pool_wgrad_groups", grid=(4, T // tk),
        in_specs=[blk, blk],
        out_specs=pl.BlockSpec((None, POOL_GC, POOL_GC), lambda g, t: (g, 0, 0)),
        out_shape=jax.ShapeDtypeStruct((4, POOL_GC, POOL_GC), BF),
        scratch_shapes=[pltpu.VMEM((POOL_GC, POOL_GC), F32)],
        compiler_params=_params("parallel", "arbitrary"),
    )(p, dzp)


def _wgrad_stack(a, b3, name, tk=1024):
    T, M = a.shape
    N = b3.shape[2]

    def body(a_ref, b_ref, o_ref, acc_ref):
        t = pl.program_id(0)

        @pl.when(t == 0)
        def _():
            acc_ref[...] = jnp.zeros_like(acc_ref)

        av = a_ref[...]
        for w in range(3):
            acc_ref[w] += _dot_tn(av, b_ref[w])

        @pl.when(t == pl.num_programs(0) - 1)
        def _():
            o_ref[...] = acc_ref[...].astype(BF)

    return pl.pallas_call(
        body, name=name, grid=(T // tk,),
        in_specs=[_row_spec(tk, M), pl.BlockSpec((3, tk, N), lambda t: (0, t, 0))],
        out_specs=_full_spec((3, M, N)),
        out_shape=jax.ShapeDtypeStruct((3, M, N), BF),
        scratch_shapes=[pltpu.VMEM((3, M, N), F32)],
        compiler_params=_params("arbitrary"),
    )(a, b3)


def _rot_half(t):
    lane = lax.broadcasted_iota(jnp.int32, t.shape, 1)
    first = (lane % HEAD_DIM) < (HEAD_DIM // 2)
    return jnp.where(first, -pltpu.roll(t, 128 - HEAD_DIM // 2, 1), pltpu.roll(t, HEAD_DIM // 2, 1))


def _scatter_rows(dst_ref, scr_ref, d, cast):
    nc, rows, _ = scr_ref.shape
    n = rows // d
    for c in range(nc):
        sl = slice(c * 128, (c + 1) * 128)
        for r in range(d):
            src = scr_ref[c] if d == 1 else scr_ref.at[c][pl.ds(r, n, stride=d), :]
            dst_ref[r, :, sl] = src.astype(cast)


def _gather_rows(scr_ref, src_ref, d):
    nc, rows, _ = scr_ref.shape
    n = rows // d
    for c in range(nc):
        sl = slice(c * 128, (c + 1) * 128)
        for r in range(d):
            val = src_ref[r, :, sl].astype(F32)
            if d == 1:
                scr_ref[c] = val
            else:
                scr_ref.at[c][pl.ds(r, n, stride=d), :] = val


def _chunks_to_rows(scr_ref):
    nc = scr_ref.shape[0]
    return scr_ref[0] if nc == 1 else jnp.concatenate([scr_ref[c] for c in range(nc)], axis=1)


def _rows_to_chunks(scr_ref, val):
    for c in range(scr_ref.shape[0]):
        scr_ref[c] = val[:, c * 128:(c + 1) * 128]


def _rope(t, cv, sv, scale):
    return (t * cv + _rot_half(t) * sv) * scale


def _qkv_fwd(x, g, w3s, cos, sin, tm=512):
    T = x.shape[0]
    C = GROUP_LANES
    nc = C // 128

    def body(x_ref, g_ref, w0, w1, w2, cos_ref, sin_ref, o0, o1, o2, scr_ref):
        h = _rms_fwd(x_ref[...], g_ref[...]).astype(BF)
        cv = cos_ref[...]
        sv = sin_ref[...]
        for gi, (w_ref, o_ref, d) in enumerate(zip((w0, w1, w2), (o0, o1, o2), DILATIONS)):
            for w in range(3):
                t = _dot(h, w_ref[w])
                chunks = [t[:, c * 128:(c + 1) * 128] for c in range(nc)]
                if w < 2:
                    chunks = [_rope(tc, cv, sv, HEAD_DIM ** -0.5 if w == 0 else 1.0) for tc in chunks]
                if d == 1:
                    for c in range(nc):
                        o_ref[w, 0, :, c * 128:(c + 1) * 128] = chunks[c].astype(BF)
                else:
                    scr = scr_ref.at[gi * 3 + w]
                    for c in range(nc):
                        scr[c] = chunks[c]
                    _scatter_rows(o_ref.at[w], scr, d, BF)

    return pl.pallas_call(
        body, name="qkv_fwd", grid=(T // tm,),
        in_specs=[_row_spec(tm, D), _full_spec((1, D))] + [_full_spec((3, D, C))] * 3 + [_row_spec(tm, 128), _row_spec(tm, 128)],
        out_specs=[pl.BlockSpec((3, d, tm // d, C), lambda i: (0, 0, i, 0)) for d in DILATIONS],
        out_shape=[jax.ShapeDtypeStruct((3, d, T // d, C), BF) for d in DILATIONS],
        scratch_shapes=[pltpu.VMEM((9, nc, tm, 128), F32)],
        compiler_params=_params("parallel"),
    )(x, g, *w3s, cos, sin)


def _att_chunk(L):
    return min(L, 1024)


def _head_mask(shape, h):
    lane = lax.broadcasted_iota(jnp.int32, shape, 1)
    return (lane // HEAD_DIM) == (h % 2)


def _attn_fwd(qkv, nh, name):
    _, d, L, C = qkv.shape
    lc = _att_chunk(L)
    nblk = lc // ATT_W

    def body(q_ref, k_ref, kh_ref, v_ref, vh_ref, o_ref, st_ref):
        i = pl.program_id(1)
        qi = lax.broadcasted_iota(jnp.int32, (ATT_W, 2 * ATT_W), 0)
        kj = lax.broadcasted_iota(jnp.int32, (ATT_W, 2 * ATT_W), 1)
        band = jnp.logical_and(kj >= qi, kj <= qi + ATT_W)
        lane = lax.broadcasted_iota(jnp.int32, (ATT_W, 128), 1)

        def block(row0, kc, vc, mask):
            rows = pl.ds(row0, ATT_W)
            lses = []
            for hp in range(C // 128):
                sl = slice(hp * 128, (hp + 1) * 128)
                qp = q_ref[rows, sl]
                kp = kc[:, sl]
                vp = vc[:, sl]
                outs = []
                for h in range(2 * hp, min(2 * hp + 2, nh)):
                    hm = _head_mask(qp.shape, h)
                    s = _dot_nt(jnp.where(hm, qp, jnp.zeros_like(qp)), kp)
                    s = jnp.where(mask, s, NEG_INF)
                    m = jnp.max(s, axis=-1, keepdims=True)
                    e = jnp.exp(s - m)
                    den = jnp.sum(e, axis=-1, keepdims=True)
                    p = (e * pl.reciprocal(den)).astype(BF)
                    outs.append(_dot(p, vp))
                    lses.append(m + jnp.log(den))
                if len(outs) == 2:
                    o = jnp.where(_head_mask(outs[0].shape, 0), outs[0], outs[1])
                else:
                    o = jnp.where(_head_mask(outs[0].shape, 0), outs[0], 0.0)
                o_ref[rows, sl] = o.astype(BF)
            mm = lses[0]
            for l in lses[1:]:
                mm = jnp.maximum(mm, l)
            tot = jnp.exp(lses[0] - mm)
            for l in lses[1:]:
                tot = tot + jnp.exp(l - mm)
            tile = jnp.where(lane == LSE_GROUP_LANE, mm + jnp.log(tot) - math.log(nh), 0.0)
            for h, l in enumerate(lses):
                tile = jnp.where(lane == h, l, tile)
            st_ref[rows, :] = tile

        first_mask = jnp.logical_and(band, jnp.logical_or(kj >= ATT_W, i > 0))
        block(0, jnp.concatenate([kh_ref[...], k_ref[pl.ds(0, ATT_W), :]], axis=0),
              jnp.concatenate([vh_ref[...], v_ref[pl.ds(0, ATT_W), :]], axis=0), first_mask)

        if nblk > 1:
            def step(blk, carry):
                prev = pl.ds(pl.multiple_of((blk - 1) * ATT_W, ATT_W), 2 * ATT_W)
                block(pl.multiple_of(blk * ATT_W, ATT_W), k_ref[prev, :], v_ref[prev, :], band)
                return carry
            lax.fori_loop(1, nblk, step, 0, unroll=True)

    main = lambda w: pl.BlockSpec((None, None, lc, C), lambda r, i: (w, r, i, 0))
    halo = lambda w: pl.BlockSpec((None, None, ATT_W, C), lambda r, i: (w, r, jnp.maximum(i * nblk - 1, 0), 0))
    return pl.pallas_call(
        body, name=name, grid=(d, L // lc),
        in_specs=[main(0), main(1), halo(1), main(2), halo(2)],
        out_specs=[pl.BlockSpec((None, lc, C), lambda r, i: (r, i, 0)), pl.BlockSpec((None, lc, 128), lambda r, i: (r, i, 0))],
        out_shape=[jax.ShapeDtypeStruct((d, L, C), BF), jax.ShapeDtypeStruct((d, L, 128), F32)],
        compiler_params=_params("parallel", "arbitrary"),
    )(qkv, qkv, qkv, qkv, qkv)


def _attn_bwd(qkv, do, st, dst, nh, name):
    _, d, L, C = qkv.shape
    lc = _att_chunk(L)
    nblk = lc // ATT_W
    nchunk = L // lc

    def body(q_ref, qn_ref, k_ref, kh_ref, v_ref, vh_ref, do_ref, don_ref, st_ref, stn_ref, ds_ref, dsn_ref, o_ref,
             p_scr, dsc_scr):
        i = pl.program_id(1)
        qi = lax.broadcasted_iota(jnp.int32, (ATT_W, 2 * ATT_W), 0)
        kj = lax.broadcasted_iota(jnp.int32, (ATT_W, 2 * ATT_W), 1)
        band_q = jnp.logical_and(kj >= qi, kj <= qi + ATT_W)
        qa = lax.broadcasted_iota(jnp.int32, (2 * ATT_W, ATT_W), 0)
        kb = lax.broadcasted_iota(jnp.int32, (2 * ATT_W, ATT_W), 1)
        band_k = jnp.logical_and(qa >= kb, qa <= kb + ATT_W)

        def probs(qm, kp, lse, mask):
            s = _dot_nt(qm, kp)
            return jnp.where(mask, jnp.exp(s - lse), 0.0)

        def q_block(blk, kc, vc, mask):
            rows = pl.ds(blk * ATT_W, ATT_W)
            stv = st_ref[rows, :]
            dsv = ds_ref[rows, :]
            for hp in range(C // 128):
                sl = slice(hp * 128, (hp + 1) * 128)
                qp = q_ref[rows, sl]
                dop = do_ref[rows, sl]
                kp = kc[:, sl]
                vp = vc[:, sl]
                outs = []
                for h in range(2 * hp, min(2 * hp + 2, nh)):
                    hm = _head_mask(qp.shape, h)
                    p = probs(jnp.where(hm, qp, jnp.zeros_like(qp)), kp, _lane_col(stv, h), mask)
                    dp = _dot_nt(jnp.where(hm, dop, jnp.zeros_like(dop)), vp)
                    dsc = (p * (dp - _lane_col(dsv, h))).astype(BF)
                    p_scr[blk, h] = p.astype(BF)
                    dsc_scr[blk, h] = dsc
                    outs.append(_dot(dsc, kp))
                if len(outs) == 2:
                    dq = jnp.where(_head_mask(outs[0].shape, 0), outs[0], outs[1])
                else:
                    dq = jnp.where(_head_mask(outs[0].shape, 0), outs[0], 0.0)
                o_ref[0, rows, sl] = dq.astype(BF)

        def k_block_kept(m):
            rows = pl.ds(m * ATT_W, ATT_W)
            two = pl.ds(m * ATT_W, 2 * ATT_W)
            for hp in range(C // 128):
                sl = slice(hp * 128, (hp + 1) * 128)
                qp = q_ref[two, sl]
                dop = do_ref[two, sl]
                dk, dv = None, None
                for h in range(2 * hp, min(2 * hp + 2, nh)):
                    hm = _head_mask(qp.shape, h)
                    qm = jnp.where(hm, qp, jnp.zeros_like(qp))
                    dom = jnp.where(hm, dop, jnp.zeros_like(dop))
                    dsc = jnp.concatenate([dsc_scr[m, h, :, ATT_W:], dsc_scr[m + 1, h, :, :ATT_W]], axis=0)
                    p = jnp.concatenate([p_scr[m, h, :, ATT_W:], p_scr[m + 1, h, :, :ATT_W]], axis=0)
                    dk_h = _dot_tn(dsc, qm)
                    dv_h = _dot_tn(p, dom)
                    dk = dk_h if dk is None else dk + dk_h
                    dv = dv_h if dv is None else dv + dv_h
                o_ref[1, rows, sl] = dk.astype(BF)
                o_ref[2, rows, sl] = dv.astype(BF)

        def k_block(row0, qq, doo, stv, dsv, mask):
            rows = pl.ds(row0, ATT_W)
            for hp in range(C // 128):
                sl = slice(hp * 128, (hp + 1) * 128)
                qp = qq[:, sl]
                dop = doo[:, sl]
                kp = k_ref[rows, sl]
                vp = v_ref[rows, sl]
                dks, dvs = [], []
                for h in range(2 * hp, min(2 * hp + 2, nh)):
                    hm = _head_mask(qp.shape, h)
                    qm = jnp.where(hm, qp, jnp.zeros_like(qp))
                    dom = jnp.where(hm, dop, jnp.zeros_like(dop))
                    p = probs(qm, kp, _lane_col(stv, h), mask)
                    dp = _dot_nt(dom, vp)
                    dsc = (p * (dp - _lane_col(dsv, h))).astype(BF)
                    dks.append(_dot_tn(dsc, qm))
                    dvs.append(_dot_tn(p.astype(BF), dom))
                o_ref[1, rows, sl] = sum(dks[1:], dks[0]).astype(BF)
                o_ref[2, rows, sl] = sum(dvs[1:], dvs[0]).astype(BF)

        first_mask = jnp.logical_and(band_q, jnp.logical_or(kj >= ATT_W, i > 0))
        q_block(0, jnp.concatenate([kh_ref[...], k_ref[pl.ds(0, ATT_W), :]], axis=0),
                jnp.concatenate([vh_ref[...], v_ref[pl.ds(0, ATT_W), :]], axis=0), first_mask)
        for blk in range(1, nblk):
            prev = pl.ds((blk - 1) * ATT_W, 2 * ATT_W)
            q_block(blk, k_ref[prev, :], v_ref[prev, :], band_q)
        for m in range(nblk - 1):
            k_block_kept(m)

        last = pl.ds((nblk - 1) * ATT_W, ATT_W)
        last_mask = jnp.logical_and(band_k, jnp.logical_or(qa < ATT_W, i < nchunk - 1))
        k_block((nblk - 1) * ATT_W,
                jnp.concatenate([q_ref[last, :], qn_ref[...]], axis=0),
                jnp.concatenate([do_ref[last, :], don_ref[...]], axis=0),
                jnp.concatenate([st_ref[last, :], stn_ref[...]], axis=0),
                jnp.concatenate([ds_ref[last, :], dsn_ref[...]], axis=0), last_mask)

    nb_all = L // ATT_W
    main4 = lambda w: pl.BlockSpec((None, None, lc, C), lambda r, i: (w, r, i, 0))
    prev4 = lambda w: pl.BlockSpec((None, None, ATT_W, C), lambda r, i: (w, r, jnp.maximum(i * nblk - 1, 0), 0))
    next4 = lambda w: pl.BlockSpec((None, None, ATT_W, C), lambda r, i: (w, r, jnp.minimum((i + 1) * nblk, nb_all - 1), 0))
    main3 = lambda n: pl.BlockSpec((None, lc, n), lambda r, i: (r, i, 0))
    next3 = lambda n: pl.BlockSpec((None, ATT_W, n), lambda r, i: (r, jnp.minimum((i + 1) * nblk, nb_all - 1), 0))
    return pl.pallas_call(
        body, name=name, grid=(d, nchunk),
        in_specs=[main4(0), next4(0), main4(1), prev4(1), main4(2), prev4(2),
                  main3(C), next3(C), main3(128), next3(128), main3(128), next3(128)],
        out_specs=pl.BlockSpec((3, None, lc, C), lambda r, i: (0, r, i, 0)),
        out_shape=jax.ShapeDtypeStruct((3, d, L, C), BF),
        scratch_shapes=[pltpu.VMEM((nblk, nh, ATT_W, 2 * ATT_W), BF)] * 2,
        compiler_params=_params("parallel", "arbitrary"),
    )(qkv, qkv, qkv, qkv, qkv, qkv, do, do, st, st, dst, dst)


def _alpha_from(lse_nat):
    m = jnp.maximum(jnp.maximum(lse_nat[0], lse_nat[1]), lse_nat[2])
    e = [jnp.exp(l - m) for l in lse_nat]
    inv = 1.0 / (e[0] + e[1] + e[2])
    return [ei * inv for ei in e]


def _attn_out_fwd(x, os_, sts, wos, tm=512):
    T = x.shape[0]
    C = GROUP_LANES

    def body(x_ref, o0, o1, o2, s0, s1, s2, w0, w1, w2, xo_ref, m0, m1, m2, al_ref, oscr, sscr):
        o_refs, st_refs, w_refs, m_refs = (o0, o1, o2), (s0, s1, s2), (w0, w1, w2), (m0, m1, m2)
        lses = []
        for g, d in enumerate(DILATIONS):
            _gather_rows(sscr.at[g], st_refs[g], d)
            lses.append(_lane_col(sscr[g, 0], LSE_GROUP_LANE))
        alpha = _alpha_from(lses)
        y = x_ref[...]
        for g, d in enumerate(DILATIONS):
            _gather_rows(oscr, o_refs[g], d)
            mg = (_chunks_to_rows(oscr) * (3.0 * alpha[g])).astype(BF)
            m_refs[g][...] = mg
            y = y + _dot(mg, w_refs[g][...])
        xo_ref[...] = y
        lane = lax.broadcasted_iota(jnp.int32, (tm, 128), 1)
        al_ref[...] = jnp.where(lane == 0, alpha[0], jnp.where(lane == 1, alpha[1], jnp.where(lane == 2, alpha[2], 0.0)))

    o_specs = [pl.BlockSpec((d, tm // d, C), lambda i: (0, i, 0)) for d in DILATIONS]
    st_specs = [pl.BlockSpec((d, tm // d, 128), lambda i: (0, i, 0)) for d in DILATIONS]
    mshape = jax.ShapeDtypeStruct((T, C), BF)
    return pl.pallas_call(
        body, name="attn_out_fwd", grid=(T // tm,),
        in_specs=[_row_spec(tm, D)] + o_specs + st_specs + [_full_spec((C, D))] * 3,
        out_specs=[_row_spec(tm, D), _row_spec(tm, C), _row_spec(tm, C), _row_spec(tm, C), _row_spec(tm, 128)],
        out_shape=[jax.ShapeDtypeStruct((T, D), F32), mshape, mshape, mshape, jax.ShapeDtypeStruct((T, 128), F32)],
        scratch_shapes=[pltpu.VMEM((C // 128, tm, 128), F32), pltpu.VMEM((3, 1, tm, 128), F32)],
        compiler_params=_params("parallel"),
    )(x, *os_, *sts, *wos)


def _attn_out_bwd(dx, os_, sts, alpha, wos, tm=512, dep=None):
    T = dx.shape[0]
    C = GROUP_LANES

    def body(dx_ref, o0, o1, o2, s0, s1, s2, al_ref, w0, w1, w2, do0, do1, do2, ds0, ds1, ds2, dmscr, oscr, sscr, tscr):
        o_refs, st_refs, w_refs = (o0, o1, o2), (s0, s1, s2), (w0, w1, w2)
        do_refs, ds_refs = (do0, do1, do2), (ds0, ds1, ds2)
        dyb = dx_ref[...].astype(BF)
        alv = al_ref[...]
        alpha_g = [_lane_col(alv, g) for g in range(3)]
        dalpha = []
        for g, d in enumerate(DILATIONS):
            dm = _dot_nt(dyb, w_refs[g][...])
            _rows_to_chunks(dmscr.at[g], dm)
            _gather_rows(oscr.at[g], o_refs[g], d)
            _gather_rows(sscr.at[g], st_refs[g], d)
            dalpha.append(3.0 * jnp.sum(dm * _chunks_to_rows(oscr.at[g]), axis=-1, keepdims=True))
        mean_da = alpha_g[0] * dalpha[0] + alpha_g[1] * dalpha[1] + alpha_g[2] * dalpha[2]
        lane = lax.broadcasted_iota(jnp.int32, (tm, 128), 1)
        seg = (lax.broadcasted_iota(jnp.int32, (C, 128), 0) // HEAD_DIM == lax.broadcasted_iota(jnp.int32, (C, 128), 1)).astype(BF)
        for g, d in enumerate(DILATIONS):
            nh = HEAD_GROUPS[g]
            dlse_g = alpha_g[g] * (dalpha[g] - mean_da)
            do_nat = _chunks_to_rows(dmscr.at[g]) * (3.0 * alpha_g[g])
            prod = do_nat * _chunks_to_rows(oscr.at[g])
            hi = prod.astype(BF)
            delta = _dot(hi, seg) + _dot((prod - hi.astype(F32)).astype(BF), seg)
            stv = sscr[g, 0]
            dlse = dlse_g * jnp.exp(stv - _lane_col(stv, LSE_GROUP_LANE)) * (1.0 / nh)
            tile = jnp.where(lane < nh, delta - dlse, 0.0)
            _rows_to_chunks(dmscr.at[g], do_nat)
            _scatter_rows(do_refs[g], dmscr.at[g], d, BF)
            tscr[0] = tile
            _scatter_rows(ds_refs[g], tscr, d, F32)

    o_specs = [pl.BlockSpec((d, tm // d, C), lambda i: (0, i, 0)) for d in DILATIONS]
    st_specs = [pl.BlockSpec((d, tm // d, 128), lambda i: (0, i, 0)) for d in DILATIONS]
    body, in_specs, args = _after(
        dep, body, [_row_spec(tm, D)] + o_specs + st_specs + [_row_spec(tm, 128)] + [_full_spec((C, D))] * 3,
        [dx, *os_, *sts, alpha, *wos])
    return pl.pallas_call(
        body, name="attn_out_bwd", grid=(T // tm,),
        in_specs=in_specs,
        out_specs=o_specs + st_specs,
        out_shape=[jax.ShapeDtypeStruct((d, T // d, C), BF) for d in DILATIONS]
        + [jax.ShapeDtypeStruct((d, T // d, 128), F32) for d in DILATIONS],
        scratch_shapes=[pltpu.VMEM((3, C // 128, tm, 128), F32), pltpu.VMEM((3, C // 128, tm, 128), F32),
                        pltpu.VMEM((3, 1, tm, 128), F32), pltpu.VMEM((1, tm, 128), F32)],
        compiler_params=_params("parallel"),
    )(*args)


def _qkv_bwd(dqkvs, cos, sin, x2, g, w3s, dx3, tm=512):
    T = x2.shape[0]
    C = GROUP_LANES

    def body(dq0, dq1, dq2, cos_ref, sin_ref, x_ref, g_ref, w0, w1, w2, dx3_ref,
             dx_ref, n0, n1, n2, h_ref, dg_ref, scr, dh_ref):
        dq_refs, w_refs, n_refs = (dq0, dq1, dq2), (w0, w1, w2), (n0, n1, n2)

        @pl.when(pl.program_id(0) == 0)
        def _():
            dg_ref[...] = jnp.zeros_like(dg_ref)

        cv = cos_ref[...]
        sv = sin_ref[...]
        dh_ref[...] = jnp.zeros_like(dh_ref)
        for gi, d in enumerate(DILATIONS):
            for w in range(3):
                _gather_rows(scr, dq_refs[gi].at[w], d)
                if w < 2:
                    scale = HEAD_DIM ** -0.5 if w == 0 else 1.0
                    for c in range(C // 128):
                        t = scr[c]
                        scr[c] = (t * cv - _rot_half(t) * sv) * scale
                tb = _chunks_to_rows(scr).astype(BF)
                n_refs[gi][w] = tb
                dh_ref[...] += _dot_nt(tb, w_refs[gi][w])
        xv = x_ref[...]
        h_ref[...] = _rms_fwd(xv, g_ref[...]).astype(BF)
        dx, dg = _rms_bwd(xv, g_ref[...], dh_ref[...])
        dx_ref[...] = dx3_ref[...] + dx
        dg_ref[...] += dg

    dq_specs = [pl.BlockSpec((3, d, tm // d, C), lambda i: (0, 0, i, 0)) for d in DILATIONS]
    nat = pl.BlockSpec((3, tm, C), lambda i: (0, i, 0))
    nshape = jax.ShapeDtypeStruct((3, T, C), BF)
    return pl.pallas_call(
        body, name="qkv_bwd", grid=(T // tm,),
        in_specs=dq_specs + [_row_spec(tm, 128), _row_spec(tm, 128), _row_spec(tm, D), _full_spec((1, D))]
        + [pl.BlockSpec((3, D, C), lambda i: (0, 0, 0), pipeline_mode=pl.Buffered(1))] * 3 + [_row_spec(tm, D)],
        out_specs=[_row_spec(tm, D), nat, nat, nat, _row_spec(tm, D), _full_spec((1, D))],
        out_shape=[jax.ShapeDtypeStruct((T, D), F32), nshape, nshape, nshape, jax.ShapeDtypeStruct((T, D), BF),
                   jax.ShapeDtypeStruct((1, D), F32)],
        scratch_shapes=[pltpu.VMEM((C // 128, tm, 128), F32), pltpu.VMEM((tm, D), F32)],
        compiler_params=_params("arbitrary"),
    )(*dqkvs, cos, sin, x2, g, *w3s, dx3)


def _place():
    x, y, c = lax.axis_index("x"), lax.axis_index("y"), lax.axis_index("c")
    return x, y, c


def _gather_phases(ins, outs, *sems):
    n = len(ins)
    if n == 0:
        return (lambda: None), (lambda: None)
    send_sems, recv_sems, local_sems = sems
    x, y, c = _place()
    me, sibling = (x, y, c), (x, y, 1 - c)
    chips = [(1 - x, y), (x, 1 - y), (1 - x, 1 - y)]

    def slot(a, px, py, pc):
        return outs[a].at[4 * px + 2 * py + pc]

    def copy(a, k, block, to, src=None):
        return pltpu.make_async_remote_copy(
            src_ref=slot(a, *block) if src is None else src, dst_ref=slot(a, *block),
            send_sem=send_sems.at[a, k], recv_sem=recv_sems.at[a, k], device_id=to, device_id_type=MESH)

    def mine(a):
        return pltpu.make_async_copy(ins[a], slot(a, *me), local_sems.at[a])

    def first(a):
        return [copy(a, 0, me, sibling, src=ins[a])] + [copy(a, 1 + j, me, (*chip, c), src=ins[a]) for j, chip in enumerate(chips)]

    def begin():
        for a in range(n):
            mine(a).start()
        for a in range(n):
            for cp in first(a):
                cp.start()

    def finish():
        passed = []
        for a in range(n):
            for j, chip in enumerate(chips):
                copy(a, 1 + j, (*chip, c), me).wait_recv()
                fwd = copy(a, 4 + j, (*chip, c), sibling)
                fwd.start()
                passed.append(fwd)
        for a in range(n):
            copy(a, 0, sibling, me).wait_recv()
            for j, chip in enumerate(chips):
                copy(a, 4 + j, (*chip, 1 - c), me).wait_recv()
        for a in range(n):
            for cp in first(a):
                cp.wait_send()
        for cp in passed:
            cp.wait_send()
        for a in range(n):
            mine(a).wait()

    return begin, finish


def _gather_scratch(n):
    return [pltpu.SemaphoreType.DMA((n, 7)), pltpu.SemaphoreType.DMA((n, 7)), pltpu.SemaphoreType.DMA((n,))] if n else []


def _allgather(arrs, name):
    n = len(arrs)

    def body(*refs):
        begin, finish = _gather_phases(refs[:n], refs[n:2 * n], *refs[2 * n:])
        begin()
        finish()

    hbm = pl.BlockSpec(memory_space=pl.ANY)
    return pl.pallas_call(
        body, name=name,
        in_specs=[hbm] * n, out_specs=[hbm] * n,
        out_shape=[jax.ShapeDtypeStruct((N_DEV,) + a.shape, a.dtype) for a in arrs],
        scratch_shapes=_gather_scratch(n),
    )(*arrs)


def _peer(k):
    x, y, c = _place()
    px = 1 - x if k & 4 else x
    py = 1 - y if k & 2 else y
    pc = 1 - c if k & 1 else c
    return (px, py, pc), 4 * px + 2 * py + pc


HBM_SPEC = pl.BlockSpec(memory_space=pltpu.HBM)
SEM_SPEC = pl.BlockSpec(memory_space=pltpu.SEMAPHORE)
EFFECT = pltpu.SideEffectType.DATAFLOW_SIDE_EFFECTING


def _exchange_start(arrs, name, same_block=False, dep=None):
    n = len(arrs)
    n_dep = 0 if dep is None else 1

    def body(*refs):
        srcs, lands = refs[:n], refs[n:2 * n]
        send_sems, recv_sems, local_sems = refs[2 * n + n_dep:2 * n + n_dep + 3]
        token = refs[-1]
        x, y, c = _place()
        me = 4 * x + 2 * y + c
        block = (lambda a, j: srcs[a]) if same_block else (lambda a, j: srcs[a].at[j])
        for a in range(n):
            pltpu.make_async_copy(block(a, me), lands[a].at[me], local_sems.at[a]).start()
        for a in range(n):
            for k in range(1, N_DEV):
                to, to_idx = _peer(k)
                pltpu.make_async_remote_copy(
                    src_ref=block(a, to_idx), dst_ref=lands[a].at[me],
                    send_sem=send_sems.at[a * (N_DEV - 1) + k - 1], recv_sem=recv_sems.at[a * (N_DEV - 1) + k - 1], device_id=to, device_id_type=MESH).start()
        token[...] = jnp.zeros_like(token)

    land_shape = (lambda a: (N_DEV,) + a.shape) if same_block else (lambda a: a.shape)
    src_shapes = [pltpu.HBM(a.shape, a.dtype) for a in arrs]
    land_shapes = [pltpu.HBM(land_shape(a), a.dtype) for a in arrs]
    outs = pl.pallas_call(
        body, name=name,
        out_shape=(pltpu.SemaphoreType.DMA((n * (N_DEV - 1),)), pltpu.SemaphoreType.DMA((n * (N_DEV - 1),)),
                   pltpu.SemaphoreType.DMA((n,)), *src_shapes, *land_shapes, jax.ShapeDtypeStruct((8, 128), F32)),
        in_specs=[HBM_SPEC] * (2 * n) + [pl.BlockSpec(memory_space=pl.ANY)] * n_dep,
        out_specs=(SEM_SPEC, SEM_SPEC, SEM_SPEC, *([HBM_SPEC] * (2 * n)), pl.BlockSpec(memory_space=pltpu.VMEM)),
        input_output_aliases={i: 3 + i for i in range(2 * n)},
        compiler_params=pltpu.CompilerParams(has_side_effects=EFFECT),
    )(*[pltpu.with_memory_space_constraint(a, pltpu.HBM) for a in arrs],
      *[pltpu.with_memory_space_constraint(lax.empty(land_shape(a), a.dtype), pltpu.HBM) for a in arrs],
      *([] if dep is None else [dep]))
    return outs[0], outs[1], outs[2], outs[3:3 + n], outs[3 + n:3 + 2 * n], outs[3 + 2 * n]


def _exchange_wait(send_sems, recv_sems, local_sems, src_thru, land_thru, after, name):
    n = len(src_thru)
    after = list(after) if isinstance(after, (list, tuple)) else [after]
    same_block = src_thru[0].shape != land_thru[0].shape

    def body(*refs):
        srcs, lands = refs[:n], refs[n:2 * n]
        send_sems, recv_sems, local_sems = refs[2 * n:2 * n + 3]
        x, y, c = _place()
        me = 4 * x + 2 * y + c
        for a in range(n):
            pltpu.make_async_copy(srcs[a] if same_block else srcs[a].at[me], lands[a].at[me], local_sems.at[a]).wait()
            for k in range(1, N_DEV):
                frm, frm_idx = _peer(k)
                cp = pltpu.make_async_remote_copy(
                    src_ref=srcs[a] if same_block else srcs[a].at[frm_idx], dst_ref=lands[a].at[frm_idx],
                    send_sem=send_sems.at[a * (N_DEV - 1) + k - 1], recv_sem=recv_sems.at[a * (N_DEV - 1) + k - 1], device_id=frm, device_id_type=MESH)
                cp.wait_send()
                cp.wait_recv()

    outs = pl.pallas_call(
        body, name=name,
        out_shape=tuple(pltpu.HBM(a.shape, a.dtype) for a in (*src_thru, *land_thru)),
        in_specs=[HBM_SPEC] * (2 * n) + [SEM_SPEC, SEM_SPEC, SEM_SPEC] + [pl.BlockSpec(memory_space=pl.ANY)] * len(after),
        out_specs=[HBM_SPEC] * (2 * n),
        input_output_aliases={i: i for i in range(2 * n)},
        compiler_params=pltpu.CompilerParams(has_side_effects=EFFECT),
    )(*src_thru, *land_thru, send_sems, recv_sems, local_sems, *after)
    return outs[n:]


def _row_tile(rows):
    for t in (256, 176, 128, 8):
        if rows % t == 0:
            return t
    return rows


def _adamw(parts, w, m, v, name, transposed=False, layer=None, into=None):
    K = parts.shape[0]
    R, C = w.shape
    tr = 256 if transposed else _row_tile(R)
    n_into = 0 if into is None else 4

    def body(p_ref, w_ref, m_ref, v_ref, *refs):
        g_ref, d_ref, nm_ref, nv_ref = refs[n_into:]
        g = p_ref[0].astype(F32)
        for k in range(1, K):
            g = g + p_ref[k].astype(F32)
        if transposed:
            g = g.T
        nm = ADAM_B1 * m_ref[...] + (1.0 - ADAM_B1) * g
        nv = ADAM_B2 * v_ref[...] + (1.0 - ADAM_B2) * jnp.square(g)
        m_hat = nm / (1.0 - ADAM_B1 ** ADAM_STEP)
        v_hat = nv / (1.0 - ADAM_B2 ** ADAM_STEP)
        g_ref[...] = g
        d_ref[...] = -ADAM_LR * (m_hat / (jnp.sqrt(v_hat) + ADAM_EPS) + ADAM_WD * w_ref[...])
        nm_ref[...] = nm
        nv_ref[...] = nv

    blk = _row_spec(tr, C)
    parts_spec = pl.BlockSpec((K, C, tr), lambda i: (0, 0, i)) if transposed else pl.BlockSpec((K, tr, C), lambda i: (0, i, 0))
    if layer is None:
        out_blk, shp = blk, jax.ShapeDtypeStruct((R, C), F32)
    else:
        out_blk, shp = pl.BlockSpec((None, tr, C), lambda i: (layer, i, 0)), jax.ShapeDtypeStruct((2, R, C), F32)
    return pl.pallas_call(
        body, name=name, grid=(R // tr,),
        in_specs=[parts_spec, blk, blk, blk] + [pl.BlockSpec(memory_space=pl.ANY)] * n_into,
        out_specs=[out_blk] * 4,
        out_shape=[shp] * 4,
        input_output_aliases={4 + j: j for j in range(n_into)},
        compiler_params=_params("parallel"),
    )(parts, w, m, v, *(into or ()))


def _rope_tables(T):
    inv_freq = 1.0 / (ROPE_THETA ** (jnp.arange(0, HEAD_DIM, 2, dtype=F32) / HEAD_DIM))
    hi = (64.0 * jnp.arange(T // 64, dtype=F32))[:, None] * inv_freq[None, :]
    lo = jnp.arange(64, dtype=F32)[:, None] * inv_freq[None, :]
    ch, sh, cl, sl = jnp.cos(hi)[:, None], jnp.sin(hi)[:, None], jnp.cos(lo)[None], jnp.sin(lo)[None]
    cos = (ch * cl - sh * sl).reshape(T, HEAD_DIM // 2)
    sin = (sh * cl + ch * sl).reshape(T, HEAD_DIM // 2)
    return jnp.concatenate([cos] * 4, axis=-1), jnp.concatenate([sin] * 4, axis=-1)


def _pad_lanes(a, n):
    return jnp.pad(a, ((0, 0),) * (a.ndim - 1) + ((0, n - a.shape[-1]),))


POOL = ("pool_in", "pool_grp", "pool_out")
FFN0 = ("gate0", "up0", "down0")
LAYER1 = ("qkv", "attn_out", "gate1", "up1", "down1")


def _layout_weights(gw):
    w = {}
    if "pool_in" in gw:
        w["pool_in"] = gw["pool_in"].reshape(D, D)
        w["pool_grp"] = jnp.transpose(gw["pool_grp"], (1, 0, 2, 3)).reshape(4, POOL_GC, POOL_GC)
        w["pool_out"] = gw["pool_out"].reshape(D, D)
    if "qkv" in gw:
        wqkv = jnp.transpose(gw["qkv"], (1, 0, 2)).reshape(D, 3 * D)
        wo = gw["attn_out"].reshape(D, D)
        w["qkv"], w["attn_out"] = [], []
        for nh, off in zip(HEAD_GROUPS, HEAD_OFFS):
            lo, n = off * HEAD_DIM, nh * HEAD_DIM
            w["qkv"].append(jnp.stack([_pad_lanes(wqkv[:, k * D + lo:k * D + lo + n], GROUP_LANES) for k in range(3)]))
            w["attn_out"].append(jnp.pad(wo[lo:lo + n], ((0, GROUP_LANES - n), (0, 0))))
    for nm in gw:
        if nm[:-1] in ("gate", "up", "down"):
            w[nm] = gw[nm].reshape(F, D)
    return w


def _local_step(x, target, w, ffn0_shards, layer1_shards, norm_mix, norm_ffn, norm_final, pool_scale, emit):
    T = x.shape[0]
    cos, sin = _rope_tables(T)
    nm = [norm_mix[i:i + 1] for i in range(2)]
    nf = [norm_ffn[i:i + 1] for i in range(2)]
    nfin = norm_final.reshape(1, D)

    x1, p, *ffn0 = _pool_fwd(x, nm[0], w["pool_in"], w["pool_grp"], pool_scale, w["pool_out"], ffn0_shards)
    w = {**w, **_layout_weights(dict(zip(FFN0, ffn0)))}
    x2, a0, b0, s0, *layer1 = _ffn_fwd(x1, nf[0], w["gate0"], w["up0"], w["down0"], "ffn_fwd0", gather=layer1_shards)
    w = {**w, **_layout_weights(dict(zip(LAYER1, layer1)))}
    qkvs = _qkv_fwd(x2, nm[1], w["qkv"], cos, sin)
    att = [_attn_fwd(qkvs[g], HEAD_GROUPS[g], f"attn_fwd{g}") for g in range(3)]
    os_, sts = [a[0] for a in att], [a[1] for a in att]
    x3, m0, m1, m2, alpha = _attn_out_fwd(x2, os_, sts, w["attn_out"])
    dx4, a1, b1, s1, loss, dg_final = _ffn_fwd(x3, nf[1], w["gate1"], w["up1"], w["down1"], "ffn_fwd1", head=(target, nfin))

    g = {}
    dx3, da1, db1, h3, dy4, dg_ffn1 = _ffn_bwd(dx4, x3, nf[1], a1, b1, w["gate1"], w["up1"], w["down1"], "ffn_bwd1")
    g["down1"] = _wgrad(s1, dy4, "down_wgrad1", mblk=F // 2)
    g["gate1"] = _wgrad(da1, h3, "gate_wgrad1", mblk=F // 2)
    g["up1"] = _wgrad(db1, h3, "up_wgrad1", mblk=F // 2)
    dep = emit("ffn1", g)
    dos_and_stats = _attn_out_bwd(dx3, os_, sts, alpha, w["attn_out"], dep=dep)
    dos, dsts = dos_and_stats[:3], dos_and_stats[3:]
    g["attn_out"] = _wgrad_multi([m0, m1, m2], dx3, "attn_out_wgrad")
    dqkvs = [_attn_bwd(qkvs[gi], dos[gi], sts[gi], dsts[gi], HEAD_GROUPS[gi], f"attn_bwd{gi}") for gi in range(3)]
    dx2, n0, n1, n2, h2b, dg_mix1 = _qkv_bwd(dqkvs, cos, sin, x2, nm[1], w["qkv"], dx3)
    g["qkv"] = [_wgrad_stack(h2b, n, f"qkv_wgrad{i}") for i, n in enumerate((n0, n1, n2))]
    dep = emit("attn", g)
    dx1, da0, db0, h1, dy2, dg_ffn0 = _ffn_bwd(dx2, x1, nf[0], a0, b0, w["gate0"], w["up0"], w["down0"], "ffn_bwd0", dep=dep)
    g["down0"] = _wgrad(s0, dy2, "down_wgrad0", mblk=F // 2)
    g["gate0"] = _wgrad(da0, h1, "gate_wgrad0", mblk=F // 2)
    g["up0"] = _wgrad(db0, h1, "up_wgrad0", mblk=F // 2)
    dep = emit("ffn0", g)
    dx0, z, dzp, du, h0b, dscale, dg_mix0 = _pool_bwd(dx1, x, nm[0], p, w["pool_in"], w["pool_grp"], pool_scale, w["pool_out"], dep=dep)
    g["pool_out"] = _wgrad(z, dx1, "pool_out_wgrad")
    g["pool_in"] = _wgrad(h0b, du, "pool_in_wgrad")
    g["pool_grp"] = _wgrad_pool_groups(p, dzp)
    emit("pool", g)

    small = jnp.concatenate([dg_mix0, dg_mix1, dg_ffn0, dg_ffn1, dg_final, dscale,
                             jnp.broadcast_to(loss, (1, D)), jnp.zeros((1, D), F32)], axis=0)
    return dx0, small


GROUPS = {"ffn1": ("down1", "gate1", "up1"), "attn": ("qkv", "attn_out"), "ffn0": ("down0", "gate0", "up0"),
          "pool": ("pool_in", "pool_out", "pool_grp")}


def _grad_blocks(group, g):
    blocks = {}
    if group == "pool":
        blocks["pool_in"] = g["pool_in"].reshape(N_DEV, D // N_DEV, D)
        blocks["pool_out"] = g["pool_out"].reshape(N_DEV, D // N_DEV, D)
        blocks["pool_grp"] = jnp.transpose(g["pool_grp"].reshape(4, N_DEV, POOL_GC // N_DEV, POOL_GC), (1, 0, 2, 3)).reshape(N_DEV, 4 * POOL_GC // N_DEV, POOL_GC)
    elif group == "attn":
        wo = jnp.concatenate([gw[:nh * HEAD_DIM] for gw, nh in zip(g["attn_out"], HEAD_GROUPS)], axis=0)
        blocks["attn_out"] = wo.reshape(N_DEV, D // N_DEV, D)
        wqkv = jnp.concatenate([g["qkv"][gi][k][:, :HEAD_GROUPS[gi] * HEAD_DIM] for k in range(3) for gi in range(3)], axis=1)
        blocks["qkv"] = jnp.transpose(wqkv.reshape(D, N_DEV, 3 * D // N_DEV), (1, 0, 2))
    else:
        for nm in GROUPS[group]:
            blocks[nm] = g[nm].reshape(N_DEV, F // N_DEV, D)
    return [blocks[nm] for nm in GROUPS[group]]


def kernel(x, norm_mix, norm_ffn, norm_final, pool_w_in, pool_w_group, pool_scale, pool_w_out, attn_w_qkv, attn_w_out, ffn_w_gate, ffn_w_up, ffn_w_down, loss_target, m_norm_mix, m_norm_ffn, m_norm_final, m_pool_w_in, m_pool_w_group, m_pool_scale, m_pool_w_out, m_attn_w_qkv, m_attn_w_out, m_ffn_w_gate, m_ffn_w_up, m_ffn_w_down, v_norm_mix, v_norm_ffn, v_norm_final, v_pool_w_in, v_pool_w_group, v_pool_scale, v_pool_w_out, v_attn_w_qkv, v_attn_w_out, v_ffn_w_gate, v_ffn_w_up, v_ffn_w_down):
    shard = {
        "pool_in": pool_w_in[0], "pool_grp": pool_w_group[0], "pool_out": pool_w_out[0],
        "qkv": attn_w_qkv[0], "attn_out": attn_w_out[0],
    }
    for l in range(2):
        shard[f"gate{l}"] = ffn_w_gate[l].T
        shard[f"up{l}"] = ffn_w_up[l].T
        shard[f"down{l}"] = ffn_w_down[l]
    shard = {k: v.astype(BF) for k, v in shard.items()}
    w0 = _layout_weights(dict(zip(POOL, _allgather([shard[k] for k in POOL], "weights_allgather"))))

    started = {}

    def emit(group, g):
        started[group] = _exchange_start(_grad_blocks(group, g), f"grads_start_{group}")
        return started[group][5]

    grad_x, small = _local_step(x[0], loss_target[0], w0, [shard[k] for k in FFN0], [shard[k] for k in LAYER1], norm_mix, norm_ffn, norm_final, pool_scale, emit)

    small_st = _exchange_start([small], "small_start", same_block=True, dep=started["pool"][5])

    def finish(group, after):
        lands = _exchange_wait(*started[group][:5], after, f"grads_wait_{group}")
        return dict(zip(GROUPS[group], lands))

    def upd(parts, wt, mt, vt, name):
        shape = wt.shape
        r2 = lambda t: t.reshape(parts.shape[1:])
        outs = _adamw(parts, r2(wt), r2(mt), r2(vt), name)
        return [o.reshape(shape) for o in outs]

    def ffn_layer(rcv, l, other=None):
        out = {}
        for nm, wt, mt, vt in (("gate", ffn_w_gate, m_ffn_w_gate, v_ffn_w_gate), ("up", ffn_w_up, m_ffn_w_up, v_ffn_w_up),
                               ("down", ffn_w_down, m_ffn_w_down, v_ffn_w_down)):
            out[nm] = _adamw(rcv[f"{nm}{l}"], wt[l], mt[l], vt[l], f"adamw_{nm}{l}", transposed=nm != "down", layer=l,
                             into=None if other is None else other[nm])
        return out

    res = {}
    ffn1 = ffn_layer(finish("ffn1", small_st[5]), 1)
    rcv = finish("attn", [ffn1[nm][0] for nm in ffn1])
    res["attn_w_qkv"] = upd(rcv["qkv"], attn_w_qkv, m_attn_w_qkv, v_attn_w_qkv, "adamw_qkv")
    res["attn_w_out"] = upd(rcv["attn_out"], attn_w_out, m_attn_w_out, v_attn_w_out, "adamw_attn_out")
    ffn0 = ffn_layer(finish("ffn0", [res["attn_w_qkv"][0], res["attn_w_out"][0]]), 0, other=ffn1)
    rcv = finish("pool", [ffn0[nm][0] for nm in ffn0])
    res["pool_w_in"] = upd(rcv["pool_in"], pool_w_in, m_pool_w_in, v_pool_w_in, "adamw_pool_in")
    res["pool_w_group"] = upd(rcv["pool_grp"], pool_w_group, m_pool_w_group, v_pool_w_group, "adamw_pool_grp")
    res["pool_w_out"] = upd(rcv["pool_out"], pool_w_out, m_pool_w_out, v_pool_w_out, "adamw_pool_out")
    for nm in ("gate", "up", "down"):
        res[f"ffn_w_{nm}"] = ffn0[nm]
    small_all = _exchange_wait(*small_st[:5], [res[k][0] for k in ("pool_w_in", "pool_w_group", "pool_w_out")], "small_wait")[0]

    small_w = jnp.concatenate([norm_mix, norm_ffn, norm_final[None], pool_scale, jnp.zeros((2, D), F32)], axis=0)
    small_m = jnp.concatenate([m_norm_mix, m_norm_ffn, m_norm_final[None], m_pool_scale, jnp.zeros((2, D), F32)], axis=0)
    small_v = jnp.concatenate([v_norm_mix, v_norm_ffn, v_norm_final[None], v_pool_scale, jnp.ones((2, D), F32)], axis=0)
    sg, sd, sm, sv = _adamw(small_all, small_w, small_m, small_v, "adamw_small")
    loss = sg[6, 0]
    res["norm_mix"] = [t[0:2] for t in (sg, sd, sm, sv)]
    res["norm_ffn"] = [t[2:4] for t in (sg, sd, sm, sv)]
    res["norm_final"] = [t[4] for t in (sg, sd, sm, sv)]
    res["pool_scale"] = [t[5:6] for t in (sg, sd, sm, sv)]

    order = ["norm_mix", "norm_ffn", "norm_final", "pool_w_in", "pool_w_group", "pool_scale", "pool_w_out",
             "attn_w_qkv", "attn_w_out", "ffn_w_gate", "ffn_w_up", "ffn_w_down"]
    return (loss, grad_x[None], *[res[k][0] for k in order], *[res[k][1] for k in order],
            *[res[k][2] for k in order], *[res[k][3] for k in order])
```

```python
import math

import jax
import jax.numpy as jnp
from jax import lax
from jax.experimental import pallas as pl
from jax.experimental.pallas import tpu as pltpu

D = 1024
F = 2816
N_DEV = 8
EPS = 1e-6
NEG_INF = -1e30
POOL_WINDOWS = (2, 4, 8, 16)
POOL_HALO = 16
POOL_GC = 256
HEAD_DIM = 64
HEAD_GROUPS = (6, 5, 5)
HEAD_OFFS = (0, 6, 11)
DILATIONS = (1, 4, 16)
ATT_W = 128
GROUP_LANES = 384
LSE_GROUP_LANE = 8
ROPE_THETA = 10000.0
ADAM_LR, ADAM_B1, ADAM_B2, ADAM_EPS, ADAM_WD, ADAM_STEP = 0.001, 0.9, 0.999, 1e-08, 0.01, 10

BF = jnp.bfloat16
F32 = jnp.float32
VMEM_LIMIT = 56 * 1024 * 1024
MESH = pl.DeviceIdType.MESH


def _params(*sem):
    return pltpu.CompilerParams(dimension_semantics=sem, vmem_limit_bytes=VMEM_LIMIT)


def _dot(a, b):
    return jnp.dot(a, b, preferred_element_type=F32)


def _dot_nt(a, b):
    return lax.dot_general(a, b, (((1,), (1,)), ((), ())), preferred_element_type=F32)


def _dot_tn(a, b):
    return lax.dot_general(a, b, (((0,), (0,)), ((), ())), preferred_element_type=F32)


def _rms_fwd(xv, g):
    r = lax.rsqrt(jnp.mean(xv * xv, axis=-1, keepdims=True) + EPS)
    return (xv * r) * g


def _rms_bwd(xv, g, dh):
    r = lax.rsqrt(jnp.mean(xv * xv, axis=-1, keepdims=True) + EPS)
    xhat = xv * r
    dg = jnp.sum(dh * xhat, axis=0, keepdims=True)
    dxh = dh * g
    dx = r * (dxh - xhat * jnp.mean(dxh * xhat, axis=-1, keepdims=True))
    return dx, dg


def _lane_col(tile, j):
    lane = lax.broadcasted_iota(jnp.int32, tile.shape, 1)
    return jnp.sum(jnp.where(lane == j, tile, 0.0), axis=-1, keepdims=True)


def _row_spec(tm, n):
    return pl.BlockSpec((tm, n), lambda i: (i, 0))


def _full_spec(shape):
    nd = len(shape)
    return pl.BlockSpec(shape, lambda *_: (0,) * nd)


def _after(dep, body, in_specs, args):
    if dep is None:
        return body, list(in_specs), list(args)

    def body_after(dep_ref, *refs):
        body(*refs)

    return body_after, [pl.BlockSpec(memory_space=pl.ANY)] + list(in_specs), [dep] + list(args)


def _pool_fwd(x, g, w_in, w_grp, scale, w_out, gather, tm=512, dep=None):
    T = x.shape[0]
    nt = T // tm
    n = tm + POOL_HALO
    ng = len(gather)

    def body(x_ref, g_ref, win_ref, wg_ref, sc_ref, wout_ref, *refs):
        shard_refs, (x1_ref, p_ref), full_refs = refs[:ng], refs[ng:ng + 2], refs[ng + 2:2 * ng + 2]
        tail_ref, z_ref = refs[2 * ng + 2:2 * ng + 4]
        begin, finish = _gather_phases(shard_refs, full_refs, *refs[2 * ng + 4:])
        i = pl.program_id(0)

        @pl.when(i == 0)
        def _():
            begin()
            tail_ref[...] = jnp.zeros_like(tail_ref)

        u = _dot(_rms_fwd(x_ref[...], g_ref[...]).astype(BF), win_ref[...])
        pos = i * tm + lax.broadcasted_iota(jnp.int32, (tm, 1), 0)
        for g, w in enumerate(POOL_WINDOWS):
            sl = slice(g * POOL_GC, (g + 1) * POOL_GC)
            ug = u[:, sl]
            s = jnp.concatenate([tail_ref[:, sl], ug], axis=0)
            step = 1
            while step < w:
                s = s + pltpu.roll(s, step, 0)
                step *= 2
            cnt = jnp.minimum(pos + 1, w).astype(F32)
            pg = (s[POOL_HALO:, :] / cnt - ug).astype(BF)
            p_ref[:, sl] = pg
            z_ref[:, sl] = (_dot(pg, wg_ref[g]) * sc_ref[:, sl]).astype(BF)
        tail_ref[...] = u[tm - POOL_HALO:, :]
        x1_ref[...] = x_ref[...] + _dot(z_ref[...], wout_ref[...])

        @pl.when(i == nt - 1)
        def _():
            finish()

    hbm = pl.BlockSpec(memory_space=pl.ANY)
    body, in_specs, args = _after(
        dep, body,
        [_row_spec(tm, D), _full_spec((1, D)), _full_spec((D, D)), _full_spec((4, POOL_GC, POOL_GC)), _full_spec((1, D)),
         _full_spec((D, D))] + [hbm] * ng,
        [x, g, w_in, w_grp, scale, w_out, *gather])
    return pl.pallas_call(
        body, name="pool_fwd", grid=(nt,),
        in_specs=in_specs,
        out_specs=[_row_spec(tm, D), _row_spec(tm, D)] + [hbm] * ng,
        out_shape=[jax.ShapeDtypeStruct((T, D), F32), jax.ShapeDtypeStruct((T, D), BF)]
        + [jax.ShapeDtypeStruct((N_DEV,) + a.shape, a.dtype) for a in gather],
        scratch_shapes=[pltpu.VMEM((POOL_HALO, D), F32), pltpu.VMEM((tm, D), BF)] + _gather_scratch(ng),
        compiler_params=_params("arbitrary"),
    )(*args)


def _pool_bwd(dx1, x0, g0, p, w_in, w_grp, scale, w_out, tm=512, dep=None):
    T = x0.shape[0]
    nt = T // tm
    n = tm + POOL_HALO
    rev = lambda i: (nt - 1 - i, 0)

    def body(dx1_ref, x0_ref, g_ref, p_ref, win_ref, wg_ref, sc_ref, wout_ref,
             dx0_ref, z_ref, dzp_ref, du_ref, h0_ref, dsc_ref, dg_ref, head_ref):
        i = pl.program_id(0)

        @pl.when(i == 0)
        def _():
            head_ref[...] = jnp.zeros_like(head_ref)
            dsc_ref[...] = jnp.zeros_like(dsc_ref)
            dg_ref[...] = jnp.zeros_like(dg_ref)

        dx1v = dx1_ref[...]
        dz = _dot_nt(dx1v.astype(BF), wout_ref[...])
        pos = (nt - 1 - i) * tm + lax.broadcasted_iota(jnp.int32, (tm, 1), 0)
        for g, w in enumerate(POOL_WINDOWS):
            sl = slice(g * POOL_GC, (g + 1) * POOL_GC)
            zpre = _dot(p_ref[:, sl], wg_ref[g])
            dzg = dz[:, sl]
            dsc_ref[:, sl] += jnp.sum(dzg * zpre, axis=0, keepdims=True)
            z_ref[:, sl] = (zpre * sc_ref[:, sl]).astype(BF)
            dzp = (dzg * sc_ref[:, sl]).astype(BF)
            dzp_ref[:, sl] = dzp
            dp = _dot_nt(dzp, wg_ref[g])
            cnt = jnp.minimum(pos + 1, w).astype(F32)
            dpc = dp / cnt
            s = jnp.concatenate([dpc, head_ref[:, sl]], axis=0)
            step = 1
            while step < w:
                s = s + pltpu.roll(s, n - step, 0)
                step *= 2
            head_ref[:, sl] = dpc[:POOL_HALO, :]
            du_ref[:, sl] = (s[:tm, :] - dp).astype(BF)
        dh0 = _dot_nt(du_ref[...], win_ref[...])
        x0v = x0_ref[...]
        h0_ref[...] = _rms_fwd(x0v, g_ref[...]).astype(BF)
        dx, dg = _rms_bwd(x0v, g_ref[...], dh0)
        dx0_ref[...] = dx1v + dx
        dg_ref[...] += dg

    bf_rows = jax.ShapeDtypeStruct((T, D), BF)
    vec = jax.ShapeDtypeStruct((1, D), F32)
    body, in_specs, args = _after(
        dep, body,
        [pl.BlockSpec((tm, D), rev), pl.BlockSpec((tm, D), rev), _full_spec((1, D)), pl.BlockSpec((tm, D), rev),
         _full_spec((D, D)), _full_spec((4, POOL_GC, POOL_GC)), _full_spec((1, D)), _full_spec((D, D))],
        [dx1, x0, g0, p, w_in, w_grp, scale, w_out])
    return pl.pallas_call(
        body, name="pool_bwd", grid=(nt,),
        in_specs=in_specs,
        out_specs=[pl.BlockSpec((tm, D), rev)] * 5 + [_full_spec((1, D))] * 2,
        out_shape=[jax.ShapeDtypeStruct((T, D), F32), bf_rows, bf_rows, bf_rows, bf_rows, vec, vec],
        scratch_shapes=[pltpu.VMEM((POOL_HALO, D), F32)],
        compiler_params=_params("arbitrary"),
    )(*args)


def _loss_head(xv, tv, gv):
    diff = _rms_fwd(xv, gv) - tv
    loss = 0.5 * jnp.sum(jnp.mean(diff * diff, axis=-1, keepdims=True), axis=0, keepdims=True)
    dx, dg = _rms_bwd(xv, gv, diff * (1.0 / D))
    return loss, dx, dg


def _ffn_fwd(x, g, wg_t, wu_t, wd, name, gather=(), head=None, tm=512, fk=1408):
    T = x.shape[0]
    ng = len(gather)
    nh_in, nh_out = (2, 2) if head is not None else (0, 0)
    ni, nk = T // tm, F // fk

    def body(x_ref, g_ref, wg_ref, wu_ref, wd_ref, *refs):
        head_in, refs = refs[:nh_in], refs[nh_in:]
        shard_refs, (xo_ref, a_ref, b_ref, s_ref), full_refs = refs[:ng], refs[ng:ng + 4], refs[ng + 4:2 * ng + 4]
        refs = refs[2 * ng + 4:]
        head_out, refs = refs[:nh_out], refs[nh_out:]
        acc_ref, h_ref = refs[:2]
        begin, finish = _gather_phases(shard_refs, full_refs, *refs[2:])
        i = pl.program_id(0)
        k = pl.program_id(1)

        @pl.when(jnp.logical_and(i == 0, k == 0))
        def _():
            begin()
            for r in head_out:
                r[...] = jnp.zeros_like(r)

        @pl.when(k == 0)
        def _():
            acc_ref[...] = jnp.zeros_like(acc_ref)
            h_ref[...] = _rms_fwd(x_ref[...], g_ref[...]).astype(BF)

        hv = h_ref[...]
        a = _dot_nt(hv, wg_ref[...])
        b = _dot_nt(hv, wu_ref[...])
        s = ((a * jax.nn.sigmoid(a)) * b).astype(BF)
        a_ref[...] = a.astype(BF)
        b_ref[...] = b.astype(BF)
        s_ref[...] = s
        acc_ref[...] += _dot(s, wd_ref[...])

        @pl.when(k == nk - 1)
        def _():
            xo = x_ref[...] + acc_ref[...]
            if head is None:
                xo_ref[...] = xo
            else:
                loss, dx, dg = _loss_head(xo, head_in[0][...], head_in[1][...])
                xo_ref[...] = dx
                head_out[0][...] += loss
                head_out[1][...] += dg

        @pl.when(jnp.logical_and(i == ni - 1, k == nk - 1))
        def _():
            finish()

    row = pl.BlockSpec((tm, D), lambda i, k: (i, 0))
    wsp = pl.BlockSpec((fk, D), lambda i, k: (k, 0))
    act = pl.BlockSpec((tm, fk), lambda i, k: (i, k))
    hbm = pl.BlockSpec(memory_space=pl.ANY)
    act_shape = jax.ShapeDtypeStruct((T, F), BF)
    vec = pl.BlockSpec((1, D), lambda i, k: (0, 0))
    one = pl.BlockSpec((1, 1), lambda i, k: (0, 0))
    return pl.pallas_call(
        body, name=name, grid=(ni, nk),
        in_specs=[row, vec, wsp, wsp, wsp] + [row, vec][:nh_in] + [hbm] * ng,
        out_specs=[row, act, act, act] + [hbm] * ng + [one, vec][:nh_out],
        out_shape=[jax.ShapeDtypeStruct((T, D), F32), act_shape, act_shape, act_shape]
        + [jax.ShapeDtypeStruct((N_DEV,) + a.shape, a.dtype) for a in gather]
        + [jax.ShapeDtypeStruct((1, 1), F32), jax.ShapeDtypeStruct((1, D), F32)][:nh_out],
        scratch_shapes=[pltpu.VMEM((tm, D), F32), pltpu.VMEM((tm, D), BF)] + _gather_scratch(ng),
        compiler_params=_params("arbitrary", "arbitrary"),
    )(x, g, wg_t, wu_t, wd, *(head or ()), *gather)


def _ffn_bwd(dxo, x_in, g, a, b, wg_t, wu_t, wd, name, tm=512, fk=1408, dep=None):
    T = x_in.shape[0]

    def body(dxo_ref, x_ref, g_ref, a_ref, b_ref, wg_ref, wu_ref, wd_ref,
             dx_ref, da_ref, db_ref, h_ref, dy_ref, dg_ref, dh_ref):
        i = pl.program_id(0)
        k = pl.program_id(1)

        @pl.when(jnp.logical_and(i == 0, k == 0))
        def _():
            dg_ref[...] = jnp.zeros_like(dg_ref)

        @pl.when(k == 0)
        def _():
            h_ref[...] = _rms_fwd(x_ref[...], g_ref[...]).astype(BF)
            dy_ref[...] = dxo_ref[...].astype(BF)
            dh_ref[...] = jnp.zeros_like(dh_ref)

        ds = _dot_nt(dy_ref[...], wd_ref[...])
        av = a_ref[...].astype(F32)
        bv = b_ref[...].astype(F32)
        sig = jax.nn.sigmoid(av)
        db = (ds * (av * sig)).astype(BF)
        da = (ds * bv * (sig * (1.0 + av * (1.0 - sig)))).astype(BF)
        da_ref[...] = da
        db_ref[...] = db
        dh_ref[...] += _dot(da, wg_ref[...]) + _dot(db, wu_ref[...])

        @pl.when(k == pl.num_programs(1) - 1)
        def _():
            dx, dg = _rms_bwd(x_ref[...], g_ref[...], dh_ref[...])
            dx_ref[...] = dxo_ref[...] + dx
            dg_ref[...] += dg

    row = pl.BlockSpec((tm, D), lambda i, k: (i, 0))
    wsp = pl.BlockSpec((fk, D), lambda i, k: (k, 0))
    act = pl.BlockSpec((tm, fk), lambda i, k: (i, k))
    vec = pl.BlockSpec((1, D), lambda i, k: (0, 0))
    act_shape = jax.ShapeDtypeStruct((T, F), BF)
    body, in_specs, args = _after(dep, body, [row, row, vec, act, act, wsp, wsp, wsp], [dxo, x_in, g, a, b, wg_t, wu_t, wd])
    return pl.pallas_call(
        body, name=name, grid=(T // tm, F // fk),
        in_specs=in_specs,
        out_specs=[row, act, act, row, row, vec],
        out_shape=[jax.ShapeDtypeStruct((T, D), F32), act_shape, act_shape, jax.ShapeDtypeStruct((T, D), BF),
                   jax.ShapeDtypeStruct((T, D), BF), jax.ShapeDtypeStruct((1, D), F32)],
        scratch_shapes=[pltpu.VMEM((tm, D), F32)],
        compiler_params=_params("arbitrary", "arbitrary"),
    )(*args)


def _wgrad(a, b, name, tk=1024, mblk=None):
    T, M = a.shape
    N = b.shape[1]
    mblk = M if mblk is None else mblk

    def body(a_ref, b_ref, o_ref, acc_ref):
        t = pl.program_id(1)

        @pl.when(t == 0)
        def _():
            acc_ref[...] = jnp.zeros_like(acc_ref)

        acc_ref[...] += _dot_tn(a_ref[...].astype(BF), b_ref[...].astype(BF))

        @pl.when(t == pl.num_programs(1) - 1)
        def _():
            o_ref[...] = acc_ref[...].astype(BF)

    return pl.pallas_call(
        body, name=name, grid=(M // mblk, T // tk),
        in_specs=[pl.BlockSpec((tk, mblk), lambda m, t: (t, m)), pl.BlockSpec((tk, N), lambda m, t: (t, 0))],
        out_specs=pl.BlockSpec((mblk, N), lambda m, t: (m, 0)),
        out_shape=jax.ShapeDtypeStruct((M, N), BF),
        scratch_shapes=[pltpu.VMEM((mblk, N), F32)],
        compiler_params=_params("parallel", "arbitrary"),
    )(a, b)


def _wgrad_pool_groups(p, dzp, tk=2048):
    T = p.shape[0]

    def body(p_ref, d_ref, o_ref, acc_ref):
        t = pl.program_id(1)

        @pl.when(t == 0)
        def _():
            acc_ref[...] = jnp.zeros_like(acc_ref)

        acc_ref[...] += _dot_tn(p_ref[...], d_ref[...])

        @pl.when(t == pl.num_programs(1) - 1)
        def _():
            o_ref[...] = acc_ref[...].astype(BF)

    blk = pl.BlockSpec((tk, POOL_GC), lambda g, t: (t, g))
    return pl.pallas_call(
        body, name="pool_wgrad_groups", grid=(4, T // tk),
        in_specs=[blk, blk],
        out_specs=pl.BlockSpec((None, POOL_GC, POOL_GC), lambda g, t: (g, 0, 0)),
        out_shape=jax.ShapeDtypeStruct((4, POOL_GC, POOL_GC), BF),
        scratch_shapes=[pltpu.VMEM((POOL_GC, POOL_GC), F32)],
        compiler_params=_params("parallel", "arbitrary"),
    )(p, dzp)


def _rot_half(t):
    lane = lax.broadcasted_iota(jnp.int32, t.shape, 1)
    first = (lane % HEAD_DIM) < (HEAD_DIM // 2)
    return jnp.where(first, -pltpu.roll(t, 128 - HEAD_DIM // 2, 1), pltpu.roll(t, HEAD_DIM // 2, 1))


def _scatter_rows(dst_ref, scr_ref, d, cast):
    nc, rows, _ = scr_ref.shape
    n = rows // d
    for c in range(nc):
        sl = slice(c * 128, (c + 1) * 128)
        for r in range(d):
            src = scr_ref[c] if d == 1 else scr_ref.at[c][pl.ds(r, n, stride=d), :]
            dst_ref[r, :, sl] = src.astype(cast)


def _gather_rows(scr_ref, src_ref, d):
    nc, rows, _ = scr_ref.shape
    n = rows // d
    for c in range(nc):
        sl = slice(c * 128, (c + 1) * 128)
        for r in range(d):
            val = src_ref[r, :, sl].astype(F32)
            if d == 1:
                scr_ref[c] = val
            else:
                scr_ref.at[c][pl.ds(r, n, stride=d), :] = val


def _chunks_to_rows(scr_ref):
    nc = scr_ref.shape[0]
    return scr_ref[0] if nc == 1 else jnp.concatenate([scr_ref[c] for c in range(nc)], axis=1)


def _rows_to_chunks(scr_ref, val):
    for c in range(scr_ref.shape[0]):
        scr_ref[c] = val[:, c * 128:(c + 1) * 128]


def _rope(t, cv, sv, scale):
    return (t * cv + _rot_half(t) * sv) * scale


def _qkv_fwd(x, g, w3s, cos, sin, tm=512):
    T = x.shape[0]
    C = GROUP_LANES
    nc = C // 128

    def body(x_ref, g_ref, w0, w1, w2, cos_ref, sin_ref, o0, o1, o2, scr_ref):
        h = _rms_fwd(x_ref[...], g_ref[...]).astype(BF)
        cv = cos_ref[...]
        sv = sin_ref[...]
        for gi, (w_ref, o_ref, d) in enumerate(zip((w0, w1, w2), (o0, o1, o2), DILATIONS)):
            t_all = _dot(h, w_ref[...])
            for w in range(3):
                chunks = [t_all[:, w * C + c * 128:w * C + (c + 1) * 128] for c in range(nc)]
                if w < 2:
                    chunks = [_rope(tc, cv, sv, HEAD_DIM ** -0.5 if w == 0 else 1.0) for tc in chunks]
                if d == 1:
                    for c in range(nc):
                        o_ref[w, 0, :, c * 128:(c + 1) * 128] = chunks[c].astype(BF)
                else:
                    scr = scr_ref.at[gi * 3 + w]
                    for c in range(nc):
                        scr[c] = chunks[c]
                    _scatter_rows(o_ref.at[w], scr, d, BF)

    return pl.pallas_call(
        body, name="qkv_fwd", grid=(T // tm,),
        in_specs=[_row_spec(tm, D), _full_spec((1, D))] + [_full_spec((D, 3 * C))] * 3 + [_row_spec(tm, 128), _row_spec(tm, 128)],
        out_specs=[pl.BlockSpec((3, d, tm // d, C), lambda i: (0, 0, i, 0)) for d in DILATIONS],
        out_shape=[jax.ShapeDtypeStruct((3, d, T // d, C), BF) for d in DILATIONS],
        scratch_shapes=[pltpu.VMEM((9, nc, tm, 128), F32)],
        compiler_params=_params("parallel"),
    )(x, g, *w3s, cos, sin)


def _att_chunk(L):
    return min(L, 1024)


def _head_mask(shape, h):
    lane = lax.broadcasted_iota(jnp.int32, shape, 1)
    return (lane // HEAD_DIM) == (h % 2)


def _attn_fwd(qkv, nh, name):
    _, d, L, C = qkv.shape
    lc = _att_chunk(L)
    nblk = lc // ATT_W

    def body(q_ref, k_ref, kh_ref, v_ref, vh_ref, o_ref, st_ref):
        i = pl.program_id(1)
        qi = lax.broadcasted_iota(jnp.int32, (ATT_W, 2 * ATT_W), 0)
        kj = lax.broadcasted_iota(jnp.int32, (ATT_W, 2 * ATT_W), 1)
        band = jnp.logical_and(kj >= qi, kj <= qi + ATT_W)
        lane = lax.broadcasted_iota(jnp.int32, (ATT_W, 128), 1)

        def block(row0, kc, vc, mask):
            rows = pl.ds(row0, ATT_W)
            lses = []
            for hp in range(C // 128):
                sl = slice(hp * 128, (hp + 1) * 128)
                qp = q_ref[rows, sl]
                kp = kc[:, sl]
                vp = vc[:, sl]
                outs = []
                for h in range(2 * hp, min(2 * hp + 2, nh)):
                    hm = _head_mask(qp.shape, h)
                    s = _dot_nt(jnp.where(hm, qp, jnp.zeros_like(qp)), kp)
                    s = jnp.where(mask, s, NEG_INF)
                    m = jnp.max(s, axis=-1, keepdims=True)
                    e = jnp.exp(s - m)
                    den = jnp.sum(e, axis=-1, keepdims=True)
                    p = (e * pl.reciprocal(den)).astype(BF)
                    outs.append(_dot(p, vp))
                    lses.append(m + jnp.log(den))
                if len(outs) == 2:
                    o = jnp.where(_head_mask(outs[0].shape, 0), outs[0], outs[1])
                else:
                    o = jnp.where(_head_mask(outs[0].shape, 0), outs[0], 0.0)
                o_ref[rows, sl] = o.astype(BF)
            mm = lses[0]
            for l in lses[1:]:
                mm = jnp.maximum(mm, l)
            tot = jnp.exp(lses[0] - mm)
            for l in lses[1:]:
                tot = tot + jnp.exp(l - mm)
            tile = jnp.where(lane == LSE_GROUP_LANE, mm + jnp.log(tot) - math.log(nh), 0.0)
            for h, l in enumerate(lses):
                tile = jnp.where(lane == h, l, tile)
            st_ref[rows, :] = tile

        first_mask = jnp.logical_and(band, jnp.logical_or(kj >= ATT_W, i > 0))
        block(0, jnp.concatenate([kh_ref[...], k_ref[pl.ds(0, ATT_W), :]], axis=0),
              jnp.concatenate([vh_ref[...], v_ref[pl.ds(0, ATT_W), :]], axis=0), first_mask)

        if nblk > 1:
            def step(blk, carry):
                prev = pl.ds(pl.multiple_of((blk - 1) * ATT_W, ATT_W), 2 * ATT_W)
                block(pl.multiple_of(blk * ATT_W, ATT_W), k_ref[prev, :], v_ref[prev, :], band)
                return carry
            lax.fori_loop(1, nblk, step, 0, unroll=True)

    main = lambda w: pl.BlockSpec((None, None, lc, C), lambda r, i: (w, r, i, 0))
    halo = lambda w: pl.BlockSpec((None, None, ATT_W, C), lambda r, i: (w, r, jnp.maximum(i * nblk - 1, 0), 0))
    return pl.pallas_call(
        body, name=name, grid=(d, L // lc),
        in_specs=[main(0), main(1), halo(1), main(2), halo(2)],
        out_specs=[pl.BlockSpec((None, lc, C), lambda r, i: (r, i, 0)), pl.BlockSpec((None, lc, 128), lambda r, i: (r, i, 0))],
        out_shape=[jax.ShapeDtypeStruct((d, L, C), BF), jax.ShapeDtypeStruct((d, L, 128), F32)],
        compiler_params=_params("parallel", "arbitrary"),
    )(qkv, qkv, qkv, qkv, qkv)


def _attn_bwd(qkv, do, st, dst, nh, name):
    _, d, L, C = qkv.shape
    lc = _att_chunk(L)
    nblk = lc // ATT_W
    nchunk = L // lc

    def body(q_ref, qn_ref, k_ref, kh_ref, v_ref, vh_ref, do_ref, don_ref, st_ref, stn_ref, ds_ref, dsn_ref, o_ref,
             p_scr, dsc_scr):
        i = pl.program_id(1)
        qi = lax.broadcasted_iota(jnp.int32, (ATT_W, 2 * ATT_W), 0)
        kj = lax.broadcasted_iota(jnp.int32, (ATT_W, 2 * ATT_W), 1)
        band_q = jnp.logical_and(kj >= qi, kj <= qi + ATT_W)
        qa = lax.broadcasted_iota(jnp.int32, (2 * ATT_W, ATT_W), 0)
        kb = lax.broadcasted_iota(jnp.int32, (2 * ATT_W, ATT_W), 1)
        band_k = jnp.logical_and(qa >= kb, qa <= kb + ATT_W)

        def probs(qm, kp, lse, mask):
            s = _dot_nt(qm, kp)
            return jnp.where(mask, jnp.exp(s - lse), 0.0)

        def q_block(blk, kc, vc, mask):
            rows = pl.ds(blk * ATT_W, ATT_W)
            stv = st_ref[rows, :]
            dsv = ds_ref[rows, :]
            for hp in range(C // 128):
                sl = slice(hp * 128, (hp + 1) * 128)
                qp = q_ref[rows, sl]
                dop = do_ref[rows, sl]
                kp = kc[:, sl]
                vp = vc[:, sl]
                outs = []
                for h in range(2 * hp, min(2 * hp + 2, nh)):
                    hm = _head_mask(qp.shape, h)
                    p = probs(jnp.where(hm, qp, jnp.zeros_like(qp)), kp, _lane_col(stv, h), mask)
                    dp = _dot_nt(jnp.where(hm, dop, jnp.zeros_like(dop)), vp)
                    dsc = (p * (dp - _lane_col(dsv, h))).astype(BF)
                    p_scr[blk, h] = p.astype(BF)
                    dsc_scr[blk, h] = dsc
                    outs.append(_dot(dsc, kp))
                if len(outs) == 2:
                    dq = jnp.where(_head_mask(outs[0].shape, 0), outs[0], outs[1])
                else:
                    dq = jnp.where(_head_mask(outs[0].shape, 0), outs[0], 0.0)
                o_ref[0, rows, sl] = dq.astype(BF)

        def k_block_kept(m):
            rows = pl.ds(m * ATT_W, ATT_W)
            two = pl.ds(m * ATT_W, 2 * ATT_W)
            for hp in range(C // 128):
                sl = slice(hp * 128, (hp + 1) * 128)
                qp = q_ref[two, sl]
                dop = do_ref[two, sl]
                dk, dv = None, None
                for h in range(2 * hp, min(2 * hp + 2, nh)):
                    hm = _head_mask(qp.shape, h)
                    qm = jnp.where(hm, qp, jnp.zeros_like(qp))
                    dom = jnp.where(hm, dop, jnp.zeros_like(dop))
                    dsc = jnp.concatenate([dsc_scr[m, h, :, ATT_W:], dsc_scr[m + 1, h, :, :ATT_W]], axis=0)
                    p = jnp.concatenate([p_scr[m, h, :, ATT_W:], p_scr[m + 1, h, :, :ATT_W]], axis=0)
                    dk_h = _dot_tn(dsc, qm)
                    dv_h = _dot_tn(p, dom)
                    dk = dk_h if dk is None else dk + dk_h
                    dv = dv_h if dv is None else dv + dv_h
                o_ref[1, rows, sl] = dk.astype(BF)
                o_ref[2, rows, sl] = dv.astype(BF)

        def k_block(row0, qq, doo, stv, dsv, mask):
            rows = pl.ds(row0, ATT_W)
            for hp in range(C // 128):
                sl = slice(hp * 128, (hp + 1) * 128)
                qp = qq[:, sl]
                dop = doo[:, sl]
                kp = k_ref[rows, sl]
                vp = v_ref[rows, sl]
                dks, dvs = [], []
                for h in range(2 * hp, min(2 * hp + 2, nh)):
                    hm = _head_mask(qp.shape, h)
                    qm = jnp.where(hm, qp, jnp.zeros_like(qp))
                    dom = jnp.where(hm, dop, jnp.zeros_like(dop))
                    p = probs(qm, kp, _lane_col(stv, h), mask)
                    dp = _dot_nt(dom, vp)
                    dsc = (p * (dp - _lane_col(dsv, h))).astype(BF)
                    dks.append(_dot_tn(dsc, qm))
                    dvs.append(_dot_tn(p.astype(BF), dom))
                o_ref[1, rows, sl] = sum(dks[1:], dks[0]).astype(BF)
                o_ref[2, rows, sl] = sum(dvs[1:], dvs[0]).astype(BF)

        first_mask = jnp.logical_and(band_q, jnp.logical_or(kj >= ATT_W, i > 0))
        q_block(0, jnp.concatenate([kh_ref[...], k_ref[pl.ds(0, ATT_W), :]], axis=0),
                jnp.concatenate([vh_ref[...], v_ref[pl.ds(0, ATT_W), :]], axis=0), first_mask)
        for blk in range(1, nblk):
            prev = pl.ds((blk - 1) * ATT_W, 2 * ATT_W)
            q_block(blk, k_ref[prev, :], v_ref[prev, :], band_q)
        for m in range(nblk - 1):
            k_block_kept(m)

        last = pl.ds((nblk - 1) * ATT_W, ATT_W)
        last_mask = jnp.logical_and(band_k, jnp.logical_or(qa < ATT_W, i < nchunk - 1))
        k_block((nblk - 1) * ATT_W,
                jnp.concatenate([q_ref[last, :], qn_ref[...]], axis=0),
                jnp.concatenate([do_ref[last, :], don_ref[...]], axis=0),
                jnp.concatenate([st_ref[last, :], stn_ref[...]], axis=0),
                jnp.concatenate([ds_ref[last, :], dsn_ref[...]], axis=0), last_mask)

    nb_all = L // ATT_W
    main4 = lambda w: pl.BlockSpec((None, None, lc, C), lambda r, i: (w, r, i, 0))
    prev4 = lambda w: pl.BlockSpec((None, None, ATT_W, C), lambda r, i: (w, r, jnp.maximum(i * nblk - 1, 0), 0))
    next4 = lambda w: pl.BlockSpec((None, None, ATT_W, C), lambda r, i: (w, r, jnp.minimum((i + 1) * nblk, nb_all - 1), 0))
    main3 = lambda n: pl.BlockSpec((None, lc, n), lambda r, i: (r, i, 0))
    next3 = lambda n: pl.BlockSpec((None, ATT_W, n), lambda r, i: (r, jnp.minimum((i + 1) * nblk, nb_all - 1), 0))
    return pl.pallas_call(
        body, name=name, grid=(d, nchunk),
        in_specs=[main4(0), next4(0), main4(1), prev4(1), main4(2), prev4(2),
                  main3(C), next3(C), main3(128), next3(128), main3(128), next3(128)],
        out_specs=pl.BlockSpec((3, None, lc, C), lambda r, i: (0, r, i, 0)),
        out_shape=jax.ShapeDtypeStruct((3, d, L, C), BF),
        scratch_shapes=[pltpu.VMEM((nblk, nh, ATT_W, 2 * ATT_W), BF)] * 2,
        compiler_params=_params("parallel", "arbitrary"),
    )(qkv, qkv, qkv, qkv, qkv, qkv, do, do, st, st, dst, dst)


def _alpha_from(lse_nat):
    m = jnp.maximum(jnp.maximum(lse_nat[0], lse_nat[1]), lse_nat[2])
    e = [jnp.exp(l - m) for l in lse_nat]
    inv = 1.0 / (e[0] + e[1] + e[2])
    return [ei * inv for ei in e]


def _attn_out_fwd(x, os_, sts, wo, tm=512):
    T = x.shape[0]
    C = GROUP_LANES

    def body(x_ref, o0, o1, o2, s0, s1, s2, w_ref, xo_ref, m_ref, al_ref, oscr, sscr):
        o_refs, st_refs = (o0, o1, o2), (s0, s1, s2)
        lses = []
        for g, d in enumerate(DILATIONS):
            _gather_rows(sscr.at[g], st_refs[g], d)
            lses.append(_lane_col(sscr[g, 0], LSE_GROUP_LANE))
        alpha = _alpha_from(lses)
        for g, d in enumerate(DILATIONS):
            _gather_rows(oscr, o_refs[g], d)
            m_ref[:, g * C:(g + 1) * C] = (_chunks_to_rows(oscr) * (3.0 * alpha[g])).astype(BF)
        xo_ref[...] = x_ref[...] + _dot(m_ref[...], w_ref[...])
        lane = lax.broadcasted_iota(jnp.int32, (tm, 128), 1)
        al_ref[...] = jnp.where(lane == 0, alpha[0], jnp.where(lane == 1, alpha[1], jnp.where(lane == 2, alpha[2], 0.0)))

    o_specs = [pl.BlockSpec((d, tm // d, C), lambda i: (0, i, 0)) for d in DILATIONS]
    st_specs = [pl.BlockSpec((d, tm // d, 128), lambda i: (0, i, 0)) for d in DILATIONS]
    return pl.pallas_call(
        body, name="attn_out_fwd", grid=(T // tm,),
        in_specs=[_row_spec(tm, D)] + o_specs + st_specs + [_full_spec((3 * C, D))],
        out_specs=[_row_spec(tm, D), _row_spec(tm, 3 * C), _row_spec(tm, 128)],
        out_shape=[jax.ShapeDtypeStruct((T, D), F32), jax.ShapeDtypeStruct((T, 3 * C), BF), jax.ShapeDtypeStruct((T, 128), F32)],
        scratch_shapes=[pltpu.VMEM((C // 128, tm, 128), F32), pltpu.VMEM((3, 1, tm, 128), F32)],
        compiler_params=_params("parallel"),
    )(x, *os_, *sts, wo)


def _attn_out_bwd(dx, os_, sts, alpha, wo, tm=512, dep=None):
    T = dx.shape[0]
    C = GROUP_LANES

    def body(dx_ref, o0, o1, o2, s0, s1, s2, al_ref, w_ref, do0, do1, do2, ds0, ds1, ds2, dmscr, oscr, sscr, tscr):
        o_refs, st_refs = (o0, o1, o2), (s0, s1, s2)
        do_refs, ds_refs = (do0, do1, do2), (ds0, ds1, ds2)
        dyb = dx_ref[...].astype(BF)
        alv = al_ref[...]
        alpha_g = [_lane_col(alv, g) for g in range(3)]
        dalpha = []
        for g, d in enumerate(DILATIONS):
            dm = _dot_nt(dyb, w_ref[g * C:(g + 1) * C, :])
            _rows_to_chunks(dmscr.at[g], dm)
            _gather_rows(oscr.at[g], o_refs[g], d)
            _gather_rows(sscr.at[g], st_refs[g], d)
            dalpha.append(3.0 * jnp.sum(dm * _chunks_to_rows(oscr.at[g]), axis=-1, keepdims=True))
        mean_da = alpha_g[0] * dalpha[0] + alpha_g[1] * dalpha[1] + alpha_g[2] * dalpha[2]
        lane = lax.broadcasted_iota(jnp.int32, (tm, 128), 1)
        seg = (lax.broadcasted_iota(jnp.int32, (C, 128), 0) // HEAD_DIM == lax.broadcasted_iota(jnp.int32, (C, 128), 1)).astype(BF)
        for g, d in enumerate(DILATIONS):
            nh = HEAD_GROUPS[g]
            dlse_g = alpha_g[g] * (dalpha[g] - mean_da)
            do_nat = _chunks_to_rows(dmscr.at[g]) * (3.0 * alpha_g[g])
            prod = do_nat * _chunks_to_rows(oscr.at[g])
            hi = prod.astype(BF)
            delta = _dot(hi, seg) + _dot((prod - hi.astype(F32)).astype(BF), seg)
            stv = sscr[g, 0]
            dlse = dlse_g * jnp.exp(stv - _lane_col(stv, LSE_GROUP_LANE)) * (1.0 / nh)
            tile = jnp.where(lane < nh, delta - dlse, 0.0)
            _rows_to_chunks(dmscr.at[g], do_nat)
            _scatter_rows(do_refs[g], dmscr.at[g], d, BF)
            tscr[0] = tile
            _scatter_rows(ds_refs[g], tscr, d, F32)

    o_specs = [pl.BlockSpec((d, tm // d, C), lambda i: (0, i, 0)) for d in DILATIONS]
    st_specs = [pl.BlockSpec((d, tm // d, 128), lambda i: (0, i, 0)) for d in DILATIONS]
    body, in_specs, args = _after(
        dep, body, [_row_spec(tm, D)] + o_specs + st_specs + [_row_spec(tm, 128)] + [_full_spec((3 * C, D))],
        [dx, *os_, *sts, alpha, wo])
    return pl.pallas_call(
        body, name="attn_out_bwd", grid=(T // tm,),
        in_specs=in_specs,
        out_specs=o_specs + st_specs,
        out_shape=[jax.ShapeDtypeStruct((d, T // d, C), BF) for d in DILATIONS]
        + [jax.ShapeDtypeStruct((d, T // d, 128), F32) for d in DILATIONS],
        scratch_shapes=[pltpu.VMEM((3, C // 128, tm, 128), F32), pltpu.VMEM((3, C // 128, tm, 128), F32),
                        pltpu.VMEM((3, 1, tm, 128), F32), pltpu.VMEM((1, tm, 128), F32)],
        compiler_params=_params("parallel"),
    )(*args)


def _qkv_bwd(dqkvs, cos, sin, x2, g, w3s, dx3, tm=512):
    T = x2.shape[0]
    C = GROUP_LANES

    def body(dq0, dq1, dq2, cos_ref, sin_ref, x_ref, g_ref, w0, w1, w2, dx3_ref,
             dx_ref, n0, n1, n2, h_ref, dg_ref, scr, dh_ref):
        dq_refs, w_refs, n_refs = (dq0, dq1, dq2), (w0, w1, w2), (n0, n1, n2)

        @pl.when(pl.program_id(0) == 0)
        def _():
            dg_ref[...] = jnp.zeros_like(dg_ref)

        cv = cos_ref[...]
        sv = sin_ref[...]
        dh_ref[...] = jnp.zeros_like(dh_ref)
        for gi, d in enumerate(DILATIONS):
            for w in range(3):
                _gather_rows(scr, dq_refs[gi].at[w], d)
                if w < 2:
                    scale = HEAD_DIM ** -0.5 if w == 0 else 1.0
                    for c in range(C // 128):
                        t = scr[c]
                        scr[c] = (t * cv - _rot_half(t) * sv) * scale
                n_refs[gi][:, w * C:(w + 1) * C] = _chunks_to_rows(scr).astype(BF)
            dh_ref[...] += _dot_nt(n_refs[gi][...], w_refs[gi][...])
        xv = x_ref[...]
        h_ref[...] = _rms_fwd(xv, g_ref[...]).astype(BF)
        dx, dg = _rms_bwd(xv, g_ref[...], dh_ref[...])
        dx_ref[...] = dx3_ref[...] + dx
        dg_ref[...] += dg

    dq_specs = [pl.BlockSpec((3, d, tm // d, C), lambda i: (0, 0, i, 0)) for d in DILATIONS]
    nat = _row_spec(tm, 3 * C)
    nshape = jax.ShapeDtypeStruct((T, 3 * C), BF)
    return pl.pallas_call(
        body, name="qkv_bwd", grid=(T // tm,),
        in_specs=dq_specs + [_row_spec(tm, 128), _row_spec(tm, 128), _row_spec(tm, D), _full_spec((1, D))]
        + [pl.BlockSpec((D, 3 * C), lambda i: (0, 0), pipeline_mode=pl.Buffered(1))] * 3 + [_row_spec(tm, D)],
        out_specs=[_row_spec(tm, D), nat, nat, nat, _row_spec(tm, D), _full_spec((1, D))],
        out_shape=[jax.ShapeDtypeStruct((T, D), F32), nshape, nshape, nshape, jax.ShapeDtypeStruct((T, D), BF),
                   jax.ShapeDtypeStruct((1, D), F32)],
        scratch_shapes=[pltpu.VMEM((C // 128, tm, 128), F32), pltpu.VMEM((tm, D), F32)],
        compiler_params=_params("arbitrary"),
    )(*dqkvs, cos, sin, x2, g, *w3s, dx3)


def _place():
    x, y, c = lax.axis_index("x"), lax.axis_index("y"), lax.axis_index("c")
    return x, y, c


def _gather_phases(ins, outs, *sems):
    n = len(ins)
    if n == 0:
        return (lambda: None), (lambda: None)
    send_sems, recv_sems, local_sems = sems
    x, y, c = _place()
    me, sibling = (x, y, c), (x, y, 1 - c)
    chips = [(1 - x, y), (x, 1 - y), (1 - x, 1 - y)]

    def slot(a, px, py, pc):
        return outs[a].at[4 * px + 2 * py + pc]

    def copy(a, k, block, to, src=None):
        return pltpu.make_async_remote_copy(
            src_ref=slot(a, *block) if src is None else src, dst_ref=slot(a, *block),
            send_sem=send_sems.at[a, k], recv_sem=recv_sems.at[a, k], device_id=to, device_id_type=MESH)

    def mine(a):
        return pltpu.make_async_copy(ins[a], slot(a, *me), local_sems.at[a])

    def first(a):
        return [copy(a, 0, me, sibling, src=ins[a])] + [copy(a, 1 + j, me, (*chip, c), src=ins[a]) for j, chip in enumerate(chips)]

    def begin():
        for a in range(n):
            mine(a).start()
        for a in range(n):
            for cp in first(a):
                cp.start()

    def finish():
        passed = []
        for a in range(n):
            for j, chip in enumerate(chips):
                copy(a, 1 + j, (*chip, c), me).wait_recv()
                fwd = copy(a, 4 + j, (*chip, c), sibling)
                fwd.start()
                passed.append(fwd)
        for a in range(n):
            copy(a, 0, sibling, me).wait_recv()
            for j, chip in enumerate(chips):
                copy(a, 4 + j, (*chip, 1 - c), me).wait_recv()
        for a in range(n):
            for cp in first(a):
                cp.wait_send()
        for cp in passed:
            cp.wait_send()
        for a in range(n):
            mine(a).wait()

    return begin, finish


def _gather_scratch(n):
    return [pltpu.SemaphoreType.DMA((n, 7)), pltpu.SemaphoreType.DMA((n, 7)), pltpu.SemaphoreType.DMA((n,))] if n else []


def _allgather(arrs, name):
    n = len(arrs)

    def body(*refs):
        begin, finish = _gather_phases(refs[:n], refs[n:2 * n], *refs[2 * n:])
        begin()
        finish()

    hbm = pl.BlockSpec(memory_space=pl.ANY)
    return pl.pallas_call(
        body, name=name,
        in_specs=[hbm] * n, out_specs=[hbm] * n,
        out_shape=[jax.ShapeDtypeStruct((N_DEV,) + a.shape, a.dtype) for a in arrs],
        scratch_shapes=_gather_scratch(n),
    )(*arrs)


def _peer(k):
    x, y, c = _place()
    px = 1 - x if k & 4 else x
    py = 1 - y if k & 2 else y
    pc = 1 - c if k & 1 else c
    return (px, py, pc), 4 * px + 2 * py + pc


HBM_SPEC = pl.BlockSpec(memory_space=pltpu.HBM)
SEM_SPEC = pl.BlockSpec(memory_space=pltpu.SEMAPHORE)
EFFECT = pltpu.SideEffectType.DATAFLOW_SIDE_EFFECTING


def _exchange_start(arrs, name, same_block=False, dep=None):
    n = len(arrs)
    n_dep = 0 if dep is None else 1

    def body(*refs):
        srcs, lands = refs[:n], refs[n:2 * n]
        send_sems, recv_sems, local_sems = refs[2 * n + n_dep:2 * n + n_dep + 3]
        token = refs[-1]
        x, y, c = _place()
        me = 4 * x + 2 * y + c
        block = (lambda a, j: srcs[a]) if same_block else (lambda a, j: srcs[a].at[j])
        for a in range(n):
            pltpu.make_async_copy(block(a, me), lands[a].at[me], local_sems.at[a]).start()
        for a in range(n):
            for k in range(1, N_DEV):
                to, to_idx = _peer(k)
                pltpu.make_async_remote_copy(
                    src_ref=block(a, to_idx), dst_ref=lands[a].at[me],
                    send_sem=send_sems.at[a * (N_DEV - 1) + k - 1], recv_sem=recv_sems.at[a * (N_DEV - 1) + k - 1], device_id=to, device_id_type=MESH).start()
        token[...] = jnp.zeros_like(token)

    land_shape = (lambda a: (N_DEV,) + a.shape) if same_block else (lambda a: a.shape)
    src_shapes = [pltpu.HBM(a.shape, a.dtype) for a in arrs]
    land_shapes = [pltpu.HBM(land_shape(a), a.dtype) for a in arrs]
    outs = pl.pallas_call(
        body, name=name,
        out_shape=(pltpu.SemaphoreType.DMA((n * (N_DEV - 1),)), pltpu.SemaphoreType.DMA((n * (N_DEV - 1),)),
                   pltpu.SemaphoreType.DMA((n,)), *src_shapes, *land_shapes, jax.ShapeDtypeStruct((8, 128), F32)),
        in_specs=[HBM_SPEC] * (2 * n) + [pl.BlockSpec(memory_space=pl.ANY)] * n_dep,
        out_specs=(SEM_SPEC, SEM_SPEC, SEM_SPEC, *([HBM_SPEC] * (2 * n)), pl.BlockSpec(memory_space=pltpu.VMEM)),
        input_output_aliases={i: 3 + i for i in range(2 * n)},
        compiler_params=pltpu.CompilerParams(has_side_effects=EFFECT),
    )(*[pltpu.with_memory_space_constraint(a, pltpu.HBM) for a in arrs],
      *[pltpu.with_memory_space_constraint(lax.empty(land_shape(a), a.dtype), pltpu.HBM) for a in arrs],
      *([] if dep is None else [dep]))
    return outs[0], outs[1], outs[2], outs[3:3 + n], outs[3 + n:3 + 2 * n], outs[3 + 2 * n]


def _exchange_wait(send_sems, recv_sems, local_sems, src_thru, land_thru, after, name):
    n = len(src_thru)
    after = list(after) if isinstance(after, (list, tuple)) else [after]
    same_block = src_thru[0].shape != land_thru[0].shape

    def body(*refs):
        srcs, lands = refs[:n], refs[n:2 * n]
        send_sems, recv_sems, local_sems = refs[2 * n:2 * n + 3]
        x, y, c = _place()
        me = 4 * x + 2 * y + c
        for a in range(n):
            pltpu.make_async_copy(srcs[a] if same_block else srcs[a].at[me], lands[a].at[me], local_sems.at[a]).wait()
            for k in range(1, N_DEV):
                frm, frm_idx = _peer(k)
                cp = pltpu.make_async_remote_copy(
                    src_ref=srcs[a] if same_block else srcs[a].at[frm_idx], dst_ref=lands[a].at[frm_idx],
                    send_sem=send_sems.at[a * (N_DEV - 1) + k - 1], recv_sem=recv_sems.at[a * (N_DEV - 1) + k - 1], device_id=frm, device_id_type=MESH)
                cp.wait_send()
                cp.wait_recv()

    outs = pl.pallas_call(
        body, name=name,
        out_shape=tuple(pltpu.HBM(a.shape, a.dtype) for a in (*src_thru, *land_thru)),
        in_specs=[HBM_SPEC] * (2 * n) + [SEM_SPEC, SEM_SPEC, SEM_SPEC] + [pl.BlockSpec(memory_space=pl.ANY)] * len(after),
        out_specs=[HBM_SPEC] * (2 * n),
        input_output_aliases={i: i for i in range(2 * n)},
        compiler_params=pltpu.CompilerParams(has_side_effects=EFFECT),
    )(*src_thru, *land_thru, send_sems, recv_sems, local_sems, *after)
    return outs[n:]


def _row_tile(rows):
    for t in (256, 176, 128, 8):
        if rows % t == 0:
            return t
    return rows


def _adamw(parts, w, m, v, name, transposed=False, layer=None, into=None):
    K = parts.shape[0]
    R, C = w.shape
    tr = 256 if transposed else _row_tile(R)
    n_into = 0 if into is None else 4

    def body(p_ref, w_ref, m_ref, v_ref, *refs):
        g_ref, d_ref, nm_ref, nv_ref = refs[n_into:]
        g = p_ref[0].astype(F32)
        for k in range(1, K):
            g = g + p_ref[k].astype(F32)
        if transposed:
            g = g.T
        nm = ADAM_B1 * m_ref[...] + (1.0 - ADAM_B1) * g
        nv = ADAM_B2 * v_ref[...] + (1.0 - ADAM_B2) * jnp.square(g)
        m_hat = nm / (1.0 - ADAM_B1 ** ADAM_STEP)
        v_hat = nv / (1.0 - ADAM_B2 ** ADAM_STEP)
        g_ref[...] = g
        d_ref[...] = -ADAM_LR * (m_hat / (jnp.sqrt(v_hat) + ADAM_EPS) + ADAM_WD * w_ref[...])
        nm_ref[...] = nm
        nv_ref[...] = nv

    blk = _row_spec(tr, C)
    parts_spec = pl.BlockSpec((K, C, tr), lambda i: (0, 0, i)) if transposed else pl.BlockSpec((K, tr, C), lambda i: (0, i, 0))
    if layer is None:
        out_blk, shp = blk, jax.ShapeDtypeStruct((R, C), F32)
    else:
        out_blk, shp = pl.BlockSpec((None, tr, C), lambda i: (layer, i, 0)), jax.ShapeDtypeStruct((2, R, C), F32)
    return pl.pallas_call(
        body, name=name, grid=(R // tr,),
        in_specs=[parts_spec, blk, blk, blk] + [pl.BlockSpec(memory_space=pl.ANY)] * n_into,
        out_specs=[out_blk] * 4,
        out_shape=[shp] * 4,
        input_output_aliases={4 + j: j for j in range(n_into)},
        compiler_params=_params("parallel"),
    )(parts, w, m, v, *(into or ()))


def _rope_tables(T):
    inv_freq = 1.0 / (ROPE_THETA ** (jnp.arange(0, HEAD_DIM, 2, dtype=F32) / HEAD_DIM))
    hi = (64.0 * jnp.arange(T // 64, dtype=F32))[:, None] * inv_freq[None, :]
    lo = jnp.arange(64, dtype=F32)[:, None] * inv_freq[None, :]
    ch, sh, cl, sl = jnp.cos(hi)[:, None], jnp.sin(hi)[:, None], jnp.cos(lo)[None], jnp.sin(lo)[None]
    cos = (ch * cl - sh * sl).reshape(T, HEAD_DIM // 2)
    sin = (sh * cl + ch * sl).reshape(T, HEAD_DIM // 2)
    return jnp.concatenate([cos] * 4, axis=-1), jnp.concatenate([sin] * 4, axis=-1)


def _pad_lanes(a, n):
    return jnp.pad(a, ((0, 0),) * (a.ndim - 1) + ((0, n - a.shape[-1]),))


POOL = ("pool_in", "pool_grp", "pool_out")
FFN0 = ("gate0", "up0", "down0")
LAYER1 = ("qkv", "attn_out", "gate1", "up1", "down1")


def _layout_weights(gw):
    w = {}
    if "pool_in" in gw:
        w["pool_in"] = gw["pool_in"].reshape(D, D)
        w["pool_grp"] = jnp.transpose(gw["pool_grp"], (1, 0, 2, 3)).reshape(4, POOL_GC, POOL_GC)
        w["pool_out"] = gw["pool_out"].reshape(D, D)
    if "qkv" in gw:
        wqkv = jnp.transpose(gw["qkv"], (1, 0, 2)).reshape(D, 3 * D)
        wo = gw["attn_out"].reshape(D, D)
        w["qkv"], w["attn_out"] = [], []
        for nh, off in zip(HEAD_GROUPS, HEAD_OFFS):
            lo, n = off * HEAD_DIM, nh * HEAD_DIM
            w["qkv"].append(jnp.concatenate([_pad_lanes(wqkv[:, k * D + lo:k * D + lo + n], GROUP_LANES) for k in range(3)], axis=1))
            w["attn_out"].append(jnp.pad(wo[lo:lo + n], ((0, GROUP_LANES - n), (0, 0))))
        w["attn_out"] = jnp.concatenate(w["attn_out"], axis=0)
    for nm in gw:
        if nm[:-1] in ("gate", "up", "down"):
            w[nm] = gw[nm].reshape(F, D)
    return w


def _local_step(x, target, w, ffn0_shards, layer1_shards, norm_mix, norm_ffn, norm_final, pool_scale, emit):
    T = x.shape[0]
    cos, sin = _rope_tables(T)
    nm = [norm_mix[i:i + 1] for i in range(2)]
    nf = [norm_ffn[i:i + 1] for i in range(2)]
    nfin = norm_final.reshape(1, D)

    x1, p, *ffn0 = _pool_fwd(x, nm[0], w["pool_in"], w["pool_grp"], pool_scale, w["pool_out"], ffn0_shards)
    w = {**w, **_layout_weights(dict(zip(FFN0, ffn0)))}
    x2, a0, b0, s0, *layer1 = _ffn_fwd(x1, nf[0], w["gate0"], w["up0"], w["down0"], "ffn_fwd0", gather=layer1_shards)
    w = {**w, **_layout_weights(dict(zip(LAYER1, layer1)))}
    qkvs = _qkv_fwd(x2, nm[1], w["qkv"], cos, sin)
    att = [_attn_fwd(qkvs[g], HEAD_GROUPS[g], f"attn_fwd{g}") for g in range(3)]
    os_, sts = [a[0] for a in att], [a[1] for a in att]
    x3, merged, alpha = _attn_out_fwd(x2, os_, sts, w["attn_out"])
    dx4, a1, b1, s1, loss, dg_final = _ffn_fwd(x3, nf[1], w["gate1"], w["up1"], w["down1"], "ffn_fwd1", head=(target, nfin))

    g = {}
    dx3, da1, db1, h3, dy4, dg_ffn1 = _ffn_bwd(dx4, x3, nf[1], a1, b1, w["gate1"], w["up1"], w["down1"], "ffn_bwd1")
    g["down1"] = _wgrad(s1, dy4, "down_wgrad1", mblk=F // 2)
    g["gate1"] = _wgrad(da1, h3, "gate_wgrad1", mblk=F // 2)
    g["up1"] = _wgrad(db1, h3, "up_wgrad1", mblk=F // 2)
    dep = emit("ffn1", g)
    dos_and_stats = _attn_out_bwd(dx3, os_, sts, alpha, w["attn_out"], dep=dep)
    dos, dsts = dos_and_stats[:3], dos_and_stats[3:]
    g["attn_out"] = _wgrad(merged, dx3, "attn_out_wgrad")
    dqkvs = [_attn_bwd(qkvs[gi], dos[gi], sts[gi], dsts[gi], HEAD_GROUPS[gi], f"attn_bwd{gi}") for gi in range(3)]
    dx2, n0, n1, n2, h2b, dg_mix1 = _qkv_bwd(dqkvs, cos, sin, x2, nm[1], w["qkv"], dx3)
    g["qkv"] = [_wgrad(h2b, n, f"qkv_wgrad{i}") for i, n in enumerate((n0, n1, n2))]
    dep = emit("attn", g)
    dx1, da0, db0, h1, dy2, dg_ffn0 = _ffn_bwd(dx2, x1, nf[0], a0, b0, w["gate0"], w["up0"], w["down0"], "ffn_bwd0", dep=dep)
    g["down0"] = _wgrad(s0, dy2, "down_wgrad0", mblk=F // 2)
    g["gate0"] = _wgrad(da0, h1, "gate_wgrad0", mblk=F // 2)
    g["up0"] = _wgrad(db0, h1, "up_wgrad0", mblk=F // 2)
    dep = emit("ffn0", g)
    dx0, z, dzp, du, h0b, dscale, dg_mix0 = _pool_bwd(dx1, x, nm[0], p, w["pool_in"], w["pool_grp"], pool_scale, w["pool_out"], dep=dep)
    g["pool_out"] = _wgrad(z, dx1, "pool_out_wgrad")
    g["pool_in"] = _wgrad(h0b, du, "pool_in_wgrad")
    g["pool_grp"] = _wgrad_pool_groups(p, dzp)
    emit("pool", g)

    small = jnp.concatenate([dg_mix0, dg_mix1, dg_ffn0, dg_ffn1, dg_final, dscale,
                             jnp.broadcast_to(loss, (1, D)), jnp.zeros((1, D), F32)], axis=0)
    return dx0, small


GROUPS = {"ffn1": ("down1", "gate1", "up1"), "attn": ("qkv", "attn_out"), "ffn0": ("down0", "gate0", "up0"),
          "pool": ("pool_in", "pool_out", "pool_grp")}


def _grad_blocks(group, g):
    blocks = {}
    if group == "pool":
        blocks["pool_in"] = g["pool_in"].reshape(N_DEV, D // N_DEV, D)
        blocks["pool_out"] = g["pool_out"].reshape(N_DEV, D // N_DEV, D)
        blocks["pool_grp"] = jnp.transpose(g["pool_grp"].reshape(4, N_DEV, POOL_GC // N_DEV, POOL_GC), (1, 0, 2, 3)).reshape(N_DEV, 4 * POOL_GC // N_DEV, POOL_GC)
    elif group == "attn":
        wo = jnp.concatenate([g["attn_out"][gi * GROUP_LANES:gi * GROUP_LANES + nh * HEAD_DIM] for gi, nh in enumerate(HEAD_GROUPS)], axis=0)
        blocks["attn_out"] = wo.reshape(N_DEV, D // N_DEV, D)
        wqkv = jnp.concatenate([g["qkv"][gi][:, k * GROUP_LANES:k * GROUP_LANES + HEAD_GROUPS[gi] * HEAD_DIM]
                                for k in range(3) for gi in range(3)], axis=1)
        blocks["qkv"] = jnp.transpose(wqkv.reshape(D, N_DEV, 3 * D // N_DEV), (1, 0, 2))
    else:
        for nm in GROUPS[group]:
            blocks[nm] = g[nm].reshape(N_DEV, F // N_DEV, D)
    return [blocks[nm] for nm in GROUPS[group]]


def kernel(x, norm_mix, norm_ffn, norm_final, pool_w_in, pool_w_group, pool_scale, pool_w_out, attn_w_qkv, attn_w_out, ffn_w_gate, ffn_w_up, ffn_w_down, loss_target, m_norm_mix, m_norm_ffn, m_norm_final, m_pool_w_in, m_pool_w_group, m_pool_scale, m_pool_w_out, m_attn_w_qkv, m_attn_w_out, m_ffn_w_gate, m_ffn_w_up, m_ffn_w_down, v_norm_mix, v_norm_ffn, v_norm_final, v_pool_w_in, v_pool_w_group, v_pool_scale, v_pool_w_out, v_attn_w_qkv, v_attn_w_out, v_ffn_w_gate, v_ffn_w_up, v_ffn_w_down):
    shard = {
        "pool_in": pool_w_in[0], "pool_grp": pool_w_group[0], "pool_out": pool_w_out[0],
        "qkv": attn_w_qkv[0], "attn_out": attn_w_out[0],
    }
    for l in range(2):
        shard[f"gate{l}"] = ffn_w_gate[l].T
        shard[f"up{l}"] = ffn_w_up[l].T
        shard[f"down{l}"] = ffn_w_down[l]
    shard = {k: v.astype(BF) for k, v in shard.items()}
    w0 = _layout_weights(dict(zip(POOL, _allgather([shard[k] for k in POOL], "weights_allgather"))))

    started = {}

    def emit(group, g):
        started[group] = _exchange_start(_grad_blocks(group, g), f"grads_start_{group}")
        return started[group][5]

    grad_x, small = _local_step(x[0], loss_target[0], w0, [shard[k] for k in FFN0], [shard[k] for k in LAYER1], norm_mix, norm_ffn, norm_final, pool_scale, emit)

    small_st = _exchange_start([small], "small_start", same_block=True, dep=started["pool"][5])

    def finish(group, after):
        lands = _exchange_wait(*started[group][:5], after, f"grads_wait_{group}")
        return dict(zip(GROUPS[group], lands))

    def upd(parts, wt, mt, vt, name):
        shape = wt.shape
        r2 = lambda t: t.reshape(parts.shape[1:])
        outs = _adamw(parts, r2(wt), r2(mt), r2(vt), name)
        return [o.reshape(shape) for o in outs]

    def ffn_layer(rcv, l, other=None):
        out = {}
        for nm, wt, mt, vt in (("gate", ffn_w_gate, m_ffn_w_gate, v_ffn_w_gate), ("up", ffn_w_up, m_ffn_w_up, v_ffn_w_up),
                               ("down", ffn_w_down, m_ffn_w_down, v_ffn_w_down)):
            out[nm] = _adamw(rcv[f"{nm}{l}"], wt[l], mt[l], vt[l], f"adamw_{nm}{l}", transposed=nm != "down", layer=l,
                             into=None if other is None else other[nm])
        return out

    res = {}
    ffn1 = ffn_layer(finish("ffn1", small_st[5]), 1)
    rcv = finish("attn", [ffn1[nm][0] for nm in ffn1])
    res["attn_w_qkv"] = upd(rcv["qkv"], attn_w_qkv, m_attn_w_qkv, v_attn_w_qkv, "adamw_qkv")
    res["attn_w_out"] = upd(rcv["attn_out"], attn_w_out, m_attn_w_out, v_attn_w_out, "adamw_attn_out")
    ffn0 = ffn_layer(finish("ffn0", [res["attn_w_qkv"][0], res["attn_w_out"][0]]), 0, other=ffn1)
    rcv = finish("pool", [ffn0[nm][0] for nm in ffn0])
    res["pool_w_in"] = upd(rcv["pool_in"], pool_w_in, m_pool_w_in, v_pool_w_in, "adamw_pool_in")
    res["pool_w_group"] = upd(rcv["pool_grp"], pool_w_group, m_pool_w_group, v_pool_w_group, "adamw_pool_grp")
    res["pool_w_out"] = upd(rcv["pool_out"], pool_w_out, m_pool_w_out, v_pool_w_out, "adamw_pool_out")
    for nm in ("gate", "up", "down"):
        res[f"ffn_w_{nm}"] = ffn0[nm]
    small_all = _exchange_wait(*small_st[:5], [res[k][0] for k in ("pool_w_in", "pool_w_group", "pool_w_out")], "small_wait")[0]

    small_w = jnp.concatenate([norm_mix, norm_ffn, norm_final[None], pool_scale, jnp.zeros((2, D), F32)], axis=0)
    small_m = jnp.concatenate([m_norm_mix, m_norm_ffn, m_norm_final[None], m_pool_scale, jnp.zeros((2, D), F32)], axis=0)
    small_v = jnp.concatenate([v_norm_mix, v_norm_ffn, v_norm_final[None], v_pool_scale, jnp.ones((2, D), F32)], axis=0)
    sg, sd, sm, sv = _adamw(small_all, small_w, small_m, small_v, "adamw_small")
    loss = sg[6, 0]
    res["norm_mix"] = [t[0:2] for t in (sg, sd, sm, sv)]
    res["norm_ffn"] = [t[2:4] for t in (sg, sd, sm, sv)]
    res["norm_final"] = [t[4] for t in (sg, sd, sm, sv)]
    res["pool_scale"] = [t[5:6] for t in (sg, sd, sm, sv)]

    order = ["norm_mix", "norm_ffn", "norm_final", "pool_w_in", "pool_w_group", "pool_scale", "pool_w_out",
             "attn_w_qkv", "attn_w_out", "ffn_w_gate", "ffn_w_up", "ffn_w_down"]
    return (loss, grad_x[None], *[res[k][0] for k in order], *[res[k][1] for k in order],
            *[res[k][2] for k in order], *[res[k][3] for k in order])
```

```python
import math

import jax
import jax.numpy as jnp
from jax import lax
from jax.experimental import pallas as pl
from jax.experimental.pallas import tpu as pltpu

D = 1024
F = 2816
N_DEV = 8
EPS = 1e-6
NEG_INF = -1e30
POOL_WINDOWS = (2, 4, 8, 16)
POOL_HALO = 16
POOL_GC = 256
HEAD_DIM = 64
HEAD_GROUPS = (6, 5, 5)
HEAD_OFFS = (0, 6, 11)
DILATIONS = (1, 4, 16)
ATT_W = 128
GROUP_LANES = 384
LSE_GROUP_LANE = 8
ROPE_THETA = 10000.0
ADAM_LR, ADAM_B1, ADAM_B2, ADAM_EPS, ADAM_WD, ADAM_STEP = 0.001, 0.9, 0.999, 1e-08, 0.01, 10

BF = jnp.bfloat16
F32 = jnp.float32
VMEM_LIMIT = 56 * 1024 * 1024
MESH = pl.DeviceIdType.MESH


def _params(*sem):
    return pltpu.CompilerParams(dimension_semantics=sem, vmem_limit_bytes=VMEM_LIMIT)


def _dot(a, b):
    return jnp.dot(a, b, preferred_element_type=F32)


def _dot_nt(a, b):
    return lax.dot_general(a, b, (((1,), (1,)), ((), ())), preferred_element_type=F32)


def _dot_tn(a, b):
    return lax.dot_general(a, b, (((0,), (0,)), ((), ())), preferred_element_type=F32)


def _rms_fwd(xv, g):
    r = lax.rsqrt(jnp.mean(xv * xv, axis=-1, keepdims=True) + EPS)
    return (xv * r) * g


def _rms_bwd(xv, g, dh):
    r = lax.rsqrt(jnp.mean(xv * xv, axis=-1, keepdims=True) + EPS)
    xhat = xv * r
    dg = jnp.sum(dh * xhat, axis=0, keepdims=True)
    dxh = dh * g
    dx = r * (dxh - xhat * jnp.mean(dxh * xhat, axis=-1, keepdims=True))
    return dx, dg


def _lane_col(tile, j):
    lane = lax.broadcasted_iota(jnp.int32, tile.shape, 1)
    return jnp.sum(jnp.where(lane == j, tile, 0.0), axis=-1, keepdims=True)


def _row_spec(tm, n):
    return pl.BlockSpec((tm, n), lambda i: (i, 0))


def _full_spec(shape):
    nd = len(shape)
    return pl.BlockSpec(shape, lambda *_: (0,) * nd)


def _after(dep, body, in_specs, args):
    if dep is None:
        return body, list(in_specs), list(args)

    def body_after(dep_ref, *refs):
        body(*refs)

    return body_after, [pl.BlockSpec(memory_space=pl.ANY)] + list(in_specs), [dep] + list(args)


def _pool_fwd(x, g, w_in, w_grp, scale, w_out, gather, tm=512, dep=None):
    T = x.shape[0]
    nt = T // tm
    n = tm + POOL_HALO
    ng = len(gather)

    def body(x_ref, g_ref, win_ref, wg_ref, sc_ref, wout_ref, *refs):
        shard_refs, (x1_ref, p_ref), full_refs = refs[:ng], refs[ng:ng + 2], refs[ng + 2:2 * ng + 2]
        tail_ref, z_ref = refs[2 * ng + 2:2 * ng + 4]
        begin, finish = _gather_phases(shard_refs, full_refs, *refs[2 * ng + 4:])
        i = pl.program_id(0)

        @pl.when(i == 0)
        def _():
            begin()
            tail_ref[...] = jnp.zeros_like(tail_ref)

        u = _dot(_rms_fwd(x_ref[...], g_ref[...]).astype(BF), win_ref[...])
        pos = i * tm + lax.broadcasted_iota(jnp.int32, (tm, 1), 0)
        for g, w in enumerate(POOL_WINDOWS):
            sl = slice(g * POOL_GC, (g + 1) * POOL_GC)
            ug = u[:, sl]
            s = jnp.concatenate([tail_ref[:, sl], ug], axis=0)
            step = 1
            while step < w:
                s = s + pltpu.roll(s, step, 0)
                step *= 2
            cnt = jnp.minimum(pos + 1, w).astype(F32)
            pg = (s[POOL_HALO:, :] / cnt - ug).astype(BF)
            p_ref[:, sl] = pg
            z_ref[:, sl] = (_dot(pg, wg_ref[g]) * sc_ref[:, sl]).astype(BF)
        tail_ref[...] = u[tm - POOL_HALO:, :]
        x1_ref[...] = x_ref[...] + _dot(z_ref[...], wout_ref[...])

        @pl.when(i == nt - 1)
        def _():
            finish()

    hbm = pl.BlockSpec(memory_space=pl.ANY)
    body, in_specs, args = _after(
        dep, body,
        [_row_spec(tm, D), _full_spec((1, D)), _full_spec((D, D)), _full_spec((4, POOL_GC, POOL_GC)), _full_spec((1, D)),
         _full_spec((D, D))] + [hbm] * ng,
        [x, g, w_in, w_grp, scale, w_out, *gather])
    return pl.pallas_call(
        body, name="pool_fwd", grid=(nt,),
        in_specs=in_specs,
        out_specs=[_row_spec(tm, D), _row_spec(tm, D)] + [hbm] * ng,
        out_shape=[jax.ShapeDtypeStruct((T, D), F32), jax.ShapeDtypeStruct((T, D), BF)]
        + [jax.ShapeDtypeStruct((N_DEV,) + a.shape, a.dtype) for a in gather],
        scratch_shapes=[pltpu.VMEM((POOL_HALO, D), F32), pltpu.VMEM((tm, D), BF)] + _gather_scratch(ng),
        compiler_params=_params("arbitrary"),
    )(*args)


def _pool_bwd(dx1, x0, g0, p, w_in, w_grp, scale, w_out, tm=512, dep=None):
    T = x0.shape[0]
    nt = T // tm
    n = tm + POOL_HALO
    rev = lambda i: (nt - 1 - i, 0)

    def body(dx1_ref, x0_ref, g_ref, p_ref, win_ref, wg_ref, sc_ref, wout_ref,
             dx0_ref, z_ref, dzp_ref, du_ref, h0_ref, dsc_ref, dg_ref, head_ref):
        i = pl.program_id(0)

        @pl.when(i == 0)
        def _():
            head_ref[...] = jnp.zeros_like(head_ref)
            dsc_ref[...] = jnp.zeros_like(dsc_ref)
            dg_ref[...] = jnp.zeros_like(dg_ref)

        dx1v = dx1_ref[...]
        dz = _dot_nt(dx1v.astype(BF), wout_ref[...])
        pos = (nt - 1 - i) * tm + lax.broadcasted_iota(jnp.int32, (tm, 1), 0)
        for g, w in enumerate(POOL_WINDOWS):
            sl = slice(g * POOL_GC, (g + 1) * POOL_GC)
            zpre = _dot(p_ref[:, sl], wg_ref[g])
            dzg = dz[:, sl]
            dsc_ref[:, sl] += jnp.sum(dzg * zpre, axis=0, keepdims=True)
            z_ref[:, sl] = (zpre * sc_ref[:, sl]).astype(BF)
            dzp = (dzg * sc_ref[:, sl]).astype(BF)
            dzp_ref[:, sl] = dzp
            dp = _dot_nt(dzp, wg_ref[g])
            cnt = jnp.minimum(pos + 1, w).astype(F32)
            dpc = dp / cnt
            s = jnp.concatenate([dpc, head_ref[:, sl]], axis=0)
            step = 1
            while step < w:
                s = s + pltpu.roll(s, n - step, 0)
                step *= 2
            head_ref[:, sl] = dpc[:POOL_HALO, :]
            du_ref[:, sl] = (s[:tm, :] - dp).astype(BF)
        dh0 = _dot_nt(du_ref[...], win_ref[...])
        x0v = x0_ref[...]
        h0_ref[...] = _rms_fwd(x0v, g_ref[...]).astype(BF)
        dx, dg = _rms_bwd(x0v, g_ref[...], dh0)
        dx0_ref[...] = dx1v + dx
        dg_ref[...] += dg

    bf_rows = jax.ShapeDtypeStruct((T, D), BF)
    vec = jax.ShapeDtypeStruct((1, D), F32)
    body, in_specs, args = _after(
        dep, body,
        [pl.BlockSpec((tm, D), rev), pl.BlockSpec((tm, D), rev), _full_spec((1, D)), pl.BlockSpec((tm, D), rev),
         _full_spec((D, D)), _full_spec((4, POOL_GC, POOL_GC)), _full_spec((1, D)), _full_spec((D, D))],
        [dx1, x0, g0, p, w_in, w_grp, scale, w_out])
    return pl.pallas_call(
        body, name="pool_bwd", grid=(nt,),
        in_specs=in_specs,
        out_specs=[pl.BlockSpec((tm, D), rev)] * 5 + [_full_spec((1, D))] * 2,
        out_shape=[jax.ShapeDtypeStruct((T, D), F32), bf_rows, bf_rows, bf_rows, bf_rows, vec, vec],
        scratch_shapes=[pltpu.VMEM((POOL_HALO, D), F32)],
        compiler_params=_params("arbitrary"),
    )(*args)


def _loss_head(xv, tv, gv):
    diff = _rms_fwd(xv, gv) - tv
    loss = 0.5 * jnp.sum(jnp.mean(diff * diff, axis=-1, keepdims=True), axis=0, keepdims=True)
    dx, dg = _rms_bwd(xv, gv, diff * (1.0 / D))
    return loss, dx, dg


def _ffn_fwd(x, g, wg_t, wu_t, wd, name, gather=(), head=None, tm=512, fk=1408):
    T = x.shape[0]
    ng = len(gather)
    nh_in, nh_out = (2, 2) if head is not None else (0, 0)
    ni, nk = T // tm, F // fk

    def body(x_ref, g_ref, wg_ref, wu_ref, wd_ref, *refs):
        head_in, refs = refs[:nh_in], refs[nh_in:]
        shard_refs, (xo_ref, a_ref, b_ref, s_ref), full_refs = refs[:ng], refs[ng:ng + 4], refs[ng + 4:2 * ng + 4]
        refs = refs[2 * ng + 4:]
        head_out, refs = refs[:nh_out], refs[nh_out:]
        acc_ref, h_ref = refs[:2]
        begin, finish = _gather_phases(shard_refs, full_refs, *refs[2:])
        i = pl.program_id(0)
        k = pl.program_id(1)

        @pl.when(jnp.logical_and(i == 0, k == 0))
        def _():
            begin()
            for r in head_out:
                r[...] = jnp.zeros_like(r)

        @pl.when(k == 0)
        def _():
            acc_ref[...] = jnp.zeros_like(acc_ref)
            h_ref[...] = _rms_fwd(x_ref[...], g_ref[...]).astype(BF)

        hv = h_ref[...]
        a = _dot_nt(hv, wg_ref[...])
        b = _dot_nt(hv, wu_ref[...])
        s = ((a * jax.nn.sigmoid(a)) * b).astype(BF)
        a_ref[...] = a.astype(BF)
        b_ref[...] = b.astype(BF)
        s_ref[...] = s
        acc_ref[...] += _dot(s, wd_ref[...])

        @pl.when(k == nk - 1)
        def _():
            xo = x_ref[...] + acc_ref[...]
            if head is None:
                xo_ref[...] = xo
            else:
                loss, dx, dg = _loss_head(xo, head_in[0][...], head_in[1][...])
                xo_ref[...] = dx
                head_out[0][...] += loss
                head_out[1][...] += dg

        @pl.when(jnp.logical_and(i == ni - 1, k == nk - 1))
        def _():
            finish()

    row = pl.BlockSpec((tm, D), lambda i, k: (i, 0))
    wsp = pl.BlockSpec((fk, D), lambda i, k: (k, 0))
    act = pl.BlockSpec((tm, fk), lambda i, k: (i, k))
    hbm = pl.BlockSpec(memory_space=pl.ANY)
    act_shape = jax.ShapeDtypeStruct((T, F), BF)
    vec = pl.BlockSpec((1, D), lambda i, k: (0, 0))
    one = pl.BlockSpec((1, 1), lambda i, k: (0, 0))
    return pl.pallas_call(
        body, name=name, grid=(ni, nk),
        in_specs=[row, vec, wsp, wsp, wsp] + [row, vec][:nh_in] + [hbm] * ng,
        out_specs=[row, act, act, act] + [hbm] * ng + [one, vec][:nh_out],
        out_shape=[jax.ShapeDtypeStruct((T, D), F32), act_shape, act_shape, act_shape]
        + [jax.ShapeDtypeStruct((N_DEV,) + a.shape, a.dtype) for a in gather]
        + [jax.ShapeDtypeStruct((1, 1), F32), jax.ShapeDtypeStruct((1, D), F32)][:nh_out],
        scratch_shapes=[pltpu.VMEM((tm, D), F32), pltpu.VMEM((tm, D), BF)] + _gather_scratch(ng),
        compiler_params=_params("arbitrary", "arbitrary"),
    )(x, g, wg_t, wu_t, wd, *(head or ()), *gather)


def _ffn_bwd(dxo, x_in, g, a, b, wg_t, wu_t, wd, name, tm=512, fk=1408, dep=None):
    T = x_in.shape[0]

    def body(dxo_ref, x_ref, g_ref, a_ref, b_ref, wg_ref, wu_ref, wd_ref,
             dx_ref, da_ref, db_ref, h_ref, dy_ref, dg_ref, dh_ref):
        i = pl.program_id(0)
        k = pl.program_id(1)

        @pl.when(jnp.logical_and(i == 0, k == 0))
        def _():
            dg_ref[...] = jnp.zeros_like(dg_ref)

        @pl.when(k == 0)
        def _():
            h_ref[...] = _rms_fwd(x_ref[...], g_ref[...]).astype(BF)
            dy_ref[...] = dxo_ref[...].astype(BF)
            dh_ref[...] = jnp.zeros_like(dh_ref)

        ds = _dot_nt(dy_ref[...], wd_ref[...])
        av = a_ref[...].astype(F32)
        bv = b_ref[...].astype(F32)
        sig = jax.nn.sigmoid(av)
        db = (ds * (av * sig)).astype(BF)
        da = (ds * bv * (sig * (1.0 + av * (1.0 - sig)))).astype(BF)
        da_ref[...] = da
        db_ref[...] = db
        dh_ref[...] += _dot(da, wg_ref[...]) + _dot(db, wu_ref[...])

        @pl.when(k == pl.num_programs(1) - 1)
        def _():
            dx, dg = _rms_bwd(x_ref[...], g_ref[...], dh_ref[...])
            dx_ref[...] = dxo_ref[...] + dx
            dg_ref[...] += dg

    row = pl.BlockSpec((tm, D), lambda i, k: (i, 0))
    wsp = pl.BlockSpec((fk, D), lambda i, k: (k, 0))
    act = pl.BlockSpec((tm, fk), lambda i, k: (i, k))
    vec = pl.BlockSpec((1, D), lambda i, k: (0, 0))
    act_shape = jax.ShapeDtypeStruct((T, F), BF)
    body, in_specs, args = _after(dep, body, [row, row, vec, act, act, wsp, wsp, wsp], [dxo, x_in, g, a, b, wg_t, wu_t, wd])
    return pl.pallas_call(
        body, name=name, grid=(T // tm, F // fk),
        in_specs=in_specs,
        out_specs=[row, act, act, row, row, vec],
        out_shape=[jax.ShapeDtypeStruct((T, D), F32), act_shape, act_shape, jax.ShapeDtypeStruct((T, D), BF),
                   jax.ShapeDtypeStruct((T, D), BF), jax.ShapeDtypeStruct((1, D), F32)],
        scratch_shapes=[pltpu.VMEM((tm, D), F32)],
        compiler_params=_params("arbitrary", "arbitrary"),
    )(*args)


def _wgrad(a, b, name, tk=1024, mblk=None):
    T, M = a.shape
    N = b.shape[1]
    mblk = M if mblk is None else mblk

    def body(a_ref, b_ref, o_ref, acc_ref):
        t = pl.program_id(1)

        @pl.when(t == 0)
        def _():
            acc_ref[...] = jnp.zeros_like(acc_ref)

        acc_ref[...] += _dot_tn(a_ref[...].astype(BF), b_ref[...].astype(BF))

        @pl.when(t == pl.num_programs(1) - 1)
        def _():
            o_ref[...] = acc_ref[...].astype(BF)

    return pl.pallas_call(
        body, name=name, grid=(M // mblk, T // tk),
        in_specs=[pl.BlockSpec((tk, mblk), lambda m, t: (t, m)), pl.BlockSpec((tk, N), lambda m, t: (t, 0))],
        out_specs=pl.BlockSpec((mblk, N), lambda m, t: (m, 0)),
        out_shape=jax.ShapeDtypeStruct((M, N), BF),
        scratch_shapes=[pltpu.VMEM((mblk, N), F32)],
        compiler_params=_params("parallel", "arbitrary"),
    )(a, b)


def _wgrad_pool_groups(p, dzp, tk=2048):
    T = p.shape[0]

    def body(p_ref, d_ref, o_ref, acc_ref):
        t = pl.program_id(1)

        @pl.when(t == 0)
        def _():
            acc_ref[...] = jnp.zeros_like(acc_ref)

        acc_ref[...] += _dot_tn(p_ref[...], d_ref[...])

        @pl.when(t == pl.num_programs(1) - 1)
        def _():
            o_ref[...] = acc_ref[...].astype(BF)

    blk = pl.BlockSpec((tk, POOL_GC), lambda g, t: (t, g))
    return pl.pallas_call(
        body, name="pool_wgrad_groups", grid=(4, T // tk),
        in_specs=[blk, blk],
        out_specs=pl.BlockSpec((None, POOL_GC, POOL_GC), lambda g, t: (g, 0, 0)),
        out_shape=jax.ShapeDtypeStruct((4, POOL_GC, POOL_GC), BF),
        scratch_shapes=[pltpu.VMEM((POOL_GC, POOL_GC), F32)],
        compiler_params=_params("parallel", "arbitrary"),
    )(p, dzp)


def _rot_half(t):
    lane = lax.broadcasted_iota(jnp.int32, t.shape, 1)
    first = (lane % HEAD_DIM) < (HEAD_DIM // 2)
    return jnp.where(first, -pltpu.roll(t, 128 - HEAD_DIM // 2, 1), pltpu.roll(t, HEAD_DIM // 2, 1))


def _scatter_rows(dst_ref, scr_ref, d, cast):
    nc, rows, _ = scr_ref.shape
    n = rows // d
    for c in range(nc):
        sl = slice(c * 128, (c + 1) * 128)
        for r in range(d):
            src = scr_ref[c] if d == 1 else scr_ref.at[c][pl.ds(r, n, stride=d), :]
            dst_ref[r, :, sl] = src.astype(cast)


def _gather_rows(scr_ref, src_ref, d):
    nc, rows, _ = scr_ref.shape
    n = rows // d
    for c in range(nc):
        sl = slice(c * 128, (c + 1) * 128)
        for r in range(d):
            val = src_ref[r, :, sl].astype(F32)
            if d == 1:
                scr_ref[c] = val
            else:
                scr_ref.at[c][pl.ds(r, n, stride=d), :] = val


def _chunks_to_rows(scr_ref):
    nc = scr_ref.shape[0]
    return scr_ref[0] if nc == 1 else jnp.concatenate([scr_ref[c] for c in range(nc)], axis=1)


def _rows_to_chunks(scr_ref, val):
    for c in range(scr_ref.shape[0]):
        scr_ref[c] = val[:, c * 128:(c + 1) * 128]


def _rope(t, cv, sv, scale):
    return (t * cv + _rot_half(t) * sv) * scale


def _qkv_fwd(x, g, w3s, cos, sin, tm=512):
    T = x.shape[0]
    C = GROUP_LANES
    nc = C // 128

    def body(x_ref, g_ref, w0, w1, w2, cos_ref, sin_ref, o0, o1, o2, scr_ref):
        h = _rms_fwd(x_ref[...], g_ref[...]).astype(BF)
        cv = cos_ref[...]
        sv = sin_ref[...]
        for gi, (w_ref, o_ref, d) in enumerate(zip((w0, w1, w2), (o0, o1, o2), DILATIONS)):
            t_all = _dot(h, w_ref[...])
            for w in range(3):
                chunks = [t_all[:, w * C + c * 128:w * C + (c + 1) * 128] for c in range(nc)]
                if w < 2:
                    chunks = [_rope(tc, cv, sv, HEAD_DIM ** -0.5 if w == 0 else 1.0) for tc in chunks]
                if d == 1:
                    for c in range(nc):
                        o_ref[w, 0, :, c * 128:(c + 1) * 128] = chunks[c].astype(BF)
                else:
                    scr = scr_ref.at[gi * 3 + w]
                    for c in range(nc):
                        scr[c] = chunks[c]
                    _scatter_rows(o_ref.at[w], scr, d, BF)

    return pl.pallas_call(
        body, name="qkv_fwd", grid=(T // tm,),
        in_specs=[_row_spec(tm, D), _full_spec((1, D))] + [_full_spec((D, 3 * C))] * 3 + [_row_spec(tm, 128), _row_spec(tm, 128)],
        out_specs=[pl.BlockSpec((3, d, tm // d, C), lambda i: (0, 0, i, 0)) for d in DILATIONS],
        out_shape=[jax.ShapeDtypeStruct((3, d, T // d, C), BF) for d in DILATIONS],
        scratch_shapes=[pltpu.VMEM((9, nc, tm, 128), F32)],
        compiler_params=_params("parallel"),
    )(x, g, *w3s, cos, sin)


def _att_chunk(L):
    return min(L, 2048)


def _head_mask(shape, h):
    lane = lax.broadcasted_iota(jnp.int32, shape, 1)
    return (lane // HEAD_DIM) == (h % 2)


def _attn_fwd(qkv, nh, name):
    _, d, L, C = qkv.shape
    lc = _att_chunk(L)
    nblk = lc // ATT_W

    def body(q_ref, k_ref, kh_ref, v_ref, vh_ref, o_ref, st_ref):
        i = pl.program_id(1)
        qi = lax.broadcasted_iota(jnp.int32, (ATT_W, 2 * ATT_W), 0)
        kj = lax.broadcasted_iota(jnp.int32, (ATT_W, 2 * ATT_W), 1)
        band = jnp.logical_and(kj >= qi, kj <= qi + ATT_W)
        lane = lax.broadcasted_iota(jnp.int32, (ATT_W, 128), 1)

        def block(row0, kc, vc, mask):
            rows = pl.ds(row0, ATT_W)
            lses = []
            for hp in range(C // 128):
                sl = slice(hp * 128, (hp + 1) * 128)
                qp = q_ref[rows, sl]
                kp = kc[:, sl]
                vp = vc[:, sl]
                outs = []
                for h in range(2 * hp, min(2 * hp + 2, nh)):
                    hm = _head_mask(qp.shape, h)
                    s = _dot_nt(jnp.where(hm, qp, jnp.zeros_like(qp)), kp)
                    s = jnp.where(mask, s, NEG_INF)
                    m = jnp.max(s, axis=-1, keepdims=True)
                    e = jnp.exp(s - m)
                    den = jnp.sum(e, axis=-1, keepdims=True)
                    p = (e * pl.reciprocal(den)).astype(BF)
                    outs.append(_dot(p, vp))
                    lses.append(m + jnp.log(den))
                if len(outs) == 2:
                    o = jnp.where(_head_mask(outs[0].shape, 0), outs[0], outs[1])
                else:
                    o = jnp.where(_head_mask(outs[0].shape, 0), outs[0], 0.0)
                o_ref[rows, sl] = o.astype(BF)
            mm = lses[0]
            for l in lses[1:]:
                mm = jnp.maximum(mm, l)
            tot = jnp.exp(lses[0] - mm)
            for l in lses[1:]:
                tot = tot + jnp.exp(l - mm)
            tile = jnp.where(lane == LSE_GROUP_LANE, mm + jnp.log(tot) - math.log(nh), 0.0)
            for h, l in enumerate(lses):
                tile = jnp.where(lane == h, l, tile)
            st_ref[rows, :] = tile

        first_mask = jnp.logical_and(band, jnp.logical_or(kj >= ATT_W, i > 0))
        block(0, jnp.concatenate([kh_ref[...], k_ref[pl.ds(0, ATT_W), :]], axis=0),
              jnp.concatenate([vh_ref[...], v_ref[pl.ds(0, ATT_W), :]], axis=0), first_mask)

        if nblk > 1:
            def step(blk, carry):
                prev = pl.ds(pl.multiple_of((blk - 1) * ATT_W, ATT_W), 2 * ATT_W)
                block(pl.multiple_of(blk * ATT_W, ATT_W), k_ref[prev, :], v_ref[prev, :], band)
                return carry
            lax.fori_loop(1, nblk, step, 0, unroll=True)

    main = lambda w: pl.BlockSpec((None, None, lc, C), lambda r, i: (w, r, i, 0))
    halo = lambda w: pl.BlockSpec((None, None, ATT_W, C), lambda r, i: (w, r, jnp.maximum(i * nblk - 1, 0), 0))
    return pl.pallas_call(
        body, name=name, grid=(d, L // lc),
        in_specs=[main(0), main(1), halo(1), main(2), halo(2)],
        out_specs=[pl.BlockSpec((None, lc, C), lambda r, i: (r, i, 0)), pl.BlockSpec((None, lc, 128), lambda r, i: (r, i, 0))],
        out_shape=[jax.ShapeDtypeStruct((d, L, C), BF), jax.ShapeDtypeStruct((d, L, 128), F32)],
        compiler_params=_params("parallel", "arbitrary"),
    )(qkv, qkv, qkv, qkv, qkv)


def _attn_bwd(qkv, do, st, dst, nh, name):
    _, d, L, C = qkv.shape
    lc = _att_chunk(L)
    nblk = lc // ATT_W
    nchunk = L // lc

    def body(q_ref, qn_ref, k_ref, kh_ref, v_ref, vh_ref, do_ref, don_ref, st_ref, stn_ref, ds_ref, dsn_ref, o_ref,
             p_scr, dsc_scr):
        i = pl.program_id(1)
        qi = lax.broadcasted_iota(jnp.int32, (ATT_W, 2 * ATT_W), 0)
        kj = lax.broadcasted_iota(jnp.int32, (ATT_W, 2 * ATT_W), 1)
        band_q = jnp.logical_and(kj >= qi, kj <= qi + ATT_W)
        qa = lax.broadcasted_iota(jnp.int32, (2 * ATT_W, ATT_W), 0)
        kb = lax.broadcasted_iota(jnp.int32, (2 * ATT_W, ATT_W), 1)
        band_k = jnp.logical_and(qa >= kb, qa <= kb + ATT_W)

        def probs(qm, kp, lse, mask):
            s = _dot_nt(qm, kp)
            return jnp.where(mask, jnp.exp(s - lse), 0.0)

        def q_block(blk, kc, vc, mask):
            rows = pl.ds(blk * ATT_W, ATT_W)
            stv = st_ref[rows, :]
            dsv = ds_ref[rows, :]
            for hp in range(C // 128):
                sl = slice(hp * 128, (hp + 1) * 128)
                qp = q_ref[rows, sl]
                dop = do_ref[rows, sl]
                kp = kc[:, sl]
                vp = vc[:, sl]
                outs = []
                for h in range(2 * hp, min(2 * hp + 2, nh)):
                    hm = _head_mask(qp.shape, h)
                    p = probs(jnp.where(hm, qp, jnp.zeros_like(qp)), kp, _lane_col(stv, h), mask)
                    dp = _dot_nt(jnp.where(hm, dop, jnp.zeros_like(dop)), vp)
                    dsc = (p * (dp - _lane_col(dsv, h))).astype(BF)
                    p_scr[blk, h] = p.astype(BF)
                    dsc_scr[blk, h] = dsc
                    outs.append(_dot(dsc, kp))
                if len(outs) == 2:
                    dq = jnp.where(_head_mask(outs[0].shape, 0), outs[0], outs[1])
                else:
                    dq = jnp.where(_head_mask(outs[0].shape, 0), outs[0], 0.0)
                o_ref[0, rows, sl] = dq.astype(BF)

        def k_block_kept(m):
            rows = pl.ds(m * ATT_W, ATT_W)
            two = pl.ds(m * ATT_W, 2 * ATT_W)
            for hp in range(C // 128):
                sl = slice(hp * 128, (hp + 1) * 128)
                qp = q_ref[two, sl]
                dop = do_ref[two, sl]
                dk, dv = None, None
                for h in range(2 * hp, min(2 * hp + 2, nh)):
                    hm = _head_mask(qp.shape, h)
                    qm = jnp.where(hm, qp, jnp.zeros_like(qp))
                    dom = jnp.where(hm, dop, jnp.zeros_like(dop))
                    dsc = jnp.concatenate([dsc_scr[m, h, :, ATT_W:], dsc_scr[m + 1, h, :, :ATT_W]], axis=0)
                    p = jnp.concatenate([p_scr[m, h, :, ATT_W:], p_scr[m + 1, h, :, :ATT_W]], axis=0)
                    dk_h = _dot_tn(dsc, qm)
                    dv_h = _dot_tn(p, dom)
                    dk = dk_h if dk is None else dk + dk_h
                    dv = dv_h if dv is None else dv + dv_h
                o_ref[1, rows, sl] = dk.astype(BF)
                o_ref[2, rows, sl] = dv.astype(BF)

        def k_block(row0, qq, doo, stv, dsv, mask):
            rows = pl.ds(row0, ATT_W)
            for hp in range(C // 128):
                sl = slice(hp * 128, (hp + 1) * 128)
                qp = qq[:, sl]
                dop = doo[:, sl]
                kp = k_ref[rows, sl]
                vp = v_ref[rows, sl]
                dks, dvs = [], []
                for h in range(2 * hp, min(2 * hp + 2, nh)):
                    hm = _head_mask(qp.shape, h)
                    qm = jnp.where(hm, qp, jnp.zeros_like(qp))
                    dom = jnp.where(hm, dop, jnp.zeros_like(dop))
                    p = probs(qm, kp, _lane_col(stv, h), mask)
                    dp = _dot_nt(dom, vp)
                    dsc = (p * (dp - _lane_col(dsv, h))).astype(BF)
                    dks.append(_dot_tn(dsc, qm))
                    dvs.append(_dot_tn(p.astype(BF), dom))
                o_ref[1, rows, sl] = sum(dks[1:], dks[0]).astype(BF)
                o_ref[2, rows, sl] = sum(dvs[1:], dvs[0]).astype(BF)

        first_mask = jnp.logical_and(band_q, jnp.logical_or(kj >= ATT_W, i > 0))
        q_block(0, jnp.concatenate([kh_ref[...], k_ref[pl.ds(0, ATT_W), :]], axis=0),
                jnp.concatenate([vh_ref[...], v_ref[pl.ds(0, ATT_W), :]], axis=0), first_mask)
        for blk in range(1, nblk):
            prev = pl.ds((blk - 1) * ATT_W, 2 * ATT_W)
            q_block(blk, k_ref[prev, :], v_ref[prev, :], band_q)
        for m in range(nblk - 1):
            k_block_kept(m)

        last = pl.ds((nblk - 1) * ATT_W, ATT_W)
        last_mask = jnp.logical_and(band_k, jnp.logical_or(qa < ATT_W, i < nchunk - 1))
        k_block((nblk - 1) * ATT_W,
                jnp.concatenate([q_ref[last, :], qn_ref[...]], axis=0),
                jnp.concatenate([do_ref[last, :], don_ref[...]], axis=0),
                jnp.concatenate([st_ref[last, :], stn_ref[...]], axis=0),
                jnp.concatenate([ds_ref[last, :], dsn_ref[...]], axis=0), last_mask)

    nb_all = L // ATT_W
    main4 = lambda w: pl.BlockSpec((None, None, lc, C), lambda r, i: (w, r, i, 0))
    prev4 = lambda w: pl.BlockSpec((None, None, ATT_W, C), lambda r, i: (w, r, jnp.maximum(i * nblk - 1, 0), 0))
    next4 = lambda w: pl.BlockSpec((None, None, ATT_W, C), lambda r, i: (w, r, jnp.minimum((i + 1) * nblk, nb_all - 1), 0))
    main3 = lambda n: pl.BlockSpec((None, lc, n), lambda r, i: (r, i, 0))
    next3 = lambda n: pl.BlockSpec((None, ATT_W, n), lambda r, i: (r, jnp.minimum((i + 1) * nblk, nb_all - 1), 0))
    return pl.pallas_call(
        body, name=name, grid=(d, nchunk),
        in_specs=[main4(0), next4(0), main4(1), prev4(1), main4(2), prev4(2),
                  main3(C), next3(C), main3(128), next3(128), main3(128), next3(128)],
        out_specs=pl.BlockSpec((3, None, lc, C), lambda r, i: (0, r, i, 0)),
        out_shape=jax.ShapeDtypeStruct((3, d, L, C), BF),
        scratch_shapes=[pltpu.VMEM((nblk, nh, ATT_W, 2 * ATT_W), BF)] * 2,
        compiler_params=_params("parallel", "arbitrary"),
    )(qkv, qkv, qkv, qkv, qkv, qkv, do, do, st, st, dst, dst)


def _alpha_from(lse_nat):
    m = jnp.maximum(jnp.maximum(lse_nat[0], lse_nat[1]), lse_nat[2])
    e = [jnp.exp(l - m) for l in lse_nat]
    inv = 1.0 / (e[0] + e[1] + e[2])
    return [ei * inv for ei in e]


def _attn_out_fwd(x, os_, sts, wo, tm=1024):
    T = x.shape[0]
    C = GROUP_LANES

    def body(x_ref, o0, o1, o2, s0, s1, s2, w_ref, xo_ref, m_ref, al_ref, oscr, sscr):
        o_refs, st_refs = (o0, o1, o2), (s0, s1, s2)
        lses = []
        for g, d in enumerate(DILATIONS):
            _gather_rows(sscr.at[g], st_refs[g], d)
            lses.append(_lane_col(sscr[g, 0], LSE_GROUP_LANE))
        alpha = _alpha_from(lses)
        for g, d in enumerate(DILATIONS):
            _gather_rows(oscr, o_refs[g], d)
            m_ref[:, g * C:(g + 1) * C] = (_chunks_to_rows(oscr) * (3.0 * alpha[g])).astype(BF)
        xo_ref[...] = x_ref[...] + _dot(m_ref[...], w_ref[...])
        lane = lax.broadcasted_iota(jnp.int32, (tm, 128), 1)
        al_ref[...] = jnp.where(lane == 0, alpha[0], jnp.where(lane == 1, alpha[1], jnp.where(lane == 2, alpha[2], 0.0)))

    o_specs = [pl.BlockSpec((d, tm // d, C), lambda i: (0, i, 0)) for d in DILATIONS]
    st_specs = [pl.BlockSpec((d, tm // d, 128), lambda i: (0, i, 0)) for d in DILATIONS]
    return pl.pallas_call(
        body, name="attn_out_fwd", grid=(T // tm,),
        in_specs=[_row_spec(tm, D)] + o_specs + st_specs + [_full_spec((3 * C, D))],
        out_specs=[_row_spec(tm, D), _row_spec(tm, 3 * C), _row_spec(tm, 128)],
        out_shape=[jax.ShapeDtypeStruct((T, D), F32), jax.ShapeDtypeStruct((T, 3 * C), BF), jax.ShapeDtypeStruct((T, 128), F32)],
        scratch_shapes=[pltpu.VMEM((C // 128, tm, 128), F32), pltpu.VMEM((3, 1, tm, 128), F32)],
        compiler_params=_params("parallel"),
    )(x, *os_, *sts, wo)


def _attn_out_bwd(dx, os_, sts, alpha, wo, tm=1024, dep=None):
    T = dx.shape[0]
    C = GROUP_LANES

    def body(dx_ref, o0, o1, o2, s0, s1, s2, al_ref, w_ref, do0, do1, do2, ds0, ds1, ds2, dmscr, oscr, sscr, tscr):
        o_refs, st_refs = (o0, o1, o2), (s0, s1, s2)
        do_refs, ds_refs = (do0, do1, do2), (ds0, ds1, ds2)
        dyb = dx_ref[...].astype(BF)
        alv = al_ref[...]
        alpha_g = [_lane_col(alv, g) for g in range(3)]
        dalpha = []
        for g, d in enumerate(DILATIONS):
            dm = _dot_nt(dyb, w_ref[g * C:(g + 1) * C, :])
            _rows_to_chunks(dmscr.at[g], dm)
            _gather_rows(oscr.at[g], o_refs[g], d)
            _gather_rows(sscr.at[g], st_refs[g], d)
            dalpha.append(3.0 * jnp.sum(dm * _chunks_to_rows(oscr.at[g]), axis=-1, keepdims=True))
        mean_da = alpha_g[0] * dalpha[0] + alpha_g[1] * dalpha[1] + alpha_g[2] * dalpha[2]
        lane = lax.broadcasted_iota(jnp.int32, (tm, 128), 1)
        seg = (lax.broadcasted_iota(jnp.int32, (C, 128), 0) // HEAD_DIM == lax.broadcasted_iota(jnp.int32, (C, 128), 1)).astype(BF)
        for g, d in enumerate(DILATIONS):
            nh = HEAD_GROUPS[g]
            dlse_g = alpha_g[g] * (dalpha[g] - mean_da)
            do_nat = _chunks_to_rows(dmscr.at[g]) * (3.0 * alpha_g[g])
            prod = do_nat * _chunks_to_rows(oscr.at[g])
            hi = prod.astype(BF)
            delta = _dot(hi, seg) + _dot((prod - hi.astype(F32)).astype(BF), seg)
            stv = sscr[g, 0]
            dlse = dlse_g * jnp.exp(stv - _lane_col(stv, LSE_GROUP_LANE)) * (1.0 / nh)
            tile = jnp.where(lane < nh, delta - dlse, 0.0)
            _rows_to_chunks(dmscr.at[g], do_nat)
            _scatter_rows(do_refs[g], dmscr.at[g], d, BF)
            tscr[0] = tile
            _scatter_rows(ds_refs[g], tscr, d, F32)

    o_specs = [pl.BlockSpec((d, tm // d, C), lambda i: (0, i, 0)) for d in DILATIONS]
    st_specs = [pl.BlockSpec((d, tm // d, 128), lambda i: (0, i, 0)) for d in DILATIONS]
    body, in_specs, args = _after(
        dep, body, [_row_spec(tm, D)] + o_specs + st_specs + [_row_spec(tm, 128)] + [_full_spec((3 * C, D))],
        [dx, *os_, *sts, alpha, wo])
    return pl.pallas_call(
        body, name="attn_out_bwd", grid=(T // tm,),
        in_specs=in_specs,
        out_specs=o_specs + st_specs,
        out_shape=[jax.ShapeDtypeStruct((d, T // d, C), BF) for d in DILATIONS]
        + [jax.ShapeDtypeStruct((d, T // d, 128), F32) for d in DILATIONS],
        scratch_shapes=[pltpu.VMEM((3, C // 128, tm, 128), F32), pltpu.VMEM((3, C // 128, tm, 128), F32),
                        pltpu.VMEM((3, 1, tm, 128), F32), pltpu.VMEM((1, tm, 128), F32)],
        compiler_params=_params("parallel"),
    )(*args)


def _qkv_bwd(dqkvs, cos, sin, x2, g, w3s, dx3, tm=512):
    T = x2.shape[0]
    C = GROUP_LANES

    def body(dq0, dq1, dq2, cos_ref, sin_ref, x_ref, g_ref, w0, w1, w2, dx3_ref,
             dx_ref, n0, n1, n2, h_ref, dg_ref, scr, dh_ref):
        dq_refs, w_refs, n_refs = (dq0, dq1, dq2), (w0, w1, w2), (n0, n1, n2)

        @pl.when(pl.program_id(0) == 0)
        def _():
            dg_ref[...] = jnp.zeros_like(dg_ref)

        cv = cos_ref[...]
        sv = sin_ref[...]
        dh_ref[...] = jnp.zeros_like(dh_ref)
        for gi, d in enumerate(DILATIONS):
            for w in range(3):
                _gather_rows(scr, dq_refs[gi].at[w], d)
                if w < 2:
                    scale = HEAD_DIM ** -0.5 if w == 0 else 1.0
                    for c in range(C // 128):
                        t = scr[c]
                        scr[c] = (t * cv - _rot_half(t) * sv) * scale
                n_refs[gi][:, w * C:(w + 1) * C] = _chunks_to_rows(scr).astype(BF)
            dh_ref[...] += _dot_nt(n_refs[gi][...], w_refs[gi][...])
        xv = x_ref[...]
        h_ref[...] = _rms_fwd(xv, g_ref[...]).astype(BF)
        dx, dg = _rms_bwd(xv, g_ref[...], dh_ref[...])
        dx_ref[...] = dx3_ref[...] + dx
        dg_ref[...] += dg

    dq_specs = [pl.BlockSpec((3, d, tm // d, C), lambda i: (0, 0, i, 0)) for d in DILATIONS]
    nat = _row_spec(tm, 3 * C)
    nshape = jax.ShapeDtypeStruct((T, 3 * C), BF)
    return pl.pallas_call(
        body, name="qkv_bwd", grid=(T // tm,),
        in_specs=dq_specs + [_row_spec(tm, 128), _row_spec(tm, 128), _row_spec(tm, D), _full_spec((1, D))]
        + [pl.BlockSpec((D, 3 * C), lambda i: (0, 0), pipeline_mode=pl.Buffered(1))] * 3 + [_row_spec(tm, D)],
        out_specs=[_row_spec(tm, D), nat, nat, nat, _row_spec(tm, D), _full_spec((1, D))],
        out_shape=[jax.ShapeDtypeStruct((T, D), F32), nshape, nshape, nshape, jax.ShapeDtypeStruct((T, D), BF),
                   jax.ShapeDtypeStruct((1, D), F32)],
        scratch_shapes=[pltpu.VMEM((C // 128, tm, 128), F32), pltpu.VMEM((tm, D), F32)],
        compiler_params=_params("arbitrary"),
    )(*dqkvs, cos, sin, x2, g, *w3s, dx3)


def _place():
    x, y, c = lax.axis_index("x"), lax.axis_index("y"), lax.axis_index("c")
    return x, y, c


def _gather_phases(ins, outs, *sems):
    n = len(ins)
    if n == 0:
        return (lambda: None), (lambda: None)
    send_sems, recv_sems, local_sems = sems
    x, y, c = _place()
    me, sibling = (x, y, c), (x, y, 1 - c)
    chips = [(1 - x, y), (x, 1 - y), (1 - x, 1 - y)]

    def slot(a, px, py, pc):
        return outs[a].at[4 * px + 2 * py + pc]

    def copy(a, k, block, to, src=None):
        return pltpu.make_async_remote_copy(
            src_ref=slot(a, *block) if src is None else src, dst_ref=slot(a, *block),
            send_sem=send_sems.at[a, k], recv_sem=recv_sems.at[a, k], device_id=to, device_id_type=MESH)

    def mine(a):
        return pltpu.make_async_copy(ins[a], slot(a, *me), local_sems.at[a])

    def first(a):
        return [copy(a, 0, me, sibling, src=ins[a])] + [copy(a, 1 + j, me, (*chip, c), src=ins[a]) for j, chip in enumerate(chips)]

    def begin():
        for a in range(n):
            mine(a).start()
        for a in range(n):
            for cp in first(a):
                cp.start()

    def finish():
        passed = []
        for a in range(n):
            for j, chip in enumerate(chips):
                copy(a, 1 + j, (*chip, c), me).wait_recv()
                fwd = copy(a, 4 + j, (*chip, c), sibling)
                fwd.start()
                passed.append(fwd)
        for a in range(n):
            copy(a, 0, sibling, me).wait_recv()
            for j, chip in enumerate(chips):
                copy(a, 4 + j, (*chip, 1 - c), me).wait_recv()
        for a in range(n):
            for cp in first(a):
                cp.wait_send()
        for cp in passed:
            cp.wait_send()
        for a in range(n):
            mine(a).wait()

    return begin, finish


def _gather_scratch(n):
    return [pltpu.SemaphoreType.DMA((n, 7)), pltpu.SemaphoreType.DMA((n, 7)), pltpu.SemaphoreType.DMA((n,))] if n else []


def _allgather(arrs, name):
    n = len(arrs)

    def body(*refs):
        begin, finish = _gather_phases(refs[:n], refs[n:2 * n], *refs[2 * n:])
        begin()
        finish()

    hbm = pl.BlockSpec(memory_space=pl.ANY)
    return pl.pallas_call(
        body, name=name,
        in_specs=[hbm] * n, out_specs=[hbm] * n,
        out_shape=[jax.ShapeDtypeStruct((N_DEV,) + a.shape, a.dtype) for a in arrs],
        scratch_shapes=_gather_scratch(n),
    )(*arrs)


def _peer(k):
    x, y, c = _place()
    px = 1 - x if k & 4 else x
    py = 1 - y if k & 2 else y
    pc = 1 - c if k & 1 else c
    return (px, py, pc), 4 * px + 2 * py + pc


HBM_SPEC = pl.BlockSpec(memory_space=pltpu.HBM)
SEM_SPEC = pl.BlockSpec(memory_space=pltpu.SEMAPHORE)
EFFECT = pltpu.SideEffectType.DATAFLOW_SIDE_EFFECTING


def _exchange_start(arrs, name, same_block=False, dep=None):
    n = len(arrs)
    n_dep = 0 if dep is None else 1

    def body(*refs):
        srcs, lands = refs[:n], refs[n:2 * n]
        send_sems, recv_sems, local_sems = refs[2 * n + n_dep:2 * n + n_dep + 3]
        token = refs[-1]
        x, y, c = _place()
        me = 4 * x + 2 * y + c
        block = (lambda a, j: srcs[a]) if same_block else (lambda a, j: srcs[a].at[j])
        for a in range(n):
            pltpu.make_async_copy(block(a, me), lands[a].at[me], local_sems.at[a]).start()
        for a in range(n):
            for k in range(1, N_DEV):
                to, to_idx = _peer(k)
                pltpu.make_async_remote_copy(
                    src_ref=block(a, to_idx), dst_ref=lands[a].at[me],
                    send_sem=send_sems.at[a * (N_DEV - 1) + k - 1], recv_sem=recv_sems.at[a * (N_DEV - 1) + k - 1], device_id=to, device_id_type=MESH).start()
        token[...] = jnp.zeros_like(token)

    land_shape = (lambda a: (N_DEV,) + a.shape) if same_block else (lambda a: a.shape)
    src_shapes = [pltpu.HBM(a.shape, a.dtype) for a in arrs]
    land_shapes = [pltpu.HBM(land_shape(a), a.dtype) for a in arrs]
    outs = pl.pallas_call(
        body, name=name,
        out_shape=(pltpu.SemaphoreType.DMA((n * (N_DEV - 1),)), pltpu.SemaphoreType.DMA((n * (N_DEV - 1),)),
                   pltpu.SemaphoreType.DMA((n,)), *src_shapes, *land_shapes, jax.ShapeDtypeStruct((8, 128), F32)),
        in_specs=[HBM_SPEC] * (2 * n) + [pl.BlockSpec(memory_space=pl.ANY)] * n_dep,
        out_specs=(SEM_SPEC, SEM_SPEC, SEM_SPEC, *([HBM_SPEC] * (2 * n)), pl.BlockSpec(memory_space=pltpu.VMEM)),
        input_output_aliases={i: 3 + i for i in range(2 * n)},
        compiler_params=pltpu.CompilerParams(has_side_effects=EFFECT),
    )(*[pltpu.with_memory_space_constraint(a, pltpu.HBM) for a in arrs],
      *[pltpu.with_memory_space_constraint(lax.empty(land_shape(a), a.dtype), pltpu.HBM) for a in arrs],
      *([] if dep is None else [dep]))
    return outs[0], outs[1], outs[2], outs[3:3 + n], outs[3 + n:3 + 2 * n], outs[3 + 2 * n]


def _exchange_wait(send_sems, recv_sems, local_sems, src_thru, land_thru, after, name):
    n = len(src_thru)
    after = list(after) if isinstance(after, (list, tuple)) else [after]
    same_block = src_thru[0].shape != land_thru[0].shape

    def body(*refs):
        srcs, lands = refs[:n], refs[n:2 * n]
        send_sems, recv_sems, local_sems = refs[2 * n:2 * n + 3]
        x, y, c = _place()
        me = 4 * x + 2 * y + c
        for a in range(n):
            pltpu.make_async_copy(srcs[a] if same_block else srcs[a].at[me], lands[a].at[me], local_sems.at[a]).wait()
            for k in range(1, N_DEV):
                frm, frm_idx = _peer(k)
                cp = pltpu.make_async_remote_copy(
                    src_ref=srcs[a] if same_block else srcs[a].at[frm_idx], dst_ref=lands[a].at[frm_idx],
                    send_sem=send_sems.at[a * (N_DEV - 1) + k - 1], recv_sem=recv_sems.at[a * (N_DEV - 1) + k - 1], device_id=frm, device_id_type=MESH)
                cp.wait_send()
                cp.wait_recv()

    outs = pl.pallas_call(
        body, name=name,
        out_shape=tuple(pltpu.HBM(a.shape, a.dtype) for a in (*src_thru, *land_thru)),
        in_specs=[HBM_SPEC] * (2 * n) + [SEM_SPEC, SEM_SPEC, SEM_SPEC] + [pl.BlockSpec(memory_space=pl.ANY)] * len(after),
        out_specs=[HBM_SPEC] * (2 * n),
        input_output_aliases={i: i for i in range(2 * n)},
        compiler_params=pltpu.CompilerParams(has_side_effects=EFFECT),
    )(*src_thru, *land_thru, send_sems, recv_sems, local_sems, *after)
    return outs[n:]


def _row_tile(rows):
    for t in (256, 176, 128, 8):
        if rows % t == 0:
            return t
    return rows


def _adamw(parts, w, m, v, name, layer=None, into=None):
    K, R, C = parts.shape
    tr = _row_tile(R)
    n_into = 0 if into is None else 4

    def body(p_ref, w_ref, m_ref, v_ref, *refs):
        g_ref, d_ref, nm_ref, nv_ref = refs[n_into:]
        g = p_ref[0].astype(F32)
        for k in range(1, K):
            g = g + p_ref[k].astype(F32)
        nm = ADAM_B1 * m_ref[...] + (1.0 - ADAM_B1) * g
        nv = ADAM_B2 * v_ref[...] + (1.0 - ADAM_B2) * jnp.square(g)
        m_hat = nm / (1.0 - ADAM_B1 ** ADAM_STEP)
        v_hat = nv / (1.0 - ADAM_B2 ** ADAM_STEP)
        g_ref[...] = g
        d_ref[...] = -ADAM_LR * (m_hat / (jnp.sqrt(v_hat) + ADAM_EPS) + ADAM_WD * w_ref[...])
        nm_ref[...] = nm
        nv_ref[...] = nv

    blk = _row_spec(tr, C)
    parts_spec = pl.BlockSpec((K, tr, C), lambda i: (0, i, 0))
    if layer is None:
        out_blk, shp = blk, jax.ShapeDtypeStruct((R, C), F32)
    else:
        out_blk, shp = pl.BlockSpec((None, tr, C), lambda i: (layer, i, 0)), jax.ShapeDtypeStruct((2, R, C), F32)
    return pl.pallas_call(
        body, name=name, grid=(R // tr,),
        in_specs=[parts_spec, blk, blk, blk] + [pl.BlockSpec(memory_space=pl.ANY)] * n_into,
        out_specs=[out_blk] * 4,
        out_shape=[shp] * 4,
        input_output_aliases={4 + j: j for j in range(n_into)},
        compiler_params=_params("parallel"),
    )(parts, w, m, v, *(into or ()))


def _rope_tables(T):
    inv_freq = 1.0 / (ROPE_THETA ** (jnp.arange(0, HEAD_DIM, 2, dtype=F32) / HEAD_DIM))
    inv_freq = jnp.tile(inv_freq, 4)
    hi = (64.0 * jnp.arange(T // 64, dtype=F32))[:, None] * inv_freq[None, :]
    lo = jnp.arange(64, dtype=F32)[:, None] * inv_freq[None, :]
    ch, sh, cl, sl = jnp.cos(hi)[:, None], jnp.sin(hi)[:, None], jnp.cos(lo)[None], jnp.sin(lo)[None]
    return (ch * cl - sh * sl).reshape(T, 128), (sh * cl + ch * sl).reshape(T, 128)


def _pad_lanes(a, n):
    return jnp.pad(a, ((0, 0),) * (a.ndim - 1) + ((0, n - a.shape[-1]),))


POOL = ("pool_in", "pool_grp", "pool_out")
FFN0 = ("gate0", "up0", "down0")
LAYER1 = ("qkv", "attn_out", "gate1", "up1", "down1")


def _layout_weights(gw):
    w = {}
    if "pool_in" in gw:
        w["pool_in"] = gw["pool_in"].reshape(D, D)
        w["pool_grp"] = jnp.transpose(gw["pool_grp"], (1, 0, 2, 3)).reshape(4, POOL_GC, POOL_GC)
        w["pool_out"] = gw["pool_out"].reshape(D, D)
    if "qkv" in gw:
        wqkv = jnp.transpose(gw["qkv"], (1, 0, 2)).reshape(D, 3 * D)
        wo = gw["attn_out"].reshape(D, D)
        w["qkv"], w["attn_out"] = [], []
        for nh, off in zip(HEAD_GROUPS, HEAD_OFFS):
            lo, n = off * HEAD_DIM, nh * HEAD_DIM
            w["qkv"].append(jnp.concatenate([_pad_lanes(wqkv[:, k * D + lo:k * D + lo + n], GROUP_LANES) for k in range(3)], axis=1))
            w["attn_out"].append(jnp.pad(wo[lo:lo + n], ((0, GROUP_LANES - n), (0, 0))))
        w["attn_out"] = jnp.concatenate(w["attn_out"], axis=0)
    for nm in gw:
        if nm[:-1] in ("gate", "up", "down"):
            w[nm] = gw[nm].reshape(F, D)
    return w


def _local_step(x, target, w, ffn0_shards, layer1_shards, norm_mix, norm_ffn, norm_final, pool_scale, emit):
    T = x.shape[0]
    cos, sin = _rope_tables(T)
    nm = [norm_mix[i:i + 1] for i in range(2)]
    nf = [norm_ffn[i:i + 1] for i in range(2)]
    nfin = norm_final.reshape(1, D)

    x1, p, *ffn0 = _pool_fwd(x, nm[0], w["pool_in"], w["pool_grp"], pool_scale, w["pool_out"], ffn0_shards)
    w = {**w, **_layout_weights(dict(zip(FFN0, ffn0)))}
    x2, a0, b0, s0, *layer1 = _ffn_fwd(x1, nf[0], w["gate0"], w["up0"], w["down0"], "ffn_fwd0", gather=layer1_shards)
    w = {**w, **_layout_weights(dict(zip(LAYER1, layer1)))}
    qkvs = _qkv_fwd(x2, nm[1], w["qkv"], cos, sin)
    att = [_attn_fwd(qkvs[g], HEAD_GROUPS[g], f"attn_fwd{g}") for g in range(3)]
    os_, sts = [a[0] for a in att], [a[1] for a in att]
    x3, merged, alpha = _attn_out_fwd(x2, os_, sts, w["attn_out"])
    dx4, a1, b1, s1, loss, dg_final = _ffn_fwd(x3, nf[1], w["gate1"], w["up1"], w["down1"], "ffn_fwd1", head=(target, nfin))

    g = {}
    dx3, da1, db1, h3, dy4, dg_ffn1 = _ffn_bwd(dx4, x3, nf[1], a1, b1, w["gate1"], w["up1"], w["down1"], "ffn_bwd1")
    g["down1"] = _wgrad(s1, dy4, "down_wgrad1", mblk=F // 2)
    g["gate1"] = _wgrad(da1, h3, "gate_wgrad1", mblk=F // 2)
    g["up1"] = _wgrad(db1, h3, "up_wgrad1", mblk=F // 2)
    dep = emit("ffn1", g)
    dos_and_stats = _attn_out_bwd(dx3, os_, sts, alpha, w["attn_out"], dep=dep)
    dos, dsts = dos_and_stats[:3], dos_and_stats[3:]
    g["attn_out"] = _wgrad(merged, dx3, "attn_out_wgrad")
    dqkvs = [_attn_bwd(qkvs[gi], dos[gi], sts[gi], dsts[gi], HEAD_GROUPS[gi], f"attn_bwd{gi}") for gi in range(3)]
    dx2, n0, n1, n2, h2b, dg_mix1 = _qkv_bwd(dqkvs, cos, sin, x2, nm[1], w["qkv"], dx3)
    g["qkv"] = [_wgrad(h2b, n, f"qkv_wgrad{i}") for i, n in enumerate((n0, n1, n2))]
    dep = emit("attn", g)
    dx1, da0, db0, h1, dy2, dg_ffn0 = _ffn_bwd(dx2, x1, nf[0], a0, b0, w["gate0"], w["up0"], w["down0"], "ffn_bwd0", dep=dep)
    g["down0"] = _wgrad(s0, dy2, "down_wgrad0", mblk=F // 2)
    g["gate0"] = _wgrad(da0, h1, "gate_wgrad0", mblk=F // 2)
    g["up0"] = _wgrad(db0, h1, "up_wgrad0", mblk=F // 2)
    dep = emit("ffn0", g)
    dx0, z, dzp, du, h0b, dscale, dg_mix0 = _pool_bwd(dx1, x, nm[0], p, w["pool_in"], w["pool_grp"], pool_scale, w["pool_out"], dep=dep)
    g["pool_out"] = _wgrad(z, dx1, "pool_out_wgrad")
    g["pool_in"] = _wgrad(h0b, du, "pool_in_wgrad")
    g["pool_grp"] = _wgrad_pool_groups(p, dzp)
    emit("pool", g)

    small = jnp.concatenate([dg_mix0, dg_mix1, dg_ffn0, dg_ffn1, dg_final, dscale,
                             jnp.broadcast_to(loss, (1, D)), jnp.zeros((1, D), F32)], axis=0)
    return dx0, small


GROUPS = {"ffn1": ("down1", "gate1", "up1"), "attn": ("qkv", "attn_out"), "ffn0": ("down0", "gate0", "up0"),
          "pool": ("pool_in", "pool_out", "pool_grp")}


def _grad_blocks(group, g):
    blocks = {}
    if group == "pool":
        blocks["pool_in"] = g["pool_in"].reshape(N_DEV, D // N_DEV, D)
        blocks["pool_out"] = g["pool_out"].reshape(N_DEV, D // N_DEV, D)
        blocks["pool_grp"] = jnp.transpose(g["pool_grp"].reshape(4, N_DEV, POOL_GC // N_DEV, POOL_GC), (1, 0, 2, 3)).reshape(N_DEV, 4 * POOL_GC // N_DEV, POOL_GC)
    elif group == "attn":
        wo = jnp.concatenate([g["attn_out"][gi * GROUP_LANES:gi * GROUP_LANES + nh * HEAD_DIM] for gi, nh in enumerate(HEAD_GROUPS)], axis=0)
        blocks["attn_out"] = wo.reshape(N_DEV, D // N_DEV, D)
        wqkv = jnp.concatenate([g["qkv"][gi][:, k * GROUP_LANES:k * GROUP_LANES + HEAD_GROUPS[gi] * HEAD_DIM]
                                for k in range(3) for gi in range(3)], axis=1)
        blocks["qkv"] = jnp.transpose(wqkv.reshape(D, N_DEV, 3 * D // N_DEV), (1, 0, 2))
    else:
        for nm in GROUPS[group]:
            blocks[nm] = g[nm].reshape(N_DEV, F // N_DEV, D)
    return [blocks[nm] for nm in GROUPS[group]]


def kernel(x, norm_mix, norm_ffn, norm_final, pool_w_in, pool_w_group, pool_scale, pool_w_out, attn_w_qkv, attn_w_out, ffn_w_gate, ffn_w_up, ffn_w_down, loss_target, m_norm_mix, m_norm_ffn, m_norm_final, m_pool_w_in, m_pool_w_group, m_pool_scale, m_pool_w_out, m_attn_w_qkv, m_attn_w_out, m_ffn_w_gate, m_ffn_w_up, m_ffn_w_down, v_norm_mix, v_norm_ffn, v_norm_final, v_pool_w_in, v_pool_w_group, v_pool_scale, v_pool_w_out, v_attn_w_qkv, v_attn_w_out, v_ffn_w_gate, v_ffn_w_up, v_ffn_w_down):
    shard = {
        "pool_in": pool_w_in[0], "pool_grp": pool_w_group[0], "pool_out": pool_w_out[0],
        "qkv": attn_w_qkv[0], "attn_out": attn_w_out[0],
    }
    tr3 = lambda t: jnp.transpose(t, (0, 2, 1))
    gate_t, up_t = tr3(ffn_w_gate), tr3(ffn_w_up)
    for l in range(2):
        shard[f"gate{l}"] = gate_t[l]
        shard[f"up{l}"] = up_t[l]
        shard[f"down{l}"] = ffn_w_down[l]
    shard = {k: v.astype(BF) for k, v in shard.items()}
    w0 = _layout_weights(dict(zip(POOL, _allgather([shard[k] for k in POOL], "weights_allgather"))))

    started = {}

    def emit(group, g):
        started[group] = _exchange_start(_grad_blocks(group, g), f"grads_start_{group}")
        return started[group][5]

    grad_x, small = _local_step(x[0], loss_target[0], w0, [shard[k] for k in FFN0], [shard[k] for k in LAYER1], norm_mix, norm_ffn, norm_final, pool_scale, emit)

    small_st = _exchange_start([small], "small_start", same_block=True, dep=started["pool"][5])

    def finish(group, after):
        lands = _exchange_wait(*started[group][:5], after, f"grads_wait_{group}")
        return dict(zip(GROUPS[group], lands))

    def upd(parts, wt, mt, vt, name):
        shape = wt.shape
        r2 = lambda t: t.reshape(parts.shape[1:])
        outs = _adamw(parts, r2(wt), r2(mt), r2(vt), name)
        return [o.reshape(shape) for o in outs]

    ffn_state = (("gate", gate_t, tr3(m_ffn_w_gate), tr3(v_ffn_w_gate)), ("up", up_t, tr3(m_ffn_w_up), tr3(v_ffn_w_up)),
                 ("down", ffn_w_down, m_ffn_w_down, v_ffn_w_down))

    def ffn_layer(rcv, l, other=None):
        return {nm: _adamw(rcv[f"{nm}{l}"], wt[l], mt[l], vt[l], f"adamw_{nm}{l}", layer=l,
                           into=None if other is None else other[nm]) for nm, wt, mt, vt in ffn_state}

    res = {}
    ffn1 = ffn_layer(finish("ffn1", small_st[5]), 1)
    rcv = finish("attn", [ffn1[nm][0] for nm in ffn1])
    res["attn_w_qkv"] = upd(rcv["qkv"], attn_w_qkv, m_attn_w_qkv, v_attn_w_qkv, "adamw_qkv")
    res["attn_w_out"] = upd(rcv["attn_out"], attn_w_out, m_attn_w_out, v_attn_w_out, "adamw_attn_out")
    ffn0 = ffn_layer(finish("ffn0", [res["attn_w_qkv"][0], res["attn_w_out"][0]]), 0, other=ffn1)
    rcv = finish("pool", [ffn0[nm][0] for nm in ffn0])
    res["pool_w_in"] = upd(rcv["pool_in"], pool_w_in, m_pool_w_in, v_pool_w_in, "adamw_pool_in")
    res["pool_w_group"] = upd(rcv["pool_grp"], pool_w_group, m_pool_w_group, v_pool_w_group, "adamw_pool_grp")
    res["pool_w_out"] = upd(rcv["pool_out"], pool_w_out, m_pool_w_out, v_pool_w_out, "adamw_pool_out")
    res["ffn_w_gate"] = [tr3(t) for t in ffn0["gate"]]
    res["ffn_w_up"] = [tr3(t) for t in ffn0["up"]]
    res["ffn_w_down"] = ffn0["down"]
    small_all = _exchange_wait(*small_st[:5], [res[k][0] for k in ("pool_w_in", "pool_w_group", "pool_w_out")], "small_wait")[0]

    small_w = jnp.concatenate([norm_mix, norm_ffn, norm_final[None], pool_scale, jnp.zeros((2, D), F32)], axis=0)
    small_m = jnp.concatenate([m_norm_mix, m_norm_ffn, m_norm_final[None], m_pool_scale, jnp.zeros((2, D), F32)], axis=0)
    small_v = jnp.concatenate([v_norm_mix, v_norm_ffn, v_norm_final[None], v_pool_scale, jnp.ones((2, D), F32)], axis=0)
    sg, sd, sm, sv = _adamw(small_all, small_w, small_m, small_v, "adamw_small")
    loss = sg[6, 0]
    res["norm_mix"] = [t[0:2] for t in (sg, sd, sm, sv)]
    res["norm_ffn"] = [t[2:4] for t in (sg, sd, sm, sv)]
    res["norm_final"] = [t[4] for t in (sg, sd, sm, sv)]
    res["pool_scale"] = [t[5:6] for t in (sg, sd, sm, sv)]

    order = ["norm_mix", "norm_ffn", "norm_final", "pool_w_in", "pool_w_group", "pool_scale", "pool_w_out",
             "attn_w_qkv", "attn_w_out", "ffn_w_gate", "ffn_w_up", "ffn_w_down"]
    return (loss, grad_x[None], *[res[k][0] for k in order], *[res[k][1] for k in order],
            *[res[k][2] for k in order], *[res[k][3] for k in order])
```

```python
import math

import jax
import jax.numpy as jnp
from jax import lax
from jax.experimental import pallas as pl
from jax.experimental.pallas import tpu as pltpu

D = 1024
F = 2816
N_DEV = 8
EPS = 1e-6
NEG_INF = -1e30
POOL_WINDOWS = (2, 4, 8, 16)
POOL_HALO = 16
POOL_GC = 256
HEAD_DIM = 64
HEAD_GROUPS = (6, 5, 5)
HEAD_OFFS = (0, 6, 11)
DILATIONS = (1, 4, 16)
ATT_W = 128
GROUP_LANES = 384
LSE_GROUP_LANE = 8
ROPE_THETA = 10000.0
ADAM_LR, ADAM_B1, ADAM_B2, ADAM_EPS, ADAM_WD, ADAM_STEP = 0.001, 0.9, 0.999, 1e-08, 0.01, 10

BF = jnp.bfloat16
F32 = jnp.float32
VMEM_LIMIT = 56 * 1024 * 1024
MESH = pl.DeviceIdType.MESH


def _params(*sem):
    return pltpu.CompilerParams(dimension_semantics=sem, vmem_limit_bytes=VMEM_LIMIT)


def _dot(a, b):
    return jnp.dot(a, b, preferred_element_type=F32)


def _dot_nt(a, b):
    return lax.dot_general(a, b, (((1,), (1,)), ((), ())), preferred_element_type=F32)


def _dot_tn(a, b):
    return lax.dot_general(a, b, (((0,), (0,)), ((), ())), preferred_element_type=F32)


def _rms_fwd(xv, g):
    r = lax.rsqrt(jnp.mean(xv * xv, axis=-1, keepdims=True) + EPS)
    return (xv * r) * g


def _rms_bwd(xv, g, dh):
    r = lax.rsqrt(jnp.mean(xv * xv, axis=-1, keepdims=True) + EPS)
    xhat = xv * r
    dg = jnp.sum(dh * xhat, axis=0, keepdims=True)
    dxh = dh * g
    dx = r * (dxh - xhat * jnp.mean(dxh * xhat, axis=-1, keepdims=True))
    return dx, dg


def _lane_col(tile, j):
    lane = lax.broadcasted_iota(jnp.int32, tile.shape, 1)
    return jnp.sum(jnp.where(lane == j, tile, 0.0), axis=-1, keepdims=True)


def _row_spec(tm, n):
    return pl.BlockSpec((tm, n), lambda i: (i, 0))


def _full_spec(shape):
    nd = len(shape)
    return pl.BlockSpec(shape, lambda *_: (0,) * nd)


def _after(dep, body, in_specs, args):
    if dep is None:
        return body, list(in_specs), list(args)

    def body_after(dep_ref, *refs):
        body(*refs)

    return body_after, [pl.BlockSpec(memory_space=pl.ANY)] + list(in_specs), [dep] + list(args)


def _pool_fwd(x, g, w_in, w_grp, scale, w_out, gather, tm=512, dep=None):
    T = x.shape[0]
    nt = T // tm
    n = tm + POOL_HALO
    ng = len(gather)

    def body(x_ref, g_ref, win_ref, wg_ref, sc_ref, wout_ref, *refs):
        shard_refs, (x1_ref, p_ref), full_refs = refs[:ng], refs[ng:ng + 2], refs[ng + 2:2 * ng + 2]
        tail_ref, z_ref = refs[2 * ng + 2:2 * ng + 4]
        begin, finish = _gather_phases(shard_refs, full_refs, *refs[2 * ng + 4:])
        i = pl.program_id(0)

        @pl.when(i == 0)
        def _():
            begin()
            tail_ref[...] = jnp.zeros_like(tail_ref)

        u = _dot(_rms_fwd(x_ref[...], g_ref[...]).astype(BF), win_ref[...])
        pos = i * tm + lax.broadcasted_iota(jnp.int32, (tm, 1), 0)
        for g, w in enumerate(POOL_WINDOWS):
            sl = slice(g * POOL_GC, (g + 1) * POOL_GC)
            ug = u[:, sl]
            s = jnp.concatenate([tail_ref[:, sl], ug], axis=0)
            step = 1
            while step < w:
                s = s + pltpu.roll(s, step, 0)
                step *= 2
            cnt = jnp.minimum(pos + 1, w).astype(F32)
            pg = (s[POOL_HALO:, :] / cnt - ug).astype(BF)
            p_ref[:, sl] = pg
            z_ref[:, sl] = (_dot(pg, wg_ref[g]) * sc_ref[:, sl]).astype(BF)
        tail_ref[...] = u[tm - POOL_HALO:, :]
        x1_ref[...] = x_ref[...] + _dot(z_ref[...], wout_ref[...])

        @pl.when(i == nt - 1)
        def _():
            finish()

    hbm = pl.BlockSpec(memory_space=pl.ANY)
    body, in_specs, args = _after(
        dep, body,
        [_row_spec(tm, D), _full_spec((1, D)), _full_spec((D, D)), _full_spec((4, POOL_GC, POOL_GC)), _full_spec((1, D)),
         _full_spec((D, D))] + [hbm] * ng,
        [x, g, w_in, w_grp, scale, w_out, *gather])
    return pl.pallas_call(
        body, name="pool_fwd", grid=(nt,),
        in_specs=in_specs,
        out_specs=[_row_spec(tm, D), _row_spec(tm, D)] + [hbm] * ng,
        out_shape=[jax.ShapeDtypeStruct((T, D), F32), jax.ShapeDtypeStruct((T, D), BF)]
        + [jax.ShapeDtypeStruct((N_DEV,) + a.shape, a.dtype) for a in gather],
        scratch_shapes=[pltpu.VMEM((POOL_HALO, D), F32), pltpu.VMEM((tm, D), BF)] + _gather_scratch(ng),
        compiler_params=_params("arbitrary"),
    )(*args)


def _pool_bwd(dx1, x0, g0, p, w_in, w_grp, scale, w_out, tm=512, dep=None):
    T = x0.shape[0]
    nt = T // tm
    n = tm + POOL_HALO
    rev = lambda i: (nt - 1 - i, 0)

    def body(dx1_ref, x0_ref, g_ref, p_ref, win_ref, wg_ref, sc_ref, wout_ref,
             dx0_ref, z_ref, dzp_ref, du_ref, h0_ref, dsc_ref, dg_ref, head_ref):
        i = pl.program_id(0)

        @pl.when(i == 0)
        def _():
            head_ref[...] = jnp.zeros_like(head_ref)
            dsc_ref[...] = jnp.zeros_like(dsc_ref)
            dg_ref[...] = jnp.zeros_like(dg_ref)

        dx1v = dx1_ref[...]
        dz = _dot_nt(dx1v.astype(BF), wout_ref[...])
        pos = (nt - 1 - i) * tm + lax.broadcasted_iota(jnp.int32, (tm, 1), 0)
        for g, w in enumerate(POOL_WINDOWS):
            sl = slice(g * POOL_GC, (g + 1) * POOL_GC)
            zpre = _dot(p_ref[:, sl], wg_ref[g])
            dzg = dz[:, sl]
            dsc_ref[:, sl] += jnp.sum(dzg * zpre, axis=0, keepdims=True)
            z_ref[:, sl] = (zpre * sc_ref[:, sl]).astype(BF)
            dzp = (dzg * sc_ref[:, sl]).astype(BF)
            dzp_ref[:, sl] = dzp
            dp = _dot_nt(dzp, wg_ref[g])
            cnt = jnp.minimum(pos + 1, w).astype(F32)
            dpc = dp / cnt
            s = jnp.concatenate([dpc, head_ref[:, sl]], axis=0)
            step = 1
            while step < w:
                s = s + pltpu.roll(s, n - step, 0)
                step *= 2
            head_ref[:, sl] = dpc[:POOL_HALO, :]
            du_ref[:, sl] = (s[:tm, :] - dp).astype(BF)
        dh0 = _dot_nt(du_ref[...], win_ref[...])
        x0v = x0_ref[...]
        h0_ref[...] = _rms_fwd(x0v, g_ref[...]).astype(BF)
        dx, dg = _rms_bwd(x0v, g_ref[...], dh0)
        dx0_ref[...] = dx1v + dx
        dg_ref[...] += dg

    bf_rows = jax.ShapeDtypeStruct((T, D), BF)
    vec = jax.ShapeDtypeStruct((1, D), F32)
    body, in_specs, args = _after(
        dep, body,
        [pl.BlockSpec((tm, D), rev), pl.BlockSpec((tm, D), rev), _full_spec((1, D)), pl.BlockSpec((tm, D), rev),
         _full_spec((D, D)), _full_spec((4, POOL_GC, POOL_GC)), _full_spec((1, D)), _full_spec((D, D))],
        [dx1, x0, g0, p, w_in, w_grp, scale, w_out])
    return pl.pallas_call(
        body, name="pool_bwd", grid=(nt,),
        in_specs=in_specs,
        out_specs=[pl.BlockSpec((tm, D), rev)] * 5 + [_full_spec((1, D))] * 2,
        out_shape=[jax.ShapeDtypeStruct((T, D), F32), bf_rows, bf_rows, bf_rows, bf_rows, vec, vec],
        scratch_shapes=[pltpu.VMEM((POOL_HALO, D), F32)],
        compiler_params=_params("arbitrary"),
    )(*args)


def _loss_head(xv, tv, gv):
    diff = _rms_fwd(xv, gv) - tv
    loss = 0.5 * jnp.sum(jnp.mean(diff * diff, axis=-1, keepdims=True), axis=0, keepdims=True)
    dx, dg = _rms_bwd(xv, gv, diff * (1.0 / D))
    return loss, dx, dg


def _ffn_fwd(x, g, wg_t, wu_t, wd, name, gather=(), head=None, tm=512, fk=1408):
    T = x.shape[0]
    ng = len(gather)
    nh_in, nh_out = (2, 2) if head is not None else (0, 0)
    ni, nk = T // tm, F // fk

    def body(x_ref, g_ref, wg_ref, wu_ref, wd_ref, *refs):
        head_in, refs = refs[:nh_in], refs[nh_in:]
        shard_refs, (xo_ref, a_ref, b_ref, s_ref), full_refs = refs[:ng], refs[ng:ng + 4], refs[ng + 4:2 * ng + 4]
        refs = refs[2 * ng + 4:]
        head_out, refs = refs[:nh_out], refs[nh_out:]
        acc_ref, h_ref = refs[:2]
        begin, finish = _gather_phases(shard_refs, full_refs, *refs[2:])
        i = pl.program_id(0)
        k = pl.program_id(1)

        @pl.when(jnp.logical_and(i == 0, k == 0))
        def _():
            begin()
            for r in head_out:
                r[...] = jnp.zeros_like(r)

        @pl.when(k == 0)
        def _():
            acc_ref[...] = jnp.zeros_like(acc_ref)
            h_ref[...] = _rms_fwd(x_ref[...], g_ref[...]).astype(BF)

        hv = h_ref[...]
        a = _dot_nt(hv, wg_ref[...])
        b = _dot_nt(hv, wu_ref[...])
        s = ((a * jax.nn.sigmoid(a)) * b).astype(BF)
        a_ref[...] = a.astype(BF)
        b_ref[...] = b.astype(BF)
        s_ref[...] = s
        acc_ref[...] += _dot(s, wd_ref[...])

        @pl.when(k == nk - 1)
        def _():
            xo = x_ref[...] + acc_ref[...]
            if head is None:
                xo_ref[...] = xo
            else:
                loss, dx, dg = _loss_head(xo, head_in[0][...], head_in[1][...])
                xo_ref[...] = dx
                head_out[0][...] += loss
                head_out[1][...] += dg

        @pl.when(jnp.logical_and(i == ni - 1, k == nk - 1))
        def _():
            finish()

    row = pl.BlockSpec((tm, D), lambda i, k: (i, 0))
    wsp = pl.BlockSpec((fk, D), lambda i, k: (k, 0))
    act = pl.BlockSpec((tm, fk), lambda i, k: (i, k))
    hbm = pl.BlockSpec(memory_space=pl.ANY)
    act_shape = jax.ShapeDtypeStruct((T, F), BF)
    vec = pl.BlockSpec((1, D), lambda i, k: (0, 0))
    one = pl.BlockSpec((1, 1), lambda i, k: (0, 0))
    return pl.pallas_call(
        body, name=name, grid=(ni, nk),
        in_specs=[row, vec, wsp, wsp, wsp] + [row, vec][:nh_in] + [hbm] * ng,
        out_specs=[row, act, act, act] + [hbm] * ng + [one, vec][:nh_out],
        out_shape=[jax.ShapeDtypeStruct((T, D), F32), act_shape, act_shape, act_shape]
        + [jax.ShapeDtypeStruct((N_DEV,) + a.shape, a.dtype) for a in gather]
        + [jax.ShapeDtypeStruct((1, 1), F32), jax.ShapeDtypeStruct((1, D), F32)][:nh_out],
        scratch_shapes=[pltpu.VMEM((tm, D), F32), pltpu.VMEM((tm, D), BF)] + _gather_scratch(ng),
        compiler_params=_params("arbitrary", "arbitrary"),
    )(x, g, wg_t, wu_t, wd, *(head or ()), *gather)


def _ffn_bwd(dxo, x_in, g, a, b, wg_t, wu_t, wd, name, tm=512, fk=1408, dep=None):
    T = x_in.shape[0]

    def body(dxo_ref, x_ref, g_ref, a_ref, b_ref, wg_ref, wu_ref, wd_ref,
             dx_ref, da_ref, db_ref, h_ref, dy_ref, dg_ref, dh_ref):
        i = pl.program_id(0)
        k = pl.program_id(1)

        @pl.when(jnp.logical_and(i == 0, k == 0))
        def _():
            dg_ref[...] = jnp.zeros_like(dg_ref)

        @pl.when(k == 0)
        def _():
            h_ref[...] = _rms_fwd(x_ref[...], g_ref[...]).astype(BF)
            dy_ref[...] = dxo_ref[...].astype(BF)
            dh_ref[...] = jnp.zeros_like(dh_ref)

        ds = _dot_nt(dy_ref[...], wd_ref[...])
        av = a_ref[...].astype(F32)
        bv = b_ref[...].astype(F32)
        sig = jax.nn.sigmoid(av)
        db = (ds * (av * sig)).astype(BF)
        da = (ds * bv * (sig * (1.0 + av * (1.0 - sig)))).astype(BF)
        da_ref[...] = da
        db_ref[...] = db
        dh_ref[...] += _dot(da, wg_ref[...]) + _dot(db, wu_ref[...])

        @pl.when(k == pl.num_programs(1) - 1)
        def _():
            dx, dg = _rms_bwd(x_ref[...], g_ref[...], dh_ref[...])
            dx_ref[...] = dxo_ref[...] + dx
            dg_ref[...] += dg

    row = pl.BlockSpec((tm, D), lambda i, k: (i, 0))
    wsp = pl.BlockSpec((fk, D), lambda i, k: (k, 0))
    act = pl.BlockSpec((tm, fk), lambda i, k: (i, k))
    vec = pl.BlockSpec((1, D), lambda i, k: (0, 0))
    act_shape = jax.ShapeDtypeStruct((T, F), BF)
    body, in_specs, args = _after(dep, body, [row, row, vec, act, act, wsp, wsp, wsp], [dxo, x_in, g, a, b, wg_t, wu_t, wd])
    return pl.pallas_call(
        body, name=name, grid=(T // tm, F // fk),
        in_specs=in_specs,
        out_specs=[row, act, act, row, row, vec],
        out_shape=[jax.ShapeDtypeStruct((T, D), F32), act_shape, act_shape, jax.ShapeDtypeStruct((T, D), BF),
                   jax.ShapeDtypeStruct((T, D), BF), jax.ShapeDtypeStruct((1, D), F32)],
        scratch_shapes=[pltpu.VMEM((tm, D), F32)],
        compiler_params=_params("arbitrary", "arbitrary"),
    )(*args)


def _wgrad(a, b, name, tk=1024, mblk=None):
    T, M = a.shape
    N = b.shape[1]
    mblk = M if mblk is None else mblk

    def body(a_ref, b_ref, o_ref, acc_ref):
        t = pl.program_id(1)

        @pl.when(t == 0)
        def _():
            acc_ref[...] = jnp.zeros_like(acc_ref)

        acc_ref[...] += _dot_tn(a_ref[...].astype(BF), b_ref[...].astype(BF))

        @pl.when(t == pl.num_programs(1) - 1)
        def _():
            o_ref[...] = acc_ref[...].astype(BF)

    return pl.pallas_call(
        body, name=name, grid=(M // mblk, T // tk),
        in_specs=[pl.BlockSpec((tk, mblk), lambda m, t: (t, m)), pl.BlockSpec((tk, N), lambda m, t: (t, 0))],
        out_specs=pl.BlockSpec((mblk, N), lambda m, t: (m, 0)),
        out_shape=jax.ShapeDtypeStruct((M, N), BF),
        scratch_shapes=[pltpu.VMEM((mblk, N), F32)],
        compiler_params=_params("parallel", "arbitrary"),
    )(a, b)


def _wgrad_pool_groups(p, dzp, tk=2048):
    T = p.shape[0]

    def body(p_ref, d_ref, o_ref, acc_ref):
        t = pl.program_id(1)

        @pl.when(t == 0)
        def _():
            acc_ref[...] = jnp.zeros_like(acc_ref)

        acc_ref[...] += _dot_tn(p_ref[...], d_ref[...])

        @pl.when(t == pl.num_programs(1) - 1)
        def _():
            o_ref[...] = acc_ref[...].astype(BF)

    blk = pl.BlockSpec((tk, POOL_GC), lambda g, t: (t, g))
    return pl.pallas_call(
        body, name="pool_wgrad_groups", grid=(4, T // tk),
        in_specs=[blk, blk],
        out_specs=pl.BlockSpec((None, POOL_GC, POOL_GC), lambda g, t: (g, 0, 0)),
        out_shape=jax.ShapeDtypeStruct((4, POOL_GC, POOL_GC), BF),
        scratch_shapes=[pltpu.VMEM((POOL_GC, POOL_GC), F32)],
        compiler_params=_params("parallel", "arbitrary"),
    )(p, dzp)


def _rot_half(t):
    lane = lax.broadcasted_iota(jnp.int32, t.shape, 1)
    first = (lane % HEAD_DIM) < (HEAD_DIM // 2)
    return jnp.where(first, -pltpu.roll(t, 128 - HEAD_DIM // 2, 1), pltpu.roll(t, HEAD_DIM // 2, 1))


def _scatter_rows(dst_ref, scr_ref, d, cast):
    nc, rows, _ = scr_ref.shape
    n = rows // d
    for c in range(nc):
        sl = slice(c * 128, (c + 1) * 128)
        for r in range(d):
            src = scr_ref[c] if d == 1 else scr_ref.at[c][pl.ds(r, n, stride=d), :]
            dst_ref[r, :, sl] = src.astype(cast)


def _gather_rows(scr_ref, src_ref, d):
    nc, rows, _ = scr_ref.shape
    n = rows // d
    for c in range(nc):
        sl = slice(c * 128, (c + 1) * 128)
        for r in range(d):
            val = src_ref[r, :, sl].astype(F32)
            if d == 1:
                scr_ref[c] = val
            else:
                scr_ref.at[c][pl.ds(r, n, stride=d), :] = val


def _chunks_to_rows(scr_ref):
    nc = scr_ref.shape[0]
    return scr_ref[0] if nc == 1 else jnp.concatenate([scr_ref[c] for c in range(nc)], axis=1)


def _rows_to_chunks(scr_ref, val):
    for c in range(scr_ref.shape[0]):
        scr_ref[c] = val[:, c * 128:(c + 1) * 128]


def _rope(t, cv, sv, scale):
    return (t * cv + _rot_half(t) * sv) * scale


def _qkv_fwd(x, g, w3s, cos, sin, tm=512):
    T = x.shape[0]
    C = GROUP_LANES
    nc = C // 128

    def body(x_ref, g_ref, w0, w1, w2, cos_ref, sin_ref, o0, o1, o2, scr_ref):
        h = _rms_fwd(x_ref[...], g_ref[...]).astype(BF)
        cv = cos_ref[...]
        sv = sin_ref[...]
        for gi, (w_ref, o_ref, d) in enumerate(zip((w0, w1, w2), (o0, o1, o2), DILATIONS)):
            t_all = _dot(h, w_ref[...])
            for w in range(3):
                chunks = [t_all[:, w * C + c * 128:w * C + (c + 1) * 128] for c in range(nc)]
                if w < 2:
                    chunks = [_rope(tc, cv, sv, HEAD_DIM ** -0.5 if w == 0 else 1.0) for tc in chunks]
                if d == 1:
                    for c in range(nc):
                        o_ref[w, 0, :, c * 128:(c + 1) * 128] = chunks[c].astype(BF)
                else:
                    scr = scr_ref.at[gi * 3 + w]
                    for c in range(nc):
                        scr[c] = chunks[c]
                    _scatter_rows(o_ref.at[w], scr, d, BF)

    return pl.pallas_call(
        body, name="qkv_fwd", grid=(T // tm,),
        in_specs=[_row_spec(tm, D), _full_spec((1, D))] + [_full_spec((D, 3 * C))] * 3 + [_row_spec(tm, 128), _row_spec(tm, 128)],
        out_specs=[pl.BlockSpec((3, d, tm // d, C), lambda i: (0, 0, i, 0)) for d in DILATIONS],
        out_shape=[jax.ShapeDtypeStruct((3, d, T // d, C), BF) for d in DILATIONS],
        scratch_shapes=[pltpu.VMEM((9, nc, tm, 128), F32)],
        compiler_params=_params("parallel"),
    )(x, g, *w3s, cos, sin)


def _att_chunk(L):
    return min(L, 2048)


def _head_mask(shape, h):
    lane = lax.broadcasted_iota(jnp.int32, shape, 1)
    return (lane // HEAD_DIM) == (h % 2)


def _attn_fwd(qkv, nh, name):
    _, d, L, C = qkv.shape
    lc = _att_chunk(L)
    nblk = lc // ATT_W

    def body(q_ref, k_ref, kh_ref, v_ref, vh_ref, o_ref, st_ref):
        i = pl.program_id(1)
        qi = lax.broadcasted_iota(jnp.int32, (ATT_W, 2 * ATT_W), 0)
        kj = lax.broadcasted_iota(jnp.int32, (ATT_W, 2 * ATT_W), 1)
        band = jnp.logical_and(kj >= qi, kj <= qi + ATT_W)
        lane = lax.broadcasted_iota(jnp.int32, (ATT_W, 128), 1)

        def block(row0, kc, vc, mask):
            rows = pl.ds(row0, ATT_W)
            lses = []
            for hp in range(C // 128):
                sl = slice(hp * 128, (hp + 1) * 128)
                qp = q_ref[rows, sl]
                kp = kc[:, sl]
                vp = vc[:, sl]
                outs = []
                for h in range(2 * hp, min(2 * hp + 2, nh)):
                    hm = _head_mask(qp.shape, h)
                    s = _dot_nt(jnp.where(hm, qp, jnp.zeros_like(qp)), kp)
                    s = jnp.where(mask, s, NEG_INF)
                    m = jnp.max(s, axis=-1, keepdims=True)
                    e = jnp.exp(s - m)
                    den = jnp.sum(e, axis=-1, keepdims=True)
                    p = (e * pl.reciprocal(den)).astype(BF)
                    outs.append(_dot(p, vp))
                    lses.append(m + jnp.log(den))
                if len(outs) == 2:
                    o = jnp.where(_head_mask(outs[0].shape, 0), outs[0], outs[1])
                else:
                    o = jnp.where(_head_mask(outs[0].shape, 0), outs[0], 0.0)
                o_ref[rows, sl] = o.astype(BF)
            mm = lses[0]
            for l in lses[1:]:
                mm = jnp.maximum(mm, l)
            tot = jnp.exp(lses[0] - mm)
            for l in lses[1:]:
                tot = tot + jnp.exp(l - mm)
            tile = jnp.where(lane == LSE_GROUP_LANE, mm + jnp.log(tot) - math.log(nh), 0.0)
            for h, l in enumerate(lses):
                tile = jnp.where(lane == h, l, tile)
            st_ref[rows, :] = tile

        first_mask = jnp.logical_and(band, jnp.logical_or(kj >= ATT_W, i > 0))
        block(0, jnp.concatenate([kh_ref[...], k_ref[pl.ds(0, ATT_W), :]], axis=0),
              jnp.concatenate([vh_ref[...], v_ref[pl.ds(0, ATT_W), :]], axis=0), first_mask)

        if nblk > 1:
            def step(blk, carry):
                prev = pl.ds(pl.multiple_of((blk - 1) * ATT_W, ATT_W), 2 * ATT_W)
                block(pl.multiple_of(blk * ATT_W, ATT_W), k_ref[prev, :], v_ref[prev, :], band)
                return carry
            lax.fori_loop(1, nblk, step, 0, unroll=True)

    main = lambda w: pl.BlockSpec((None, None, lc, C), lambda r, i: (w, r, i, 0))
    halo = lambda w: pl.BlockSpec((None, None, ATT_W, C), lambda r, i: (w, r, jnp.maximum(i * nblk - 1, 0), 0))
    return pl.pallas_call(
        body, name=name, grid=(d, L // lc),
        in_specs=[main(0), main(1), halo(1), main(2), halo(2)],
        out_specs=[pl.BlockSpec((None, lc, C), lambda r, i: (r, i, 0)), pl.BlockSpec((None, lc, 128), lambda r, i: (r, i, 0))],
        out_shape=[jax.ShapeDtypeStruct((d, L, C), BF), jax.ShapeDtypeStruct((d, L, 128), F32)],
        compiler_params=_params("parallel", "arbitrary"),
    )(qkv, qkv, qkv, qkv, qkv)


def _attn_bwd(qkv, do, st, dst, nh, name):
    _, d, L, C = qkv.shape
    lc = _att_chunk(L)
    nblk = lc // ATT_W
    nchunk = L // lc

    def body(q_ref, qn_ref, k_ref, kh_ref, v_ref, vh_ref, do_ref, don_ref, st_ref, stn_ref, ds_ref, dsn_ref, o_ref,
             p_scr, dsc_scr):
        i = pl.program_id(1)
        qi = lax.broadcasted_iota(jnp.int32, (ATT_W, 2 * ATT_W), 0)
        kj = lax.broadcasted_iota(jnp.int32, (ATT_W, 2 * ATT_W), 1)
        band_q = jnp.logical_and(kj >= qi, kj <= qi + ATT_W)
        qa = lax.broadcasted_iota(jnp.int32, (2 * ATT_W, ATT_W), 0)
        kb = lax.broadcasted_iota(jnp.int32, (2 * ATT_W, ATT_W), 1)
        band_k = jnp.logical_and(qa >= kb, qa <= kb + ATT_W)

        def probs(qm, kp, lse, mask):
            s = _dot_nt(qm, kp)
            return jnp.where(mask, jnp.exp(s - lse), 0.0)

        def q_block(blk, kc, vc, mask):
            rows = pl.ds(blk * ATT_W, ATT_W)
            stv = st_ref[rows, :]
            dsv = ds_ref[rows, :]
            for hp in range(C // 128):
                sl = slice(hp * 128, (hp + 1) * 128)
                qp = q_ref[rows, sl]
                dop = do_ref[rows, sl]
                kp = kc[:, sl]
                vp = vc[:, sl]
                outs = []
                for h in range(2 * hp, min(2 * hp + 2, nh)):
                    hm = _head_mask(qp.shape, h)
                    p = probs(jnp.where(hm, qp, jnp.zeros_like(qp)), kp, _lane_col(stv, h), mask)
                    dp = _dot_nt(jnp.where(hm, dop, jnp.zeros_like(dop)), vp)
                    dsc = (p * (dp - _lane_col(dsv, h))).astype(BF)
                    p_scr[blk, h] = p.astype(BF)
                    dsc_scr[blk, h] = dsc
                    outs.append(_dot(dsc, kp))
                if len(outs) == 2:
                    dq = jnp.where(_head_mask(outs[0].shape, 0), outs[0], outs[1])
                else:
                    dq = jnp.where(_head_mask(outs[0].shape, 0), outs[0], 0.0)
                o_ref[0, rows, sl] = dq.astype(BF)

        def k_block_kept(m):
            rows = pl.ds(m * ATT_W, ATT_W)
            two = pl.ds(m * ATT_W, 2 * ATT_W)
            for hp in range(C // 128):
                sl = slice(hp * 128, (hp + 1) * 128)
                qp = q_ref[two, sl]
                dop = do_ref[two, sl]
                dk, dv = None, None
                for h in range(2 * hp, min(2 * hp + 2, nh)):
                    hm = _head_mask(qp.shape, h)
                    qm = jnp.where(hm, qp, jnp.zeros_like(qp))
                    dom = jnp.where(hm, dop, jnp.zeros_like(dop))
                    dsc = jnp.concatenate([dsc_scr[m, h, :, ATT_W:], dsc_scr[m + 1, h, :, :ATT_W]], axis=0)
                    p = jnp.concatenate([p_scr[m, h, :, ATT_W:], p_scr[m + 1, h, :, :ATT_W]], axis=0)
                    dk_h = _dot_tn(dsc, qm)
                    dv_h = _dot_tn(p, dom)
                    dk = dk_h if dk is None else dk + dk_h
                    dv = dv_h if dv is None else dv + dv_h
                o_ref[1, rows, sl] = dk.astype(BF)
                o_ref[2, rows, sl] = dv.astype(BF)

        def k_block(row0, qq, doo, stv, dsv, mask):
            rows = pl.ds(row0, ATT_W)
            for hp in range(C // 128):
                sl = slice(hp * 128, (hp + 1) * 128)
                qp = qq[:, sl]
                dop = doo[:, sl]
                kp = k_ref[rows, sl]
                vp = v_ref[rows, sl]
                dks, dvs = [], []
                for h in range(2 * hp, min(2 * hp + 2, nh)):
                    hm = _head_mask(qp.shape, h)
                    qm = jnp.where(hm, qp, jnp.zeros_like(qp))
                    dom = jnp.where(hm, dop, jnp.zeros_like(dop))
                    p = probs(qm, kp, _lane_col(stv, h), mask)
                    dp = _dot_nt(dom, vp)
                    dsc = (p * (dp - _lane_col(dsv, h))).astype(BF)
                    dks.append(_dot_tn(dsc, qm))
                    dvs.append(_dot_tn(p.astype(BF), dom))
                o_ref[1, rows, sl] = sum(dks[1:], dks[0]).astype(BF)
                o_ref[2, rows, sl] = sum(dvs[1:], dvs[0]).astype(BF)

        first_mask = jnp.logical_and(band_q, jnp.logical_or(kj >= ATT_W, i > 0))
        q_block(0, jnp.concatenate([kh_ref[...], k_ref[pl.ds(0, ATT_W), :]], axis=0),
                jnp.concatenate([vh_ref[...], v_ref[pl.ds(0, ATT_W), :]], axis=0), first_mask)
        for blk in range(1, nblk):
            prev = pl.ds((blk - 1) * ATT_W, 2 * ATT_W)
            q_block(blk, k_ref[prev, :], v_ref[prev, :], band_q)
        for m in range(nblk - 1):
            k_block_kept(m)

        last = pl.ds((nblk - 1) * ATT_W, ATT_W)
        last_mask = jnp.logical_and(band_k, jnp.logical_or(qa < ATT_W, i < nchunk - 1))
        k_block((nblk - 1) * ATT_W,
                jnp.concatenate([q_ref[last, :], qn_ref[...]], axis=0),
                jnp.concatenate([do_ref[last, :], don_ref[...]], axis=0),
                jnp.concatenate([st_ref[last, :], stn_ref[...]], axis=0),
                jnp.concatenate([ds_ref[last, :], dsn_ref[...]], axis=0), last_mask)

    nb_all = L // ATT_W
    main4 = lambda w: pl.BlockSpec((None, None, lc, C), lambda r, i: (w, r, i, 0))
    prev4 = lambda w: pl.BlockSpec((None, None, ATT_W, C), lambda r, i: (w, r, jnp.maximum(i * nblk - 1, 0), 0))
    next4 = lambda w: pl.BlockSpec((None, None, ATT_W, C), lambda r, i: (w, r, jnp.minimum((i + 1) * nblk, nb_all - 1), 0))
    main3 = lambda n: pl.BlockSpec((None, lc, n), lambda r, i: (r, i, 0))
    next3 = lambda n: pl.BlockSpec((None, ATT_W, n), lambda r, i: (r, jnp.minimum((i + 1) * nblk, nb_all - 1), 0))
    return pl.pallas_call(
        body, name=name, grid=(d, nchunk),
        in_specs=[main4(0), next4(0), main4(1), prev4(1), main4(2), prev4(2),
                  main3(C), next3(C), main3(128), next3(128), main3(128), next3(128)],
        out_specs=pl.BlockSpec((3, None, lc, C), lambda r, i: (0, r, i, 0)),
        out_shape=jax.ShapeDtypeStruct((3, d, L, C), BF),
        scratch_shapes=[pltpu.VMEM((nblk, nh, ATT_W, 2 * ATT_W), BF)] * 2,
        compiler_params=_params("parallel", "arbitrary"),
    )(qkv, qkv, qkv, qkv, qkv, qkv, do, do, st, st, dst, dst)


def _alpha_from(lse_nat):
    m = jnp.maximum(jnp.maximum(lse_nat[0], lse_nat[1]), lse_nat[2])
    e = [jnp.exp(l - m) for l in lse_nat]
    inv = 1.0 / (e[0] + e[1] + e[2])
    return [ei * inv for ei in e]


def _attn_out_fwd(x, os_, sts, wo, tm=1024):
    T = x.shape[0]
    C = GROUP_LANES

    def body(x_ref, o0, o1, o2, s0, s1, s2, w_ref, xo_ref, m_ref, al_ref, oscr, sscr):
        o_refs, st_refs = (o0, o1, o2), (s0, s1, s2)
        lses = []
        for g, d in enumerate(DILATIONS):
            _gather_rows(sscr.at[g], st_refs[g], d)
            lses.append(_lane_col(sscr[g, 0], LSE_GROUP_LANE))
        alpha = _alpha_from(lses)
        for g, d in enumerate(DILATIONS):
            _gather_rows(oscr, o_refs[g], d)
            m_ref[:, g * C:(g + 1) * C] = (_chunks_to_rows(oscr) * (3.0 * alpha[g])).astype(BF)
        xo_ref[...] = x_ref[...] + _dot(m_ref[...], w_ref[...])
        lane = lax.broadcasted_iota(jnp.int32, (tm, 128), 1)
        al_ref[...] = jnp.where(lane == 0, alpha[0], jnp.where(lane == 1, alpha[1], jnp.where(lane == 2, alpha[2], 0.0)))

    o_specs = [pl.BlockSpec((d, tm // d, C), lambda i: (0, i, 0)) for d in DILATIONS]
    st_specs = [pl.BlockSpec((d, tm // d, 128), lambda i: (0, i, 0)) for d in DILATIONS]
    return pl.pallas_call(
        body, name="attn_out_fwd", grid=(T // tm,),
        in_specs=[_row_spec(tm, D)] + o_specs + st_specs + [_full_spec((3 * C, D))],
        out_specs=[_row_spec(tm, D), _row_spec(tm, 3 * C), _row_spec(tm, 128)],
        out_shape=[jax.ShapeDtypeStruct((T, D), F32), jax.ShapeDtypeStruct((T, 3 * C), BF), jax.ShapeDtypeStruct((T, 128), F32)],
        scratch_shapes=[pltpu.VMEM((C // 128, tm, 128), F32), pltpu.VMEM((3, 1, tm, 128), F32)],
        compiler_params=_params("parallel"),
    )(x, *os_, *sts, wo)


def _attn_out_bwd(dx, os_, sts, alpha, wo, tm=1024, dep=None):
    T = dx.shape[0]
    C = GROUP_LANES

    def body(dx_ref, o0, o1, o2, s0, s1, s2, al_ref, w_ref, do0, do1, do2, ds0, ds1, ds2, dmscr, oscr, sscr, tscr):
        o_refs, st_refs = (o0, o1, o2), (s0, s1, s2)
        do_refs, ds_refs = (do0, do1, do2), (ds0, ds1, ds2)
        dyb = dx_ref[...].astype(BF)
        alv = al_ref[...]
        alpha_g = [_lane_col(alv, g) for g in range(3)]
        dalpha = []
        for g, d in enumerate(DILATIONS):
            dm = _dot_nt(dyb, w_ref[g * C:(g + 1) * C, :])
            _rows_to_chunks(dmscr.at[g], dm)
            _gather_rows(oscr.at[g], o_refs[g], d)
            _gather_rows(sscr.at[g], st_refs[g], d)
            dalpha.append(3.0 * jnp.sum(dm * _chunks_to_rows(oscr.at[g]), axis=-1, keepdims=True))
        mean_da = alpha_g[0] * dalpha[0] + alpha_g[1] * dalpha[1] + alpha_g[2] * dalpha[2]
        lane = lax.broadcasted_iota(jnp.int32, (tm, 128), 1)
        seg = (lax.broadcasted_iota(jnp.int32, (C, 128), 0) // HEAD_DIM == lax.broadcasted_iota(jnp.int32, (C, 128), 1)).astype(BF)
        for g, d in enumerate(DILATIONS):
            nh = HEAD_GROUPS[g]
            dlse_g = alpha_g[g] * (dalpha[g] - mean_da)
            do_nat = _chunks_to_rows(dmscr.at[g]) * (3.0 * alpha_g[g])
            prod = do_nat * _chunks_to_rows(oscr.at[g])
            hi = prod.astype(BF)
            delta = _dot(hi, seg) + _dot((prod - hi.astype(F32)).astype(BF), seg)
            stv = sscr[g, 0]
            dlse = dlse_g * jnp.exp(stv - _lane_col(stv, LSE_GROUP_LANE)) * (1.0 / nh)
            tile = jnp.where(lane < nh, delta - dlse, 0.0)
            _rows_to_chunks(dmscr.at[g], do_nat)
            _scatter_rows(do_refs[g], dmscr.at[g], d, BF)
            tscr[0] = tile
            _scatter_rows(ds_refs[g], tscr, d, F32)

    o_specs = [pl.BlockSpec((d, tm // d, C), lambda i: (0, i, 0)) for d in DILATIONS]
    st_specs = [pl.BlockSpec((d, tm // d, 128), lambda i: (0, i, 0)) for d in DILATIONS]
    body, in_specs, args = _after(
        dep, body, [_row_spec(tm, D)] + o_specs + st_specs + [_row_spec(tm, 128)] + [_full_spec((3 * C, D))],
        [dx, *os_, *sts, alpha, wo])
    return pl.pallas_call(
        body, name="attn_out_bwd", grid=(T // tm,),
        in_specs=in_specs,
        out_specs=o_specs + st_specs,
        out_shape=[jax.ShapeDtypeStruct((d, T // d, C), BF) for d in DILATIONS]
        + [jax.ShapeDtypeStruct((d, T // d, 128), F32) for d in DILATIONS],
        scratch_shapes=[pltpu.VMEM((3, C // 128, tm, 128), F32), pltpu.VMEM((3, C // 128, tm, 128), F32),
                        pltpu.VMEM((3, 1, tm, 128), F32), pltpu.VMEM((1, tm, 128), F32)],
        compiler_params=_params("parallel"),
    )(*args)


def _qkv_bwd(dqkvs, cos, sin, x2, g, w3s, dx3, tm=512):
    T = x2.shape[0]
    C = GROUP_LANES

    def body(dq0, dq1, dq2, cos_ref, sin_ref, x_ref, g_ref, w0, w1, w2, dx3_ref,
             dx_ref, n0, n1, n2, h_ref, dg_ref, scr, dh_ref):
        dq_refs, w_refs, n_refs = (dq0, dq1, dq2), (w0, w1, w2), (n0, n1, n2)

        @pl.when(pl.program_id(0) == 0)
        def _():
            dg_ref[...] = jnp.zeros_like(dg_ref)

        cv = cos_ref[...]
        sv = sin_ref[...]
        dh_ref[...] = jnp.zeros_like(dh_ref)
        for gi, d in enumerate(DILATIONS):
            for w in range(3):
                _gather_rows(scr, dq_refs[gi].at[w], d)
                if w < 2:
                    scale = HEAD_DIM ** -0.5 if w == 0 else 1.0
                    for c in range(C // 128):
                        t = scr[c]
                        scr[c] = (t * cv - _rot_half(t) * sv) * scale
                n_refs[gi][:, w * C:(w + 1) * C] = _chunks_to_rows(scr).astype(BF)
            dh_ref[...] += _dot_nt(n_refs[gi][...], w_refs[gi][...])
        xv = x_ref[...]
        h_ref[...] = _rms_fwd(xv, g_ref[...]).astype(BF)
        dx, dg = _rms_bwd(xv, g_ref[...], dh_ref[...])
        dx_ref[...] = dx3_ref[...] + dx
        dg_ref[...] += dg

    dq_specs = [pl.BlockSpec((3, d, tm // d, C), lambda i: (0, 0, i, 0)) for d in DILATIONS]
    nat = _row_spec(tm, 3 * C)
    nshape = jax.ShapeDtypeStruct((T, 3 * C), BF)
    return pl.pallas_call(
        body, name="qkv_bwd", grid=(T // tm,),
        in_specs=dq_specs + [_row_spec(tm, 128), _row_spec(tm, 128), _row_spec(tm, D), _full_spec((1, D))]
        + [pl.BlockSpec((D, 3 * C), lambda i: (0, 0), pipeline_mode=pl.Buffered(1))] * 3 + [_row_spec(tm, D)],
        out_specs=[_row_spec(tm, D), nat, nat, nat, _row_spec(tm, D), _full_spec((1, D))],
        out_shape=[jax.ShapeDtypeStruct((T, D), F32), nshape, nshape, nshape, jax.ShapeDtypeStruct((T, D), BF),
                   jax.ShapeDtypeStruct((1, D), F32)],
        scratch_shapes=[pltpu.VMEM((C // 128, tm, 128), F32), pltpu.VMEM((tm, D), F32)],
        compiler_params=_params("arbitrary"),
    )(*dqkvs, cos, sin, x2, g, *w3s, dx3)


def _place():
    x, y, c = lax.axis_index("x"), lax.axis_index("y"), lax.axis_index("c")
    return x, y, c


def _gather_phases(ins, outs, *sems):
    n = len(ins)
    if n == 0:
        return (lambda: None), (lambda: None)
    send_sems, recv_sems, local_sems = sems
    x, y, c = _place()
    me, sibling = (x, y, c), (x, y, 1 - c)
    chips = [(1 - x, y), (x, 1 - y), (1 - x, 1 - y)]

    def slot(a, px, py, pc):
        return outs[a].at[4 * px + 2 * py + pc]

    def copy(a, k, block, to, src=None):
        return pltpu.make_async_remote_copy(
            src_ref=slot(a, *block) if src is None else src, dst_ref=slot(a, *block),
            send_sem=send_sems.at[a, k], recv_sem=recv_sems.at[a, k], device_id=to, device_id_type=MESH)

    def mine(a):
        return pltpu.make_async_copy(ins[a], slot(a, *me), local_sems.at[a])

    def first(a):
        return [copy(a, 0, me, sibling, src=ins[a])] + [copy(a, 1 + j, me, (*chip, c), src=ins[a]) for j, chip in enumerate(chips)]

    def begin():
        for a in range(n):
            mine(a).start()
        for a in range(n):
            for cp in first(a):
                cp.start()

    def finish():
        passed = []
        for a in range(n):
            for j, chip in enumerate(chips):
                copy(a, 1 + j, (*chip, c), me).wait_recv()
                fwd = copy(a, 4 + j, (*chip, c), sibling)
                fwd.start()
                passed.append(fwd)
        for a in range(n):
            copy(a, 0, sibling, me).wait_recv()
            for j, chip in enumerate(chips):
                copy(a, 4 + j, (*chip, 1 - c), me).wait_recv()
        for a in range(n):
            for cp in first(a):
                cp.wait_send()
        for cp in passed:
            cp.wait_send()
        for a in range(n):
            mine(a).wait()

    return begin, finish


def _gather_scratch(n):
    return [pltpu.SemaphoreType.DMA((n, 7)), pltpu.SemaphoreType.DMA((n, 7)), pltpu.SemaphoreType.DMA((n,))] if n else []


def _allgather(arrs, name):
    n = len(arrs)

    def body(*refs):
        begin, finish = _gather_phases(refs[:n], refs[n:2 * n], *refs[2 * n:])
        begin()
        finish()

    hbm = pl.BlockSpec(memory_space=pl.ANY)
    return pl.pallas_call(
        body, name=name,
        in_specs=[hbm] * n, out_specs=[hbm] * n,
        out_shape=[jax.ShapeDtypeStruct((N_DEV,) + a.shape, a.dtype) for a in arrs],
        scratch_shapes=_gather_scratch(n),
    )(*arrs)


def _peer(k):
    x, y, c = _place()
    px = 1 - x if k & 4 else x
    py = 1 - y if k & 2 else y
    pc = 1 - c if k & 1 else c
    return (px, py, pc), 4 * px + 2 * py + pc


HBM_SPEC = pl.BlockSpec(memory_space=pltpu.HBM)
SEM_SPEC = pl.BlockSpec(memory_space=pltpu.SEMAPHORE)
EFFECT = pltpu.SideEffectType.DATAFLOW_SIDE_EFFECTING


def _exchange_start(arrs, name, same_block=False, dep=None):
    n = len(arrs)
    n_dep = 0 if dep is None else 1

    def body(*refs):
        srcs, lands = refs[:n], refs[n:2 * n]
        send_sems, recv_sems, local_sems = refs[2 * n + n_dep:2 * n + n_dep + 3]
        token = refs[-1]
        x, y, c = _place()
        me = 4 * x + 2 * y + c
        block = (lambda a, j: srcs[a]) if same_block else (lambda a, j: srcs[a].at[j])
        for a in range(n):
            pltpu.make_async_copy(block(a, me), lands[a].at[me], local_sems.at[a]).start()
        for a in range(n):
            for k in range(1, N_DEV):
                to, to_idx = _peer(k)
                pltpu.make_async_remote_copy(
                    src_ref=block(a, to_idx), dst_ref=lands[a].at[me],
                    send_sem=send_sems.at[a * (N_DEV - 1) + k - 1], recv_sem=recv_sems.at[a * (N_DEV - 1) + k - 1], device_id=to, device_id_type=MESH).start()
        token[...] = jnp.zeros_like(token)

    land_shape = (lambda a: (N_DEV,) + a.shape) if same_block else (lambda a: a.shape)
    src_shapes = [pltpu.HBM(a.shape, a.dtype) for a in arrs]
    land_shapes = [pltpu.HBM(land_shape(a), a.dtype) for a in arrs]
    outs = pl.pallas_call(
        body, name=name,
        out_shape=(pltpu.SemaphoreType.DMA((n * (N_DEV - 1),)), pltpu.SemaphoreType.DMA((n * (N_DEV - 1),)),
                   pltpu.SemaphoreType.DMA((n,)), *src_shapes, *land_shapes, jax.ShapeDtypeStruct((8, 128), F32)),
        in_specs=[HBM_SPEC] * (2 * n) + [pl.BlockSpec(memory_space=pl.ANY)] * n_dep,
        out_specs=(SEM_SPEC, SEM_SPEC, SEM_SPEC, *([HBM_SPEC] * (2 * n)), pl.BlockSpec(memory_space=pltpu.VMEM)),
        input_output_aliases={i: 3 + i for i in range(2 * n)},
        compiler_params=pltpu.CompilerParams(has_side_effects=EFFECT),
    )(*[pltpu.with_memory_space_constraint(a, pltpu.HBM) for a in arrs],
      *[pltpu.with_memory_space_constraint(lax.empty(land_shape(a), a.dtype), pltpu.HBM) for a in arrs],
      *([] if dep is None else [dep]))
    return outs[0], outs[1], outs[2], outs[3:3 + n], outs[3 + n:3 + 2 * n], outs[3 + 2 * n]


def _exchange_wait(send_sems, recv_sems, local_sems, src_thru, land_thru, after, name):
    n = len(src_thru)
    after = list(after) if isinstance(after, (list, tuple)) else [after]
    same_block = src_thru[0].shape != land_thru[0].shape

    def body(*refs):
        srcs, lands = refs[:n], refs[n:2 * n]
        send_sems, recv_sems, local_sems = refs[2 * n:2 * n + 3]
        x, y, c = _place()
        me = 4 * x + 2 * y + c
        for a in range(n):
            pltpu.make_async_copy(srcs[a] if same_block else srcs[a].at[me], lands[a].at[me], local_sems.at[a]).wait()
            for k in range(1, N_DEV):
                frm, frm_idx = _peer(k)
                cp = pltpu.make_async_remote_copy(
                    src_ref=srcs[a] if same_block else srcs[a].at[frm_idx], dst_ref=lands[a].at[frm_idx],
                    send_sem=send_sems.at[a * (N_DEV - 1) + k - 1], recv_sem=recv_sems.at[a * (N_DEV - 1) + k - 1], device_id=frm, device_id_type=MESH)
                cp.wait_send()
                cp.wait_recv()

    outs = pl.pallas_call(
        body, name=name,
        out_shape=tuple(pltpu.HBM(a.shape, a.dtype) for a in (*src_thru, *land_thru)),
        in_specs=[HBM_SPEC] * (2 * n) + [SEM_SPEC, SEM_SPEC, SEM_SPEC] + [pl.BlockSpec(memory_space=pl.ANY)] * len(after),
        out_specs=[HBM_SPEC] * (2 * n),
        input_output_aliases={i: i for i in range(2 * n)},
        compiler_params=pltpu.CompilerParams(has_side_effects=EFFECT),
    )(*src_thru, *land_thru, send_sems, recv_sems, local_sems, *after)
    return outs[n:]


def _row_tile(rows):
    for t in (256, 176, 128, 8):
        if rows % t == 0:
            return t
    return rows


def _adamw(parts, w, m, v, name, layer=None, into=None):
    K, R, C = parts.shape
    tr = _row_tile(R)
    n_into = 0 if into is None else 4

    def body(p_ref, w_ref, m_ref, v_ref, *refs):
        g_ref, d_ref, nm_ref, nv_ref = refs[n_into:]
        g = p_ref[0].astype(F32)
        for k in range(1, K):
            g = g + p_ref[k].astype(F32)
        nm = ADAM_B1 * m_ref[...] + (1.0 - ADAM_B1) * g
        nv = ADAM_B2 * v_ref[...] + (1.0 - ADAM_B2) * jnp.square(g)
        m_hat = nm / (1.0 - ADAM_B1 ** ADAM_STEP)
        v_hat = nv / (1.0 - ADAM_B2 ** ADAM_STEP)
        g_ref[...] = g
        d_ref[...] = -ADAM_LR * (m_hat / (jnp.sqrt(v_hat) + ADAM_EPS) + ADAM_WD * w_ref[...])
        nm_ref[...] = nm
        nv_ref[...] = nv

    blk = _row_spec(tr, C)
    parts_spec = pl.BlockSpec((K, tr, C), lambda i: (0, i, 0))
    if layer is None:
        out_blk, shp = blk, jax.ShapeDtypeStruct((R, C), F32)
    else:
        out_blk, shp = pl.BlockSpec((None, tr, C), lambda i: (layer, i, 0)), jax.ShapeDtypeStruct((2, R, C), F32)
    return pl.pallas_call(
        body, name=name, grid=(R // tr,),
        in_specs=[parts_spec, blk, blk, blk] + [pl.BlockSpec(memory_space=pl.ANY)] * n_into,
        out_specs=[out_blk] * 4,
        out_shape=[shp] * 4,
        input_output_aliases={4 + j: j for j in range(n_into)},
        compiler_params=_params("parallel"),
    )(parts, w, m, v, *(into or ()))


def _rope_tables(T):
    inv_freq = 1.0 / (ROPE_THETA ** (jnp.arange(0, HEAD_DIM, 2, dtype=F32) / HEAD_DIM))
    inv_freq = jnp.tile(inv_freq, 4)
    hi = (64.0 * jnp.arange(T // 64, dtype=F32))[:, None] * inv_freq[None, :]
    lo = jnp.arange(64, dtype=F32)[:, None] * inv_freq[None, :]
    ch, sh, cl, sl = jnp.cos(hi)[:, None], jnp.sin(hi)[:, None], jnp.cos(lo)[None], jnp.sin(lo)[None]
    return (ch * cl - sh * sl).reshape(T, 128), (sh * cl + ch * sl).reshape(T, 128)


def _pad_lanes(a, n):
    return jnp.pad(a, ((0, 0),) * (a.ndim - 1) + ((0, n - a.shape[-1]),))


POOL = ("pool_in", "pool_grp", "pool_out")
FFN0 = ("gate0", "up0", "down0")
LAYER1 = ("qkv", "attn_out", "gate1", "up1", "down1")


def _qkv_regions():
    return [(k * D + off * HEAD_DIM, nh * HEAD_DIM, gi, k)
            for k in range(3) for gi, (nh, off) in enumerate(zip(HEAD_GROUPS, HEAD_OFFS))]


def _overlaps(lo, n, step):
    return [(j, max(lo, j * step) - j * step, min(lo + n, (j + 1) * step) - j * step)
            for j in range(lo // step, (lo + n - 1) // step + 1)]


def _layout_weights(gw):
    w = {}
    if "pool_in" in gw:
        w["pool_in"] = gw["pool_in"].reshape(D, D)
        w["pool_grp"] = jnp.transpose(gw["pool_grp"], (1, 0, 2, 3)).reshape(4, POOL_GC, POOL_GC)
        w["pool_out"] = gw["pool_out"].reshape(D, D)
    if "qkv" in gw:
        wo = gw["attn_out"].reshape(D, D)
        shard_cols = 3 * D // N_DEV
        pieces = [[] for _ in HEAD_GROUPS]
        for lo, n, gi, _ in _qkv_regions():
            cols = [gw["qkv"][j][:, a:b] for j, a, b in _overlaps(lo, n, shard_cols)]
            pieces[gi].append(_pad_lanes(jnp.concatenate(cols, axis=1), GROUP_LANES))
        w["qkv"] = [jnp.concatenate(p, axis=1) for p in pieces]
        w["attn_out"] = jnp.concatenate([jnp.pad(wo[off * HEAD_DIM:(off + nh) * HEAD_DIM], ((0, GROUP_LANES - nh * HEAD_DIM), (0, 0)))
                                         for nh, off in zip(HEAD_GROUPS, HEAD_OFFS)], axis=0)
    for nm in gw:
        if nm[:-1] in ("gate", "up", "down"):
            w[nm] = gw[nm].reshape(F, D)
    return w


def _local_step(x, target, w, ffn0_shards, layer1_shards, norm_mix, norm_ffn, norm_final, pool_scale, emit):
    T = x.shape[0]
    cos, sin = _rope_tables(T)
    nm = [norm_mix[i:i + 1] for i in range(2)]
    nf = [norm_ffn[i:i + 1] for i in range(2)]
    nfin = norm_final.reshape(1, D)

    x1, p, *ffn0 = _pool_fwd(x, nm[0], w["pool_in"], w["pool_grp"], pool_scale, w["pool_out"], ffn0_shards)
    w = {**w, **_layout_weights(dict(zip(FFN0, ffn0)))}
    x2, a0, b0, s0, *layer1 = _ffn_fwd(x1, nf[0], w["gate0"], w["up0"], w["down0"], "ffn_fwd0", gather=layer1_shards)
    w = {**w, **_layout_weights(dict(zip(LAYER1, layer1)))}
    qkvs = _qkv_fwd(x2, nm[1], w["qkv"], cos, sin)
    att = [_attn_fwd(qkvs[g], HEAD_GROUPS[g], f"attn_fwd{g}") for g in range(3)]
    os_, sts = [a[0] for a in att], [a[1] for a in att]
    x3, merged, alpha = _attn_out_fwd(x2, os_, sts, w["attn_out"])
    dx4, a1, b1, s1, loss, dg_final = _ffn_fwd(x3, nf[1], w["gate1"], w["up1"], w["down1"], "ffn_fwd1", head=(target, nfin))

    g = {}
    dx3, da1, db1, h3, dy4, dg_ffn1 = _ffn_bwd(dx4, x3, nf[1], a1, b1, w["gate1"], w["up1"], w["down1"], "ffn_bwd1")
    g["down1"] = _wgrad(s1, dy4, "down_wgrad1", mblk=F // 2)
    g["gate1"] = _wgrad(da1, h3, "gate_wgrad1", mblk=F // 2)
    g["up1"] = _wgrad(db1, h3, "up_wgrad1", mblk=F // 2)
    dep = emit("ffn1", g)
    dos_and_stats = _attn_out_bwd(dx3, os_, sts, alpha, w["attn_out"], dep=dep)
    dos, dsts = dos_and_stats[:3], dos_and_stats[3:]
    g["attn_out"] = _wgrad(merged, dx3, "attn_out_wgrad")
    dqkvs = [_attn_bwd(qkvs[gi], dos[gi], sts[gi], dsts[gi], HEAD_GROUPS[gi], f"attn_bwd{gi}") for gi in range(3)]
    dx2, n0, n1, n2, h2b, dg_mix1 = _qkv_bwd(dqkvs, cos, sin, x2, nm[1], w["qkv"], dx3)
    g["qkv"] = [_wgrad(h2b, n, f"qkv_wgrad{i}") for i, n in enumerate((n0, n1, n2))]
    dep = emit("attn", g)
    dx1, da0, db0, h1, dy2, dg_ffn0 = _ffn_bwd(dx2, x1, nf[0], a0, b0, w["gate0"], w["up0"], w["down0"], "ffn_bwd0", dep=dep)
    g["down0"] = _wgrad(s0, dy2, "down_wgrad0", mblk=F // 2)
    g["gate0"] = _wgrad(da0, h1, "gate_wgrad0", mblk=F // 2)
    g["up0"] = _wgrad(db0, h1, "up_wgrad0", mblk=F // 2)
    dep = emit("ffn0", g)
    dx0, z, dzp, du, h0b, dscale, dg_mix0 = _pool_bwd(dx1, x, nm[0], p, w["pool_in"], w["pool_grp"], pool_scale, w["pool_out"], dep=dep)
    g["pool_out"] = _wgrad(z, dx1, "pool_out_wgrad")
    g["pool_in"] = _wgrad(h0b, du, "pool_in_wgrad")
    g["pool_grp"] = _wgrad_pool_groups(p, dzp)
    emit("pool", g)

    small = jnp.concatenate([dg_mix0, dg_mix1, dg_ffn0, dg_ffn1, dg_final, dscale,
                             jnp.broadcast_to(loss, (1, D)), jnp.zeros((1, D), F32)], axis=0)
    return dx0, small


GROUPS = {"ffn1": ("down1", "gate1", "up1"), "attn": ("qkv", "attn_out"), "ffn0": ("down0", "gate0", "up0"),
          "pool": ("pool_in", "pool_out", "pool_grp")}


def _grad_blocks(group, g):
    blocks = {}
    if group == "pool":
        blocks["pool_in"] = g["pool_in"].reshape(N_DEV, D // N_DEV, D)
        blocks["pool_out"] = g["pool_out"].reshape(N_DEV, D // N_DEV, D)
        blocks["pool_grp"] = jnp.transpose(g["pool_grp"].reshape(4, N_DEV, POOL_GC // N_DEV, POOL_GC), (1, 0, 2, 3)).reshape(N_DEV, 4 * POOL_GC // N_DEV, POOL_GC)
    elif group == "attn":
        wo = jnp.concatenate([g["attn_out"][gi * GROUP_LANES:gi * GROUP_LANES + nh * HEAD_DIM] for gi, nh in enumerate(HEAD_GROUPS)], axis=0)
        blocks["attn_out"] = wo.reshape(N_DEV, D // N_DEV, D)
        shard_cols = 3 * D // N_DEV
        per_dev = [[] for _ in range(N_DEV)]
        for lo, n, gi, k in _qkv_regions():
            for j, a, b in _overlaps(lo, n, shard_cols):
                start = k * GROUP_LANES + j * shard_cols + a - lo
                per_dev[j].append(g["qkv"][gi][:, start:start + b - a])
        blocks["qkv"] = jnp.stack([jnp.concatenate(p, axis=1) for p in per_dev])
    else:
        for nm in GROUPS[group]:
            blocks[nm] = g[nm].reshape(N_DEV, F // N_DEV, D)
    return [blocks[nm] for nm in GROUPS[group]]


def kernel(x, norm_mix, norm_ffn, norm_final, pool_w_in, pool_w_group, pool_scale, pool_w_out, attn_w_qkv, attn_w_out, ffn_w_gate, ffn_w_up, ffn_w_down, loss_target, m_norm_mix, m_norm_ffn, m_norm_final, m_pool_w_in, m_pool_w_group, m_pool_scale, m_pool_w_out, m_attn_w_qkv, m_attn_w_out, m_ffn_w_gate, m_ffn_w_up, m_ffn_w_down, v_norm_mix, v_norm_ffn, v_norm_final, v_pool_w_in, v_pool_w_group, v_pool_scale, v_pool_w_out, v_attn_w_qkv, v_attn_w_out, v_ffn_w_gate, v_ffn_w_up, v_ffn_w_down):
    shard = {
        "pool_in": pool_w_in[0], "pool_grp": pool_w_group[0], "pool_out": pool_w_out[0],
        "qkv": attn_w_qkv[0], "attn_out": attn_w_out[0],
    }
    tr3 = lambda t: jnp.transpose(t, (0, 2, 1))
    gate_t, up_t = tr3(ffn_w_gate), tr3(ffn_w_up)
    for l in range(2):
        shard[f"gate{l}"] = gate_t[l]
        shard[f"up{l}"] = up_t[l]
        shard[f"down{l}"] = ffn_w_down[l]
    shard = {k: v.astype(BF) for k, v in shard.items()}
    w0 = _layout_weights(dict(zip(POOL, _allgather([shard[k] for k in POOL], "weights_allgather"))))

    started = {}

    def emit(group, g):
        started[group] = _exchange_start(_grad_blocks(group, g), f"grads_start_{group}")
        return started[group][5]

    grad_x, small = _local_step(x[0], loss_target[0], w0, [shard[k] for k in FFN0], [shard[k] for k in LAYER1], norm_mix, norm_ffn, norm_final, pool_scale, emit)

    small_st = _exchange_start([small], "small_start", same_block=True, dep=started["pool"][5])

    def finish(group, after):
        lands = _exchange_wait(*started[group][:5], after, f"grads_wait_{group}")
        return dict(zip(GROUPS[group], lands))

    def upd(parts, wt, mt, vt, name):
        shape = wt.shape
        r2 = lambda t: t.reshape(parts.shape[1:])
        outs = _adamw(parts, r2(wt), r2(mt), r2(vt), name)
        return [o.reshape(shape) for o in outs]

    ffn_state = (("gate", gate_t, tr3(m_ffn_w_gate), tr3(v_ffn_w_gate)), ("up", up_t, tr3(m_ffn_w_up), tr3(v_ffn_w_up)),
                 ("down", ffn_w_down, m_ffn_w_down, v_ffn_w_down))

    def ffn_layer(rcv, l, other=None):
        return {nm: _adamw(rcv[f"{nm}{l}"], wt[l], mt[l], vt[l], f"adamw_{nm}{l}", layer=l,
                           into=None if other is None else other[nm]) for nm, wt, mt, vt in ffn_state}

    res = {}
    ffn1 = ffn_layer(finish("ffn1", small_st[5]), 1)
    rcv = finish("attn", [ffn1[nm][0] for nm in ffn1])
    res["attn_w_qkv"] = upd(rcv["qkv"], attn_w_qkv, m_attn_w_qkv, v_attn_w_qkv, "adamw_qkv")
    res["attn_w_out"] = upd(rcv["attn_out"], attn_w_out, m_attn_w_out, v_attn_w_out, "adamw_attn_out")
    ffn0 = ffn_layer(finish("ffn0", [res["attn_w_qkv"][0], res["attn_w_out"][0]]), 0, other=ffn1)
    rcv = finish("pool", [ffn0[nm][0] for nm in ffn0])
    res["pool_w_in"] = upd(rcv["pool_in"], pool_w_in, m_pool_w_in, v_pool_w_in, "adamw_pool_in")
    res["pool_w_group"] = upd(rcv["pool_grp"], pool_w_group, m_pool_w_group, v_pool_w_group, "adamw_pool_grp")
    res["pool_w_out"] = upd(rcv["pool_out"], pool_w_out, m_pool_w_out, v_pool_w_out, "adamw_pool_out")
    res["ffn_w_gate"] = [tr3(t) for t in ffn0["gate"]]
    res["ffn_w_up"] = [tr3(t) for t in ffn0["up"]]
    res["ffn_w_down"] = ffn0["down"]
    small_all = _exchange_wait(*small_st[:5], [res[k][0] for k in ("pool_w_in", "pool_w_group", "pool_w_out")], "small_wait")[0]

    small_w = jnp.concatenate([norm_mix, norm_ffn, norm_final[None], pool_scale, jnp.zeros((2, D), F32)], axis=0)
    small_m = jnp.concatenate([m_norm_mix, m_norm_ffn, m_norm_final[None], m_pool_scale, jnp.zeros((2, D), F32)], axis=0)
    small_v = jnp.concatenate([v_norm_mix, v_norm_ffn, v_norm_final[None], v_pool_scale, jnp.ones((2, D), F32)], axis=0)
    sg, sd, sm, sv = _adamw(small_all, small_w, small_m, small_v, "adamw_small")
    loss = sg[6, 0]
    res["norm_mix"] = [t[0:2] for t in (sg, sd, sm, sv)]
    res["norm_ffn"] = [t[2:4] for t in (sg, sd, sm, sv)]
    res["norm_final"] = [t[4] for t in (sg, sd, sm, sv)]
    res["pool_scale"] = [t[5:6] for t in (sg, sd, sm, sv)]

    order = ["norm_mix", "norm_ffn", "norm_final", "pool_w_in", "pool_w_group", "pool_scale", "pool_w_out",
             "attn_w_qkv", "attn_w_out", "ffn_w_gate", "ffn_w_up", "ffn_w_down"]
    return (loss, grad_x[None], *[res[k][0] for k in order], *[res[k][1] for k in order],
            *[res[k][2] for k in order], *[res[k][3] for k in order])
```

```python
import math

import jax
import jax.numpy as jnp
from jax import lax
from jax.experimental import pallas as pl
from jax.experimental.pallas import tpu as pltpu

D = 1024
F = 2816
N_DEV = 8
EPS = 1e-6
NEG_INF = -1e30
POOL_WINDOWS = (2, 4, 8, 16)
POOL_HALO = 16
POOL_GC = 256
HEAD_DIM = 64
HEAD_GROUPS = (6, 5, 5)
HEAD_OFFS = (0, 6, 11)
DILATIONS = (1, 4, 16)
ATT_W = 128
GROUP_LANES = 384
LSE_GROUP_LANE = 8
ROPE_THETA = 10000.0
ADAM_LR, ADAM_B1, ADAM_B2, ADAM_EPS, ADAM_WD, ADAM_STEP = 0.001, 0.9, 0.999, 1e-08, 0.01, 10

BF = jnp.bfloat16
F32 = jnp.float32
VMEM_LIMIT = 56 * 1024 * 1024
MESH = pl.DeviceIdType.MESH


def _params(*sem):
    return pltpu.CompilerParams(dimension_semantics=sem, vmem_limit_bytes=VMEM_LIMIT)


def _dot(a, b):
    return jnp.dot(a, b, preferred_element_type=F32)


def _dot_nt(a, b):
    return lax.dot_general(a, b, (((1,), (1,)), ((), ())), preferred_element_type=F32)


def _dot_tn(a, b):
    return lax.dot_general(a, b, (((0,), (0,)), ((), ())), preferred_element_type=F32)


def _rms_fwd(xv, g):
    r = lax.rsqrt(jnp.mean(xv * xv, axis=-1, keepdims=True) + EPS)
    return (xv * r) * g


def _rms_bwd(xv, g, dh):
    r = lax.rsqrt(jnp.mean(xv * xv, axis=-1, keepdims=True) + EPS)
    xhat = xv * r
    dg = jnp.sum(dh * xhat, axis=0, keepdims=True)
    dxh = dh * g
    dx = r * (dxh - xhat * jnp.mean(dxh * xhat, axis=-1, keepdims=True))
    return dx, dg


def _lane_col(tile, j):
    lane = lax.broadcasted_iota(jnp.int32, tile.shape, 1)
    return jnp.sum(jnp.where(lane == j, tile, 0.0), axis=-1, keepdims=True)


def _row_spec(tm, n):
    return pl.BlockSpec((tm, n), lambda i: (i, 0))


def _full_spec(shape):
    nd = len(shape)
    return pl.BlockSpec(shape, lambda *_: (0,) * nd)


def _after(dep, body, in_specs, args):
    if dep is None:
        return body, list(in_specs), list(args)

    def body_after(dep_ref, *refs):
        body(*refs)

    return body_after, [pl.BlockSpec(memory_space=pl.ANY)] + list(in_specs), [dep] + list(args)


def _pool_fwd(x, g, w_in, w_grp, scale, w_out, gather, tm=512, dep=None):
    T = x.shape[0]
    nt = T // tm
    n = tm + POOL_HALO
    ng = len(gather)

    def body(x_ref, g_ref, win_ref, wg_ref, sc_ref, wout_ref, *refs):
        shard_refs, (x1_ref, p_ref), full_refs = refs[:ng], refs[ng:ng + 2], refs[ng + 2:2 * ng + 2]
        tail_ref, z_ref = refs[2 * ng + 2:2 * ng + 4]
        begin, forward, finish = _gather_phases(shard_refs, full_refs, *refs[2 * ng + 4:])
        i = pl.program_id(0)

        @pl.when(i == 0)
        def _():
            begin()
            tail_ref[...] = jnp.zeros_like(tail_ref)

        u = _dot(_rms_fwd(x_ref[...], g_ref[...]).astype(BF), win_ref[...])
        pos = i * tm + lax.broadcasted_iota(jnp.int32, (tm, 1), 0)
        for g, w in enumerate(POOL_WINDOWS):
            sl = slice(g * POOL_GC, (g + 1) * POOL_GC)
            ug = u[:, sl]
            s = jnp.concatenate([tail_ref[:, sl], ug], axis=0)
            step = 1
            while step < w:
                s = s + pltpu.roll(s, step, 0)
                step *= 2
            cnt = jnp.minimum(pos + 1, w).astype(F32)
            pg = (s[POOL_HALO:, :] / cnt - ug).astype(BF)
            p_ref[:, sl] = pg
            z_ref[:, sl] = (_dot(pg, wg_ref[g]) * sc_ref[:, sl]).astype(BF)
        tail_ref[...] = u[tm - POOL_HALO:, :]
        x1_ref[...] = x_ref[...] + _dot(z_ref[...], wout_ref[...])

        @pl.when(i == nt - 1)
        def _():
            forward()
            finish()

    hbm = pl.BlockSpec(memory_space=pl.ANY)
    body, in_specs, args = _after(
        dep, body,
        [_row_spec(tm, D), _full_spec((1, D)), _full_spec((D, D)), _full_spec((4, POOL_GC, POOL_GC)), _full_spec((1, D)),
         _full_spec((D, D))] + [hbm] * ng,
        [x, g, w_in, w_grp, scale, w_out, *gather])
    return pl.pallas_call(
        body, name="pool_fwd", grid=(nt,),
        in_specs=in_specs,
        out_specs=[_row_spec(tm, D), _row_spec(tm, D)] + [hbm] * ng,
        out_shape=[jax.ShapeDtypeStruct((T, D), F32), jax.ShapeDtypeStruct((T, D), BF)]
        + [jax.ShapeDtypeStruct((N_DEV,) + a.shape, a.dtype) for a in gather],
        scratch_shapes=[pltpu.VMEM((POOL_HALO, D), F32), pltpu.VMEM((tm, D), BF)] + _gather_scratch(ng),
        compiler_params=_params("arbitrary"),
    )(*args)


def _pool_bwd(dx1, x0, g0, p, w_in, w_grp, scale, w_out, tm=512, dep=None):
    T = x0.shape[0]
    nt = T // tm
    n = tm + POOL_HALO
    rev = lambda i: (nt - 1 - i, 0)

    def body(dx1_ref, x0_ref, g_ref, p_ref, win_ref, wg_ref, sc_ref, wout_ref,
             dx0_ref, z_ref, dzp_ref, du_ref, h0_ref, dsc_ref, dg_ref, head_ref):
        i = pl.program_id(0)

        @pl.when(i == 0)
        def _():
            head_ref[...] = jnp.zeros_like(head_ref)
            dsc_ref[...] = jnp.zeros_like(dsc_ref)
            dg_ref[...] = jnp.zeros_like(dg_ref)

        dx1v = dx1_ref[...]
        dz = _dot_nt(dx1v.astype(BF), wout_ref[...])
        pos = (nt - 1 - i) * tm + lax.broadcasted_iota(jnp.int32, (tm, 1), 0)
        for g, w in enumerate(POOL_WINDOWS):
            sl = slice(g * POOL_GC, (g + 1) * POOL_GC)
            zpre = _dot(p_ref[:, sl], wg_ref[g])
            dzg = dz[:, sl]
            dsc_ref[:, sl] += jnp.sum(dzg * zpre, axis=0, keepdims=True)
            z_ref[:, sl] = (zpre * sc_ref[:, sl]).astype(BF)
            dzp = (dzg * sc_ref[:, sl]).astype(BF)
            dzp_ref[:, sl] = dzp
            dp = _dot_nt(dzp, wg_ref[g])
            cnt = jnp.minimum(pos + 1, w).astype(F32)
            dpc = dp / cnt
            s = jnp.concatenate([dpc, head_ref[:, sl]], axis=0)
            step = 1
            while step < w:
                s = s + pltpu.roll(s, n - step, 0)
                step *= 2
            head_ref[:, sl] = dpc[:POOL_HALO, :]
            du_ref[:, sl] = (s[:tm, :] - dp).astype(BF)
        dh0 = _dot_nt(du_ref[...], win_ref[...])
        x0v = x0_ref[...]
        h0_ref[...] = _rms_fwd(x0v, g_ref[...]).astype(BF)
        dx, dg = _rms_bwd(x0v, g_ref[...], dh0)
        dx0_ref[...] = dx1v + dx
        dg_ref[...] += dg

    bf_rows = jax.ShapeDtypeStruct((T, D), BF)
    vec = jax.ShapeDtypeStruct((1, D), F32)
    body, in_specs, args = _after(
        dep, body,
        [pl.BlockSpec((tm, D), rev), pl.BlockSpec((tm, D), rev), _full_spec((1, D)), pl.BlockSpec((tm, D), rev),
         _full_spec((D, D)), _full_spec((4, POOL_GC, POOL_GC)), _full_spec((1, D)), _full_spec((D, D))],
        [dx1, x0, g0, p, w_in, w_grp, scale, w_out])
    return pl.pallas_call(
        body, name="pool_bwd", grid=(nt,),
        in_specs=in_specs,
        out_specs=[pl.BlockSpec((tm, D), rev)] * 5 + [_full_spec((1, D))] * 2,
        out_shape=[jax.ShapeDtypeStruct((T, D), F32), bf_rows, bf_rows, bf_rows, bf_rows, vec, vec],
        scratch_shapes=[pltpu.VMEM((POOL_HALO, D), F32)],
        compiler_params=_params("arbitrary"),
    )(*args)


def _loss_head(xv, tv, gv):
    diff = _rms_fwd(xv, gv) - tv
    loss = 0.5 * jnp.sum(jnp.mean(diff * diff, axis=-1, keepdims=True), axis=0, keepdims=True)
    dx, dg = _rms_bwd(xv, gv, diff * (1.0 / D))
    return loss, dx, dg


def _ffn_fwd(x, g, wg_t, wu_t, wd, name, gather=(), head=None, tm=512, fk=1408):
    T = x.shape[0]
    ng = len(gather)
    nh_in, nh_out = (2, 2) if head is not None else (0, 0)
    ni, nk = T // tm, F // fk

    def body(x_ref, g_ref, wg_ref, wu_ref, wd_ref, *refs):
        head_in, refs = refs[:nh_in], refs[nh_in:]
        shard_refs, (xo_ref, a_ref, b_ref, s_ref), full_refs = refs[:ng], refs[ng:ng + 4], refs[ng + 4:2 * ng + 4]
        refs = refs[2 * ng + 4:]
        head_out, refs = refs[:nh_out], refs[nh_out:]
        acc_ref, h_ref = refs[:2]
        begin, forward, finish = _gather_phases(shard_refs, full_refs, *refs[2:])
        i = pl.program_id(0)
        k = pl.program_id(1)

        @pl.when(jnp.logical_and(i == 0, k == 0))
        def _():
            begin()
            for r in head_out:
                r[...] = jnp.zeros_like(r)

        @pl.when(k == 0)
        def _():
            acc_ref[...] = jnp.zeros_like(acc_ref)
            h_ref[...] = _rms_fwd(x_ref[...], g_ref[...]).astype(BF)

        hv = h_ref[...]
        a = _dot_nt(hv, wg_ref[...])
        b = _dot_nt(hv, wu_ref[...])
        s = ((a * jax.nn.sigmoid(a)) * b).astype(BF)
        a_ref[...] = a.astype(BF)
        b_ref[...] = b.astype(BF)
        s_ref[...] = s
        acc_ref[...] += _dot(s, wd_ref[...])

        @pl.when(k == nk - 1)
        def _():
            xo = x_ref[...] + acc_ref[...]
            if head is None:
                xo_ref[...] = xo
            else:
                loss, dx, dg = _loss_head(xo, head_in[0][...], head_in[1][...])
                xo_ref[...] = dx
                head_out[0][...] += loss
                head_out[1][...] += dg

        @pl.when(jnp.logical_and(i == (3 * ni) // 4, k == 0))
        def _():
            forward()

        @pl.when(jnp.logical_and(i == ni - 1, k == nk - 1))
        def _():
            finish()

    row = pl.BlockSpec((tm, D), lambda i, k: (i, 0))
    wsp = pl.BlockSpec((fk, D), lambda i, k: (k, 0))
    act = pl.BlockSpec((tm, fk), lambda i, k: (i, k))
    hbm = pl.BlockSpec(memory_space=pl.ANY)
    act_shape = jax.ShapeDtypeStruct((T, F), BF)
    vec = pl.BlockSpec((1, D), lambda i, k: (0, 0))
    one = pl.BlockSpec((1, 1), lambda i, k: (0, 0))
    return pl.pallas_call(
        body, name=name, grid=(ni, nk),
        in_specs=[row, vec, wsp, wsp, wsp] + [row, vec][:nh_in] + [hbm] * ng,
        out_specs=[row, act, act, act] + [hbm] * ng + [one, vec][:nh_out],
        out_shape=[jax.ShapeDtypeStruct((T, D), F32), act_shape, act_shape, act_shape]
        + [jax.ShapeDtypeStruct((N_DEV,) + a.shape, a.dtype) for a in gather]
        + [jax.ShapeDtypeStruct((1, 1), F32), jax.ShapeDtypeStruct((1, D), F32)][:nh_out],
        scratch_shapes=[pltpu.VMEM((tm, D), F32), pltpu.VMEM((tm, D), BF)] + _gather_scratch(ng),
        compiler_params=_params("arbitrary", "arbitrary"),
    )(x, g, wg_t, wu_t, wd, *(head or ()), *gather)


def _ffn_bwd(dxo, x_in, g, a, b, wg_t, wu_t, wd, name, tm=512, fk=1408, dep=None):
    T = x_in.shape[0]

    def body(dxo_ref, x_ref, g_ref, a_ref, b_ref, wg_ref, wu_ref, wd_ref,
             dx_ref, da_ref, db_ref, h_ref, dy_ref, dg_ref, dh_ref):
        i = pl.program_id(0)
        k = pl.program_id(1)

        @pl.when(jnp.logical_and(i == 0, k == 0))
        def _():
            dg_ref[...] = jnp.zeros_like(dg_ref)

        @pl.when(k == 0)
        def _():
            h_ref[...] = _rms_fwd(x_ref[...], g_ref[...]).astype(BF)
            dy_ref[...] = dxo_ref[...].astype(BF)
            dh_ref[...] = jnp.zeros_like(dh_ref)

        ds = _dot_nt(dy_ref[...], wd_ref[...])
        av = a_ref[...].astype(F32)
        bv = b_ref[...].astype(F32)
        sig = jax.nn.sigmoid(av)
        db = (ds * (av * sig)).astype(BF)
        da = (ds * bv * (sig * (1.0 + av * (1.0 - sig)))).astype(BF)
        da_ref[...] = da
        db_ref[...] = db
        dh_ref[...] += _dot(da, wg_ref[...]) + _dot(db, wu_ref[...])

        @pl.when(k == pl.num_programs(1) - 1)
        def _():
            dx, dg = _rms_bwd(x_ref[...], g_ref[...], dh_ref[...])
            dx_ref[...] = dxo_ref[...] + dx
            dg_ref[...] += dg

    row = pl.BlockSpec((tm, D), lambda i, k: (i, 0))
    wsp = pl.BlockSpec((fk, D), lambda i, k: (k, 0))
    act = pl.BlockSpec((tm, fk), lambda i, k: (i, k))
    vec = pl.BlockSpec((1, D), lambda i, k: (0, 0))
    act_shape = jax.ShapeDtypeStruct((T, F), BF)
    body, in_specs, args = _after(dep, body, [row, row, vec, act, act, wsp, wsp, wsp], [dxo, x_in, g, a, b, wg_t, wu_t, wd])
    return pl.pallas_call(
        body, name=name, grid=(T // tm, F // fk),
        in_specs=in_specs,
        out_specs=[row, act, act, row, row, vec],
        out_shape=[jax.ShapeDtypeStruct((T, D), F32), act_shape, act_shape, jax.ShapeDtypeStruct((T, D), BF),
                   jax.ShapeDtypeStruct((T, D), BF), jax.ShapeDtypeStruct((1, D), F32)],
        scratch_shapes=[pltpu.VMEM((tm, D), F32)],
        compiler_params=_params("arbitrary", "arbitrary"),
    )(*args)


def _wgrad(a, b, name, tk=1024, mblk=None):
    T, M = a.shape
    N = b.shape[1]
    mblk = M if mblk is None else mblk

    def body(a_ref, b_ref, o_ref, acc_ref):
        t = pl.program_id(1)

        @pl.when(t == 0)
        def _():
            acc_ref[...] = jnp.zeros_like(acc_ref)

        acc_ref[...] += _dot_tn(a_ref[...].astype(BF), b_ref[...].astype(BF))

        @pl.when(t == pl.num_programs(1) - 1)
        def _():
            o_ref[...] = acc_ref[...].astype(BF)

    return pl.pallas_call(
        body, name=name, grid=(M // mblk, T // tk),
        in_specs=[pl.BlockSpec((tk, mblk), lambda m, t: (t, m)), pl.BlockSpec((tk, N), lambda m, t: (t, 0))],
        out_specs=pl.BlockSpec((mblk, N), lambda m, t: (m, 0)),
        out_shape=jax.ShapeDtypeStruct((M, N), BF),
        scratch_shapes=[pltpu.VMEM((mblk, N), F32)],
        compiler_params=_params("parallel", "arbitrary"),
    )(a, b)


def _wgrad_pool_groups(p, dzp, tk=2048):
    T = p.shape[0]

    def body(p_ref, d_ref, o_ref, acc_ref):
        t = pl.program_id(1)

        @pl.when(t == 0)
        def _():
            acc_ref[...] = jnp.zeros_like(acc_ref)

        acc_ref[...] += _dot_tn(p_ref[...], d_ref[...])

        @pl.when(t == pl.num_programs(1) - 1)
        def _():
            o_ref[...] = acc_ref[...].astype(BF)

    blk = pl.BlockSpec((tk, POOL_GC), lambda g, t: (t, g))
    return pl.pallas_call(
        body, name="pool_wgrad_groups", grid=(4, T // tk),
        in_specs=[blk, blk],
        out_specs=pl.BlockSpec((None, POOL_GC, POOL_GC), lambda g, t: (g, 0, 0)),
        out_shape=jax.ShapeDtypeStruct((4, POOL_GC, POOL_GC), BF),
        scratch_shapes=[pltpu.VMEM((POOL_GC, POOL_GC), F32)],
        compiler_params=_params("parallel", "arbitrary"),
    )(p, dzp)


def _rot_half(t):
    lane = lax.broadcasted_iota(jnp.int32, t.shape, 1)
    first = (lane % HEAD_DIM) < (HEAD_DIM // 2)
    return jnp.where(first, -pltpu.roll(t, 128 - HEAD_DIM // 2, 1), pltpu.roll(t, HEAD_DIM // 2, 1))


def _scatter_rows(dst_ref, scr_ref, d, cast):
    nc, rows, _ = scr_ref.shape
    n = rows // d
    for c in range(nc):
        sl = slice(c * 128, (c + 1) * 128)
        for r in range(d):
            src = scr_ref[c] if d == 1 else scr_ref.at[c][pl.ds(r, n, stride=d), :]
            dst_ref[r, :, sl] = src.astype(cast)


def _gather_rows(scr_ref, src_ref, d):
    nc, rows, _ = scr_ref.shape
    n = rows // d
    for c in range(nc):
        sl = slice(c * 128, (c + 1) * 128)
        for r in range(d):
            val = src_ref[r, :, sl].astype(F32)
            if d == 1:
                scr_ref[c] = val
            else:
                scr_ref.at[c][pl.ds(r, n, stride=d), :] = val


def _chunks_to_rows(scr_ref):
    nc = scr_ref.shape[0]
    return scr_ref[0] if nc == 1 else jnp.concatenate([scr_ref[c] for c in range(nc)], axis=1)


def _rows_to_chunks(scr_ref, val):
    for c in range(scr_ref.shape[0]):
        scr_ref[c] = val[:, c * 128:(c + 1) * 128]


def _rope(t, cv, sv, scale):
    return (t * cv + _rot_half(t) * sv) * scale


def _qkv_fwd(x, g, w3s, cos, sin, tm=512):
    T = x.shape[0]
    C = GROUP_LANES
    nc = C // 128

    def body(x_ref, g_ref, w0, w1, w2, cos_ref, sin_ref, o0, o1, o2, scr_ref):
        h = _rms_fwd(x_ref[...], g_ref[...]).astype(BF)
        cv = cos_ref[...]
        sv = sin_ref[...]
        for gi, (w_ref, o_ref, d) in enumerate(zip((w0, w1, w2), (o0, o1, o2), DILATIONS)):
            t_all = _dot(h, w_ref[...])
            for w in range(3):
                chunks = [t_all[:, w * C + c * 128:w * C + (c + 1) * 128] for c in range(nc)]
                if w < 2:
                    chunks = [_rope(tc, cv, sv, HEAD_DIM ** -0.5 if w == 0 else 1.0) for tc in chunks]
                if d == 1:
                    for c in range(nc):
                        o_ref[w, 0, :, c * 128:(c + 1) * 128] = chunks[c].astype(BF)
                else:
                    scr = scr_ref.at[gi * 3 + w]
                    for c in range(nc):
                        scr[c] = chunks[c]
                    _scatter_rows(o_ref.at[w], scr, d, BF)

    return pl.pallas_call(
        body, name="qkv_fwd", grid=(T // tm,),
        in_specs=[_row_spec(tm, D), _full_spec((1, D))] + [_full_spec((D, 3 * C))] * 3 + [_row_spec(tm, 128), _row_spec(tm, 128)],
        out_specs=[pl.BlockSpec((3, d, tm // d, C), lambda i: (0, 0, i, 0)) for d in DILATIONS],
        out_shape=[jax.ShapeDtypeStruct((3, d, T // d, C), BF) for d in DILATIONS],
        scratch_shapes=[pltpu.VMEM((9, nc, tm, 128), F32)],
        compiler_params=_params("parallel"),
    )(x, g, *w3s, cos, sin)


def _att_chunk(L):
    return min(L, 2048)


def _head_mask(shape, h):
    lane = lax.broadcasted_iota(jnp.int32, shape, 1)
    return (lane // HEAD_DIM) == (h % 2)


def _attn_fwd(qkv, nh, name):
    _, d, L, C = qkv.shape
    lc = _att_chunk(L)
    nblk = lc // ATT_W

    def body(q_ref, k_ref, kh_ref, v_ref, vh_ref, o_ref, st_ref):
        i = pl.program_id(1)
        qi = lax.broadcasted_iota(jnp.int32, (ATT_W, 2 * ATT_W), 0)
        kj = lax.broadcasted_iota(jnp.int32, (ATT_W, 2 * ATT_W), 1)
        band = jnp.logical_and(kj >= qi, kj <= qi + ATT_W)
        lane = lax.broadcasted_iota(jnp.int32, (ATT_W, 128), 1)

        def block(row0, kc, vc, mask):
            rows = pl.ds(row0, ATT_W)
            lses = []
            for hp in range(C // 128):
                sl = slice(hp * 128, (hp + 1) * 128)
                qp = q_ref[rows, sl]
                kp = kc[:, sl]
                vp = vc[:, sl]
                outs = []
                for h in range(2 * hp, min(2 * hp + 2, nh)):
                    hm = _head_mask(qp.shape, h)
                    s = _dot_nt(jnp.where(hm, qp, jnp.zeros_like(qp)), kp)
                    s = jnp.where(mask, s, NEG_INF)
                    m = jnp.max(s, axis=-1, keepdims=True)
                    e = jnp.exp(s - m)
                    den = jnp.sum(e, axis=-1, keepdims=True)
                    p = (e * pl.reciprocal(den)).astype(BF)
                    outs.append(_dot(p, vp))
                    lses.append(m + jnp.log(den))
                if len(outs) == 2:
                    o = jnp.where(_head_mask(outs[0].shape, 0), outs[0], outs[1])
                else:
                    o = jnp.where(_head_mask(outs[0].shape, 0), outs[0], 0.0)
                o_ref[rows, sl] = o.astype(BF)
            mm = lses[0]
            for l in lses[1:]:
                mm = jnp.maximum(mm, l)
            tot = jnp.exp(lses[0] - mm)
            for l in lses[1:]:
                tot = tot + jnp.exp(l - mm)
            tile = jnp.where(lane == LSE_GROUP_LANE, mm + jnp.log(tot) - math.log(nh), 0.0)
            for h, l in enumerate(lses):
                tile = jnp.where(lane == h, l, tile)
            st_ref[rows, :] = tile

        first_mask = jnp.logical_and(band, jnp.logical_or(kj >= ATT_W, i > 0))
        block(0, jnp.concatenate([kh_ref[...], k_ref[pl.ds(0, ATT_W), :]], axis=0),
              jnp.concatenate([vh_ref[...], v_ref[pl.ds(0, ATT_W), :]], axis=0), first_mask)

        if nblk > 1:
            def step(blk, carry):
                prev = pl.ds(pl.multiple_of((blk - 1) * ATT_W, ATT_W), 2 * ATT_W)
                block(pl.multiple_of(blk * ATT_W, ATT_W), k_ref[prev, :], v_ref[prev, :], band)
                return carry
            lax.fori_loop(1, nblk, step, 0, unroll=True)

    main = lambda w: pl.BlockSpec((None, None, lc, C), lambda r, i: (w, r, i, 0))
    halo = lambda w: pl.BlockSpec((None, None, ATT_W, C), lambda r, i: (w, r, jnp.maximum(i * nblk - 1, 0), 0))
    return pl.pallas_call(
        body, name=name, grid=(d, L // lc),
        in_specs=[main(0), main(1), halo(1), main(2), halo(2)],
        out_specs=[pl.BlockSpec((None, lc, C), lambda r, i: (r, i, 0)), pl.BlockSpec((None, lc, 128), lambda r, i: (r, i, 0))],
        out_shape=[jax.ShapeDtypeStruct((d, L, C), BF), jax.ShapeDtypeStruct((d, L, 128), F32)],
        compiler_params=_params("parallel", "arbitrary"),
    )(qkv, qkv, qkv, qkv, qkv)


def _attn_bwd(qkv, do, st, dst, nh, name):
    _, d, L, C = qkv.shape
    lc = _att_chunk(L)
    nblk = lc // ATT_W
    nchunk = L // lc

    def body(q_ref, qn_ref, k_ref, kh_ref, v_ref, vh_ref, do_ref, don_ref, st_ref, stn_ref, ds_ref, dsn_ref, o_ref,
             p_scr, dsc_scr):
        i = pl.program_id(1)
        qi = lax.broadcasted_iota(jnp.int32, (ATT_W, 2 * ATT_W), 0)
        kj = lax.broadcasted_iota(jnp.int32, (ATT_W, 2 * ATT_W), 1)
        band_q = jnp.logical_and(kj >= qi, kj <= qi + ATT_W)
        qa = lax.broadcasted_iota(jnp.int32, (2 * ATT_W, ATT_W), 0)
        kb = lax.broadcasted_iota(jnp.int32, (2 * ATT_W, ATT_W), 1)
        band_k = jnp.logical_and(qa >= kb, qa <= kb + ATT_W)

        def probs(qm, kp, lse, mask):
            s = _dot_nt(qm, kp)
            return jnp.where(mask, jnp.exp(s - lse), 0.0)

        def q_block(blk, kc, vc, mask):
            rows = pl.ds(blk * ATT_W, ATT_W)
            stv = st_ref[rows, :]
            dsv = ds_ref[rows, :]
            for hp in range(C // 128):
                sl = slice(hp * 128, (hp + 1) * 128)
                qp = q_ref[rows, sl]
                dop = do_ref[rows, sl]
                kp = kc[:, sl]
                vp = vc[:, sl]
                outs = []
                for h in range(2 * hp, min(2 * hp + 2, nh)):
                    hm = _head_mask(qp.shape, h)
                    p = probs(jnp.where(hm, qp, jnp.zeros_like(qp)), kp, _lane_col(stv, h), mask)
                    dp = _dot_nt(jnp.where(hm, dop, jnp.zeros_like(dop)), vp)
                    dsc = (p * (dp - _lane_col(dsv, h))).astype(BF)
                    p_scr[blk, h] = p.astype(BF)
                    dsc_scr[blk, h] = dsc
                    outs.append(_dot(dsc, kp))
                if len(outs) == 2:
                    dq = jnp.where(_head_mask(outs[0].shape, 0), outs[0], outs[1])
                else:
                    dq = jnp.where(_head_mask(outs[0].shape, 0), outs[0], 0.0)
                o_ref[0, rows, sl] = dq.astype(BF)

        def k_block_kept(m):
            rows = pl.ds(m * ATT_W, ATT_W)
            two = pl.ds(m * ATT_W, 2 * ATT_W)
            for hp in range(C // 128):
                sl = slice(hp * 128, (hp + 1) * 128)
                qp = q_ref[two, sl]
                dop = do_ref[two, sl]
                dk, dv = None, None
                for h in range(2 * hp, min(2 * hp + 2, nh)):
                    hm = _head_mask(qp.shape, h)
                    qm = jnp.where(hm, qp, jnp.zeros_like(qp))
                    dom = jnp.where(hm, dop, jnp.zeros_like(dop))
                    dsc = jnp.concatenate([dsc_scr[m, h, :, ATT_W:], dsc_scr[m + 1, h, :, :ATT_W]], axis=0)
                    p = jnp.concatenate([p_scr[m, h, :, ATT_W:], p_scr[m + 1, h, :, :ATT_W]], axis=0)
                    dk_h = _dot_tn(dsc, qm)
                    dv_h = _dot_tn(p, dom)
                    dk = dk_h if dk is None else dk + dk_h
                    dv = dv_h if dv is None else dv + dv_h
                o_ref[1, rows, sl] = dk.astype(BF)
                o_ref[2, rows, sl] = dv.astype(BF)

        def k_block(row0, qq, doo, stv, dsv, mask):
            rows = pl.ds(row0, ATT_W)
            for hp in range(C // 128):
                sl = slice(hp * 128, (hp + 1) * 128)
                qp = qq[:, sl]
                dop = doo[:, sl]
                kp = k_ref[rows, sl]
                vp = v_ref[rows, sl]
                dks, dvs = [], []
                for h in range(2 * hp, min(2 * hp + 2, nh)):
                    hm = _head_mask(qp.shape, h)
                    qm = jnp.where(hm, qp, jnp.zeros_like(qp))
                    dom = jnp.where(hm, dop, jnp.zeros_like(dop))
                    p = probs(qm, kp, _lane_col(stv, h), mask)
                    dp = _dot_nt(dom, vp)
                    dsc = (p * (dp - _lane_col(dsv, h))).astype(BF)
                    dks.append(_dot_tn(dsc, qm))
                    dvs.append(_dot_tn(p.astype(BF), dom))
                o_ref[1, rows, sl] = sum(dks[1:], dks[0]).astype(BF)
                o_ref[2, rows, sl] = sum(dvs[1:], dvs[0]).astype(BF)

        first_mask = jnp.logical_and(band_q, jnp.logical_or(kj >= ATT_W, i > 0))
        q_block(0, jnp.concatenate([kh_ref[...], k_ref[pl.ds(0, ATT_W), :]], axis=0),
                jnp.concatenate([vh_ref[...], v_ref[pl.ds(0, ATT_W), :]], axis=0), first_mask)
        for blk in range(1, nblk):
            prev = pl.ds((blk - 1) * ATT_W, 2 * ATT_W)
            q_block(blk, k_ref[prev, :], v_ref[prev, :], band_q)
        for m in range(nblk - 1):
            k_block_kept(m)

        last = pl.ds((nblk - 1) * ATT_W, ATT_W)
        last_mask = jnp.logical_and(band_k, jnp.logical_or(qa < ATT_W, i < nchunk - 1))
        k_block((nblk - 1) * ATT_W,
                jnp.concatenate([q_ref[last, :], qn_ref[...]], axis=0),
                jnp.concatenate([do_ref[last, :], don_ref[...]], axis=0),
                jnp.concatenate([st_ref[last, :], stn_ref[...]], axis=0),
                jnp.concatenate([ds_ref[last, :], dsn_ref[...]], axis=0), last_mask)

    nb_all = L // ATT_W
    main4 = lambda w: pl.BlockSpec((None, None, lc, C), lambda r, i: (w, r, i, 0))
    prev4 = lambda w: pl.BlockSpec((None, None, ATT_W, C), lambda r, i: (w, r, jnp.maximum(i * nblk - 1, 0), 0))
    next4 = lambda w: pl.BlockSpec((None, None, ATT_W, C), lambda r, i: (w, r, jnp.minimum((i + 1) * nblk, nb_all - 1), 0))
    main3 = lambda n: pl.BlockSpec((None, lc, n), lambda r, i: (r, i, 0))
    next3 = lambda n: pl.BlockSpec((None, ATT_W, n), lambda r, i: (r, jnp.minimum((i + 1) * nblk, nb_all - 1), 0))
    return pl.pallas_call(
        body, name=name, grid=(d, nchunk),
        in_specs=[main4(0), next4(0), main4(1), prev4(1), main4(2), prev4(2),
                  main3(C), next3(C), main3(128), next3(128), main3(128), next3(128)],
        out_specs=pl.BlockSpec((3, None, lc, C), lambda r, i: (0, r, i, 0)),
        out_shape=jax.ShapeDtypeStruct((3, d, L, C), BF),
        scratch_shapes=[pltpu.VMEM((nblk, nh, ATT_W, 2 * ATT_W), BF)] * 2,
        compiler_params=_params("parallel", "arbitrary"),
    )(qkv, qkv, qkv, qkv, qkv, qkv, do, do, st, st, dst, dst)


def _alpha_from(lse_nat):
    m = jnp.maximum(jnp.maximum(lse_nat[0], lse_nat[1]), lse_nat[2])
    e = [jnp.exp(l - m) for l in lse_nat]
    inv = 1.0 / (e[0] + e[1] + e[2])
    return [ei * inv for ei in e]


def _attn_out_fwd(x, os_, sts, wo, tm=1024):
    T = x.shape[0]
    C = GROUP_LANES

    def body(x_ref, o0, o1, o2, s0, s1, s2, w_ref, xo_ref, m_ref, al_ref, oscr, sscr):
        o_refs, st_refs = (o0, o1, o2), (s0, s1, s2)
        lses = []
        for g, d in enumerate(DILATIONS):
            _gather_rows(sscr.at[g], st_refs[g], d)
            lses.append(_lane_col(sscr[g, 0], LSE_GROUP_LANE))
        alpha = _alpha_from(lses)
        for g, d in enumerate(DILATIONS):
            _gather_rows(oscr, o_refs[g], d)
            m_ref[:, g * C:(g + 1) * C] = (_chunks_to_rows(oscr) * (3.0 * alpha[g])).astype(BF)
        xo_ref[...] = x_ref[...] + _dot(m_ref[...], w_ref[...])
        lane = lax.broadcasted_iota(jnp.int32, (tm, 128), 1)
        al_ref[...] = jnp.where(lane == 0, alpha[0], jnp.where(lane == 1, alpha[1], jnp.where(lane == 2, alpha[2], 0.0)))

    o_specs = [pl.BlockSpec((d, tm // d, C), lambda i: (0, i, 0)) for d in DILATIONS]
    st_specs = [pl.BlockSpec((d, tm // d, 128), lambda i: (0, i, 0)) for d in DILATIONS]
    return pl.pallas_call(
        body, name="attn_out_fwd", grid=(T // tm,),
        in_specs=[_row_spec(tm, D)] + o_specs + st_specs + [_full_spec((3 * C, D))],
        out_specs=[_row_spec(tm, D), _row_spec(tm, 3 * C), _row_spec(tm, 128)],
        out_shape=[jax.ShapeDtypeStruct((T, D), F32), jax.ShapeDtypeStruct((T, 3 * C), BF), jax.ShapeDtypeStruct((T, 128), F32)],
        scratch_shapes=[pltpu.VMEM((C // 128, tm, 128), F32), pltpu.VMEM((3, 1, tm, 128), F32)],
        compiler_params=_params("parallel"),
    )(x, *os_, *sts, wo)


def _attn_out_bwd(dx, os_, sts, alpha, wo, tm=1024, dep=None):
    T = dx.shape[0]
    C = GROUP_LANES

    def body(dx_ref, o0, o1, o2, s0, s1, s2, al_ref, w_ref, do0, do1, do2, ds0, ds1, ds2, dmscr, oscr, sscr, tscr):
        o_refs, st_refs = (o0, o1, o2), (s0, s1, s2)
        do_refs, ds_refs = (do0, do1, do2), (ds0, ds1, ds2)
        dyb = dx_ref[...].astype(BF)
        alv = al_ref[...]
        alpha_g = [_lane_col(alv, g) for g in range(3)]
        dalpha = []
        for g, d in enumerate(DILATIONS):
            dm = _dot_nt(dyb, w_ref[g * C:(g + 1) * C, :])
            _rows_to_chunks(dmscr.at[g], dm)
            _gather_rows(oscr.at[g], o_refs[g], d)
            _gather_rows(sscr.at[g], st_refs[g], d)
            dalpha.append(3.0 * jnp.sum(dm * _chunks_to_rows(oscr.at[g]), axis=-1, keepdims=True))
        mean_da = alpha_g[0] * dalpha[0] + alpha_g[1] * dalpha[1] + alpha_g[2] * dalpha[2]
        lane = lax.broadcasted_iota(jnp.int32, (tm, 128), 1)
        seg = (lax.broadcasted_iota(jnp.int32, (C, 128), 0) // HEAD_DIM == lax.broadcasted_iota(jnp.int32, (C, 128), 1)).astype(BF)
        for g, d in enumerate(DILATIONS):
            nh = HEAD_GROUPS[g]
            dlse_g = alpha_g[g] * (dalpha[g] - mean_da)
            do_nat = _chunks_to_rows(dmscr.at[g]) * (3.0 * alpha_g[g])
            prod = do_nat * _chunks_to_rows(oscr.at[g])
            hi = prod.astype(BF)
            delta = _dot(hi, seg) + _dot((prod - hi.astype(F32)).astype(BF), seg)
            stv = sscr[g, 0]
            dlse = dlse_g * jnp.exp(stv - _lane_col(stv, LSE_GROUP_LANE)) * (1.0 / nh)
            tile = jnp.where(lane < nh, delta - dlse, 0.0)
            _rows_to_chunks(dmscr.at[g], do_nat)
            _scatter_rows(do_refs[g], dmscr.at[g], d, BF)
            tscr[0] = tile
            _scatter_rows(ds_refs[g], tscr, d, F32)

    o_specs = [pl.BlockSpec((d, tm // d, C), lambda i: (0, i, 0)) for d in DILATIONS]
    st_specs = [pl.BlockSpec((d, tm // d, 128), lambda i: (0, i, 0)) for d in DILATIONS]
    body, in_specs, args = _after(
        dep, body, [_row_spec(tm, D)] + o_specs + st_specs + [_row_spec(tm, 128)] + [_full_spec((3 * C, D))],
        [dx, *os_, *sts, alpha, wo])
    return pl.pallas_call(
        body, name="attn_out_bwd", grid=(T // tm,),
        in_specs=in_specs,
        out_specs=o_specs + st_specs,
        out_shape=[jax.ShapeDtypeStruct((d, T // d, C), BF) for d in DILATIONS]
        + [jax.ShapeDtypeStruct((d, T // d, 128), F32) for d in DILATIONS],
        scratch_shapes=[pltpu.VMEM((3, C // 128, tm, 128), F32), pltpu.VMEM((3, C // 128, tm, 128), F32),
                        pltpu.VMEM((3, 1, tm, 128), F32), pltpu.VMEM((1, tm, 128), F32)],
        compiler_params=_params("parallel"),
    )(*args)


def _qkv_bwd(dqkvs, cos, sin, x2, g, w3s, dx3, tm=512):
    T = x2.shape[0]
    C = GROUP_LANES

    def body(dq0, dq1, dq2, cos_ref, sin_ref, x_ref, g_ref, w0, w1, w2, dx3_ref,
             dx_ref, n0, n1, n2, h_ref, dg_ref, scr, dh_ref):
        dq_refs, w_refs, n_refs = (dq0, dq1, dq2), (w0, w1, w2), (n0, n1, n2)

        @pl.when(pl.program_id(0) == 0)
        def _():
            dg_ref[...] = jnp.zeros_like(dg_ref)

        cv = cos_ref[...]
        sv = sin_ref[...]
        dh_ref[...] = jnp.zeros_like(dh_ref)
        for gi, d in enumerate(DILATIONS):
            for w in range(3):
                _gather_rows(scr, dq_refs[gi].at[w], d)
                if w < 2:
                    scale = HEAD_DIM ** -0.5 if w == 0 else 1.0
                    for c in range(C // 128):
                        t = scr[c]
                        scr[c] = (t * cv - _rot_half(t) * sv) * scale
                n_refs[gi][:, w * C:(w + 1) * C] = _chunks_to_rows(scr).astype(BF)
            dh_ref[...] += _dot_nt(n_refs[gi][...], w_refs[gi][...])
        xv = x_ref[...]
        h_ref[...] = _rms_fwd(xv, g_ref[...]).astype(BF)
        dx, dg = _rms_bwd(xv, g_ref[...], dh_ref[...])
        dx_ref[...] = dx3_ref[...] + dx
        dg_ref[...] += dg

    dq_specs = [pl.BlockSpec((3, d, tm // d, C), lambda i: (0, 0, i, 0)) for d in DILATIONS]
    nat = _row_spec(tm, 3 * C)
    nshape = jax.ShapeDtypeStruct((T, 3 * C), BF)
    return pl.pallas_call(
        body, name="qkv_bwd", grid=(T // tm,),
        in_specs=dq_specs + [_row_spec(tm, 128), _row_spec(tm, 128), _row_spec(tm, D), _full_spec((1, D))]
        + [pl.BlockSpec((D, 3 * C), lambda i: (0, 0), pipeline_mode=pl.Buffered(1))] * 3 + [_row_spec(tm, D)],
        out_specs=[_row_spec(tm, D), nat, nat, nat, _row_spec(tm, D), _full_spec((1, D))],
        out_shape=[jax.ShapeDtypeStruct((T, D), F32), nshape, nshape, nshape, jax.ShapeDtypeStruct((T, D), BF),
                   jax.ShapeDtypeStruct((1, D), F32)],
        scratch_shapes=[pltpu.VMEM((C // 128, tm, 128), F32), pltpu.VMEM((tm, D), F32)],
        compiler_params=_params("arbitrary"),
    )(*dqkvs, cos, sin, x2, g, *w3s, dx3)


def _place():
    x, y, c = lax.axis_index("x"), lax.axis_index("y"), lax.axis_index("c")
    return x, y, c


def _gather_phases(ins, outs, *sems):
    n = len(ins)
    if n == 0:
        return (lambda: None), (lambda: None), (lambda: None)
    send_sems, recv_sems, local_sems = sems
    x, y, c = _place()
    me, sibling = (x, y, c), (x, y, 1 - c)
    chips = [(1 - x, y), (x, 1 - y), (1 - x, 1 - y)]

    def slot(a, px, py, pc):
        return outs[a].at[4 * px + 2 * py + pc]

    def copy(a, k, block, to, src=None):
        return pltpu.make_async_remote_copy(
            src_ref=slot(a, *block) if src is None else src, dst_ref=slot(a, *block),
            send_sem=send_sems.at[a, k], recv_sem=recv_sems.at[a, k], device_id=to, device_id_type=MESH)

    def mine(a):
        return pltpu.make_async_copy(ins[a], slot(a, *me), local_sems.at[a])

    def first(a):
        return [copy(a, 0, me, sibling, src=ins[a])] + [copy(a, 1 + j, me, (*chip, c), src=ins[a]) for j, chip in enumerate(chips)]

    def begin():
        for a in range(n):
            mine(a).start()
        for a in range(n):
            for cp in first(a):
                cp.start()

    def forward():
        for a in range(n):
            for j, chip in enumerate(chips):
                copy(a, 1 + j, (*chip, c), me).wait_recv()
                copy(a, 4 + j, (*chip, c), sibling).start()

    def finish():
        for a in range(n):
            copy(a, 0, sibling, me).wait_recv()
            for j, chip in enumerate(chips):
                copy(a, 4 + j, (*chip, 1 - c), me).wait_recv()
        for a in range(n):
            for cp in first(a):
                cp.wait_send()
            for j, chip in enumerate(chips):
                copy(a, 4 + j, (*chip, c), sibling).wait_send()
        for a in range(n):
            mine(a).wait()

    return begin, forward, finish


def _gather_scratch(n):
    return [pltpu.SemaphoreType.DMA((n, 7)), pltpu.SemaphoreType.DMA((n, 7)), pltpu.SemaphoreType.DMA((n,))] if n else []


def _allgather(arrs, name):
    n = len(arrs)

    def body(*refs):
        begin, forward, finish = _gather_phases(refs[:n], refs[n:2 * n], *refs[2 * n:])
        begin()
        forward()
        finish()

    hbm = pl.BlockSpec(memory_space=pl.ANY)
    return pl.pallas_call(
        body, name=name,
        in_specs=[hbm] * n, out_specs=[hbm] * n,
        out_shape=[jax.ShapeDtypeStruct((N_DEV,) + a.shape, a.dtype) for a in arrs],
        scratch_shapes=_gather_scratch(n),
    )(*arrs)


def _peer(k):
    x, y, c = _place()
    px = 1 - x if k & 4 else x
    py = 1 - y if k & 2 else y
    pc = 1 - c if k & 1 else c
    return (px, py, pc), 4 * px + 2 * py + pc


HBM_SPEC = pl.BlockSpec(memory_space=pltpu.HBM)
SEM_SPEC = pl.BlockSpec(memory_space=pltpu.SEMAPHORE)
EFFECT = pltpu.SideEffectType.DATAFLOW_SIDE_EFFECTING


def _exchange_start(arrs, name, same_block=False, dep=None):
    n = len(arrs)
    n_dep = 0 if dep is None else 1

    def body(*refs):
        srcs, lands = refs[:n], refs[n:2 * n]
        send_sems, recv_sems, local_sems = refs[2 * n + n_dep:2 * n + n_dep + 3]
        token = refs[-1]
        x, y, c = _place()
        me = 4 * x + 2 * y + c
        block = (lambda a, j: srcs[a]) if same_block else (lambda a, j: srcs[a].at[j])
        for a in range(n):
            pltpu.make_async_copy(block(a, me), lands[a].at[me], local_sems.at[a]).start()
        for a in range(n):
            for k in range(1, N_DEV):
                to, to_idx = _peer(k)
                pltpu.make_async_remote_copy(
                    src_ref=block(a, to_idx), dst_ref=lands[a].at[me],
                    send_sem=send_sems.at[a * (N_DEV - 1) + k - 1], recv_sem=recv_sems.at[a * (N_DEV - 1) + k - 1], device_id=to, device_id_type=MESH).start()
        token[...] = jnp.zeros_like(token)

    land_shape = (lambda a: (N_DEV,) + a.shape) if same_block else (lambda a: a.shape)
    src_shapes = [pltpu.HBM(a.shape, a.dtype) for a in arrs]
    land_shapes = [pltpu.HBM(land_shape(a), a.dtype) for a in arrs]
    outs = pl.pallas_call(
        body, name=name,
        out_shape=(pltpu.SemaphoreType.DMA((n * (N_DEV - 1),)), pltpu.SemaphoreType.DMA((n * (N_DEV - 1),)),
                   pltpu.SemaphoreType.DMA((n,)), *src_shapes, *land_shapes, jax.ShapeDtypeStruct((8, 128), F32)),
        in_specs=[HBM_SPEC] * (2 * n) + [pl.BlockSpec(memory_space=pl.ANY)] * n_dep,
        out_specs=(SEM_SPEC, SEM_SPEC, SEM_SPEC, *([HBM_SPEC] * (2 * n)), pl.BlockSpec(memory_space=pltpu.VMEM)),
        input_output_aliases={i: 3 + i for i in range(2 * n)},
        compiler_params=pltpu.CompilerParams(has_side_effects=EFFECT),
    )(*[pltpu.with_memory_space_constraint(a, pltpu.HBM) for a in arrs],
      *[pltpu.with_memory_space_constraint(lax.empty(land_shape(a), a.dtype), pltpu.HBM) for a in arrs],
      *([] if dep is None else [dep]))
    return outs[0], outs[1], outs[2], outs[3:3 + n], outs[3 + n:3 + 2 * n], outs[3 + 2 * n]


def _exchange_wait(send_sems, recv_sems, local_sems, src_thru, land_thru, after, name):
    n = len(src_thru)
    after = list(after) if isinstance(after, (list, tuple)) else [after]
    same_block = src_thru[0].shape != land_thru[0].shape

    def body(*refs):
        srcs, lands = refs[:n], refs[n:2 * n]
        send_sems, recv_sems, local_sems = refs[2 * n:2 * n + 3]
        x, y, c = _place()
        me = 4 * x + 2 * y + c
        for a in range(n):
            pltpu.make_async_copy(srcs[a] if same_block else srcs[a].at[me], lands[a].at[me], local_sems.at[a]).wait()
            for k in range(1, N_DEV):
                frm, frm_idx = _peer(k)
                cp = pltpu.make_async_remote_copy(
                    src_ref=srcs[a] if same_block else srcs[a].at[frm_idx], dst_ref=lands[a].at[frm_idx],
                    send_sem=send_sems.at[a * (N_DEV - 1) + k - 1], recv_sem=recv_sems.at[a * (N_DEV - 1) + k - 1], device_id=frm, device_id_type=MESH)
                cp.wait_send()
                cp.wait_recv()

    outs = pl.pallas_call(
        body, name=name,
        out_shape=tuple(pltpu.HBM(a.shape, a.dtype) for a in (*src_thru, *land_thru)),
        in_specs=[HBM_SPEC] * (2 * n) + [SEM_SPEC, SEM_SPEC, SEM_SPEC] + [pl.BlockSpec(memory_space=pl.ANY)] * len(after),
        out_specs=[HBM_SPEC] * (2 * n),
        input_output_aliases={i: i for i in range(2 * n)},
        compiler_params=pltpu.CompilerParams(has_side_effects=EFFECT),
    )(*src_thru, *land_thru, send_sems, recv_sems, local_sems, *after)
    return outs[n:]


def _row_tile(rows):
    for t in (256, 176, 128, 8):
        if rows % t == 0:
            return t
    return rows


def _adamw(parts, w, m, v, name, layer=None, into=None):
    K, R, C = parts.shape
    tr = _row_tile(R)
    n_into = 0 if into is None else 4

    def body(p_ref, w_ref, m_ref, v_ref, *refs):
        g_ref, d_ref, nm_ref, nv_ref = refs[n_into:]
        g = p_ref[0].astype(F32)
        for k in range(1, K):
            g = g + p_ref[k].astype(F32)
        nm = ADAM_B1 * m_ref[...] + (1.0 - ADAM_B1) * g
        nv = ADAM_B2 * v_ref[...] + (1.0 - ADAM_B2) * jnp.square(g)
        m_hat = nm / (1.0 - ADAM_B1 ** ADAM_STEP)
        v_hat = nv / (1.0 - ADAM_B2 ** ADAM_STEP)
        g_ref[...] = g
        d_ref[...] = -ADAM_LR * (m_hat / (jnp.sqrt(v_hat) + ADAM_EPS) + ADAM_WD * w_ref[...])
        nm_ref[...] = nm
        nv_ref[...] = nv

    blk = _row_spec(tr, C)
    parts_spec = pl.BlockSpec((K, tr, C), lambda i: (0, i, 0))
    if layer is None:
        out_blk, shp = blk, jax.ShapeDtypeStruct((R, C), F32)
    else:
        out_blk, shp = pl.BlockSpec((None, tr, C), lambda i: (layer, i, 0)), jax.ShapeDtypeStruct((2, R, C), F32)
    return pl.pallas_call(
        body, name=name, grid=(R // tr,),
        in_specs=[parts_spec, blk, blk, blk] + [pl.BlockSpec(memory_space=pl.ANY)] * n_into,
        out_specs=[out_blk] * 4,
        out_shape=[shp] * 4,
        input_output_aliases={4 + j: j for j in range(n_into)},
        compiler_params=_params("parallel"),
    )(parts, w, m, v, *(into or ()))


def _rope_tables(T):
    inv_freq = 1.0 / (ROPE_THETA ** (jnp.arange(0, HEAD_DIM, 2, dtype=F32) / HEAD_DIM))
    inv_freq = jnp.tile(inv_freq, 4)
    hi = (64.0 * jnp.arange(T // 64, dtype=F32))[:, None] * inv_freq[None, :]
    lo = jnp.arange(64, dtype=F32)[:, None] * inv_freq[None, :]
    ch, sh, cl, sl = jnp.cos(hi)[:, None], jnp.sin(hi)[:, None], jnp.cos(lo)[None], jnp.sin(lo)[None]
    return (ch * cl - sh * sl).reshape(T, 128), (sh * cl + ch * sl).reshape(T, 128)


def _pad_lanes(a, n):
    return jnp.pad(a, ((0, 0),) * (a.ndim - 1) + ((0, n - a.shape[-1]),))


POOL = ("pool_in", "pool_grp", "pool_out")
FFN0 = ("gate0", "up0", "down0")
LAYER1 = ("qkv", "attn_out", "gate1", "up1", "down1")


def _qkv_regions():
    return [(k * D + off * HEAD_DIM, nh * HEAD_DIM, gi, k)
            for k in range(3) for gi, (nh, off) in enumerate(zip(HEAD_GROUPS, HEAD_OFFS))]


def _overlaps(lo, n, step):
    return [(j, max(lo, j * step) - j * step, min(lo + n, (j + 1) * step) - j * step)
            for j in range(lo // step, (lo + n - 1) // step + 1)]


def _layout_weights(gw):
    w = {}
    if "pool_in" in gw:
        w["pool_in"] = gw["pool_in"].reshape(D, D)
        w["pool_grp"] = jnp.transpose(gw["pool_grp"], (1, 0, 2, 3)).reshape(4, POOL_GC, POOL_GC)
        w["pool_out"] = gw["pool_out"].reshape(D, D)
    if "qkv" in gw:
        wo = gw["attn_out"].reshape(D, D)
        shard_cols = 3 * D // N_DEV
        pieces = [[] for _ in HEAD_GROUPS]
        for lo, n, gi, _ in _qkv_regions():
            cols = [gw["qkv"][j][:, a:b] for j, a, b in _overlaps(lo, n, shard_cols)]
            pieces[gi].append(_pad_lanes(jnp.concatenate(cols, axis=1), GROUP_LANES))
        w["qkv"] = [jnp.concatenate(p, axis=1) for p in pieces]
        w["attn_out"] = jnp.concatenate([jnp.pad(wo[off * HEAD_DIM:(off + nh) * HEAD_DIM], ((0, GROUP_LANES - nh * HEAD_DIM), (0, 0)))
                                         for nh, off in zip(HEAD_GROUPS, HEAD_OFFS)], axis=0)
    for nm in gw:
        if nm[:-1] in ("gate", "up", "down"):
            w[nm] = gw[nm].reshape(F, D)
    return w


def _local_step(x, target, w, ffn0_shards, layer1_shards, norm_mix, norm_ffn, norm_final, pool_scale, emit):
    T = x.shape[0]
    cos, sin = _rope_tables(T)
    nm = [norm_mix[i:i + 1] for i in range(2)]
    nf = [norm_ffn[i:i + 1] for i in range(2)]
    nfin = norm_final.reshape(1, D)

    x1, p, *ffn0 = _pool_fwd(x, nm[0], w["pool_in"], w["pool_grp"], pool_scale, w["pool_out"], ffn0_shards)
    w = {**w, **_layout_weights(dict(zip(FFN0, ffn0)))}
    x2, a0, b0, s0, *layer1 = _ffn_fwd(x1, nf[0], w["gate0"], w["up0"], w["down0"], "ffn_fwd0", gather=layer1_shards)
    w = {**w, **_layout_weights(dict(zip(LAYER1, layer1)))}
    qkvs = _qkv_fwd(x2, nm[1], w["qkv"], cos, sin)
    att = [_attn_fwd(qkvs[g], HEAD_GROUPS[g], f"attn_fwd{g}") for g in range(3)]
    os_, sts = [a[0] for a in att], [a[1] for a in att]
    x3, merged, alpha = _attn_out_fwd(x2, os_, sts, w["attn_out"])
    dx4, a1, b1, s1, loss, dg_final = _ffn_fwd(x3, nf[1], w["gate1"], w["up1"], w["down1"], "ffn_fwd1", head=(target, nfin))

    g = {}
    dx3, da1, db1, h3, dy4, dg_ffn1 = _ffn_bwd(dx4, x3, nf[1], a1, b1, w["gate1"], w["up1"], w["down1"], "ffn_bwd1")
    g["down1"] = _wgrad(s1, dy4, "down_wgrad1", mblk=F // 2)
    g["gate1"] = _wgrad(da1, h3, "gate_wgrad1", mblk=F // 2)
    g["up1"] = _wgrad(db1, h3, "up_wgrad1", mblk=F // 2)
    dep = emit("ffn1", g)
    dos_and_stats = _attn_out_bwd(dx3, os_, sts, alpha, w["attn_out"], dep=dep)
    dos, dsts = dos_and_stats[:3], dos_and_stats[3:]
    g["attn_out"] = _wgrad(merged, dx3, "attn_out_wgrad")
    dqkvs = [_attn_bwd(qkvs[gi], dos[gi], sts[gi], dsts[gi], HEAD_GROUPS[gi], f"attn_bwd{gi}") for gi in range(3)]
    dx2, n0, n1, n2, h2b, dg_mix1 = _qkv_bwd(dqkvs, cos, sin, x2, nm[1], w["qkv"], dx3)
    g["qkv"] = [_wgrad(h2b, n, f"qkv_wgrad{i}") for i, n in enumerate((n0, n1, n2))]
    dep = emit("attn", g)
    dx1, da0, db0, h1, dy2, dg_ffn0 = _ffn_bwd(dx2, x1, nf[0], a0, b0, w["gate0"], w["up0"], w["down0"], "ffn_bwd0", dep=dep)
    g["down0"] = _wgrad(s0, dy2, "down_wgrad0", mblk=F // 2)
    g["gate0"] = _wgrad(da0, h1, "gate_wgrad0", mblk=F // 2)
    g["up0"] = _wgrad(db0, h1, "up_wgrad0", mblk=F // 2)
    dep = emit("ffn0", g)
    dx0, z, dzp, du, h0b, dscale, dg_mix0 = _pool_bwd(dx1, x, nm[0], p, w["pool_in"], w["pool_grp"], pool_scale, w["pool_out"], dep=dep)
    g["pool_out"] = _wgrad(z, dx1, "pool_out_wgrad")
    g["pool_in"] = _wgrad(h0b, du, "pool_in_wgrad")
    g["pool_grp"] = _wgrad_pool_groups(p, dzp)
    emit("pool", g)

    small = jnp.concatenate([dg_mix0, dg_mix1, dg_ffn0, dg_ffn1, dg_final, dscale,
                             jnp.broadcast_to(loss, (1, D)), jnp.zeros((1, D), F32)], axis=0)
    return dx0, small


GROUPS = {"ffn1": ("down1", "gate1", "up1"), "attn": ("qkv", "attn_out"), "ffn0": ("down0", "gate0", "up0"),
          "pool": ("pool_in", "pool_out", "pool_grp")}


def _grad_blocks(group, g):
    blocks = {}
    if group == "pool":
        blocks["pool_in"] = g["pool_in"].reshape(N_DEV, D // N_DEV, D)
        blocks["pool_out"] = g["pool_out"].reshape(N_DEV, D // N_DEV, D)
        blocks["pool_grp"] = jnp.transpose(g["pool_grp"].reshape(4, N_DEV, POOL_GC // N_DEV, POOL_GC), (1, 0, 2, 3)).reshape(N_DEV, 4 * POOL_GC // N_DEV, POOL_GC)
    elif group == "attn":
        wo = jnp.concatenate([g["attn_out"][gi * GROUP_LANES:gi * GROUP_LANES + nh * HEAD_DIM] for gi, nh in enumerate(HEAD_GROUPS)], axis=0)
        blocks["attn_out"] = wo.reshape(N_DEV, D // N_DEV, D)
        shard_cols = 3 * D // N_DEV
        per_dev = [[] for _ in range(N_DEV)]
        for lo, n, gi, k in _qkv_regions():
            for j, a, b in _overlaps(lo, n, shard_cols):
                start = k * GROUP_LANES + j * shard_cols + a - lo
                per_dev[j].append(g["qkv"][gi][:, start:start + b - a])
        blocks["qkv"] = jnp.stack([jnp.concatenate(p, axis=1) for p in per_dev])
    else:
        for nm in GROUPS[group]:
            blocks[nm] = g[nm].reshape(N_DEV, F // N_DEV, D)
    return [blocks[nm] for nm in GROUPS[group]]


def kernel(x, norm_mix, norm_ffn, norm_final, pool_w_in, pool_w_group, pool_scale, pool_w_out, attn_w_qkv, attn_w_out, ffn_w_gate, ffn_w_up, ffn_w_down, loss_target, m_norm_mix, m_norm_ffn, m_norm_final, m_pool_w_in, m_pool_w_group, m_pool_scale, m_pool_w_out, m_attn_w_qkv, m_attn_w_out, m_ffn_w_gate, m_ffn_w_up, m_ffn_w_down, v_norm_mix, v_norm_ffn, v_norm_final, v_pool_w_in, v_pool_w_group, v_pool_scale, v_pool_w_out, v_attn_w_qkv, v_attn_w_out, v_ffn_w_gate, v_ffn_w_up, v_ffn_w_down):
    shard = {
        "pool_in": pool_w_in[0], "pool_grp": pool_w_group[0], "pool_out": pool_w_out[0],
        "qkv": attn_w_qkv[0], "attn_out": attn_w_out[0],
    }
    tr3 = lambda t: jnp.transpose(t, (0, 2, 1))
    gate_t, up_t = tr3(ffn_w_gate), tr3(ffn_w_up)
    for l in range(2):
        shard[f"gate{l}"] = gate_t[l]
        shard[f"up{l}"] = up_t[l]
        shard[f"down{l}"] = ffn_w_down[l]
    shard = {k: v.astype(BF) for k, v in shard.items()}
    w0 = _layout_weights(dict(zip(POOL, _allgather([shard[k] for k in POOL], "weights_allgather"))))

    started = {}

    def emit(group, g):
        started[group] = _exchange_start(_grad_blocks(group, g), f"grads_start_{group}")
        return started[group][5]

    grad_x, small = _local_step(x[0], loss_target[0], w0, [shard[k] for k in FFN0], [shard[k] for k in LAYER1], norm_mix, norm_ffn, norm_final, pool_scale, emit)

    small_st = _exchange_start([small], "small_start", same_block=True, dep=started["pool"][5])

    def finish(group, after):
        lands = _exchange_wait(*started[group][:5], after, f"grads_wait_{group}")
        return dict(zip(GROUPS[group], lands))

    def upd(parts, wt, mt, vt, name):
        shape = wt.shape
        r2 = lambda t: t.reshape(parts.shape[1:])
        outs = _adamw(parts, r2(wt), r2(mt), r2(vt), name)
        return [o.reshape(shape) for o in outs]

    ffn_state = (("gate", gate_t, tr3(m_ffn_w_gate), tr3(v_ffn_w_gate)), ("up", up_t, tr3(m_ffn_w_up), tr3(v_ffn_w_up)),
                 ("down", ffn_w_down, m_ffn_w_down, v_ffn_w_down))

    def ffn_layer(rcv, l, other=None):
        return {nm: _adamw(rcv[f"{nm}{l}"], wt[l], mt[l], vt[l], f"adamw_{nm}{l}", layer=l,
                           into=None if other is None else other[nm]) for nm, wt, mt, vt in ffn_state}

    res = {}
    ffn1 = ffn_layer(finish("ffn1", small_st[5]), 1)
    rcv = finish("attn", [ffn1[nm][0] for nm in ffn1])
    res["attn_w_qkv"] = upd(rcv["qkv"], attn_w_qkv, m_attn_w_qkv, v_attn_w_qkv, "adamw_qkv")
    res["attn_w_out"] = upd(rcv["attn_out"], attn_w_out, m_attn_w_out, v_attn_w_out, "adamw_attn_out")
    ffn0 = ffn_layer(finish("ffn0", [res["attn_w_qkv"][0], res["attn_w_out"][0]]), 0, other=ffn1)
    rcv = finish("pool", [ffn0[nm][0] for nm in ffn0])
    res["pool_w_in"] = upd(rcv["pool_in"], pool_w_in, m_pool_w_in, v_pool_w_in, "adamw_pool_in")
    res["pool_w_group"] = upd(rcv["pool_grp"], pool_w_group, m_pool_w_group, v_pool_w_group, "adamw_pool_grp")
    res["pool_w_out"] = upd(rcv["pool_out"], pool_w_out, m_pool_w_out, v_pool_w_out, "adamw_pool_out")
    res["ffn_w_gate"] = [tr3(t) for t in ffn0["gate"]]
    res["ffn_w_up"] = [tr3(t) for t in ffn0["up"]]
    res["ffn_w_down"] = ffn0["down"]
    small_all = _exchange_wait(*small_st[:5], [res[k][0] for k in ("pool_w_in", "pool_w_group", "pool_w_out")], "small_wait")[0]

    small_w = jnp.concatenate([norm_mix, norm_ffn, norm_final[None], pool_scale, jnp.zeros((2, D), F32)], axis=0)
    small_m = jnp.concatenate([m_norm_mix, m_norm_ffn, m_norm_final[None], m_pool_scale, jnp.zeros((2, D), F32)], axis=0)
    small_v = jnp.concatenate([v_norm_mix, v_norm_ffn, v_norm_final[None], v_pool_scale, jnp.ones((2, D), F32)], axis=0)
    sg, sd, sm, sv = _adamw(small_all, small_w, small_m, small_v, "adamw_small")
    loss = sg[6, 0]
    res["norm_mix"] = [t[0:2] for t in (sg, sd, sm, sv)]
    res["norm_ffn"] = [t[2:4] for t in (sg, sd, sm, sv)]
    res["norm_final"] = [t[4] for t in (sg, sd, sm, sv)]
    res["pool_scale"] = [t[5:6] for t in (sg, sd, sm, sv)]

    order = ["norm_mix", "norm_ffn", "norm_final", "pool_w_in", "pool_w_group", "pool_scale", "pool_w_out",
             "attn_w_qkv", "attn_w_out", "ffn_w_gate", "ffn_w_up", "ffn_w_down"]
    return (loss, grad_x[None], *[res[k][0] for k in order], *[res[k][1] for k in order],
            *[res[k][2] for k in order], *[res[k][3] for k in order])
```
